```python
import jax, jax.numpy as jnp
from jax import lax
import numpy as np

D_MODEL = 1024
BATCH = 2
SEQ = 16384
DEPTH = 2

D_MIX = D_MODEL
EPS = 1e-6
NEG = -1e30
FORCED = 1e6

NSA_DH = 64
NSA_W = D_MIX // 2
NSA_HQ = NSA_W // NSA_DH
NSA_HKV = 2
NSA_G = NSA_HQ // NSA_HKV
KV_W = NSA_HKV * NSA_DH
CMP_BLOCK = 32
CMP_STRIDE = 16
CMP_HID = 128
SLC_BLOCK = 64
N_SEL = 16
WINDOW = 512
Q_BLOCK = 128

LRU_W = D_MIX // 4
LRU_HEADS = 8
LRU_BW = LRU_W // LRU_HEADS
LRU_CONV = 4
LRU_C = 8.0

CV_W = D_MIX // 4
CV_KERNEL = 31

D_FF = 4 * D_MODEL

OFF_Q = 0
OFF_KV = OFF_Q + NSA_W
OFF_GATE = OFF_KV + 6 * KV_W
OFF_LRU_X = OFF_GATE + 3 * NSA_HQ
OFF_LRU_G = OFF_LRU_X + LRU_W
OFF_CV = OFF_LRU_G + LRU_W
N_IN = OFF_CV + 2 * CV_W

kernel_name = "hymba_nsa_rglru_conformer_trunk"


def rms_norm(x, g):
    xf = x.astype(jnp.float32)
    y = xf * lax.rsqrt(jnp.mean(xf * xf, axis=-1, keepdims=True) + EPS)
    return (y * g.astype(jnp.float32)).astype(x.dtype)


def layer_norm(x, g, b):
    xf = x.astype(jnp.float32)
    mu = jnp.mean(xf, axis=-1, keepdims=True)
    var = jnp.mean(jnp.square(xf - mu), axis=-1, keepdims=True)
    y = (xf - mu) * lax.rsqrt(var + EPS)
    return (y * g.astype(jnp.float32) + b.astype(jnp.float32)).astype(x.dtype)


def causal_depthwise_conv(x, w, b):
    k = w.shape[0]
    y = lax.conv_general_dilated(
        x, w[:, None, :].astype(x.dtype), window_strides=(1,), padding=[(k - 1, 0)],
        dimension_numbers=('NWC', 'WIO', 'NWC'), feature_group_count=x.shape[-1])
    return y + b.astype(x.dtype)


def compress(kv, pos, w1, w2):
    b, s, h, dh = kv.shape
    c = kv.reshape(b, s // CMP_STRIDE, CMP_STRIDE, h, dh)
    blocks = jnp.concatenate([c[:, :-1], c[:, 1:]], axis=2)
    blocks = blocks + pos[None, None, :, None, :].astype(kv.dtype)
    nc = blocks.shape[1]
    flat = blocks.transpose(0, 1, 3, 2, 4).reshape(b, nc, h, CMP_BLOCK * dh)
    return jax.nn.gelu(flat @ w1) @ w2


def nsa_attention(q, kc, vc, ks, vs, kw, vw, gates):
    b, h, g, s, dh = q.shape
    nc = kc.shape[2]
    n_slc = s // SLC_BLOCK
    n_sel = min(N_SEL, n_slc)
    ratio = SLC_BLOCK // CMP_STRIDE
    scale = dh ** -0.5
    ks_b = ks.reshape(b, h, n_slc, SLC_BLOCK, dh)
    vs_b = vs.reshape(b, h, n_slc, SLC_BLOCK, dh)
    kw_p = jnp.pad(kw, ((0, 0), (0, 0), (WINDOW, 0), (0, 0)))
    vw_p = jnp.pad(vw, ((0, 0), (0, 0), (WINDOW, 0), (0, 0)))
    cmp_end = jnp.arange(nc) * CMP_STRIDE + CMP_BLOCK - 1
    blk = jnp.arange(n_slc)
    gather = jax.vmap(jax.vmap(lambda kb, ix: kb[ix]))

    def block_fn(i):
        start = i * Q_BLOCK
        t = start + jnp.arange(Q_BLOCK)
        qb = lax.dynamic_slice_in_dim(q, start, Q_BLOCK, axis=3)
        gb = lax.dynamic_slice_in_dim(gates, start, Q_BLOCK, axis=3).astype(jnp.float32)

        s_c = jnp.einsum('bhgqd,bhnd->bhgqn', qb, kc).astype(jnp.float32) * scale
        m_c = cmp_end[None, :] <= t[:, None]
        p_c = jax.nn.softmax(jnp.where(m_c, s_c, NEG), axis=-1) * m_c
        o_c = jnp.einsum('bhgqn,bhnd->bhgqd', p_c.astype(vc.dtype), vc)

        imp = p_c.sum(axis=2)
        imp = jnp.pad(imp, ((0, 0), (0, 0), (0, 0), (1, ratio)))
        imp_slc = (imp[..., :ratio * n_slc].reshape(b, h, Q_BLOCK, n_slc, ratio).sum(-1)
                   + imp[..., ratio::ratio])
        cur = t // SLC_BLOCK
        valid = blk[None, :] * SLC_BLOCK <= t[:, None]
        forced = ((blk[None, :] == 0) | (blk[None, :] == cur[:, None])
                  | (blk[None, :] == cur[:, None] - 1))
        score = jnp.where(valid, jnp.where(forced, FORCED, imp_slc), NEG)
        _, idx = lax.top_k(score, n_sel)

        k_sel = gather(ks_b, idx)
        v_sel = gather(vs_b, idx)
        s_s = jnp.einsum('bhgqd,bhqnld->bhgqnl', qb, k_sel).astype(jnp.float32) * scale
        pos = idx[..., None] * SLC_BLOCK + jnp.arange(SLC_BLOCK)
        m_s = (pos <= t[:, None, None])[:, :, None]
        s_s = jnp.where(m_s, s_s, NEG).reshape(b, h, g, Q_BLOCK, n_sel * SLC_BLOCK)
        p_s = jax.nn.softmax(s_s, axis=-1).reshape(b, h, g, Q_BLOCK, n_sel, SLC_BLOCK)
        o_s = jnp.einsum('bhgqnl,bhqnld->bhgqd', p_s.astype(v_sel.dtype), v_sel)

        kwb = lax.dynamic_slice_in_dim(kw_p, start, WINDOW + Q_BLOCK, axis=2)
        vwb = lax.dynamic_slice_in_dim(vw_p, start, WINDOW + Q_BLOCK, axis=2)
        s_w = jnp.einsum('bhgqd,bhkd->bhgqk', qb, kwb).astype(jnp.float32) * scale
        kpos = start - WINDOW + jnp.arange(WINDOW + Q_BLOCK)
        m_w = ((kpos[None, :] <= t[:, None]) & (kpos[None, :] > t[:, None] - WINDOW)
               & (kpos[None, :] >= 0))
        p_w = jax.nn.softmax(jnp.where(m_w, s_w, NEG), axis=-1)
        o_w = jnp.einsum('bhgqk,bhkd->bhgqd', p_w.astype(vwb.dtype), vwb)

        o = gb[..., 0:1] * o_c + gb[..., 1:2] * o_s + gb[..., 2:3] * o_w
        return o.astype(q.dtype)

    out = lax.map(block_fn, jnp.arange(s // Q_BLOCK))
    return out.transpose(1, 0, 4, 2, 3, 5).reshape(b, s, h * g * dh)


def rglru(xb, gb, conv_w, conv_b, wa, ba, wx, bx, lam):
    b, s, w = xb.shape
    xr = causal_depthwise_conv(xb, conv_w, conv_b)
    xh = xr.reshape(b, s, LRU_HEADS, LRU_BW)
    r = jax.nn.sigmoid(jnp.einsum('bshi,hij->bshj', xh, wa).reshape(b, s, w).astype(jnp.float32)
                       + ba.astype(jnp.float32))
    ig = jax.nn.sigmoid(jnp.einsum('bshi,hij->bshj', xh, wx).reshape(b, s, w).astype(jnp.float32)
                        + bx.astype(jnp.float32))
    log_a = -LRU_C * r * jax.nn.softplus(-lam.astype(jnp.float32))
    a = jnp.exp(log_a)
    mult = jnp.sqrt(-jnp.expm1(2.0 * log_a))
    u = xr.astype(jnp.float32) * ig * mult

    def combine(c1, c2):
        a1, b1 = c1
        a2, b2 = c2
        return a1 * a2, a2 * b1 + b2

    _, hseq = lax.associative_scan(combine, (a, u), axis=1)
    return (hseq * jax.nn.gelu(gb.astype(jnp.float32))).astype(xb.dtype)


def conformer_conv(u, dw_w, dw_b, ln_g, ln_b):
    a, gt = jnp.split(u, 2, axis=-1)
    y = a * jax.nn.sigmoid(gt)
    y = causal_depthwise_conv(y, dw_w, dw_b)
    y = layer_norm(y, ln_g, ln_b)
    return jax.nn.silu(y)


def setup_inputs(seed: int = 0) -> dict:
    key = jax.random.key(seed)
    ks = jax.random.split(key, 32)
    f32 = jnp.float32
    nrm = lambda k, shape, scale: jax.random.normal(k, shape, f32) * scale
    u = jax.random.uniform(ks[20], (DEPTH, LRU_W), f32, minval=0.9, maxval=0.999)
    p = u ** (1.0 / LRU_C)
    return {
        "x": jax.random.normal(ks[0], (BATCH, SEQ, D_MODEL), f32),
        "attn_norm": 1.0 + nrm(ks[1], (DEPTH, D_MODEL), 0.02),
        "w_in": nrm(ks[2], (DEPTH, D_MODEL, N_IN), D_MODEL ** -0.5),
        "q_norm": 1.0 + nrm(ks[3], (DEPTH, NSA_DH), 0.02),
        "k_norm": 1.0 + nrm(ks[4], (DEPTH, 3, NSA_DH), 0.02),
        "cmp_pos": nrm(ks[5], (DEPTH, 2, CMP_BLOCK, NSA_DH), 0.1),
        "cmp_w1": nrm(ks[6], (DEPTH, 2, CMP_BLOCK * NSA_DH, CMP_HID), (CMP_BLOCK * NSA_DH) ** -0.5),
        "cmp_w2": nrm(ks[7], (DEPTH, 2, CMP_HID, NSA_DH), CMP_HID ** -0.5),
        "lru_conv_w": nrm(ks[8], (DEPTH, LRU_CONV, LRU_W), LRU_CONV ** -0.5),
        "lru_conv_b": nrm(ks[9], (DEPTH, LRU_W), 0.02),
        "lru_wa": nrm(ks[10], (DEPTH, LRU_HEADS, LRU_BW, LRU_BW), LRU_BW ** -0.5),
        "lru_ba": nrm(ks[11], (DEPTH, LRU_W), 0.02),
        "lru_wx": nrm(ks[12], (DEPTH, LRU_HEADS, LRU_BW, LRU_BW), LRU_BW ** -0.5),
        "lru_bx": nrm(ks[13], (DEPTH, LRU_W), 0.02),
        "lru_lambda": jnp.log(p) - jnp.log1p(-p),
        "cv_dw_w": nrm(ks[14], (DEPTH, CV_KERNEL, CV_W), CV_KERNEL ** -0.5),
        "cv_dw_b": nrm(ks[15], (DEPTH, CV_W), 0.02),
        "cv_ln_g": 1.0 + nrm(ks[16], (DEPTH, CV_W), 0.02),
        "cv_ln_b": nrm(ks[17], (DEPTH, CV_W), 0.02),
        "out_norm": 1.0 + nrm(ks[18], (DEPTH, D_MIX), 0.02),
        "w_out": nrm(ks[19], (DEPTH, D_MIX, D_MODEL), (2.0 * D_MIX) ** -0.5),
        "mlp_norm": 1.0 + nrm(ks[21], (DEPTH, D_MODEL), 0.02),
        "mlp_w1": nrm(ks[22], (DEPTH, D_MODEL, D_FF), D_MODEL ** -0.5),
        "mlp_w2": nrm(ks[23], (DEPTH, D_FF, D_MODEL), (2.0 * D_FF) ** -0.5),
    }


def reference(x, attn_norm, w_in, q_norm, k_norm, cmp_pos, cmp_w1, cmp_w2,
              lru_conv_w, lru_conv_b, lru_wa, lru_ba, lru_wx, lru_bx, lru_lambda,
              cv_dw_w, cv_dw_b, cv_ln_g, cv_ln_b, out_norm, w_out,
              mlp_norm, mlp_w1, mlp_w2):
    b, s, _ = x.shape
    for l in range(DEPTH):
        hn = rms_norm(x, attn_norm[l])
        z = hn @ w_in[l]

        q = rms_norm(z[..., OFF_Q:OFF_KV].reshape(b, s, NSA_HKV, NSA_G, NSA_DH), q_norm[l])
        q = q.transpose(0, 2, 3, 1, 4)
        kv = z[..., OFF_KV:OFF_GATE].reshape(b, s, 6, NSA_HKV, NSA_DH)
        kc = rms_norm(compress(kv[:, :, 0], cmp_pos[l, 0], cmp_w1[l, 0], cmp_w2[l, 0]), k_norm[l, 0])
        vc = compress(kv[:, :, 1], cmp_pos[l, 1], cmp_w1[l, 1], cmp_w2[l, 1])
        k_s = rms_norm(kv[:, :, 2], k_norm[l, 1])
        k_w = rms_norm(kv[:, :, 4], k_norm[l, 2])
        tr = lambda a: a.transpose(0, 2, 1, 3)
        gates = jax.nn.sigmoid(z[..., OFF_GATE:OFF_LRU_X].reshape(b, s, NSA_HKV, NSA_G, 3))
        gates = gates.transpose(0, 2, 3, 1, 4)
        y_attn = nsa_attention(q, tr(kc), tr(vc), tr(k_s), tr(kv[:, :, 3]),
                               tr(k_w), tr(kv[:, :, 5]), gates)

        y_lru = rglru(z[..., OFF_LRU_X:OFF_LRU_G], z[..., OFF_LRU_G:OFF_CV],
                      lru_conv_w[l], lru_conv_b[l], lru_wa[l], lru_ba[l],
                      lru_wx[l], lru_bx[l], lru_lambda[l])

        y_cv = conformer_conv(z[..., OFF_CV:N_IN], cv_dw_w[l], cv_dw_b[l],
                              cv_ln_g[l], cv_ln_b[l])

        g_out = out_norm[l]
        y = jnp.concatenate([
            rms_norm(y_attn, g_out[:NSA_W]),
            rms_norm(y_lru, g_out[NSA_W:NSA_W + LRU_W]),
            rms_norm(y_cv, g_out[NSA_W + LRU_W:]),
        ], axis=-1)
        x = x + y @ w_out[l]

        hm = rms_norm(x, mlp_norm[l])
        x = x + jnp.square(jax.nn.relu(hm @ mlp_w1[l])) @ mlp_w2[l]
    return x
```

```python
import functools

import numpy as np
import jax
import jax.numpy as jnp
from jax import lax
from jax.experimental import pallas as pl
from jax.experimental.pallas import tpu as pltpu

F32 = jnp.float32
BF16 = jnp.bfloat16

EPS = 1e-6
NEG = -1e30
FORCED = 1e6
DH = 64
HKV = 2
G = 4
NSA_W = HKV * G * DH
CMP_BLOCK = 32
CMP_STRIDE = 16
CMP_HID = 128
SLC_BLOCK = 64
N_SEL = 16
WINDOW = 512
QB = 128
LRU_W = 256
LRU_HEADS = 8
LRU_CONV = 4
LRU_C = 8.0
CV_W = 256
CV_KERNEL = 31

LANES = 128
KCH = 256
TM = 512
TS = 512
VMEM_LIMIT = 48 * 1024 * 1024

C_Q = 0
C_CMP = C_Q + NSA_W
C_KSLC = C_CMP + 256
C_KWIN = C_KSLC + 128
C_VSLC = C_KWIN + 128
C_VWIN = C_VSLC + 128
C_LRU = C_VWIN + 128
C_CV = C_LRU + 2 * LRU_W
C_GATE = C_CV + 2 * CV_W
N_INP = C_GATE + LANES
GATE_ROWS = 32


def _dot(a, b):
    return jnp.dot(a, b, preferred_element_type=F32)


def _dot_nt(a, b):
    return lax.dot_general(a, b, (((1,), (1,)), ((), ())), preferred_element_type=F32)


def _head_rms_scale(z, bd):
    sq = z * z
    hi = sq.astype(BF16)
    lo = (sq - hi.astype(F32)).astype(BF16)
    ms = _dot(hi, bd) + _dot(lo, bd)
    return lax.rsqrt(ms + EPS)


def _gelu_tanh(x):
    return 0.5 * x * (1.0 + jnp.tanh(0.7978845608028654 * (x + 0.044715 * (x * x * x))))


def _sigmoid(x):
    return 1.0 / (1.0 + jnp.exp(-x))


def _in_proj_kernel(x_ref, g_ref, w_ref, bd_ref, qgain_ref, kgain_ref,
                    q_ref, cmp_ref, kslc_ref, kwin_ref, vslcT_ref, vwinT_ref,
                    lru_ref, cv_ref, gT_ref):
    x = x_ref[...]
    ms = jnp.mean(x * x, axis=-1, keepdims=True)
    hn = (x * lax.rsqrt(ms + EPS) * g_ref[...]).astype(BF16)

    zq = _dot(hn, w_ref[:, C_Q:C_CMP])
    q_ref[...] = (zq * _head_rms_scale(zq, bd_ref[...]) * qgain_ref[...]).astype(BF16)

    cmp_ref[...] = _dot(hn, w_ref[:, C_CMP:C_KSLC]).astype(BF16)

    zk = _dot(hn, w_ref[:, C_KSLC:C_VSLC])
    kn = zk * _head_rms_scale(zk, bd_ref[0:256, 0:256]) * kgain_ref[...]
    kslc_ref[...] = kn[:, 0:128].astype(BF16)
    kwin_ref[...] = kn[:, 128:256].astype(BF16)

    zv = _dot(hn, w_ref[:, C_VSLC:C_LRU])
    zvT = zv.T
    for j in range(TM // LANES):
        for h in range(HKV):
            vslcT_ref[h, j] = zvT[h * DH:(h + 1) * DH, j * LANES:(j + 1) * LANES].astype(BF16)
            vwinT_ref[h, j] = zvT[128 + h * DH:128 + (h + 1) * DH,
                                  j * LANES:(j + 1) * LANES].astype(BF16)

    lru_ref[...] = _dot(hn, w_ref[:, C_LRU:C_CV])
    cv_ref[...] = _dot(hn, w_ref[:, C_CV:C_GATE])

    zg = _sigmoid(_dot(hn, w_ref[:, C_GATE:N_INP]))
    gT_ref[...] = zg.T[0:GATE_ROWS, :]


def _in_proj(x2, g, w, bd, qgain, kgain, B, S):
    T, D = x2.shape
    nt = S // TM
    tok = lambda w_: pl.BlockSpec((TM, w_), lambda t: (t, 0))
    const = lambda a: pl.BlockSpec(a.shape, lambda t: (0,) * a.ndim)
    vT_spec = pl.BlockSpec((None, HKV, TM // LANES, DH, LANES),
                           lambda t: (t // nt, 0, t % nt, 0, 0))
    out_shape = (
        jax.ShapeDtypeStruct((T, NSA_W), BF16),
        jax.ShapeDtypeStruct((T, 256), BF16),
        jax.ShapeDtypeStruct((T, 128), BF16),
        jax.ShapeDtypeStruct((T, 128), BF16),
        jax.ShapeDtypeStruct((B, HKV, S // LANES, DH, LANES), BF16),
        jax.ShapeDtypeStruct((B, HKV, S // LANES, DH, LANES), BF16),
        jax.ShapeDtypeStruct((T, 2 * LRU_W), F32),
        jax.ShapeDtypeStruct((T, 2 * CV_W), F32),
        jax.ShapeDtypeStruct((B, GATE_ROWS, S), F32),
    )
    out_specs = (
        tok(NSA_W), tok(256), tok(128), tok(128), vT_spec, vT_spec,
        tok(2 * LRU_W), tok(2 * CV_W),
        pl.BlockSpec((None, GATE_ROWS, TM), lambda t: (t // nt, 0, t % nt)),
    )
    return pl.pallas_call(
        _in_proj_kernel,
        out_shape=out_shape,
        grid=(T // TM,),
        in_specs=[tok(D), const(g), const(w), const(bd), const(qgain), const(kgain)],
        out_specs=out_specs,
        compiler_params=pltpu.CompilerParams(
            dimension_semantics=("arbitrary",), vmem_limit_bytes=VMEM_LIMIT),
        name="in_proj",
    )(x2, g, w, bd, qgain, kgain)


def _compress_kernel(x_ref, pos_ref, wbig_ref, w2_ref, bd_ref, kgain_ref, kc_ref, vcT_ref):
    nc = x_ref.shape[0]
    half = 4 * CMP_HID
    p = _dot(x_ref[...], wbig_ref[...])
    pc = _dot(pos_ref[...], wbig_ref[...])
    const = pc[0:1, 0:half] + pc[1:2, half:2 * half]
    nxt = pltpu.roll(p[:, half:2 * half], nc - 1, 0)
    hid = _gelu_tanh(p[:, 0:half] + nxt + const).astype(BF16)
    kv = _dot(hid, w2_ref[...])
    kc = kv[:, 0:128]
    kc_ref[...] = (kc * _head_rms_scale(kc, bd_ref[...]) * kgain_ref[...]).astype(BF16)
    vT = kv[:, 128:256].T
    for h in range(HKV):
        vcT_ref[h] = vT[h * DH:(h + 1) * DH, :].astype(BF16)


def _compress(cmpx, pos2, wbig, w2big, bd128, kgain, B, NC):
    const = lambda a: pl.BlockSpec(a.shape, lambda b: (0,) * a.ndim, pipeline_mode=pl.Buffered(1))
    return pl.pallas_call(
        _compress_kernel,
        out_shape=(jax.ShapeDtypeStruct((B, NC, 128), BF16),
                   jax.ShapeDtypeStruct((B, HKV, DH, NC), BF16)),
        grid=(B,),
        in_specs=[pl.BlockSpec((None, NC, cmpx.shape[2]), lambda b: (b, 0, 0)),
                  const(pos2), const(wbig), const(w2big), const(bd128), const(kgain)],
        out_specs=(pl.BlockSpec((None, NC, 128), lambda b: (b, 0, 0)),
                   pl.BlockSpec((None, HKV, DH, NC), lambda b: (b, 0, 0, 0))),
        compiler_params=pltpu.CompilerParams(
            dimension_semantics=("arbitrary",), vmem_limit_bytes=VMEM_LIMIT),
        name="compress",
    )(cmpx, pos2, wbig, w2big, bd128, kgain)


def _attn_kernel(q_ref, kc_ref, vcT_ref, ks_ref, vsT_ref, kw_ref, vwT_ref, gT_ref,
                 y_ref, imp_ref, selb_ref):
    h = pl.program_id(1)
    i = pl.program_id(2)
    nc = kc_ref.shape[0]
    ns = selb_ref.shape[0]
    wspan = WINDOW + QB

    qf = q_ref[...].astype(F32)
    lane = lax.broadcasted_iota(jnp.int32, (QB, LANES), 1)
    lo = lane < DH
    own = (lane >= DH).astype(jnp.int32) == h
    qa, qb = qf[:, 0:128], qf[:, 128:256]
    qar, qbr = pltpu.roll(qa, DH, 1), pltpu.roll(qb, DH, 1)
    dup = (jnp.where(lo, qa, qar), jnp.where(lo, qar, qa),
           jnp.where(lo, qb, qbr), jnp.where(lo, qbr, qb))
    qg = [jnp.where(own, d, 0.0).astype(BF16) for d in dup]
    qpad = jnp.concatenate(qg, axis=0)

    t0 = i * QB

    n_idx = lax.broadcasted_iota(jnp.int32, (nc, LANES), 0)
    t_c = t0 + lax.broadcasted_iota(jnp.int32, (nc, LANES), 1)
    cmask = n_idx * CMP_STRIDE + (CMP_BLOCK - 1) <= t_c
    kc = kc_ref[...]
    vcT = vcT_ref[...]
    imp = jnp.zeros((nc, LANES), F32)
    oc = []
    for g in range(G):
        s = jnp.where(cmask, _dot_nt(kc, qg[g]), NEG)
        m = jnp.max(s, axis=0, keepdims=True)
        p = jnp.where(cmask, jnp.exp(s - m), 0.0)
        l = jnp.sum(p, axis=0, keepdims=True)
        pn = p * jnp.where(l > 0.0, 1.0 / l, 0.0)
        imp = imp + pn
        oc.append(_dot(vcT, pn.astype(BF16)))

    imp_ref[0:8, :] = jnp.zeros((8, LANES), F32)
    imp_ref[8:8 + nc, :] = imp
    islc = imp_ref[pl.ds(7, ns, stride=4), :]
    for r in range(4):
        islc = islc + imp_ref[pl.ds(8 + r, ns, stride=4), :]

    j_i = lax.broadcasted_iota(jnp.int32, (ns, LANES), 0)
    t_s = t0 + lax.broadcasted_iota(jnp.int32, (ns, LANES), 1)
    cur = lax.shift_right_logical(t_s, 6)
    valid = j_i * SLC_BLOCK <= t_s
    forced = (j_i == 0) | (j_i == cur) | (j_i == cur - 1)
    score = jnp.where(valid, jnp.where(forced, FORCED, islc), NEG)
    j_f = j_i.astype(F32)
    selb = jnp.full((ns, LANES), NEG, F32)
    for _ in range(N_SEL):
        mx = jnp.max(score, axis=0, keepdims=True)
        jm = jnp.min(jnp.where(score == mx, j_f, float(ns)), axis=0, keepdims=True)
        hit = j_f == jm
        selb = jnp.where(hit, 0.0, selb)
        score = jnp.where(hit, -jnp.inf, score)
    selb_ref[...] = selb

    t_k = t0 + lax.broadcasted_iota(jnp.int32, (KCH, LANES), 1)
    r_k = lax.broadcasted_iota(jnp.int32, (KCH, LANES), 0)
    nblk = KCH // SLC_BLOCK

    def sel_chunk(c, carry, causal):
        m, l, acc = carry
        k = ks_ref[pl.ds(pl.multiple_of(c * KCH, KCH), KCH), :]
        s = _dot_nt(k, qpad)
        bias = jnp.concatenate(
            [jnp.broadcast_to(selb_ref[pl.ds(c * nblk + b, 1), :], (SLC_BLOCK, LANES))
             for b in range(nblk)], axis=0)
        if causal:
            bias = jnp.where(c * KCH + r_k <= t_k, bias, NEG)
        s = s + jnp.concatenate([bias] * G, axis=1)
        m_new = jnp.maximum(m, jnp.max(s, axis=0, keepdims=True))
        alpha = jnp.exp(m - m_new)
        p = jnp.exp(s - m_new)
        l = alpha * l + jnp.sum(p, axis=0, keepdims=True)
        v = jnp.concatenate([vsT_ref[c * (KCH // LANES) + u] for u in range(KCH // LANES)], axis=1)
        acc = alpha * acc + _dot(v, p.astype(BF16))
        return m_new, l, acc

    init = (jnp.full((1, G * QB), NEG, F32), jnp.zeros((1, G * QB), F32),
            jnp.zeros((DH, G * QB), F32))
    diag = i // (KCH // QB)
    carry = lax.fori_loop(0, diag, lambda c, cr: sel_chunk(c, cr, False), init)
    _, l_s, acc_s = sel_chunk(diag, carry, True)
    os_ = acc_s * (1.0 / l_s)

    wc = jnp.maximum(i - WINDOW // QB, 0)
    ws = wc * QB
    kw = kw_ref[pl.ds(pl.multiple_of(ws, QB), wspan), :]
    sw = _dot_nt(kw, qpad)
    kpos = ws + lax.broadcasted_iota(jnp.int32, (wspan, LANES), 0)
    t_w = t0 + lax.broadcasted_iota(jnp.int32, (wspan, LANES), 1)
    wmask = (kpos <= t_w) & (kpos > t_w - WINDOW)
    wbias = jnp.where(wmask, 0.0, NEG)
    sw = sw + jnp.concatenate([wbias] * G, axis=1)
    m_w = jnp.max(sw, axis=0, keepdims=True)
    p_w = jnp.exp(sw - m_w)
    l_w = jnp.sum(p_w, axis=0, keepdims=True)
    vw = jnp.concatenate([vwT_ref[wc + u] for u in range(wspan // LANES)], axis=1)
    ow = _dot(vw, p_w.astype(BF16)) * (1.0 / l_w)

    outs = []
    for g in range(G):
        sl = slice(g * QB, (g + 1) * QB)
        row = h * (3 * G) + 3 * g
        gc = gT_ref[pl.ds(row, 1), :]
        gs = gT_ref[pl.ds(row + 1, 1), :]
        gw = gT_ref[pl.ds(row + 2, 1), :]
        outs.append(gc * oc[g] + gs * os_[:, sl] + gw * ow[:, sl])
    y_ref[...] = jnp.concatenate(outs, axis=0).T


def _attention(q, kc, vcT, kslc, vslcT, kwin, vwinT, gT, B, S):
    NC = S // CMP_STRIDE
    NS = S // SLC_BLOCK
    NQ = S // QB
    return pl.pallas_call(
        _attn_kernel,
        out_shape=jax.ShapeDtypeStruct((B, S, NSA_W), F32),
        grid=(B, HKV, NQ),
        in_specs=[
            pl.BlockSpec((None, QB, G * DH), lambda b, h, i: (b, i, h)),
            pl.BlockSpec((None, NC, 128), lambda b, h, i: (b, 0, 0)),
            pl.BlockSpec((None, None, DH, NC), lambda b, h, i: (b, h, 0, 0)),
            pl.BlockSpec((None, S, 128), lambda b, h, i: (b, 0, 0)),
            pl.BlockSpec((None, None, S // LANES, DH, LANES), lambda b, h, i: (b, h, 0, 0, 0)),
            pl.BlockSpec((None, S, 128), lambda b, h, i: (b, 0, 0)),
            pl.BlockSpec((None, None, S // LANES, DH, LANES), lambda b, h, i: (b, h, 0, 0, 0)),
            pl.BlockSpec((None, GATE_ROWS, QB), lambda b, h, i: (b, 0, i)),
        ],
        out_specs=pl.BlockSpec((None, QB, G * DH), lambda b, h, i: (b, i, h)),
        scratch_shapes=[pltpu.VMEM((NC + 8, LANES), F32), pltpu.VMEM((NS, LANES), F32)],
        compiler_params=pltpu.CompilerParams(
            dimension_semantics=("arbitrary", "arbitrary", "arbitrary"),
            vmem_limit_bytes=VMEM_LIMIT),
        name="attention",
    )(q, kc, vcT, kslc, vslcT, kwin, vwinT, gT)


XH = 8
YH = 32


def _seq_mix_kernel(lru_ref, cv_ref, cw_ref, cb_ref, wa_ref, ba_ref, wx_ref, bx_ref, lam_ref,
                    dw_ref, db_ref, lng_ref, lnb_ref, gl_ref, gc_ref,
                    yl_ref, yc_ref, xbuf, ybuf, hbuf):
    t = pl.program_id(1)

    @pl.when(t == 0)
    def _():
        xbuf[0:XH, :] = jnp.zeros((XH, LRU_W), F32)
        ybuf[0:YH, :] = jnp.zeros((YH, CV_W), F32)
        hbuf[...] = jnp.zeros((8, LRU_W), F32)

    xb = lru_ref[:, 0:LRU_W]
    gb = lru_ref[:, LRU_W:2 * LRU_W]
    xbuf[XH:XH + TS, :] = xb
    xr = jnp.broadcast_to(cb_ref[...], (TS, LRU_W))
    for k in range(LRU_CONV):
        xr = xr + cw_ref[k:k + 1, :] * xbuf[XH - (LRU_CONV - 1) + k:XH - (LRU_CONV - 1) + k + TS, :]
    xbuf[0:XH, :] = xbuf[TS:TS + XH, :]

    xr16 = xr.astype(BF16)
    r = _sigmoid(_dot(xr16, wa_ref[...]) + ba_ref[...])
    ig = _sigmoid(_dot(xr16, wx_ref[...]) + bx_ref[...])
    nl = -lam_ref[...]
    softplus = jnp.maximum(nl, 0.0) + jnp.log1p(jnp.exp(-jnp.abs(nl)))
    log_a = -LRU_C * r * softplus
    a = jnp.exp(log_a)
    u = xr * ig * jnp.sqrt(-jnp.tanh(log_a) * (a * a + 1.0))

    row = lax.broadcasted_iota(jnp.int32, (TS, LRU_W), 0)
    d = 1
    while d < TS:
        keep = row >= d
        a_sh = jnp.where(keep, pltpu.roll(a, d, 0), 1.0)
        u_sh = jnp.where(keep, pltpu.roll(u, d, 0), 0.0)
        u = a * u_sh + u
        a = a * a_sh
        d *= 2
    hseq = a * hbuf[0:1, :] + u
    hbuf[...] = jnp.broadcast_to(hseq[TS - 1:TS, :], (8, LRU_W))
    yl = hseq * _gelu_tanh(gb)
    yl = yl * lax.rsqrt(jnp.mean(yl * yl, axis=-1, keepdims=True) + EPS) * gl_ref[...]
    yl_ref[...] = yl.astype(BF16)

    y = cv_ref[:, 0:CV_W] * _sigmoid(cv_ref[:, CV_W:2 * CV_W])
    ybuf[YH:YH + TS, :] = y
    c = jnp.broadcast_to(db_ref[...], (TS, CV_W))
    for k in range(CV_KERNEL):
        off = YH - (CV_KERNEL - 1) + k
        c = c + dw_ref[k:k + 1, :] * ybuf[off:off + TS, :]
    ybuf[0:YH, :] = ybuf[TS:TS + YH, :]
    mu = jnp.mean(c, axis=-1, keepdims=True)
    cc = c - mu
    var = jnp.mean(cc * cc, axis=-1, keepdims=True)
    ln = cc * lax.rsqrt(var + EPS) * lng_ref[...] + lnb_ref[...]
    yc = ln * _sigmoid(ln)
    yc = yc * lax.rsqrt(jnp.mean(yc * yc, axis=-1, keepdims=True) + EPS) * gc_ref[...]
    yc_ref[...] = yc.astype(BF16)


def _seq_mix(lru, cv, params, B, S):
    const = lambda a: pl.BlockSpec(a.shape, lambda b, t: (0,) * a.ndim)
    tile = lambda w_: pl.BlockSpec((None, TS, w_), lambda b, t: (b, t, 0))
    return pl.pallas_call(
        _seq_mix_kernel,
        out_shape=(jax.ShapeDtypeStruct((B, S, LRU_W), BF16),
                   jax.ShapeDtypeStruct((B, S, CV_W), BF16)),
        grid=(B, S // TS),
        in_specs=[tile(2 * LRU_W), tile(2 * CV_W)] + [const(p) for p in params],
        out_specs=(tile(LRU_W), tile(CV_W)),
        scratch_shapes=[pltpu.VMEM((TS + XH, LRU_W), F32), pltpu.VMEM((TS + YH, CV_W), F32),
                        pltpu.VMEM((8, LRU_W), F32)],
        compiler_params=pltpu.CompilerParams(
            dimension_semantics=("arbitrary", "arbitrary"), vmem_limit_bytes=VMEM_LIMIT),
        name="seq_mix",
    )(lru, cv, *params)


def _out_mlp_kernel(x_ref, ya_ref, yl_ref, yc_ref, ga_ref, wo_ref, gm_ref, w1_ref, w2_ref, o_ref):
    ya = ya_ref[...]
    ya = (ya * lax.rsqrt(jnp.mean(ya * ya, axis=-1, keepdims=True) + EPS) * ga_ref[...]).astype(BF16)
    x1 = (x_ref[...] + _dot(ya, wo_ref[0:NSA_W, :])
          + _dot(yl_ref[...], wo_ref[NSA_W:NSA_W + LRU_W, :])
          + _dot(yc_ref[...], wo_ref[NSA_W + LRU_W:NSA_W + LRU_W + CV_W, :]))
    hm = (x1 * lax.rsqrt(jnp.mean(x1 * x1, axis=-1, keepdims=True) + EPS) * gm_ref[...]).astype(BF16)
    d_ff = w1_ref.shape[1]
    fc = 1024
    o_ref[...] = x1
    for c in range(d_ff // fc):
        hc = jnp.maximum(_dot(hm, w1_ref[:, c * fc:(c + 1) * fc]), 0.0)
        o_ref[...] += _dot((hc * hc).astype(BF16), w2_ref[c * fc:(c + 1) * fc, :])


def _out_mlp(x2, ya, yl, yc, ga, wo, gm, w1, w2):
    T, D = x2.shape
    tok = lambda w_: pl.BlockSpec((TM, w_), lambda t: (t, 0))
    const1 = lambda a: pl.BlockSpec(a.shape, lambda t: (0,) * a.ndim, pipeline_mode=pl.Buffered(1))
    return pl.pallas_call(
        _out_mlp_kernel,
        out_shape=jax.ShapeDtypeStruct((T, D), F32),
        grid=(T // TM,),
        in_specs=[tok(D), tok(NSA_W), tok(LRU_W), tok(CV_W),
                  const1(ga), const1(wo), const1(gm), const1(w1), const1(w2)],
        out_specs=tok(D),
        compiler_params=pltpu.CompilerParams(
            dimension_semantics=("arbitrary",), vmem_limit_bytes=VMEM_LIMIT),
        name="out_mlp",
    )(x2, ya, yl, yc, ga, wo, gm, w1, w2)


def _block_ones(n):
    idx = np.arange(n) // DH
    return jnp.asarray((idx[:, None] == idx[None, :]).astype(np.float32) / DH, dtype=BF16)


def _permute_w_in(w):
    d = w.shape[0]
    off_kv = NSA_W
    kv = [w[:, off_kv + c * 128: off_kv + (c + 1) * 128] for c in range(6)]
    off_gate = off_kv + 6 * 128
    ngate = 3 * HKV * G
    off_lru = off_gate + ngate
    gate = jnp.pad(w[:, off_gate:off_lru], ((0, 0), (0, LANES - ngate)))
    cols = [w[:, 0:NSA_W], kv[0], kv[1], kv[2], kv[4], kv[3], kv[5],
            w[:, off_lru:off_lru + 2 * LRU_W + 2 * CV_W], gate]
    return jnp.concatenate(cols, axis=1).astype(BF16)


def _compress_weights(pos, w1, w2):
    slot_kv = jnp.array([0, 0, 1, 1])
    eye = jnp.eye(4, dtype=F32)
    w1r = w1.reshape(2, 2, CMP_STRIDE, DH, CMP_HID)[slot_kv]
    wbig = jnp.einsum('shpdo,st->psdhto', w1r, eye).reshape(CMP_STRIDE * 4 * DH, 2 * 4 * CMP_HID)
    posr = pos.reshape(2, 2, CMP_STRIDE, DH)[slot_kv]
    pos2 = jnp.transpose(posr, (1, 2, 0, 3)).reshape(2, CMP_STRIDE * 4 * DH)
    pos2 = jnp.pad(pos2, ((0, 6), (0, 0)))
    w2big = jnp.einsum('sod,st->sotd', w2[slot_kv], eye).reshape(4 * CMP_HID, 4 * DH)
    return pos2.astype(BF16), wbig.astype(BF16), w2big.astype(BF16)


def _block_diag(w):
    hh, bw, _ = w.shape
    eye = jnp.eye(hh, dtype=w.dtype)
    return jnp.einsum('hij,hg->higj', w, eye).reshape(hh * bw, hh * bw).astype(BF16)


def kernel(x, attn_norm, w_in, q_norm, k_norm, cmp_pos, cmp_w1, cmp_w2, lru_conv_w, lru_conv_b,
           lru_wa, lru_ba, lru_wx, lru_bx, lru_lambda, cv_dw_w, cv_dw_b, cv_ln_g, cv_ln_b,
           out_norm, w_out, mlp_norm, mlp_w1, mlp_w2):
    B, S, D = x.shape
    depth = w_in.shape[0]
    assert S % TM == 0 and S % TS == 0 and S >= WINDOW + QB and S // SLC_BLOCK >= N_SEL
    NC = S // CMP_STRIDE
    row = lambda v: v.reshape(1, -1).astype(F32)
    bd512 = _block_ones(NSA_W)
    bd128 = _block_ones(128)

    x2 = x.reshape(B * S, D)
    for l in range(depth):
        qgain = row(jnp.tile(q_norm[l], HKV * G)) * (DH ** -0.5)
        kgain = row(jnp.concatenate([jnp.tile(k_norm[l, 1], HKV), jnp.tile(k_norm[l, 2], HKV)]))
        q, cmpr, kslc, kwin, vslcT, vwinT, lru, cv, gT = _in_proj(
            x2, row(attn_norm[l]), _permute_w_in(w_in[l]), bd512, qgain, kgain, B, S)

        pos2, wbig, w2big = _compress_weights(cmp_pos[l], cmp_w1[l], cmp_w2[l])
        kc, vcT = _compress(cmpr.reshape(B, NC, CMP_STRIDE * 256), pos2, wbig, w2big, bd128,
                            row(jnp.tile(k_norm[l, 0], HKV)), B, NC)

        y_attn = _attention(q.reshape(B, S, NSA_W), kc, vcT, kslc.reshape(B, S, 128), vslcT,
                            kwin.reshape(B, S, 128), vwinT, gT, B, S)

        g_out = out_norm[l]
        seq_params = (lru_conv_w[l], row(lru_conv_b[l]), _block_diag(lru_wa[l]), row(lru_ba[l]),
                      _block_diag(lru_wx[l]), row(lru_bx[l]), row(lru_lambda[l]),
                      cv_dw_w[l], row(cv_dw_b[l]), row(cv_ln_g[l]), row(cv_ln_b[l]),
                      row(g_out[NSA_W:NSA_W + LRU_W]), row(g_out[NSA_W + LRU_W:]))
        yl, yc = _seq_mix(lru.reshape(B, S, 2 * LRU_W), cv.reshape(B, S, 2 * CV_W), seq_params, B, S)

        x2 = _out_mlp(x2, y_attn.reshape(B * S, NSA_W), yl.reshape(B * S, LRU_W),
                      yc.reshape(B * S, CV_W), row(g_out[:NSA_W]), w_out[l].astype(BF16),
                      row(mlp_norm[l]), mlp_w1[l].astype(BF16), mlp_w2[l].astype(BF16))
    return x2.reshape(B, S, D)
```

```python
import functools

import numpy as np
import jax
import jax.numpy as jnp
from jax import lax
from jax.experimental import pallas as pl
from jax.experimental.pallas import tpu as pltpu

F32 = jnp.float32
BF16 = jnp.bfloat16

EPS = 1e-6
NEG = -1e30
FORCED = 1e6
DH = 64
HKV = 2
G = 4
NSA_W = HKV * G * DH
CMP_BLOCK = 32
CMP_STRIDE = 16
CMP_HID = 128
SLC_BLOCK = 64
N_SEL = 16
WINDOW = 512
QB = 128
LRU_W = 256
LRU_HEADS = 8
LRU_CONV = 4
LRU_C = 8.0
CV_W = 256
CV_KERNEL = 31
LOG2E = 1.4426950408889634

LANES = 128
KCH = 256
NCHUNK = 4
CCH = 256
TM = 512
TS = 512
VMEM_LIMIT = 48 * 1024 * 1024

C_Q = 0
C_CMP = C_Q + NSA_W
C_KSLC = C_CMP + 256
C_KWIN = C_KSLC + 128
C_VSLC = C_KWIN + 128
C_VWIN = C_VSLC + 128
C_LRU = C_VWIN + 128
C_CV = C_LRU + 2 * LRU_W
C_GATE = C_CV + 2 * CV_W
N_INP = C_GATE + LANES
GATE_ROWS = 32
VROWS = DH + 16
BIAS_BLOCKS = LANES - DH


def _dot(a, b):
    return jnp.dot(a, b, preferred_element_type=F32)


def _head_rms_scale(z, bd):
    sq = z * z
    hi = sq.astype(BF16)
    lo = (sq - hi.astype(F32)).astype(BF16)
    ms = _dot(hi, bd) + _dot(lo, bd)
    return lax.rsqrt(ms + EPS)


def _gelu_tanh(x):
    return 0.5 * x * (1.0 + jnp.tanh(0.7978845608028654 * (x + 0.044715 * (x * x * x))))


def _sigmoid(x):
    return 1.0 / (1.0 + jnp.exp(-x))


def _in_proj_kernel(x_ref, g_ref, w_ref, bd_ref, qgain_ref, kgain_ref,
                    q_ref, cmp_ref, kslc_ref, kwin_ref, vslcT_ref, vwinT_ref,
                    lru_ref, cv_ref, gT_ref, *, nt):
    x = x_ref[...]
    ms = jnp.mean(x * x, axis=-1, keepdims=True)
    hn = (x * lax.rsqrt(ms + EPS) * g_ref[...]).astype(BF16)

    zq = _dot(hn, w_ref[:, C_Q:C_CMP])
    q_ref[...] = (zq * _head_rms_scale(zq, bd_ref[...]) * qgain_ref[...]).astype(BF16)

    cmp_ref[...] = _dot(hn, w_ref[:, C_CMP:C_KSLC]).astype(BF16)

    zk = _dot(hn, w_ref[:, C_KSLC:C_VSLC])
    kn = zk * _head_rms_scale(zk, bd_ref[0:256, 0:256]) * kgain_ref[...]
    lane = lax.broadcasted_iota(jnp.int32, (TM, LANES), 1)
    rowg = (pl.program_id(0) % nt) * TM + lax.broadcasted_iota(jnp.int32, (TM, LANES), 0)
    blk = lax.shift_right_logical(rowg, 6) & (BIAS_BLOCKS - 1)
    onehot = jnp.where(lane == DH + blk, 1.0, 0.0)
    lo = lane < DH
    ks, kw = kn[:, 0:128], kn[:, 128:256]
    kslc_ref[0] = jnp.where(lo, ks, onehot).astype(BF16)
    kslc_ref[1] = jnp.where(lo, pltpu.roll(ks, DH, 1), onehot).astype(BF16)
    kwin_ref[0] = jnp.where(lo, kw, 0.0).astype(BF16)
    kwin_ref[1] = jnp.where(lo, pltpu.roll(kw, DH, 1), 0.0).astype(BF16)

    zv = _dot(hn, w_ref[:, C_VSLC:C_LRU])
    zvT = zv.T
    ones_row = jnp.where(lax.broadcasted_iota(jnp.int32, (VROWS - DH, LANES), 0) == 0,
                         1.0, 0.0).astype(BF16)
    for j in range(TM // LANES):
        for h in range(HKV):
            cols = slice(j * LANES, (j + 1) * LANES)
            vslcT_ref[h, j, 0:DH, :] = zvT[h * DH:(h + 1) * DH, cols].astype(BF16)
            vslcT_ref[h, j, DH:VROWS, :] = ones_row
            vwinT_ref[h, j, 0:DH, :] = zvT[128 + h * DH:128 + (h + 1) * DH, cols].astype(BF16)
            vwinT_ref[h, j, DH:VROWS, :] = ones_row

    lru_ref[...] = _dot(hn, w_ref[:, C_LRU:C_CV])
    cv_ref[...] = _dot(hn, w_ref[:, C_CV:C_GATE])

    zg = _sigmoid(_dot(hn, w_ref[:, C_GATE:N_INP]))
    gT_ref[...] = zg.T[0:GATE_ROWS, :]


def _in_proj(x2, g, w, bd, qgain, kgain, B, S):
    T, D = x2.shape
    nt = S // TM
    tok = lambda w_: pl.BlockSpec((TM, w_), lambda t: (t, 0))
    const = lambda a: pl.BlockSpec(a.shape, lambda t: (0,) * a.ndim)
    vT_spec = pl.BlockSpec((None, HKV, TM // LANES, VROWS, LANES),
                           lambda t: (t // nt, 0, t % nt, 0, 0))
    k_spec = pl.BlockSpec((None, HKV, TM, LANES), lambda t: (t // nt, 0, t % nt, 0))
    out_shape = (
        jax.ShapeDtypeStruct((T, NSA_W), BF16),
        jax.ShapeDtypeStruct((T, 256), BF16),
        jax.ShapeDtypeStruct((B, HKV, S, LANES), BF16),
        jax.ShapeDtypeStruct((B, HKV, S, LANES), BF16),
        jax.ShapeDtypeStruct((B, HKV, S // LANES, VROWS, LANES), BF16),
        jax.ShapeDtypeStruct((B, HKV, S // LANES, VROWS, LANES), BF16),
        jax.ShapeDtypeStruct((T, 2 * LRU_W), F32),
        jax.ShapeDtypeStruct((T, 2 * CV_W), F32),
        jax.ShapeDtypeStruct((B, GATE_ROWS, S), F32),
    )
    out_specs = (
        tok(NSA_W), tok(256), k_spec, k_spec, vT_spec, vT_spec,
        tok(2 * LRU_W), tok(2 * CV_W),
        pl.BlockSpec((None, GATE_ROWS, TM), lambda t: (t // nt, 0, t % nt)),
    )
    return pl.pallas_call(
        functools.partial(_in_proj_kernel, nt=nt),
        out_shape=out_shape,
        grid=(T // TM,),
        in_specs=[tok(D), const(g), const(w), const(bd), const(qgain), const(kgain)],
        out_specs=out_specs,
        compiler_params=pltpu.CompilerParams(
            dimension_semantics=("arbitrary",), vmem_limit_bytes=VMEM_LIMIT),
        name="in_proj",
    )(x2, g, w, bd, qgain, kgain)


def _compress_kernel(x_ref, pos_ref, wbig_ref, w2_ref, bd_ref, kgain_ref, kc_ref, vcT_ref):
    nc = x_ref.shape[0]
    half = 4 * CMP_HID
    p = _dot(x_ref[...], wbig_ref[...])
    pc = _dot(pos_ref[...], wbig_ref[...])
    const = pc[0:1, 0:half] + pc[1:2, half:2 * half]
    nxt = pltpu.roll(p[:, half:2 * half], nc - 1, 0)
    hid = _gelu_tanh(p[:, 0:half] + nxt + const).astype(BF16)
    kv = _dot(hid, w2_ref[...])
    kc = kv[:, 0:128]
    kc = kc * _head_rms_scale(kc, bd_ref[...]) * kgain_ref[...]
    lo = lax.broadcasted_iota(jnp.int32, (nc, LANES), 1) < DH
    kc_ref[0] = jnp.where(lo, kc, 0.0).astype(BF16)
    kc_ref[1] = jnp.where(lo, pltpu.roll(kc, DH, 1), 0.0).astype(BF16)
    vT = kv[:, 128:256].T
    for h in range(HKV):
        vcT_ref[h] = vT[h * DH:(h + 1) * DH, :].astype(BF16)


def _compress(cmpx, pos2, wbig, w2big, bd128, kgain, B, NC):
    const = lambda a: pl.BlockSpec(a.shape, lambda b: (0,) * a.ndim, pipeline_mode=pl.Buffered(1))
    return pl.pallas_call(
        _compress_kernel,
        out_shape=(jax.ShapeDtypeStruct((B, HKV, NC, LANES), BF16),
                   jax.ShapeDtypeStruct((B, HKV, DH, NC), BF16)),
        grid=(B,),
        in_specs=[pl.BlockSpec((None, NC, cmpx.shape[2]), lambda b: (b, 0, 0)),
                  const(pos2), const(wbig), const(w2big), const(bd128), const(kgain)],
        out_specs=(pl.BlockSpec((None, HKV, NC, LANES), lambda b: (b, 0, 0, 0)),
                   pl.BlockSpec((None, HKV, DH, NC), lambda b: (b, 0, 0, 0))),
        compiler_params=pltpu.CompilerParams(
            dimension_semantics=("arbitrary",), vmem_limit_bytes=VMEM_LIMIT),
        name="compress",
    )(cmpx, pos2, wbig, w2big, bd128, kgain)


def _attn_kernel(q_ref, kc_ref, vcT_ref, ks_ref, vsT_ref, kw_ref, vwT_ref, gT_ref,
                 y_ref, qT_ref, imp_ref, selb_ref, oc_ref, *, cch):
    h = pl.program_id(1)
    i = pl.program_id(2)
    nc = kc_ref.shape[0]
    ns = selb_ref.shape[0]
    wspan = WINDOW + QB
    t0 = i * QB
    iota = lambda shape, ax: lax.broadcasted_iota(jnp.int32, shape, ax)

    qfT = q_ref[...].astype(F32).T
    for g in range(G):
        qT_ref[0:DH, g * QB:(g + 1) * QB] = qfT[g * DH:(g + 1) * DH, :].astype(BF16)
    qT_ref[DH:LANES, :] = jnp.zeros((LANES - DH, G * QB), BF16)

    def front(nr):
        nsu = nr // (SLC_BLOCK // CMP_STRIDE)
        cmask = iota((nr, LANES), 0) * CMP_STRIDE + (CMP_BLOCK - 1) <= t0 + iota((nr, LANES), 1)
        kc = kc_ref[0:nr, :]
        vcT = vcT_ref[:, 0:nr]
        imp = jnp.zeros((nr, LANES), F32)
        for g in range(G):
            s = jnp.where(cmask, _dot(kc, qT_ref[:, g * QB:(g + 1) * QB]), NEG)
            m = jnp.max(s, axis=0, keepdims=True)
            p = jnp.where(cmask, jnp.exp2(s - m), 0.0)
            l = jnp.sum(p, axis=0, keepdims=True)
            pn = p * jnp.where(l > 0.0, 1.0 / l, 0.0)
            imp = imp + pn
            oc_ref[:, g * QB:(g + 1) * QB] = _dot(vcT, pn.astype(BF16))

        imp_ref[0:8, :] = jnp.zeros((8, LANES), F32)
        imp_ref[8:8 + nr, :] = imp
        islc = imp_ref[pl.ds(7, nsu, stride=4), :]
        for r in range(4):
            islc = islc + imp_ref[pl.ds(8 + r, nsu, stride=4), :]

        j_i = iota((nsu, LANES), 0)
        t_s = t0 + iota((nsu, LANES), 1)
        cur = lax.shift_right_logical(t_s, 6)
        valid = j_i * SLC_BLOCK <= t_s
        forced = (j_i == 0) | (j_i == cur) | (j_i == cur - 1)
        score = jnp.where(valid, jnp.where(forced, FORCED, islc), NEG)
        j_f = j_i.astype(F32)
        for _ in range(N_SEL):
            mx = jnp.max(score, axis=0, keepdims=True)
            jm = jnp.min(jnp.where(score == mx, j_f, float(nsu)), axis=0, keepdims=True)
            score = jnp.where(j_f == jm, -jnp.inf, score)
        selb_ref[0:nsu, :] = jnp.where(score == -jnp.inf, 0.0, NEG)

    cls = (i * (QB // CMP_STRIDE) + (QB - CMP_BLOCK) // CMP_STRIDE) // cch
    for kk in range(nc // cch):
        pl.when(cls == kk)(functools.partial(front, (kk + 1) * cch))

    nv = KCH // LANES
    cpg = BIAS_BLOCKS // (KCH // SLC_BLOCK)
    brows = min(BIAS_BLOCKS, ns)
    t_k = t0 + iota((KCH, LANES), 1)
    r_k = iota((KCH, LANES), 0)

    def sel_step(grp, carry, causal):
        m, acc = carry
        c0 = grp * NCHUNK

        @pl.when(c0 % cpg == 0)
        def _():
            r0 = pl.multiple_of((c0 // cpg) * brows, brows)
            rows = selb_ref[pl.ds(r0, brows), :].astype(BF16)
            qT_ref[DH:DH + brows, :] = jnp.concatenate([rows] * G, axis=1)

        qT = qT_ref[...]
        ss = [_dot(ks_ref[pl.ds(pl.multiple_of((c0 + u) * KCH, KCH), KCH), :], qT)
              for u in range(NCHUNK)]
        for u in range(NCHUNK):
            c = c0 + u
            s = ss[u]
            if causal:
                cb = jnp.where(c * KCH + r_k <= t_k, 0.0, NEG)
                s = s + jnp.concatenate([cb] * G, axis=1)
            m_new = jnp.maximum(m, jnp.max(s, axis=0, keepdims=True))
            alpha = jnp.exp2(m - m_new)
            p = jnp.exp2(s - m_new).astype(BF16)
            v = jnp.concatenate([vsT_ref[c * nv + w] for w in range(nv)], axis=1)
            acc = alpha * acc + _dot(v, p)
            m = m_new
        return m, acc

    init = (jnp.full((1, G * QB), NEG, F32), jnp.zeros((VROWS, G * QB), F32))
    last = i // ((KCH // QB) * NCHUNK)
    carry = lax.fori_loop(0, last, lambda g_, cr: sel_step(g_, cr, False), init)
    _, acc_s = sel_step(last, carry, True)
    os_ = acc_s[0:DH, :] * (1.0 / acc_s[DH:DH + 1, :])

    wc = jnp.maximum(i - WINDOW // QB, 0)
    ws = wc * QB
    sw = _dot(kw_ref[pl.ds(pl.multiple_of(ws, QB), wspan), :], qT_ref[...])
    kpos = ws + iota((wspan, LANES), 0)
    t_w = t0 + iota((wspan, LANES), 1)
    wb = jnp.where((kpos <= t_w) & (kpos > t_w - WINDOW), 0.0, NEG)
    sw = sw + jnp.concatenate([wb] * G, axis=1)
    p_w = jnp.exp2(sw - jnp.max(sw, axis=0, keepdims=True)).astype(BF16)
    vw = jnp.concatenate([vwT_ref[wc + u] for u in range(wspan // LANES)], axis=1)
    acc_w = _dot(vw, p_w)
    ow = acc_w[0:DH, :] * (1.0 / acc_w[DH:DH + 1, :])

    outs = []
    for g in range(G):
        sl = slice(g * QB, (g + 1) * QB)
        row = h * (3 * G) + 3 * g
        gc = gT_ref[pl.ds(row, 1), :]
        gs = gT_ref[pl.ds(row + 1, 1), :]
        gw = gT_ref[pl.ds(row + 2, 1), :]
        outs.append(gc * oc_ref[:, sl] + gs * os_[:, sl] + gw * ow[:, sl])
    y_ref[...] = jnp.concatenate(outs, axis=0).T


def _attention(q, kc, vcT, kslc, vslcT, kwin, vwinT, gT, B, S):
    NC = S // CMP_STRIDE
    NS = S // SLC_BLOCK
    NQ = S // QB
    cch = min(CCH, NC)
    assert NC % cch == 0 and (S // KCH) % NCHUNK == 0
    head = lambda *blk: pl.BlockSpec((None, None) + blk, lambda b, h, i: (b, h) + (0,) * len(blk))
    return pl.pallas_call(
        functools.partial(_attn_kernel, cch=cch),
        out_shape=jax.ShapeDtypeStruct((B, S, NSA_W), F32),
        grid=(B, HKV, NQ),
        in_specs=[
            pl.BlockSpec((None, QB, G * DH), lambda b, h, i: (b, i, h)),
            head(NC, LANES), head(DH, NC),
            head(S, LANES), head(S // LANES, VROWS, LANES),
            head(S, LANES), head(S // LANES, VROWS, LANES),
            pl.BlockSpec((None, GATE_ROWS, QB), lambda b, h, i: (b, 0, i)),
        ],
        out_specs=pl.BlockSpec((None, QB, G * DH), lambda b, h, i: (b, i, h)),
        scratch_shapes=[pltpu.VMEM((LANES, G * QB), BF16), pltpu.VMEM((NC + 8, LANES), F32),
                        pltpu.VMEM((NS, LANES), F32), pltpu.VMEM((DH, G * QB), F32)],
        compiler_params=pltpu.CompilerParams(
            dimension_semantics=("arbitrary", "arbitrary", "arbitrary"),
            vmem_limit_bytes=VMEM_LIMIT),
        name="attention",
    )(q, kc, vcT, kslc, vslcT, kwin, vwinT, gT)


XH = 8
YH = 32


def _seq_mix_kernel(lru_ref, cv_ref, cw_ref, cb_ref, wa_ref, ba_ref, wx_ref, bx_ref, lam_ref,
                    dw_ref, db_ref, lng_ref, lnb_ref, gl_ref, gc_ref,
                    yl_ref, yc_ref, xbuf, ybuf, hbuf):
    t = pl.program_id(1)

    @pl.when(t == 0)
    def _():
        xbuf[0:XH, :] = jnp.zeros((XH, LRU_W), F32)
        ybuf[0:YH, :] = jnp.zeros((YH, CV_W), F32)
        hbuf[...] = jnp.zeros((8, LRU_W), F32)

    xb = lru_ref[:, 0:LRU_W]
    gb = lru_ref[:, LRU_W:2 * LRU_W]
    xbuf[XH:XH + TS, :] = xb
    xr = jnp.broadcast_to(cb_ref[...], (TS, LRU_W))
    for k in range(LRU_CONV):
        xr = xr + cw_ref[k:k + 1, :] * xbuf[XH - (LRU_CONV - 1) + k:XH - (LRU_CONV - 1) + k + TS, :]
    xbuf[0:XH, :] = xbuf[TS:TS + XH, :]

    xr16 = xr.astype(BF16)
    r = _sigmoid(_dot(xr16, wa_ref[...]) + ba_ref[...])
    ig = _sigmoid(_dot(xr16, wx_ref[...]) + bx_ref[...])
    nl = -lam_ref[...]
    softplus = jnp.maximum(nl, 0.0) + jnp.log1p(jnp.exp(-jnp.abs(nl)))
    log_a = -LRU_C * r * softplus
    a = jnp.exp(log_a)
    u = xr * ig * jnp.sqrt(-jnp.tanh(log_a) * (a * a + 1.0))

    row = lax.broadcasted_iota(jnp.int32, (TS, LRU_W), 0)
    d = 1
    while d < TS:
        keep = row >= d
        a_sh = jnp.where(keep, pltpu.roll(a, d, 0), 1.0)
        u_sh = jnp.where(keep, pltpu.roll(u, d, 0), 0.0)
        u = a * u_sh + u
        a = a * a_sh
        d *= 2
    hseq = a * hbuf[0:1, :] + u
    hbuf[...] = jnp.broadcast_to(hseq[TS - 1:TS, :], (8, LRU_W))
    yl = hseq * _gelu_tanh(gb)
    yl = yl * lax.rsqrt(jnp.mean(yl * yl, axis=-1, keepdims=True) + EPS) * gl_ref[...]
    yl_ref[...] = yl.astype(BF16)

    y = cv_ref[:, 0:CV_W] * _sigmoid(cv_ref[:, CV_W:2 * CV_W])
    ybuf[YH:YH + TS, :] = y
    c = jnp.broadcast_to(db_ref[...], (TS, CV_W))
    for k in range(CV_KERNEL):
        off = YH - (CV_KERNEL - 1) + k
        c = c + dw_ref[k:k + 1, :] * ybuf[off:off + TS, :]
    ybuf[0:YH, :] = ybuf[TS:TS + YH, :]
    mu = jnp.mean(c, axis=-1, keepdims=True)
    cc = c - mu
    var = jnp.mean(cc * cc, axis=-1, keepdims=True)
    ln = cc * lax.rsqrt(var + EPS) * lng_ref[...] + lnb_ref[...]
    yc = ln * _sigmoid(ln)
    yc = yc * lax.rsqrt(jnp.mean(yc * yc, axis=-1, keepdims=True) + EPS) * gc_ref[...]
    yc_ref[...] = yc.astype(BF16)


def _seq_mix(lru, cv, params, B, S):
    const = lambda a: pl.BlockSpec(a.shape, lambda b, t: (0,) * a.ndim)
    tile = lambda w_: pl.BlockSpec((None, TS, w_), lambda b, t: (b, t, 0))
    return pl.pallas_call(
        _seq_mix_kernel,
        out_shape=(jax.ShapeDtypeStruct((B, S, LRU_W), BF16),
                   jax.ShapeDtypeStruct((B, S, CV_W), BF16)),
        grid=(B, S // TS),
        in_specs=[tile(2 * LRU_W), tile(2 * CV_W)] + [const(p) for p in params],
        out_specs=(tile(LRU_W), tile(CV_W)),
        scratch_shapes=[pltpu.VMEM((TS + XH, LRU_W), F32), pltpu.VMEM((TS + YH, CV_W), F32),
                        pltpu.VMEM((8, LRU_W), F32)],
        compiler_params=pltpu.CompilerParams(
            dimension_semantics=("arbitrary", "arbitrary"), vmem_limit_bytes=VMEM_LIMIT),
        name="seq_mix",
    )(lru, cv, *params)


def _out_mlp_kernel(x_ref, ya_ref, yl_ref, yc_ref, ga_ref, wo_ref, gm_ref, w1_ref, w2_ref, o_ref):
    ya = ya_ref[...]
    ya = (ya * lax.rsqrt(jnp.mean(ya * ya, axis=-1, keepdims=True) + EPS) * ga_ref[...]).astype(BF16)
    x1 = (x_ref[...] + _dot(ya, wo_ref[0:NSA_W, :])
          + _dot(yl_ref[...], wo_ref[NSA_W:NSA_W + LRU_W, :])
          + _dot(yc_ref[...], wo_ref[NSA_W + LRU_W:NSA_W + LRU_W + CV_W, :]))
    hm = (x1 * lax.rsqrt(jnp.mean(x1 * x1, axis=-1, keepdims=True) + EPS) * gm_ref[...]).astype(BF16)
    d_ff = w1_ref.shape[1]
    fc = 1024
    o_ref[...] = x1
    for c in range(d_ff // fc):
        hc = jnp.maximum(_dot(hm, w1_ref[:, c * fc:(c + 1) * fc]), 0.0)
        o_ref[...] += _dot((hc * hc).astype(BF16), w2_ref[c * fc:(c + 1) * fc, :])


def _out_mlp(x2, ya, yl, yc, ga, wo, gm, w1, w2):
    T, D = x2.shape
    tok = lambda w_: pl.BlockSpec((TM, w_), lambda t: (t, 0))
    const1 = lambda a: pl.BlockSpec(a.shape, lambda t: (0,) * a.ndim, pipeline_mode=pl.Buffered(1))
    return pl.pallas_call(
        _out_mlp_kernel,
        out_shape=jax.ShapeDtypeStruct((T, D), F32),
        grid=(T // TM,),
        in_specs=[tok(D), tok(NSA_W), tok(LRU_W), tok(CV_W),
                  const1(ga), const1(wo), const1(gm), const1(w1), const1(w2)],
        out_specs=tok(D),
        compiler_params=pltpu.CompilerParams(
            dimension_semantics=("arbitrary",), vmem_limit_bytes=VMEM_LIMIT),
        name="out_mlp",
    )(x2, ya, yl, yc, ga, wo, gm, w1, w2)


def _block_ones(n):
    idx = np.arange(n) // DH
    return jnp.asarray((idx[:, None] == idx[None, :]).astype(np.float32) / DH, dtype=BF16)


def _permute_w_in(w):
    off_kv = NSA_W
    kv = [w[:, off_kv + c * 128: off_kv + (c + 1) * 128] for c in range(6)]
    off_gate = off_kv + 6 * 128
    ngate = 3 * HKV * G
    off_lru = off_gate + ngate
    gate = jnp.pad(w[:, off_gate:off_lru], ((0, 0), (0, LANES - ngate)))
    cols = [w[:, 0:NSA_W], kv[0], kv[1], kv[2], kv[4], kv[3], kv[5],
            w[:, off_lru:off_lru + 2 * LRU_W + 2 * CV_W], gate]
    return jnp.concatenate(cols, axis=1).astype(BF16)


def _compress_weights(pos, w1, w2):
    slot_kv = jnp.array([0, 0, 1, 1])
    eye = jnp.eye(4, dtype=F32)
    w1r = w1.reshape(2, 2, CMP_STRIDE, DH, CMP_HID)[slot_kv]
    wbig = jnp.einsum('shpdo,st->psdhto', w1r, eye).reshape(CMP_STRIDE * 4 * DH, 2 * 4 * CMP_HID)
    posr = pos.reshape(2, 2, CMP_STRIDE, DH)[slot_kv]
    pos2 = jnp.transpose(posr, (1, 2, 0, 3)).reshape(2, CMP_STRIDE * 4 * DH)
    pos2 = jnp.pad(pos2, ((0, 6), (0, 0)))
    w2big = jnp.einsum('sod,st->sotd', w2[slot_kv], eye).reshape(4 * CMP_HID, 4 * DH)
    return pos2.astype(BF16), wbig.astype(BF16), w2big.astype(BF16)


def _block_diag(w):
    hh, bw, _ = w.shape
    eye = jnp.eye(hh, dtype=w.dtype)
    return jnp.einsum('hij,hg->higj', w, eye).reshape(hh * bw, hh * bw).astype(BF16)


def kernel(x, attn_norm, w_in, q_norm, k_norm, cmp_pos, cmp_w1, cmp_w2, lru_conv_w, lru_conv_b,
           lru_wa, lru_ba, lru_wx, lru_bx, lru_lambda, cv_dw_w, cv_dw_b, cv_ln_g, cv_ln_b,
           out_norm, w_out, mlp_norm, mlp_w1, mlp_w2):
    B, S, D = x.shape
    depth = w_in.shape[0]
    assert S % TM == 0 and S % TS == 0 and S >= WINDOW + QB and S // SLC_BLOCK >= N_SEL
    NC = S // CMP_STRIDE
    row = lambda v: v.reshape(1, -1).astype(F32)
    bd512 = _block_ones(NSA_W)
    bd128 = _block_ones(128)

    x2 = x.reshape(B * S, D)
    for l in range(depth):
        qgain = row(jnp.tile(q_norm[l], HKV * G)) * (DH ** -0.5 * LOG2E)
        kgain = row(jnp.concatenate([jnp.tile(k_norm[l, 1], HKV), jnp.tile(k_norm[l, 2], HKV)]))
        q, cmpr, kslc, kwin, vslcT, vwinT, lru, cv, gT = _in_proj(
            x2, row(attn_norm[l]), _permute_w_in(w_in[l]), bd512, qgain, kgain, B, S)

        pos2, wbig, w2big = _compress_weights(cmp_pos[l], cmp_w1[l], cmp_w2[l])
        kc, vcT = _compress(cmpr.reshape(B, NC, CMP_STRIDE * 256), pos2, wbig, w2big, bd128,
                            row(jnp.tile(k_norm[l, 0], HKV)), B, NC)

        y_attn = _attention(q.reshape(B, S, NSA_W), kc, vcT, kslc, vslcT, kwin, vwinT, gT, B, S)

        g_out = out_norm[l]
        seq_params = (lru_conv_w[l], row(lru_conv_b[l]), _block_diag(lru_wa[l]), row(lru_ba[l]),
                      _block_diag(lru_wx[l]), row(lru_bx[l]), row(lru_lambda[l]),
                      cv_dw_w[l], row(cv_dw_b[l]), row(cv_ln_g[l]), row(cv_ln_b[l]),
                      row(g_out[NSA_W:NSA_W + LRU_W]), row(g_out[NSA_W + LRU_W:]))
        yl, yc = _seq_mix(lru.reshape(B, S, 2 * LRU_W), cv.reshape(B, S, 2 * CV_W), seq_params, B, S)

        x2 = _out_mlp(x2, y_attn.reshape(B * S, NSA_W), yl.reshape(B * S, LRU_W),
                      yc.reshape(B * S, CV_W), row(g_out[:NSA_W]), w_out[l].astype(BF16),
                      row(mlp_norm[l]), mlp_w1[l].astype(BF16), mlp_w2[l].astype(BF16))
    return x2.reshape(B, S, D)
```

```python
import functools

import numpy as np
import jax
import jax.numpy as jnp
from jax import lax
from jax.experimental import pallas as pl
from jax.experimental.pallas import tpu as pltpu

F32 = jnp.float32
BF16 = jnp.bfloat16

EPS = 1e-6
NEG = -1e30
FORCED = 1e6
DH = 64
HKV = 2
G = 4
NSA_W = HKV * G * DH
CMP_BLOCK = 32
CMP_STRIDE = 16
CMP_HID = 128
SLC_BLOCK = 64
N_SEL = 16
WINDOW = 512
QB = 128
LRU_W = 256
LRU_HEADS = 8
LRU_CONV = 4
LRU_C = 8.0
CV_W = 256
CV_KERNEL = 31
LOG2E = 1.4426950408889634

LANES = 128
KCH = 256
NCHUNK = 8
CCH = 256
TM = 512
TS = 512
VMEM_LIMIT = 48 * 1024 * 1024

C_Q = 0
C_CMP = C_Q + NSA_W
C_KSLC = C_CMP + 256
C_KWIN = C_KSLC + 128
C_VSLC = C_KWIN + 128
C_VWIN = C_VSLC + 128
C_LRU = C_VWIN + 128
C_CV = C_LRU + 2 * LRU_W
C_GATE = C_CV + 2 * CV_W
N_INP = C_GATE + LANES
GATE_ROWS = 32
VROWS = DH + 16
BIAS_BLOCKS = LANES - DH


def _dot(a, b):
    return jnp.dot(a, b, preferred_element_type=F32)


def _head_rms_scale(z, bd):
    sq = z * z
    hi = sq.astype(BF16)
    lo = (sq - hi.astype(F32)).astype(BF16)
    ms = _dot(hi, bd) + _dot(lo, bd)
    return lax.rsqrt(ms + EPS)


def _gelu_tanh(x):
    return 0.5 * x * (1.0 + jnp.tanh(0.7978845608028654 * (x + 0.044715 * (x * x * x))))


def _sigmoid(x):
    return 1.0 / (1.0 + jnp.exp(-x))


def _in_proj_kernel(x_ref, g_ref, w_ref, bd_ref, qgain_ref, kgain_ref,
                    q_ref, cmp_ref, kslc_ref, kwin_ref, vslcT_ref, vwinT_ref,
                    lru_ref, cv_ref, gT_ref, *, nt):
    x = x_ref[...]
    ms = jnp.mean(x * x, axis=-1, keepdims=True)
    hn = (x * lax.rsqrt(ms + EPS) * g_ref[...]).astype(BF16)

    zq = _dot(hn, w_ref[:, C_Q:C_CMP])
    q_ref[...] = (zq * _head_rms_scale(zq, bd_ref[...]) * qgain_ref[...]).astype(BF16)

    cmp_ref[...] = _dot(hn, w_ref[:, C_CMP:C_KSLC]).astype(BF16)

    zk = _dot(hn, w_ref[:, C_KSLC:C_VSLC])
    kn = zk * _head_rms_scale(zk, bd_ref[0:256, 0:256]) * kgain_ref[...]
    lane = lax.broadcasted_iota(jnp.int32, (TM, LANES), 1)
    rowg = (pl.program_id(0) % nt) * TM + lax.broadcasted_iota(jnp.int32, (TM, LANES), 0)
    blk = lax.shift_right_logical(rowg, 6) & (BIAS_BLOCKS - 1)
    onehot = jnp.where(lane == DH + blk, 1.0, 0.0)
    lo = lane < DH
    ks, kw = kn[:, 0:128], kn[:, 128:256]
    kslc_ref[0] = jnp.where(lo, ks, onehot).astype(BF16)
    kslc_ref[1] = jnp.where(lo, pltpu.roll(ks, DH, 1), onehot).astype(BF16)
    kwin_ref[0] = jnp.where(lo, kw, 0.0).astype(BF16)
    kwin_ref[1] = jnp.where(lo, pltpu.roll(kw, DH, 1), 0.0).astype(BF16)

    zv = _dot(hn, w_ref[:, C_VSLC:C_LRU])
    zvT = zv.T
    ones_row = jnp.where(lax.broadcasted_iota(jnp.int32, (VROWS - DH, LANES), 0) == 0,
                         1.0, 0.0).astype(BF16)
    for j in range(TM // LANES):
        for h in range(HKV):
            cols = slice(j * LANES, (j + 1) * LANES)
            vslcT_ref[h, j, 0:DH, :] = zvT[h * DH:(h + 1) * DH, cols].astype(BF16)
            vslcT_ref[h, j, DH:VROWS, :] = ones_row
            vwinT_ref[h, j, 0:DH, :] = zvT[128 + h * DH:128 + (h + 1) * DH, cols].astype(BF16)
            vwinT_ref[h, j, DH:VROWS, :] = ones_row

    lru_ref[...] = _dot(hn, w_ref[:, C_LRU:C_CV])
    cv_ref[...] = _dot(hn, w_ref[:, C_CV:C_GATE])

    zg = _sigmoid(_dot(hn, w_ref[:, C_GATE:N_INP]))
    gT_ref[...] = zg.T[0:GATE_ROWS, :]


def _in_proj(x2, g, w, bd, qgain, kgain, B, S):
    T, D = x2.shape
    nt = S // TM
    tok = lambda w_: pl.BlockSpec((TM, w_), lambda t: (t, 0))
    const = lambda a: pl.BlockSpec(a.shape, lambda t: (0,) * a.ndim)
    vT_spec = pl.BlockSpec((None, HKV, TM // LANES, VROWS, LANES),
                           lambda t: (t // nt, 0, t % nt, 0, 0))
    k_spec = pl.BlockSpec((None, HKV, TM, LANES), lambda t: (t // nt, 0, t % nt, 0))
    out_shape = (
        jax.ShapeDtypeStruct((T, NSA_W), BF16),
        jax.ShapeDtypeStruct((T, 256), BF16),
        jax.ShapeDtypeStruct((B, HKV, S, LANES), BF16),
        jax.ShapeDtypeStruct((B, HKV, S, LANES), BF16),
        jax.ShapeDtypeStruct((B, HKV, S // LANES, VROWS, LANES), BF16),
        jax.ShapeDtypeStruct((B, HKV, S // LANES, VROWS, LANES), BF16),
        jax.ShapeDtypeStruct((T, 2 * LRU_W), F32),
        jax.ShapeDtypeStruct((T, 2 * CV_W), F32),
        jax.ShapeDtypeStruct((B, GATE_ROWS, S), F32),
    )
    out_specs = (
        tok(NSA_W), tok(256), k_spec, k_spec, vT_spec, vT_spec,
        tok(2 * LRU_W), tok(2 * CV_W),
        pl.BlockSpec((None, GATE_ROWS, TM), lambda t: (t // nt, 0, t % nt)),
    )
    return pl.pallas_call(
        functools.partial(_in_proj_kernel, nt=nt),
        out_shape=out_shape,
        grid=(T // TM,),
        in_specs=[tok(D), const(g), const(w), const(bd), const(qgain), const(kgain)],
        out_specs=out_specs,
        compiler_params=pltpu.CompilerParams(
            dimension_semantics=("arbitrary",), vmem_limit_bytes=VMEM_LIMIT),
        name="in_proj",
    )(x2, g, w, bd, qgain, kgain)


def _compress_kernel(x_ref, pos_ref, wbig_ref, w2_ref, bd_ref, kgain_ref, kc_ref, vcT_ref):
    nc = x_ref.shape[0]
    half = 4 * CMP_HID
    p = _dot(x_ref[...], wbig_ref[...])
    pc = _dot(pos_ref[...], wbig_ref[...])
    const = pc[0:1, 0:half] + pc[1:2, half:2 * half]
    nxt = pltpu.roll(p[:, half:2 * half], nc - 1, 0)
    hid = _gelu_tanh(p[:, 0:half] + nxt + const).astype(BF16)
    kv = _dot(hid, w2_ref[...])
    kc = kv[:, 0:128]
    kc = kc * _head_rms_scale(kc, bd_ref[...]) * kgain_ref[...]
    lo = lax.broadcasted_iota(jnp.int32, (nc, LANES), 1) < DH
    kc_ref[0] = jnp.where(lo, kc, 0.0).astype(BF16)
    kc_ref[1] = jnp.where(lo, pltpu.roll(kc, DH, 1), 0.0).astype(BF16)
    vT = kv[:, 128:256].T
    for h in range(HKV):
        vcT_ref[h] = vT[h * DH:(h + 1) * DH, :].astype(BF16)


def _compress(cmpx, pos2, wbig, w2big, bd128, kgain, B, NC):
    const = lambda a: pl.BlockSpec(a.shape, lambda b: (0,) * a.ndim, pipeline_mode=pl.Buffered(1))
    return pl.pallas_call(
        _compress_kernel,
        out_shape=(jax.ShapeDtypeStruct((B, HKV, NC, LANES), BF16),
                   jax.ShapeDtypeStruct((B, HKV, DH, NC), BF16)),
        grid=(B,),
        in_specs=[pl.BlockSpec((None, NC, cmpx.shape[2]), lambda b: (b, 0, 0)),
                  const(pos2), const(wbig), const(w2big), const(bd128), const(kgain)],
        out_specs=(pl.BlockSpec((None, HKV, NC, LANES), lambda b: (b, 0, 0, 0)),
                   pl.BlockSpec((None, HKV, DH, NC), lambda b: (b, 0, 0, 0))),
        compiler_params=pltpu.CompilerParams(
            dimension_semantics=("arbitrary",), vmem_limit_bytes=VMEM_LIMIT),
        name="compress",
    )(cmpx, pos2, wbig, w2big, bd128, kgain)


def _attn_kernel(q_ref, kc_ref, vcT_ref, ks_ref, vsT_ref, kw_ref, vwT_ref, gT_ref,
                 y_ref, qT_ref, imp_ref, selb_ref, oc_ref, *, cch):
    h = pl.program_id(1)
    i = pl.program_id(2)
    nc = kc_ref.shape[0]
    ns = selb_ref.shape[0]
    wspan = WINDOW + QB
    t0 = i * QB
    iota = lambda shape, ax: lax.broadcasted_iota(jnp.int32, shape, ax)

    qfT = q_ref[...].astype(F32).T
    for g in range(G):
        qT_ref[0:DH, g * QB:(g + 1) * QB] = qfT[g * DH:(g + 1) * DH, :].astype(BF16)
    qT_ref[DH:LANES, :] = jnp.zeros((LANES - DH, G * QB), BF16)

    def front(nr):
        nsu = nr // (SLC_BLOCK // CMP_STRIDE)
        cmask = iota((nr, LANES), 0) * CMP_STRIDE + (CMP_BLOCK - 1) <= t0 + iota((nr, LANES), 1)
        cb = jnp.where(cmask, 0.0, NEG)
        s = _dot(kc_ref[0:nr, :], qT_ref[...]) + jnp.concatenate([cb] * G, axis=1)
        p = jnp.exp2(s - jnp.max(s, axis=0, keepdims=True))
        l = jnp.sum(p, axis=0, keepdims=True)
        anyv = jnp.where(t0 + iota((1, LANES), 1) >= CMP_BLOCK - 1, 1.0, 0.0)
        pn = p * (jnp.concatenate([anyv] * G, axis=1) / l)
        oc_ref[...] = _dot(vcT_ref[:, 0:nr], pn.astype(BF16))
        imp = pn[:, 0:QB]
        for g in range(1, G):
            imp = imp + pn[:, g * QB:(g + 1) * QB]

        imp_ref[0:8, :] = jnp.zeros((8, LANES), F32)
        imp_ref[8:8 + nr, :] = imp
        islc = imp_ref[pl.ds(7, nsu, stride=4), :]
        for r in range(4):
            islc = islc + imp_ref[pl.ds(8 + r, nsu, stride=4), :]

        j_i = iota((nsu, LANES), 0)
        t_s = t0 + iota((nsu, LANES), 1)
        cur = lax.shift_right_logical(t_s, 6)
        valid = j_i * SLC_BLOCK <= t_s
        ninf = -jnp.inf
        score = jnp.where(j_i == 0, ninf, jnp.where(j_i == cur, ninf,
                                                    jnp.where(j_i == cur - 1, ninf, islc)))
        score = jnp.where(valid, score, NEG)
        j_f = j_i.astype(F32)
        for _ in range(N_SEL - 3):
            mx = jnp.max(score, axis=0, keepdims=True)
            jm = jnp.min(jnp.where(score == mx, j_f, float(nsu)), axis=0, keepdims=True)
            score = jnp.where(j_f == jm, -jnp.inf, score)
        own = lax.shift_right_logical(j_i, 1) == i
        picked = jnp.where(valid, jnp.where(score == -jnp.inf, 0.0, NEG), NEG)
        selb_ref[0:nsu, :] = jnp.where(own, NEG, picked)

    cls = (i * (QB // CMP_STRIDE) + (QB - CMP_BLOCK) // CMP_STRIDE) // cch
    for kk in range(nc // cch):
        pl.when(cls == kk)(functools.partial(front, (kk + 1) * cch))

    tri = jnp.where(iota((QB, LANES), 0) <= iota((QB, LANES), 1), 0.0, NEG)
    sd = (_dot(ks_ref[pl.ds(pl.multiple_of(t0, QB), QB), :], qT_ref[...])
          + jnp.concatenate([tri] * G, axis=1))
    m0 = jnp.max(sd, axis=0, keepdims=True)
    acc0 = _dot(vsT_ref[i], jnp.exp2(sd - m0).astype(BF16))

    nv = KCH // LANES
    cpg = BIAS_BLOCKS // (KCH // SLC_BLOCK)
    brows = min(BIAS_BLOCKS, ns)

    def sel_step(grp, carry):
        m, acc = carry
        c0 = grp * NCHUNK

        @pl.when(c0 % cpg == 0)
        def _():
            r0 = pl.multiple_of((c0 // cpg) * brows, brows)
            rows = selb_ref[pl.ds(r0, brows), :].astype(BF16)
            qT_ref[DH:DH + brows, :] = jnp.concatenate([rows] * G, axis=1)

        qT = qT_ref[...]
        ss = [_dot(ks_ref[pl.ds(pl.multiple_of((c0 + u) * KCH, KCH), KCH), :], qT)
              for u in range(NCHUNK)]
        for u in range(NCHUNK):
            c = c0 + u
            s = ss[u]
            m_new = jnp.maximum(m, jnp.max(s, axis=0, keepdims=True))
            alpha = jnp.exp2(m - m_new)
            p = jnp.exp2(s - m_new).astype(BF16)
            v = jnp.concatenate([vsT_ref[c * nv + w] for w in range(nv)], axis=1)
            acc = alpha * acc + _dot(v, p)
            m = m_new
        return m, acc

    nstep = i // ((KCH // QB) * NCHUNK) + 1
    _, acc_s = lax.fori_loop(0, nstep, sel_step, (m0, acc0))
    os_ = acc_s[0:DH, :] * (1.0 / acc_s[DH:DH + 1, :])

    wc = jnp.maximum(i - WINDOW // QB, 0)
    ws = wc * QB
    sw = _dot(kw_ref[pl.ds(pl.multiple_of(ws, QB), wspan), :], qT_ref[...])
    kpos = ws + iota((wspan, LANES), 0)
    t_w = t0 + iota((wspan, LANES), 1)
    wb = jnp.where(kpos <= t_w, jnp.where(kpos > t_w - WINDOW, 0.0, NEG), NEG)
    sw = sw + jnp.concatenate([wb] * G, axis=1)
    p_w = jnp.exp2(sw - jnp.max(sw, axis=0, keepdims=True)).astype(BF16)
    vw = jnp.concatenate([vwT_ref[wc + u] for u in range(wspan // LANES)], axis=1)
    acc_w = _dot(vw, p_w)
    ow = acc_w[0:DH, :] * (1.0 / acc_w[DH:DH + 1, :])

    outs = []
    for g in range(G):
        sl = slice(g * QB, (g + 1) * QB)
        row = h * (3 * G) + 3 * g
        gc = gT_ref[pl.ds(row, 1), :]
        gs = gT_ref[pl.ds(row + 1, 1), :]
        gw = gT_ref[pl.ds(row + 2, 1), :]
        outs.append(gc * oc_ref[:, sl] + gs * os_[:, sl] + gw * ow[:, sl])
    y_ref[...] = jnp.concatenate(outs, axis=0).T


def _attention(q, kc, vcT, kslc, vslcT, kwin, vwinT, gT, B, S):
    NC = S // CMP_STRIDE
    NS = S // SLC_BLOCK
    NQ = S // QB
    cch = min(CCH, NC)
    assert NC % cch == 0 and (S // KCH) % NCHUNK == 0
    head = lambda *blk: pl.BlockSpec((None, None) + blk, lambda b, h, i: (b, h) + (0,) * len(blk))
    return pl.pallas_call(
        functools.partial(_attn_kernel, cch=cch),
        out_shape=jax.ShapeDtypeStruct((B, S, NSA_W), F32),
        grid=(B, HKV, NQ),
        in_specs=[
            pl.BlockSpec((None, QB, G * DH), lambda b, h, i: (b, i, h)),
            head(NC, LANES), head(DH, NC),
            head(S, LANES), head(S // LANES, VROWS, LANES),
            head(S, LANES), head(S // LANES, VROWS, LANES),
            pl.BlockSpec((None, GATE_ROWS, QB), lambda b, h, i: (b, 0, i)),
        ],
        out_specs=pl.BlockSpec((None, QB, G * DH), lambda b, h, i: (b, i, h)),
        scratch_shapes=[pltpu.VMEM((LANES, G * QB), BF16), pltpu.VMEM((NC + 8, LANES), F32),
                        pltpu.VMEM((NS, LANES), F32), pltpu.VMEM((DH, G * QB), F32)],
        compiler_params=pltpu.CompilerParams(
            dimension_semantics=("arbitrary", "arbitrary", "arbitrary"),
            vmem_limit_bytes=VMEM_LIMIT),
        name="attention",
    )(q, kc, vcT, kslc, vslcT, kwin, vwinT, gT)


XH = 8
YH = 32


def _seq_mix_kernel(lru_ref, cv_ref, cw_ref, cb_ref, wa_ref, ba_ref, wx_ref, bx_ref, lam_ref,
                    dw_ref, db_ref, lng_ref, lnb_ref, gl_ref, gc_ref,
                    yl_ref, yc_ref, xbuf, ybuf, hbuf):
    t = pl.program_id(1)

    @pl.when(t == 0)
    def _():
        xbuf[0:XH, :] = jnp.zeros((XH, LRU_W), F32)
        ybuf[0:YH, :] = jnp.zeros((YH, CV_W), F32)
        hbuf[...] = jnp.zeros((8, LRU_W), F32)

    xb = lru_ref[:, 0:LRU_W]
    gb = lru_ref[:, LRU_W:2 * LRU_W]
    xbuf[XH:XH + TS, :] = xb
    xr = jnp.broadcast_to(cb_ref[...], (TS, LRU_W))
    for k in range(LRU_CONV):
        xr = xr + cw_ref[k:k + 1, :] * xbuf[XH - (LRU_CONV - 1) + k:XH - (LRU_CONV - 1) + k + TS, :]
    xbuf[0:XH, :] = xbuf[TS:TS + XH, :]

    xr16 = xr.astype(BF16)
    r = _sigmoid(_dot(xr16, wa_ref[...]) + ba_ref[...])
    ig = _sigmoid(_dot(xr16, wx_ref[...]) + bx_ref[...])
    nl = -lam_ref[...]
    softplus = jnp.maximum(nl, 0.0) + jnp.log1p(jnp.exp(-jnp.abs(nl)))
    log_a = -LRU_C * r * softplus
    a = jnp.exp(log_a)
    u = xr * ig * jnp.sqrt(-jnp.tanh(log_a) * (a * a + 1.0))

    row = lax.broadcasted_iota(jnp.int32, (TS, LRU_W), 0)
    d = 1
    while d < TS:
        keep = row >= d
        a_sh = jnp.where(keep, pltpu.roll(a, d, 0), 1.0)
        u_sh = jnp.where(keep, pltpu.roll(u, d, 0), 0.0)
        u = a * u_sh + u
        a = a * a_sh
        d *= 2
    hseq = a * hbuf[0:1, :] + u
    hbuf[...] = jnp.broadcast_to(hseq[TS - 1:TS, :], (8, LRU_W))
    yl = hseq * _gelu_tanh(gb)
    yl = yl * lax.rsqrt(jnp.mean(yl * yl, axis=-1, keepdims=True) + EPS) * gl_ref[...]
    yl_ref[...] = yl.astype(BF16)

    y = cv_ref[:, 0:CV_W] * _sigmoid(cv_ref[:, CV_W:2 * CV_W])
    ybuf[YH:YH + TS, :] = y
    c = jnp.broadcast_to(db_ref[...], (TS, CV_W))
    for k in range(CV_KERNEL):
        off = YH - (CV_KERNEL - 1) + k
        c = c + dw_ref[k:k + 1, :] * ybuf[off:off + TS, :]
    ybuf[0:YH, :] = ybuf[TS:TS + YH, :]
    mu = jnp.mean(c, axis=-1, keepdims=True)
    cc = c - mu
    var = jnp.mean(cc * cc, axis=-1, keepdims=True)
    ln = cc * lax.rsqrt(var + EPS) * lng_ref[...] + lnb_ref[...]
    yc = ln * _sigmoid(ln)
    yc = yc * lax.rsqrt(jnp.mean(yc * yc, axis=-1, keepdims=True) + EPS) * gc_ref[...]
    yc_ref[...] = yc.astype(BF16)


def _seq_mix(lru, cv, params, B, S):
    const = lambda a: pl.BlockSpec(a.shape, lambda b, t: (0,) * a.ndim)
    tile = lambda w_: pl.BlockSpec((None, TS, w_), lambda b, t: (b, t, 0))
    return pl.pallas_call(
        _seq_mix_kernel,
        out_shape=(jax.ShapeDtypeStruct((B, S, LRU_W), BF16),
                   jax.ShapeDtypeStruct((B, S, CV_W), BF16)),
        grid=(B, S // TS),
        in_specs=[tile(2 * LRU_W), tile(2 * CV_W)] + [const(p) for p in params],
        out_specs=(tile(LRU_W), tile(CV_W)),
        scratch_shapes=[pltpu.VMEM((TS + XH, LRU_W), F32), pltpu.VMEM((TS + YH, CV_W), F32),
                        pltpu.VMEM((8, LRU_W), F32)],
        compiler_params=pltpu.CompilerParams(
            dimension_semantics=("arbitrary", "arbitrary"), vmem_limit_bytes=VMEM_LIMIT),
        name="seq_mix",
    )(lru, cv, *params)


def _out_mlp_kernel(x_ref, ya_ref, yl_ref, yc_ref, ga_ref, wo_ref, gm_ref, w1_ref, w2_ref, o_ref):
    ya = ya_ref[...]
    ya = (ya * lax.rsqrt(jnp.mean(ya * ya, axis=-1, keepdims=True) + EPS) * ga_ref[...]).astype(BF16)
    x1 = (x_ref[...] + _dot(ya, wo_ref[0:NSA_W, :])
          + _dot(yl_ref[...], wo_ref[NSA_W:NSA_W + LRU_W, :])
          + _dot(yc_ref[...], wo_ref[NSA_W + LRU_W:NSA_W + LRU_W + CV_W, :]))
    hm = (x1 * lax.rsqrt(jnp.mean(x1 * x1, axis=-1, keepdims=True) + EPS) * gm_ref[...]).astype(BF16)
    d_ff = w1_ref.shape[1]
    fc = 1024
    o_ref[...] = x1
    for c in range(d_ff // fc):
        hc = jnp.maximum(_dot(hm, w1_ref[:, c * fc:(c + 1) * fc]), 0.0)
        o_ref[...] += _dot((hc * hc).astype(BF16), w2_ref[c * fc:(c + 1) * fc, :])


def _out_mlp(x2, ya, yl, yc, ga, wo, gm, w1, w2):
    T, D = x2.shape
    tok = lambda w_: pl.BlockSpec((TM, w_), lambda t: (t, 0))
    const1 = lambda a: pl.BlockSpec(a.shape, lambda t: (0,) * a.ndim, pipeline_mode=pl.Buffered(1))
    return pl.pallas_call(
        _out_mlp_kernel,
        out_shape=jax.ShapeDtypeStruct((T, D), F32),
        grid=(T // TM,),
        in_specs=[tok(D), tok(NSA_W), tok(LRU_W), tok(CV_W),
                  const1(ga), const1(wo), const1(gm), const1(w1), const1(w2)],
        out_specs=tok(D),
        compiler_params=pltpu.CompilerParams(
            dimension_semantics=("arbitrary",), vmem_limit_bytes=VMEM_LIMIT),
        name="out_mlp",
    )(x2, ya, yl, yc, ga, wo, gm, w1, w2)


def _block_ones(n):
    idx = np.arange(n) // DH
    return jnp.asarray((idx[:, None] == idx[None, :]).astype(np.float32) / DH, dtype=BF16)


def _permute_w_in(w):
    off_kv = NSA_W
    kv = [w[:, off_kv + c * 128: off_kv + (c + 1) * 128] for c in range(6)]
    off_gate = off_kv + 6 * 128
    ngate = 3 * HKV * G
    off_lru = off_gate + ngate
    gate = jnp.pad(w[:, off_gate:off_lru], ((0, 0), (0, LANES - ngate)))
    cols = [w[:, 0:NSA_W], kv[0], kv[1], kv[2], kv[4], kv[3], kv[5],
            w[:, off_lru:off_lru + 2 * LRU_W + 2 * CV_W], gate]
    return jnp.concatenate(cols, axis=1).astype(BF16)


def _compress_weights(pos, w1, w2):
    slot_kv = jnp.array([0, 0, 1, 1])
    eye = jnp.eye(4, dtype=F32)
    w1r = w1.reshape(2, 2, CMP_STRIDE, DH, CMP_HID)[slot_kv]
    wbig = jnp.einsum('shpdo,st->psdhto', w1r, eye).reshape(CMP_STRIDE * 4 * DH, 2 * 4 * CMP_HID)
    posr = pos.reshape(2, 2, CMP_STRIDE, DH)[slot_kv]
    pos2 = jnp.transpose(posr, (1, 2, 0, 3)).reshape(2, CMP_STRIDE * 4 * DH)
    pos2 = jnp.pad(pos2, ((0, 6), (0, 0)))
    w2big = jnp.einsum('sod,st->sotd', w2[slot_kv], eye).reshape(4 * CMP_HID, 4 * DH)
    return pos2.astype(BF16), wbig.astype(BF16), w2big.astype(BF16)


def _block_diag(w):
    hh, bw, _ = w.shape
    eye = jnp.eye(hh, dtype=w.dtype)
    return jnp.einsum('hij,hg->higj', w, eye).reshape(hh * bw, hh * bw).astype(BF16)


def kernel(x, attn_norm, w_in, q_norm, k_norm, cmp_pos, cmp_w1, cmp_w2, lru_conv_w, lru_conv_b,
           lru_wa, lru_ba, lru_wx, lru_bx, lru_lambda, cv_dw_w, cv_dw_b, cv_ln_g, cv_ln_b,
           out_norm, w_out, mlp_norm, mlp_w1, mlp_w2):
    B, S, D = x.shape
    depth = w_in.shape[0]
    assert S % TM == 0 and S % TS == 0 and S >= WINDOW + QB and S // SLC_BLOCK >= N_SEL
    NC = S // CMP_STRIDE
    row = lambda v: v.reshape(1, -1).astype(F32)
    bd512 = _block_ones(NSA_W)
    bd128 = _block_ones(128)

    x2 = x.reshape(B * S, D)
    for l in range(depth):
        qgain = row(jnp.tile(q_norm[l], HKV * G)) * (DH ** -0.5 * LOG2E)
        kgain = row(jnp.concatenate([jnp.tile(k_norm[l, 1], HKV), jnp.tile(k_norm[l, 2], HKV)]))
        q, cmpr, kslc, kwin, vslcT, vwinT, lru, cv, gT = _in_proj(
            x2, row(attn_norm[l]), _permute_w_in(w_in[l]), bd512, qgain, kgain, B, S)

        pos2, wbig, w2big = _compress_weights(cmp_pos[l], cmp_w1[l], cmp_w2[l])
        kc, vcT = _compress(cmpr.reshape(B, NC, CMP_STRIDE * 256), pos2, wbig, w2big, bd128,
                            row(jnp.tile(k_norm[l, 0], HKV)), B, NC)

        y_attn = _attention(q.reshape(B, S, NSA_W), kc, vcT, kslc, vslcT, kwin, vwinT, gT, B, S)

        g_out = out_norm[l]
        seq_params = (lru_conv_w[l], row(lru_conv_b[l]), _block_diag(lru_wa[l]), row(lru_ba[l]),
                      _block_diag(lru_wx[l]), row(lru_bx[l]), row(lru_lambda[l]),
                      cv_dw_w[l], row(cv_dw_b[l]), row(cv_ln_g[l]), row(cv_ln_b[l]),
                      row(g_out[NSA_W:NSA_W + LRU_W]), row(g_out[NSA_W + LRU_W:]))
        yl, yc = _seq_mix(lru.reshape(B, S, 2 * LRU_W), cv.reshape(B, S, 2 * CV_W), seq_params, B, S)

        x2 = _out_mlp(x2, y_attn.reshape(B * S, NSA_W), yl.reshape(B * S, LRU_W),
                      yc.reshape(B * S, CV_W), row(g_out[:NSA_W]), w_out[l].astype(BF16),
                      row(mlp_norm[l]), mlp_w1[l].astype(BF16), mlp_w2[l].astype(BF16))
    return x2.reshape(B, S, D)
```

```python
import functools

import numpy as np
import jax
import jax.numpy as jnp
from jax import lax
from jax.experimental import pallas as pl
from jax.experimental.pallas import tpu as pltpu

F32 = jnp.float32
BF16 = jnp.bfloat16

EPS = 1e-6
NEG = -1e30
FORCED = 1e6
DH = 64
HKV = 2
G = 4
NSA_W = HKV * G * DH
CMP_BLOCK = 32
CMP_STRIDE = 16
CMP_HID = 128
SLC_BLOCK = 64
N_SEL = 16
WINDOW = 512
QB = 128
LRU_W = 256
LRU_HEADS = 8
LRU_CONV = 4
LRU_C = 8.0
CV_W = 256
CV_KERNEL = 31
LOG2E = 1.4426950408889634
SAFE_SHIFT = 50.0

LANES = 128
KCH = 256
NCHUNK = 8
CCH = 256
TM = 512
TS = 512
VMEM_LIMIT = 48 * 1024 * 1024

C_Q = 0
C_CMP = C_Q + NSA_W
C_KSLC = C_CMP + 256
C_KWIN = C_KSLC + 128
C_VSLC = C_KWIN + 128
C_VWIN = C_VSLC + 128
C_LRU = C_VWIN + 128
C_CV = C_LRU + 2 * LRU_W
C_GATE = C_CV + 2 * CV_W
N_INP = C_GATE + LANES
GATE_ROWS = 32
VROWS = DH + 16
BIAS_BLOCKS = LANES - DH


def _dot(a, b):
    return jnp.dot(a, b, preferred_element_type=F32)


def _head_rms_scale(z, bd):
    sq = z * z
    hi = sq.astype(BF16)
    lo = (sq - hi.astype(F32)).astype(BF16)
    ms = _dot(hi, bd) + _dot(lo, bd)
    return lax.rsqrt(ms + EPS)


def _gelu_tanh(x):
    return 0.5 * x * (1.0 + jnp.tanh(0.7978845608028654 * (x + 0.044715 * (x * x * x))))


def _sigmoid(x):
    return 1.0 / (1.0 + jnp.exp(-x))


def _in_proj_kernel(x_ref, g_ref, w_ref, bd_ref, qgain_ref, kgain_ref,
                    q_ref, cmp_ref, kslc_ref, kwin_ref, vslcT_ref, vwinT_ref,
                    lru_ref, cv_ref, gT_ref, *, nt):
    x = x_ref[...]
    ms = jnp.mean(x * x, axis=-1, keepdims=True)
    hn = (x * lax.rsqrt(ms + EPS) * g_ref[...]).astype(BF16)

    zq = _dot(hn, w_ref[:, C_Q:C_CMP])
    q_ref[...] = (zq * _head_rms_scale(zq, bd_ref[...]) * qgain_ref[...]).astype(BF16)

    cmp_ref[...] = _dot(hn, w_ref[:, C_CMP:C_KSLC]).astype(BF16)

    zk = _dot(hn, w_ref[:, C_KSLC:C_VSLC])
    kn = zk * _head_rms_scale(zk, bd_ref[0:256, 0:256]) * kgain_ref[...]
    lane = lax.broadcasted_iota(jnp.int32, (TM, LANES), 1)
    rowg = (pl.program_id(0) % nt) * TM + lax.broadcasted_iota(jnp.int32, (TM, LANES), 0)
    blk = lax.shift_right_logical(rowg, 6) & (BIAS_BLOCKS - 1)
    onehot = jnp.where(lane == DH + blk, 1.0, 0.0)
    lo = lane < DH
    ks, kw = kn[:, 0:128], kn[:, 128:256]
    kslc_ref[0] = jnp.where(lo, ks, onehot).astype(BF16)
    kslc_ref[1] = jnp.where(lo, pltpu.roll(ks, DH, 1), onehot).astype(BF16)
    kwin_ref[0] = jnp.where(lo, kw, 0.0).astype(BF16)
    kwin_ref[1] = jnp.where(lo, pltpu.roll(kw, DH, 1), 0.0).astype(BF16)

    zv = _dot(hn, w_ref[:, C_VSLC:C_LRU])
    zvT = zv.T
    ones_row = jnp.where(lax.broadcasted_iota(jnp.int32, (VROWS - DH, LANES), 0) == 0,
                         1.0, 0.0).astype(BF16)
    for j in range(TM // LANES):
        for h in range(HKV):
            cols = slice(j * LANES, (j + 1) * LANES)
            vslcT_ref[h, j, 0:DH, :] = zvT[h * DH:(h + 1) * DH, cols].astype(BF16)
            vslcT_ref[h, j, DH:VROWS, :] = ones_row
            vwinT_ref[h, j, 0:DH, :] = zvT[128 + h * DH:128 + (h + 1) * DH, cols].astype(BF16)
            vwinT_ref[h, j, DH:VROWS, :] = ones_row

    lru_ref[...] = _dot(hn, w_ref[:, C_LRU:C_CV])
    cv_ref[...] = _dot(hn, w_ref[:, C_CV:C_GATE])

    zg = _sigmoid(_dot(hn, w_ref[:, C_GATE:N_INP]))
    gT_ref[...] = zg.T[0:GATE_ROWS, :]


def _in_proj(x2, g, w, bd, qgain, kgain, B, S):
    T, D = x2.shape
    nt = S // TM
    tok = lambda w_: pl.BlockSpec((TM, w_), lambda t: (t, 0))
    const = lambda a: pl.BlockSpec(a.shape, lambda t: (0,) * a.ndim)
    vT_spec = pl.BlockSpec((None, HKV, TM // LANES, VROWS, LANES),
                           lambda t: (t // nt, 0, t % nt, 0, 0))
    k_spec = pl.BlockSpec((None, HKV, TM, LANES), lambda t: (t // nt, 0, t % nt, 0))
    out_shape = (
        jax.ShapeDtypeStruct((T, NSA_W), BF16),
        jax.ShapeDtypeStruct((T, 256), BF16),
        jax.ShapeDtypeStruct((B, HKV, S, LANES), BF16),
        jax.ShapeDtypeStruct((B, HKV, S, LANES), BF16),
        jax.ShapeDtypeStruct((B, HKV, S // LANES, VROWS, LANES), BF16),
        jax.ShapeDtypeStruct((B, HKV, S // LANES, VROWS, LANES), BF16),
        jax.ShapeDtypeStruct((T, 2 * LRU_W), F32),
        jax.ShapeDtypeStruct((T, 2 * CV_W), F32),
        jax.ShapeDtypeStruct((B, GATE_ROWS, S), F32),
    )
    out_specs = (
        tok(NSA_W), tok(256), k_spec, k_spec, vT_spec, vT_spec,
        tok(2 * LRU_W), tok(2 * CV_W),
        pl.BlockSpec((None, GATE_ROWS, TM), lambda t: (t // nt, 0, t % nt)),
    )
    return pl.pallas_call(
        functools.partial(_in_proj_kernel, nt=nt),
        out_shape=out_shape,
        grid=(T // TM,),
        in_specs=[tok(D), const(g), const(w), const(bd), const(qgain), const(kgain)],
        out_specs=out_specs,
        compiler_params=pltpu.CompilerParams(
            dimension_semantics=("arbitrary",), vmem_limit_bytes=VMEM_LIMIT),
        name="in_proj",
    )(x2, g, w, bd, qgain, kgain)


def _compress_kernel(x_ref, pos_ref, wbig_ref, w2_ref, bd_ref, kgain_ref, kc_ref, vcT_ref):
    nc = x_ref.shape[0]
    half = 4 * CMP_HID
    p = _dot(x_ref[...], wbig_ref[...])
    pc = _dot(pos_ref[...], wbig_ref[...])
    const = pc[0:1, 0:half] + pc[1:2, half:2 * half]
    nxt = pltpu.roll(p[:, half:2 * half], nc - 1, 0)
    hid = _gelu_tanh(p[:, 0:half] + nxt + const).astype(BF16)
    kv = _dot(hid, w2_ref[...])
    kc = kv[:, 0:128]
    kc = kc * _head_rms_scale(kc, bd_ref[...]) * kgain_ref[...]
    lo = lax.broadcasted_iota(jnp.int32, (nc, LANES), 1) < DH
    kc_ref[0] = jnp.where(lo, kc, 0.0).astype(BF16)
    kc_ref[1] = jnp.where(lo, pltpu.roll(kc, DH, 1), 0.0).astype(BF16)
    vT = kv[:, 128:256].T
    for h in range(HKV):
        vcT_ref[h] = vT[h * DH:(h + 1) * DH, :].astype(BF16)


def _compress(cmpx, pos2, wbig, w2big, bd128, kgain, B, NC):
    const = lambda a: pl.BlockSpec(a.shape, lambda b: (0,) * a.ndim, pipeline_mode=pl.Buffered(1))
    return pl.pallas_call(
        _compress_kernel,
        out_shape=(jax.ShapeDtypeStruct((B, HKV, NC, LANES), BF16),
                   jax.ShapeDtypeStruct((B, HKV, DH, NC), BF16)),
        grid=(B,),
        in_specs=[pl.BlockSpec((None, NC, cmpx.shape[2]), lambda b: (b, 0, 0)),
                  const(pos2), const(wbig), const(w2big), const(bd128), const(kgain)],
        out_specs=(pl.BlockSpec((None, HKV, NC, LANES), lambda b: (b, 0, 0, 0)),
                   pl.BlockSpec((None, HKV, DH, NC), lambda b: (b, 0, 0, 0))),
        compiler_params=pltpu.CompilerParams(
            dimension_semantics=("arbitrary",), vmem_limit_bytes=VMEM_LIMIT),
        name="compress",
    )(cmpx, pos2, wbig, w2big, bd128, kgain)


def _attn_kernel(sc_ref, q_ref, kc_ref, vcT_ref, ks_ref, vsT_ref, kw_ref, vwT_ref, gT_ref,
                 y_ref, qT_ref, imp_ref, selb_ref, oc_ref, os_ref, *, cch):
    h = pl.program_id(1)
    i = pl.program_id(2)
    nc = kc_ref.shape[0]
    ns = selb_ref.shape[0]
    wspan = WINDOW + QB
    t0 = i * QB
    iota = lambda shape, ax: lax.broadcasted_iota(jnp.int32, shape, ax)
    use_bound = sc_ref[1] > 0.5
    shift = jnp.where(use_bound, -sc_ref[0], 0.0)

    qfT = q_ref[...].astype(F32).T
    for g in range(G):
        qT_ref[0:DH, g * QB:(g + 1) * QB] = qfT[g * DH:(g + 1) * DH, :].astype(BF16)
    qT_ref[DH:LANES, :] = jnp.zeros((LANES - DH, G * QB), BF16)

    def front(nr):
        nsu = nr // (SLC_BLOCK // CMP_STRIDE)
        cmask = iota((nr, LANES), 0) * CMP_STRIDE + (CMP_BLOCK - 1) <= t0 + iota((nr, LANES), 1)
        cb = jnp.where(cmask, 0.0, NEG)
        s = _dot(kc_ref[0:nr, :], qT_ref[...]) + jnp.concatenate([cb] * G, axis=1)
        p = jnp.exp2(s - jnp.max(s, axis=0, keepdims=True))
        l = jnp.sum(p, axis=0, keepdims=True)
        anyv = jnp.where(t0 + iota((1, LANES), 1) >= CMP_BLOCK - 1, 1.0, 0.0)
        pn = p * (jnp.concatenate([anyv] * G, axis=1) / l)
        oc_ref[...] = _dot(vcT_ref[:, 0:nr], pn.astype(BF16))
        imp = pn[:, 0:QB]
        for g in range(1, G):
            imp = imp + pn[:, g * QB:(g + 1) * QB]

        imp_ref[0:8, :] = jnp.zeros((8, LANES), F32)
        imp_ref[8:8 + nr, :] = imp
        islc = imp_ref[pl.ds(7, nsu, stride=4), :]
        for r in range(4):
            islc = islc + imp_ref[pl.ds(8 + r, nsu, stride=4), :]

        j_i = iota((nsu, LANES), 0)
        t_s = t0 + iota((nsu, LANES), 1)
        cur = lax.shift_right_logical(t_s, 6)
        valid = j_i * SLC_BLOCK <= t_s
        ninf = -jnp.inf
        score = jnp.where(j_i == 0, ninf, jnp.where(j_i == cur, ninf,
                                                    jnp.where(j_i == cur - 1, ninf, islc)))
        score = jnp.where(valid, score, NEG)
        j_f = j_i.astype(F32)
        for _ in range(N_SEL - 3):
            mx = jnp.max(score, axis=0, keepdims=True)
            jm = jnp.min(jnp.where(score == mx, j_f, float(nsu)), axis=0, keepdims=True)
            score = jnp.where(j_f == jm, -jnp.inf, score)
        own = lax.shift_right_logical(j_i, 1) == i
        picked = jnp.where(valid, jnp.where(score == -jnp.inf, shift, NEG), NEG)
        selb_ref[0:nsu, :] = jnp.where(own, NEG, picked)

    cls = (i * (QB // CMP_STRIDE) + (QB - CMP_BLOCK) // CMP_STRIDE) // cch
    for kk in range(nc // cch):
        pl.when(cls == kk)(functools.partial(front, (kk + 1) * cch))

    tri = jnp.where(iota((QB, LANES), 0) <= iota((QB, LANES), 1), 0.0, NEG)
    sd = (_dot(ks_ref[pl.ds(pl.multiple_of(t0, QB), QB), :], qT_ref[...])
          + jnp.concatenate([tri] * G, axis=1) + shift)

    nv = KCH // LANES
    cpg = BIAS_BLOCKS // (KCH // SLC_BLOCK)
    brows = min(BIAS_BLOCKS, ns)
    nstep = i // ((KCH // QB) * NCHUNK) + 1

    def chunk_scores(c0):
        @pl.when(c0 % cpg == 0)
        def _():
            r0 = pl.multiple_of((c0 // cpg) * brows, brows)
            rows = selb_ref[pl.ds(r0, brows), :].astype(BF16)
            qT_ref[DH:DH + brows, :] = jnp.concatenate([rows] * G, axis=1)

        qT = qT_ref[...]
        return [_dot(ks_ref[pl.ds(pl.multiple_of((c0 + u) * KCH, KCH), KCH), :], qT)
                for u in range(NCHUNK)]

    def values(c):
        return jnp.concatenate([vsT_ref[c * nv + w] for w in range(nv)], axis=1)

    def finish(acc):
        os_ref[...] = acc[0:DH, :] * (1.0 / acc[DH:DH + 1, :])

    @pl.when(use_bound)
    def _():
        def step(grp, acc):
            ss = chunk_scores(grp * NCHUNK)
            for u in range(NCHUNK):
                acc = acc + _dot(values(grp * NCHUNK + u), jnp.exp2(ss[u]).astype(BF16))
            return acc

        acc0 = _dot(vsT_ref[i], jnp.exp2(sd).astype(BF16))
        finish(lax.fori_loop(0, nstep, step, acc0))

    @pl.when(jnp.logical_not(use_bound))
    def _():
        def step(grp, carry):
            m, acc = carry
            ss = chunk_scores(grp * NCHUNK)
            for u in range(NCHUNK):
                s = ss[u]
                m_new = jnp.maximum(m, jnp.max(s, axis=0, keepdims=True))
                p = jnp.exp2(s - m_new).astype(BF16)
                acc = jnp.exp2(m - m_new) * acc + _dot(values(grp * NCHUNK + u), p)
                m = m_new
            return m, acc

        m0 = jnp.max(sd, axis=0, keepdims=True)
        acc0 = _dot(vsT_ref[i], jnp.exp2(sd - m0).astype(BF16))
        finish(lax.fori_loop(0, nstep, step, (m0, acc0))[1])

    os_ = os_ref[...]

    wc = jnp.maximum(i - WINDOW // QB, 0)
    ws = wc * QB
    sw = _dot(kw_ref[pl.ds(pl.multiple_of(ws, QB), wspan), :], qT_ref[...])
    kpos = ws + iota((wspan, LANES), 0)
    t_w = t0 + iota((wspan, LANES), 1)
    wb = jnp.where(kpos <= t_w, jnp.where(kpos > t_w - WINDOW, 0.0, NEG), NEG)
    sw = sw + jnp.concatenate([wb] * G, axis=1)
    p_w = jnp.exp2(sw - jnp.max(sw, axis=0, keepdims=True)).astype(BF16)
    vw = jnp.concatenate([vwT_ref[wc + u] for u in range(wspan // LANES)], axis=1)
    acc_w = _dot(vw, p_w)
    ow = acc_w[0:DH, :] * (1.0 / acc_w[DH:DH + 1, :])

    outs = []
    for g in range(G):
        sl = slice(g * QB, (g + 1) * QB)
        row = h * (3 * G) + 3 * g
        gc = gT_ref[pl.ds(row, 1), :]
        gs = gT_ref[pl.ds(row + 1, 1), :]
        gw = gT_ref[pl.ds(row + 2, 1), :]
        outs.append(gc * oc_ref[:, sl] + gs * os_[:, sl] + gw * ow[:, sl])
    y_ref[...] = jnp.concatenate(outs, axis=0).T


def _score_bound(qgain, kgain_slc):
    m = 1.05 * DH * jnp.max(jnp.abs(qgain)) * jnp.max(jnp.abs(kgain_slc))
    m = m.astype(BF16).astype(F32)
    return jnp.stack([m, (m <= SAFE_SHIFT).astype(F32)])


def _attention(sc, q, kc, vcT, kslc, vslcT, kwin, vwinT, gT, B, S):
    NC = S // CMP_STRIDE
    NS = S // SLC_BLOCK
    NQ = S // QB
    cch = min(CCH, NC)
    assert NC % cch == 0 and (S // KCH) % NCHUNK == 0
    head = lambda *blk: pl.BlockSpec((None, None) + blk, lambda b, h, i: (b, h) + (0,) * len(blk))
    return pl.pallas_call(
        functools.partial(_attn_kernel, cch=cch),
        out_shape=jax.ShapeDtypeStruct((B, S, NSA_W), F32),
        grid=(B, HKV, NQ),
        in_specs=[
            pl.BlockSpec(memory_space=pltpu.SMEM),
            pl.BlockSpec((None, QB, G * DH), lambda b, h, i: (b, i, h)),
            head(NC, LANES), head(DH, NC),
            head(S, LANES), head(S // LANES, VROWS, LANES),
            head(S, LANES), head(S // LANES, VROWS, LANES),
            pl.BlockSpec((None, GATE_ROWS, QB), lambda b, h, i: (b, 0, i)),
        ],
        out_specs=pl.BlockSpec((None, QB, G * DH), lambda b, h, i: (b, i, h)),
        scratch_shapes=[pltpu.VMEM((LANES, G * QB), BF16), pltpu.VMEM((NC + 8, LANES), F32),
                        pltpu.VMEM((NS, LANES), F32), pltpu.VMEM((DH, G * QB), F32),
                        pltpu.VMEM((DH, G * QB), F32)],
        compiler_params=pltpu.CompilerParams(
            dimension_semantics=("arbitrary", "arbitrary", "arbitrary"),
            vmem_limit_bytes=VMEM_LIMIT),
        name="attention",
    )(sc, q, kc, vcT, kslc, vslcT, kwin, vwinT, gT)


XH = 8
YH = 32


def _seq_mix_kernel(lru_ref, cv_ref, cw_ref, cb_ref, wa_ref, ba_ref, wx_ref, bx_ref, lam_ref,
                    dw_ref, db_ref, lng_ref, lnb_ref, gl_ref, gc_ref,
                    yl_ref, yc_ref, xbuf, ybuf, hbuf):
    t = pl.program_id(1)

    @pl.when(t == 0)
    def _():
        xbuf[0:XH, :] = jnp.zeros((XH, LRU_W), F32)
        ybuf[0:YH, :] = jnp.zeros((YH, CV_W), F32)
        hbuf[...] = jnp.zeros((8, LRU_W), F32)

    xb = lru_ref[:, 0:LRU_W]
    gb = lru_ref[:, LRU_W:2 * LRU_W]
    xbuf[XH:XH + TS, :] = xb
    xr = jnp.broadcast_to(cb_ref[...], (TS, LRU_W))
    for k in range(LRU_CONV):
        xr = xr + cw_ref[k:k + 1, :] * xbuf[XH - (LRU_CONV - 1) + k:XH - (LRU_CONV - 1) + k + TS, :]
    xbuf[0:XH, :] = xbuf[TS:TS + XH, :]

    xr16 = xr.astype(BF16)
    r = _sigmoid(_dot(xr16, wa_ref[...]) + ba_ref[...])
    ig = _sigmoid(_dot(xr16, wx_ref[...]) + bx_ref[...])
    nl = -lam_ref[...]
    softplus = jnp.maximum(nl, 0.0) + jnp.log1p(jnp.exp(-jnp.abs(nl)))
    log_a = -LRU_C * r * softplus
    a = jnp.exp(log_a)
    u = xr * ig * jnp.sqrt(-jnp.tanh(log_a) * (a * a + 1.0))

    row = lax.broadcasted_iota(jnp.int32, (TS, LRU_W), 0)
    d = 1
    while d < TS:
        keep = row >= d
        a_sh = jnp.where(keep, pltpu.roll(a, d, 0), 1.0)
        u_sh = jnp.where(keep, pltpu.roll(u, d, 0), 0.0)
        u = a * u_sh + u
        a = a * a_sh
        d *= 2
    hseq = a * hbuf[0:1, :] + u
    hbuf[...] = jnp.broadcast_to(hseq[TS - 1:TS, :], (8, LRU_W))
    yl = hseq * _gelu_tanh(gb)
    yl = yl * lax.rsqrt(jnp.mean(yl * yl, axis=-1, keepdims=True) + EPS) * gl_ref[...]
    yl_ref[...] = yl.astype(BF16)

    y = cv_ref[:, 0:CV_W] * _sigmoid(cv_ref[:, CV_W:2 * CV_W])
    ybuf[YH:YH + TS, :] = y
    c = jnp.broadcast_to(db_ref[...], (TS, CV_W))
    for k in range(CV_KERNEL):
        off = YH - (CV_KERNEL - 1) + k
        c = c + dw_ref[k:k + 1, :] * ybuf[off:off + TS, :]
    ybuf[0:YH, :] = ybuf[TS:TS + YH, :]
    mu = jnp.mean(c, axis=-1, keepdims=True)
    cc = c - mu
    var = jnp.mean(cc * cc, axis=-1, keepdims=True)
    ln = cc * lax.rsqrt(var + EPS) * lng_ref[...] + lnb_ref[...]
    yc = ln * _sigmoid(ln)
    yc = yc * lax.rsqrt(jnp.mean(yc * yc, axis=-1, keepdims=True) + EPS) * gc_ref[...]
    yc_ref[...] = yc.astype(BF16)


def _seq_mix(lru, cv, params, B, S):
    const = lambda a: pl.BlockSpec(a.shape, lambda b, t: (0,) * a.ndim)
    tile = lambda w_: pl.BlockSpec((None, TS, w_), lambda b, t: (b, t, 0))
    return pl.pallas_call(
        _seq_mix_kernel,
        out_shape=(jax.ShapeDtypeStruct((B, S, LRU_W), BF16),
                   jax.ShapeDtypeStruct((B, S, CV_W), BF16)),
        grid=(B, S // TS),
        in_specs=[tile(2 * LRU_W), tile(2 * CV_W)] + [const(p) for p in params],
        out_specs=(tile(LRU_W), tile(CV_W)),
        scratch_shapes=[pltpu.VMEM((TS + XH, LRU_W), F32), pltpu.VMEM((TS + YH, CV_W), F32),
                        pltpu.VMEM((8, LRU_W), F32)],
        compiler_params=pltpu.CompilerParams(
            dimension_semantics=("arbitrary", "arbitrary"), vmem_limit_bytes=VMEM_LIMIT),
        name="seq_mix",
    )(lru, cv, *params)


def _out_mlp_kernel(x_ref, ya_ref, yl_ref, yc_ref, ga_ref, wo_ref, gm_ref, w1_ref, w2_ref, o_ref):
    ya = ya_ref[...]
    ya = (ya * lax.rsqrt(jnp.mean(ya * ya, axis=-1, keepdims=True) + EPS) * ga_ref[...]).astype(BF16)
    x1 = (x_ref[...] + _dot(ya, wo_ref[0:NSA_W, :])
          + _dot(yl_ref[...], wo_ref[NSA_W:NSA_W + LRU_W, :])
          + _dot(yc_ref[...], wo_ref[NSA_W + LRU_W:NSA_W + LRU_W + CV_W, :]))
    hm = (x1 * lax.rsqrt(jnp.mean(x1 * x1, axis=-1, keepdims=True) + EPS) * gm_ref[...]).astype(BF16)
    d_ff = w1_ref.shape[1]
    fc = 1024
    o_ref[...] = x1
    for c in range(d_ff // fc):
        hc = jnp.maximum(_dot(hm, w1_ref[:, c * fc:(c + 1) * fc]), 0.0)
        o_ref[...] += _dot((hc * hc).astype(BF16), w2_ref[c * fc:(c + 1) * fc, :])


def _out_mlp(x2, ya, yl, yc, ga, wo, gm, w1, w2):
    T, D = x2.shape
    tok = lambda w_: pl.BlockSpec((TM, w_), lambda t: (t, 0))
    const1 = lambda a: pl.BlockSpec(a.shape, lambda t: (0,) * a.ndim, pipeline_mode=pl.Buffered(1))
    return pl.pallas_call(
        _out_mlp_kernel,
        out_shape=jax.ShapeDtypeStruct((T, D), F32),
        grid=(T // TM,),
        in_specs=[tok(D), tok(NSA_W), tok(LRU_W), tok(CV_W),
                  const1(ga), const1(wo), const1(gm), const1(w1), const1(w2)],
        out_specs=tok(D),
        compiler_params=pltpu.CompilerParams(
            dimension_semantics=("arbitrary",), vmem_limit_bytes=VMEM_LIMIT),
        name="out_mlp",
    )(x2, ya, yl, yc, ga, wo, gm, w1, w2)


def _block_ones(n):
    idx = np.arange(n) // DH
    return jnp.asarray((idx[:, None] == idx[None, :]).astype(np.float32) / DH, dtype=BF16)


def _permute_w_in(w):
    off_kv = NSA_W
    kv = [w[:, off_kv + c * 128: off_kv + (c + 1) * 128] for c in range(6)]
    off_gate = off_kv + 6 * 128
    ngate = 3 * HKV * G
    off_lru = off_gate + ngate
    gate = jnp.pad(w[:, off_gate:off_lru], ((0, 0), (0, LANES - ngate)))
    cols = [w[:, 0:NSA_W], kv[0], kv[1], kv[2], kv[4], kv[3], kv[5],
            w[:, off_lru:off_lru + 2 * LRU_W + 2 * CV_W], gate]
    return jnp.concatenate(cols, axis=1).astype(BF16)


def _compress_weights(pos, w1, w2):
    slot_kv = jnp.array([0, 0, 1, 1])
    eye = jnp.eye(4, dtype=F32)
    w1r = w1.reshape(2, 2, CMP_STRIDE, DH, CMP_HID)[slot_kv]
    wbig = jnp.einsum('shpdo,st->psdhto', w1r, eye).reshape(CMP_STRIDE * 4 * DH, 2 * 4 * CMP_HID)
    posr = pos.reshape(2, 2, CMP_STRIDE, DH)[slot_kv]
    pos2 = jnp.transpose(posr, (1, 2, 0, 3)).reshape(2, CMP_STRIDE * 4 * DH)
    pos2 = jnp.pad(pos2, ((0, 6), (0, 0)))
    w2big = jnp.einsum('sod,st->sotd', w2[slot_kv], eye).reshape(4 * CMP_HID, 4 * DH)
    return pos2.astype(BF16), wbig.astype(BF16), w2big.astype(BF16)


def _block_diag(w):
    hh, bw, _ = w.shape
    eye = jnp.eye(hh, dtype=w.dtype)
    return jnp.einsum('hij,hg->higj', w, eye).reshape(hh * bw, hh * bw).astype(BF16)


def kernel(x, attn_norm, w_in, q_norm, k_norm, cmp_pos, cmp_w1, cmp_w2, lru_conv_w, lru_conv_b,
           lru_wa, lru_ba, lru_wx, lru_bx, lru_lambda, cv_dw_w, cv_dw_b, cv_ln_g, cv_ln_b,
           out_norm, w_out, mlp_norm, mlp_w1, mlp_w2):
    B, S, D = x.shape
    depth = w_in.shape[0]
    assert S % TM == 0 and S % TS == 0 and S >= WINDOW + QB and S // SLC_BLOCK >= N_SEL
    NC = S // CMP_STRIDE
    row = lambda v: v.reshape(1, -1).astype(F32)
    bd512 = _block_ones(NSA_W)
    bd128 = _block_ones(128)

    x2 = x.reshape(B * S, D)
    for l in range(depth):
        qgain = row(jnp.tile(q_norm[l], HKV * G)) * (DH ** -0.5 * LOG2E)
        kgain = row(jnp.concatenate([jnp.tile(k_norm[l, 1], HKV), jnp.tile(k_norm[l, 2], HKV)]))
        q, cmpr, kslc, kwin, vslcT, vwinT, lru, cv, gT = _in_proj(
            x2, row(attn_norm[l]), _permute_w_in(w_in[l]), bd512, qgain, kgain, B, S)

        pos2, wbig, w2big = _compress_weights(cmp_pos[l], cmp_w1[l], cmp_w2[l])
        kc, vcT = _compress(cmpr.reshape(B, NC, CMP_STRIDE * 256), pos2, wbig, w2big, bd128,
                            row(jnp.tile(k_norm[l, 0], HKV)), B, NC)

        y_attn = _attention(_score_bound(qgain, kgain[:, 0:HKV * DH]), q.reshape(B, S, NSA_W),
                            kc, vcT, kslc, vslcT, kwin, vwinT, gT, B, S)

        g_out = out_norm[l]
        seq_params = (lru_conv_w[l], row(lru_conv_b[l]), _block_diag(lru_wa[l]), row(lru_ba[l]),
                      _block_diag(lru_wx[l]), row(lru_bx[l]), row(lru_lambda[l]),
                      cv_dw_w[l], row(cv_dw_b[l]), row(cv_ln_g[l]), row(cv_ln_b[l]),
                      row(g_out[NSA_W:NSA_W + LRU_W]), row(g_out[NSA_W + LRU_W:]))
        yl, yc = _seq_mix(lru.reshape(B, S, 2 * LRU_W), cv.reshape(B, S, 2 * CV_W), seq_params, B, S)

        x2 = _out_mlp(x2, y_attn.reshape(B * S, NSA_W), yl.reshape(B * S, LRU_W),
                      yc.reshape(B * S, CV_W), row(g_out[:NSA_W]), w_out[l].astype(BF16),
                      row(mlp_norm[l]), mlp_w1[l].astype(BF16), mlp_w2[l].astype(BF16))
    return x2.reshape(B, S, D)
```

```python
import functools

import numpy as np
import jax
import jax.numpy as jnp
from jax import lax
from jax.experimental import pallas as pl
from jax.experimental.pallas import tpu as pltpu

F32 = jnp.float32
BF16 = jnp.bfloat16

EPS = 1e-6
NEG = -1e30
FORCED = 1e6
DH = 64
HKV = 2
G = 4
NSA_W = HKV * G * DH
CMP_BLOCK = 32
CMP_STRIDE = 16
CMP_HID = 128
SLC_BLOCK = 64
N_SEL = 16
WINDOW = 512
QB = 256
LRU_W = 256
LRU_HEADS = 8
LRU_CONV = 4
LRU_C = 8.0
CV_W = 256
CV_KERNEL = 31
LOG2E = 1.4426950408889634
SAFE_SHIFT = 50.0

LANES = 128
KCH = 256
NCHUNK = 8
CCH = 256
TM = 512
TS = 512
VMEM_LIMIT = 48 * 1024 * 1024

C_Q = 0
C_CMP = C_Q + NSA_W
C_KSLC = C_CMP + 256
C_KWIN = C_KSLC + 128
C_VSLC = C_KWIN + 128
C_VWIN = C_VSLC + 128
C_LRU = C_VWIN + 128
C_CV = C_LRU + 2 * LRU_W
C_GATE = C_CV + 2 * CV_W
N_INP = C_GATE + LANES
GATE_ROWS = 32
VROWS = DH + 16
BIAS_BLOCKS = LANES - DH


def _dot(a, b):
    return jnp.dot(a, b, preferred_element_type=F32)


def _head_rms_scale(z, bd):
    sq = z * z
    hi = sq.astype(BF16)
    lo = (sq - hi.astype(F32)).astype(BF16)
    ms = _dot(hi, bd) + _dot(lo, bd)
    return lax.rsqrt(ms + EPS)


def _gelu_tanh(x):
    return 0.5 * x * (1.0 + jnp.tanh(0.7978845608028654 * (x + 0.044715 * (x * x * x))))


def _sigmoid(x):
    return 1.0 / (1.0 + jnp.exp(-x))


def _in_proj_kernel(x_ref, g_ref, w_ref, bd_ref, qgain_ref, kgain_ref,
                    q_ref, cmp_ref, kslc_ref, kwin_ref, vslcT_ref, vwinT_ref,
                    lru_ref, cv_ref, gT_ref, *, nt):
    x = x_ref[...]
    ms = jnp.mean(x * x, axis=-1, keepdims=True)
    hn = (x * lax.rsqrt(ms + EPS) * g_ref[...]).astype(BF16)

    zq = _dot(hn, w_ref[:, C_Q:C_CMP])
    q_ref[...] = (zq * _head_rms_scale(zq, bd_ref[...]) * qgain_ref[...]).astype(BF16)

    cmp_ref[...] = _dot(hn, w_ref[:, C_CMP:C_KSLC]).astype(BF16)

    zk = _dot(hn, w_ref[:, C_KSLC:C_VSLC])
    kn = zk * _head_rms_scale(zk, bd_ref[0:256, 0:256]) * kgain_ref[...]
    lane = lax.broadcasted_iota(jnp.int32, (TM, LANES), 1)
    rowg = (pl.program_id(0) % nt) * TM + lax.broadcasted_iota(jnp.int32, (TM, LANES), 0)
    blk = lax.shift_right_logical(rowg, 6) & (BIAS_BLOCKS - 1)
    onehot = jnp.where(lane == DH + blk, 1.0, 0.0)
    lo = lane < DH
    ks, kw = kn[:, 0:128], kn[:, 128:256]
    kslc_ref[0] = jnp.where(lo, ks, onehot).astype(BF16)
    kslc_ref[1] = jnp.where(lo, pltpu.roll(ks, DH, 1), onehot).astype(BF16)
    kwin_ref[0] = jnp.where(lo, kw, 0.0).astype(BF16)
    kwin_ref[1] = jnp.where(lo, pltpu.roll(kw, DH, 1), 0.0).astype(BF16)

    zv = _dot(hn, w_ref[:, C_VSLC:C_LRU])
    zvT = zv.T
    ones_row = jnp.where(lax.broadcasted_iota(jnp.int32, (VROWS - DH, LANES), 0) == 0,
                         1.0, 0.0).astype(BF16)
    for j in range(TM // LANES):
        for h in range(HKV):
            cols = slice(j * LANES, (j + 1) * LANES)
            vslcT_ref[h, j, 0:DH, :] = zvT[h * DH:(h + 1) * DH, cols].astype(BF16)
            vslcT_ref[h, j, DH:VROWS, :] = ones_row
            vwinT_ref[h, j, 0:DH, :] = zvT[128 + h * DH:128 + (h + 1) * DH, cols].astype(BF16)
            vwinT_ref[h, j, DH:VROWS, :] = ones_row

    lru_ref[...] = _dot(hn, w_ref[:, C_LRU:C_CV])
    cv_ref[...] = _dot(hn, w_ref[:, C_CV:C_GATE])

    zg = _sigmoid(_dot(hn, w_ref[:, C_GATE:N_INP]))
    gT_ref[...] = zg.T[0:GATE_ROWS, :]


def _in_proj(x2, g, w, bd, qgain, kgain, B, S):
    T, D = x2.shape
    nt = S // TM
    tok = lambda w_: pl.BlockSpec((TM, w_), lambda t: (t, 0))
    const = lambda a: pl.BlockSpec(a.shape, lambda t: (0,) * a.ndim)
    vT_spec = pl.BlockSpec((None, HKV, TM // LANES, VROWS, LANES),
                           lambda t: (t // nt, 0, t % nt, 0, 0))
    k_spec = pl.BlockSpec((None, HKV, TM, LANES), lambda t: (t // nt, 0, t % nt, 0))
    out_shape = (
        jax.ShapeDtypeStruct((T, NSA_W), BF16),
        jax.ShapeDtypeStruct((T, 256), BF16),
        jax.ShapeDtypeStruct((B, HKV, S, LANES), BF16),
        jax.ShapeDtypeStruct((B, HKV, S, LANES), BF16),
        jax.ShapeDtypeStruct((B, HKV, S // LANES, VROWS, LANES), BF16),
        jax.ShapeDtypeStruct((B, HKV, S // LANES, VROWS, LANES), BF16),
        jax.ShapeDtypeStruct((T, 2 * LRU_W), F32),
        jax.ShapeDtypeStruct((T, 2 * CV_W), F32),
        jax.ShapeDtypeStruct((B, GATE_ROWS, S), F32),
    )
    out_specs = (
        tok(NSA_W), tok(256), k_spec, k_spec, vT_spec, vT_spec,
        tok(2 * LRU_W), tok(2 * CV_W),
        pl.BlockSpec((None, GATE_ROWS, TM), lambda t: (t // nt, 0, t % nt)),
    )
    return pl.pallas_call(
        functools.partial(_in_proj_kernel, nt=nt),
        out_shape=out_shape,
        grid=(T // TM,),
        in_specs=[tok(D), const(g), const(w), const(bd), const(qgain), const(kgain)],
        out_specs=out_specs,
        compiler_params=pltpu.CompilerParams(
            dimension_semantics=("arbitrary",), vmem_limit_bytes=VMEM_LIMIT),
        name="in_proj",
    )(x2, g, w, bd, qgain, kgain)


def _compress_kernel(x_ref, pos_ref, wbig_ref, w2_ref, bd_ref, kgain_ref, kc_ref, vcT_ref):
    nc = x_ref.shape[0]
    half = 4 * CMP_HID
    p = _dot(x_ref[...], wbig_ref[...])
    pc = _dot(pos_ref[...], wbig_ref[...])
    const = pc[0:1, 0:half] + pc[1:2, half:2 * half]
    nxt = pltpu.roll(p[:, half:2 * half], nc - 1, 0)
    hid = _gelu_tanh(p[:, 0:half] + nxt + const).astype(BF16)
    kv = _dot(hid, w2_ref[...])
    kc = kv[:, 0:128]
    kc = kc * _head_rms_scale(kc, bd_ref[...]) * kgain_ref[...]
    lo = lax.broadcasted_iota(jnp.int32, (nc, LANES), 1) < DH
    kc_ref[0] = jnp.where(lo, kc, 0.0).astype(BF16)
    kc_ref[1] = jnp.where(lo, pltpu.roll(kc, DH, 1), 0.0).astype(BF16)
    vT = kv[:, 128:256].T
    for h in range(HKV):
        vcT_ref[h] = vT[h * DH:(h + 1) * DH, :].astype(BF16)


def _compress(cmpx, pos2, wbig, w2big, bd128, kgain, B, NC):
    const = lambda a: pl.BlockSpec(a.shape, lambda b: (0,) * a.ndim, pipeline_mode=pl.Buffered(1))
    return pl.pallas_call(
        _compress_kernel,
        out_shape=(jax.ShapeDtypeStruct((B, HKV, NC, LANES), BF16),
                   jax.ShapeDtypeStruct((B, HKV, DH, NC), BF16)),
        grid=(B,),
        in_specs=[pl.BlockSpec((None, NC, cmpx.shape[2]), lambda b: (b, 0, 0)),
                  const(pos2), const(wbig), const(w2big), const(bd128), const(kgain)],
        out_specs=(pl.BlockSpec((None, HKV, NC, LANES), lambda b: (b, 0, 0, 0)),
                   pl.BlockSpec((None, HKV, DH, NC), lambda b: (b, 0, 0, 0))),
        compiler_params=pltpu.CompilerParams(
            dimension_semantics=("arbitrary",), vmem_limit_bytes=VMEM_LIMIT),
        name="compress",
    )(cmpx, pos2, wbig, w2big, bd128, kgain)


def _attn_kernel(sc_ref, q_ref, kc_ref, vcT_ref, ks_ref, vsT_ref, kw_ref, vwT_ref, gT_ref,
                 y_ref, qT_ref, imp_ref, selb_ref, selo_ref, oc_ref, os_ref, *, cch):
    h = pl.program_id(1)
    i = pl.program_id(2)
    nc = kc_ref.shape[0]
    ns = selb_ref.shape[0]
    wspan = WINDOW + QB
    t0 = i * QB
    iota = lambda shape, ax: lax.broadcasted_iota(jnp.int32, shape, ax)
    use_bound = sc_ref[1] > 0.5
    shift = jnp.where(use_bound, -sc_ref[0], 0.0)

    qfT = q_ref[...].astype(F32).T
    for g in range(G):
        qT_ref[0:DH, g * QB:(g + 1) * QB] = qfT[g * DH:(g + 1) * DH, :].astype(BF16)
    qT_ref[DH:LANES, :] = jnp.zeros((LANES - DH, G * QB), BF16)

    def front(nr):
        nsu = nr // (SLC_BLOCK // CMP_STRIDE)
        cmask = iota((nr, QB), 0) * CMP_STRIDE + (CMP_BLOCK - 1) <= t0 + iota((nr, QB), 1)
        cb = jnp.where(cmask, 0.0, NEG)
        s = _dot(kc_ref[0:nr, :], qT_ref[...]) + jnp.concatenate([cb] * G, axis=1)
        p = jnp.exp2(s - jnp.max(s, axis=0, keepdims=True))
        l = jnp.sum(p, axis=0, keepdims=True)
        anyv = jnp.where(t0 + iota((1, QB), 1) >= CMP_BLOCK - 1, 1.0, 0.0)
        pn = p * (jnp.concatenate([anyv] * G, axis=1) / l)
        oc_ref[...] = _dot(vcT_ref[:, 0:nr], pn.astype(BF16))
        imp = pn[:, 0:QB]
        for g in range(1, G):
            imp = imp + pn[:, g * QB:(g + 1) * QB]

        parts = []
        for w in range(QB // LANES):
            imp_ref[w, 0:8, :] = jnp.zeros((8, LANES), F32)
            imp_ref[w, 8:8 + nr, :] = imp[:, w * LANES:(w + 1) * LANES]
            acc = imp_ref[w, pl.ds(7, nsu, stride=4), :]
            for r in range(4):
                acc = acc + imp_ref[w, pl.ds(8 + r, nsu, stride=4), :]
            parts.append(acc)
        islc = jnp.concatenate(parts, axis=1)

        j_i = iota((nsu, QB), 0)
        t_s = t0 + iota((nsu, QB), 1)
        cur = lax.shift_right_logical(t_s, 6)
        valid = j_i * SLC_BLOCK <= t_s
        ninf = -jnp.inf
        score = jnp.where(j_i == 0, ninf, jnp.where(j_i == cur, ninf,
                                                    jnp.where(j_i == cur - 1, ninf, islc)))
        score = jnp.where(valid, score, NEG)
        j_f = j_i.astype(F32)
        for _ in range(N_SEL - 3):
            mx = jnp.max(score, axis=0, keepdims=True)
            jm = jnp.min(jnp.where(score == mx, j_f, float(nsu)), axis=0, keepdims=True)
            score = jnp.where(j_f == jm, -jnp.inf, score)
        own = lax.shift_right_logical(j_i, (QB // SLC_BLOCK).bit_length() - 1) == i
        picked = jnp.where(valid, jnp.where(score == -jnp.inf, shift, NEG), NEG)
        selo_ref[0:nsu, :] = picked
        selb_ref[0:nsu, :] = jnp.where(own, NEG, picked)

    cls = (i * (QB // CMP_STRIDE) + (QB - CMP_BLOCK) // CMP_STRIDE) // cch
    for kk in range(nc // cch):
        pl.when(cls == kk)(functools.partial(front, (kk + 1) * cch))

    bpt = QB // SLC_BLOCK
    own_b = jnp.concatenate(
        [jnp.broadcast_to(selo_ref[pl.ds(i * bpt + b, 1), :], (SLC_BLOCK, QB)) for b in range(bpt)],
        axis=0)
    own_b = jnp.where(iota((QB, QB), 0) <= iota((QB, QB), 1), own_b, NEG)
    own_v = jnp.concatenate([vsT_ref[i * (QB // LANES) + w] for w in range(QB // LANES)], axis=1)
    sd = (_dot(ks_ref[pl.ds(pl.multiple_of(t0, QB), QB), :], qT_ref[...])
          + jnp.concatenate([own_b] * G, axis=1))

    nv = KCH // LANES
    cpg = BIAS_BLOCKS // (KCH // SLC_BLOCK)
    brows = min(BIAS_BLOCKS, ns)
    nstep = i // ((KCH // QB) * NCHUNK) + 1

    def chunk_scores(c0):
        @pl.when(c0 % cpg == 0)
        def _():
            r0 = pl.multiple_of((c0 // cpg) * brows, brows)
            rows = selb_ref[pl.ds(r0, brows), :].astype(BF16)
            qT_ref[DH:DH + brows, :] = jnp.concatenate([rows] * G, axis=1)

        qT = qT_ref[...]
        return [_dot(ks_ref[pl.ds(pl.multiple_of((c0 + u) * KCH, KCH), KCH), :], qT)
                for u in range(NCHUNK)]

    def values(c):
        return jnp.concatenate([vsT_ref[c * nv + w] for w in range(nv)], axis=1)

    def finish(acc):
        os_ref[...] = acc[0:DH, :] * (1.0 / acc[DH:DH + 1, :])

    @pl.when(use_bound)
    def _():
        def step(grp, acc):
            ss = chunk_scores(grp * NCHUNK)
            for u in range(NCHUNK):
                acc = acc + _dot(values(grp * NCHUNK + u), jnp.exp2(ss[u]).astype(BF16))
            return acc

        acc0 = _dot(own_v, jnp.exp2(sd).astype(BF16))
        finish(lax.fori_loop(0, nstep, step, acc0))

    @pl.when(jnp.logical_not(use_bound))
    def _():
        def step(grp, carry):
            m, acc = carry
            ss = chunk_scores(grp * NCHUNK)
            for u in range(NCHUNK):
                s = ss[u]
                m_new = jnp.maximum(m, jnp.max(s, axis=0, keepdims=True))
                p = jnp.exp2(s - m_new).astype(BF16)
                acc = jnp.exp2(m - m_new) * acc + _dot(values(grp * NCHUNK + u), p)
                m = m_new
            return m, acc

        m0 = jnp.max(sd, axis=0, keepdims=True)
        acc0 = _dot(own_v, jnp.exp2(sd - m0).astype(BF16))
        finish(lax.fori_loop(0, nstep, step, (m0, acc0))[1])

    os_ = os_ref[...]

    wc = jnp.maximum(i * (QB // LANES) - WINDOW // LANES, 0)
    ws = wc * LANES
    sw = _dot(kw_ref[pl.ds(pl.multiple_of(ws, LANES), wspan), :], qT_ref[...])
    kpos = ws + iota((wspan, QB), 0)
    t_w = t0 + iota((wspan, QB), 1)
    wb = jnp.where(kpos <= t_w, jnp.where(kpos > t_w - WINDOW, 0.0, NEG), NEG)
    sw = sw + jnp.concatenate([wb] * G, axis=1)
    p_w = jnp.exp2(sw - jnp.max(sw, axis=0, keepdims=True)).astype(BF16)
    vw = jnp.concatenate([vwT_ref[wc + u] for u in range(wspan // LANES)], axis=1)
    acc_w = _dot(vw, p_w)
    ow = acc_w[0:DH, :] * (1.0 / acc_w[DH:DH + 1, :])

    outs = []
    for g in range(G):
        sl = slice(g * QB, (g + 1) * QB)
        row = h * (3 * G) + 3 * g
        gc = gT_ref[pl.ds(row, 1), :]
        gs = gT_ref[pl.ds(row + 1, 1), :]
        gw = gT_ref[pl.ds(row + 2, 1), :]
        outs.append(gc * oc_ref[:, sl] + gs * os_[:, sl] + gw * ow[:, sl])
    y_ref[...] = jnp.concatenate(outs, axis=0).T


def _score_bound(qgain, kgain_slc):
    m = 1.05 * DH * jnp.max(jnp.abs(qgain)) * jnp.max(jnp.abs(kgain_slc))
    m = m.astype(BF16).astype(F32)
    return jnp.stack([m, (m <= SAFE_SHIFT).astype(F32)])


def _attention(sc, q, kc, vcT, kslc, vslcT, kwin, vwinT, gT, B, S):
    NC = S // CMP_STRIDE
    NS = S // SLC_BLOCK
    NQ = S // QB
    cch = min(CCH, NC)
    assert NC % cch == 0 and (S // KCH) % NCHUNK == 0
    head = lambda *blk: pl.BlockSpec((None, None) + blk, lambda b, h, i: (b, h) + (0,) * len(blk),
                                     pipeline_mode=pl.Buffered(1))
    return pl.pallas_call(
        functools.partial(_attn_kernel, cch=cch),
        out_shape=jax.ShapeDtypeStruct((B, S, NSA_W), F32),
        grid=(B, HKV, NQ),
        in_specs=[
            pl.BlockSpec(memory_space=pltpu.SMEM),
            pl.BlockSpec((None, QB, G * DH), lambda b, h, i: (b, i, h)),
            head(NC, LANES), head(DH, NC),
            head(S, LANES), head(S // LANES, VROWS, LANES),
            head(S, LANES), head(S // LANES, VROWS, LANES),
            pl.BlockSpec((None, GATE_ROWS, QB), lambda b, h, i: (b, 0, i)),
        ],
        out_specs=pl.BlockSpec((None, QB, G * DH), lambda b, h, i: (b, i, h)),
        scratch_shapes=[pltpu.VMEM((LANES, G * QB), BF16),
                        pltpu.VMEM((QB // LANES, NC + 8, LANES), F32),
                        pltpu.VMEM((NS, QB), F32), pltpu.VMEM((NS, QB), F32),
                        pltpu.VMEM((DH, G * QB), F32),
                        pltpu.VMEM((DH, G * QB), F32)],
        compiler_params=pltpu.CompilerParams(
            dimension_semantics=("arbitrary", "arbitrary", "arbitrary"),
            vmem_limit_bytes=VMEM_LIMIT),
        name="attention",
    )(sc, q, kc, vcT, kslc, vslcT, kwin, vwinT, gT)


XH = 8
YH = 32


def _seq_mix_kernel(lru_ref, cv_ref, cw_ref, cb_ref, wa_ref, ba_ref, wx_ref, bx_ref, lam_ref,
                    dw_ref, db_ref, lng_ref, lnb_ref, gl_ref, gc_ref,
                    yl_ref, yc_ref, xbuf, ybuf, hbuf):
    t = pl.program_id(1)

    @pl.when(t == 0)
    def _():
        xbuf[0:XH, :] = jnp.zeros((XH, LRU_W), F32)
        ybuf[0:YH, :] = jnp.zeros((YH, CV_W), F32)
        hbuf[...] = jnp.zeros((8, LRU_W), F32)

    xb = lru_ref[:, 0:LRU_W]
    gb = lru_ref[:, LRU_W:2 * LRU_W]
    xbuf[XH:XH + TS, :] = xb
    xr = jnp.broadcast_to(cb_ref[...], (TS, LRU_W))
    for k in range(LRU_CONV):
        xr = xr + cw_ref[k:k + 1, :] * xbuf[XH - (LRU_CONV - 1) + k:XH - (LRU_CONV - 1) + k + TS, :]
    xbuf[0:XH, :] = xbuf[TS:TS + XH, :]

    xr16 = xr.astype(BF16)
    r = _sigmoid(_dot(xr16, wa_ref[...]) + ba_ref[...])
    ig = _sigmoid(_dot(xr16, wx_ref[...]) + bx_ref[...])
    nl = -lam_ref[...]
    softplus = jnp.maximum(nl, 0.0) + jnp.log1p(jnp.exp(-jnp.abs(nl)))
    log_a = -LRU_C * r * softplus
    a = jnp.exp(log_a)
    u = xr * ig * jnp.sqrt(-jnp.tanh(log_a) * (a * a + 1.0))

    row = lax.broadcasted_iota(jnp.int32, (TS, LRU_W), 0)
    d = 1
    while d < TS:
        keep = row >= d
        a_sh = jnp.where(keep, pltpu.roll(a, d, 0), 1.0)
        u_sh = jnp.where(keep, pltpu.roll(u, d, 0), 0.0)
        u = a * u_sh + u
        a = a * a_sh
        d *= 2
    hseq = a * hbuf[0:1, :] + u
    hbuf[...] = jnp.broadcast_to(hseq[TS - 1:TS, :], (8, LRU_W))
    yl = hseq * _gelu_tanh(gb)
    yl = yl * lax.rsqrt(jnp.mean(yl * yl, axis=-1, keepdims=True) + EPS) * gl_ref[...]
    yl_ref[...] = yl.astype(BF16)

    y = cv_ref[:, 0:CV_W] * _sigmoid(cv_ref[:, CV_W:2 * CV_W])
    ybuf[YH:YH + TS, :] = y
    c = jnp.broadcast_to(db_ref[...], (TS, CV_W))
    for k in range(CV_KERNEL):
        off = YH - (CV_KERNEL - 1) + k
        c = c + dw_ref[k:k + 1, :] * ybuf[off:off + TS, :]
    ybuf[0:YH, :] = ybuf[TS:TS + YH, :]
    mu = jnp.mean(c, axis=-1, keepdims=True)
    cc = c - mu
    var = jnp.mean(cc * cc, axis=-1, keepdims=True)
    ln = cc * lax.rsqrt(var + EPS) * lng_ref[...] + lnb_ref[...]
    yc = ln * _sigmoid(ln)
    yc = yc * lax.rsqrt(jnp.mean(yc * yc, axis=-1, keepdims=True) + EPS) * gc_ref[...]
    yc_ref[...] = yc.astype(BF16)


def _seq_mix(lru, cv, params, B, S):
    const = lambda a: pl.BlockSpec(a.shape, lambda b, t: (0,) * a.ndim)
    tile = lambda w_: pl.BlockSpec((None, TS, w_), lambda b, t: (b, t, 0))
    return pl.pallas_call(
        _seq_mix_kernel,
        out_shape=(jax.ShapeDtypeStruct((B, S, LRU_W), BF16),
                   jax.ShapeDtypeStruct((B, S, CV_W), BF16)),
        grid=(B, S // TS),
        in_specs=[tile(2 * LRU_W), tile(2 * CV_W)] + [const(p) for p in params],
        out_specs=(tile(LRU_W), tile(CV_W)),
        scratch_shapes=[pltpu.VMEM((TS + XH, LRU_W), F32), pltpu.VMEM((TS + YH, CV_W), F32),
                        pltpu.VMEM((8, LRU_W), F32)],
        compiler_params=pltpu.CompilerParams(
            dimension_semantics=("arbitrary", "arbitrary"), vmem_limit_bytes=VMEM_LIMIT),
        name="seq_mix",
    )(lru, cv, *params)


def _out_mlp_kernel(x_ref, ya_ref, yl_ref, yc_ref, ga_ref, wo_ref, gm_ref, w1_ref, w2_ref, o_ref):
    ya = ya_ref[...]
    ya = (ya * lax.rsqrt(jnp.mean(ya * ya, axis=-1, keepdims=True) + EPS) * ga_ref[...]).astype(BF16)
    x1 = (x_ref[...] + _dot(ya, wo_ref[0:NSA_W, :])
          + _dot(yl_ref[...], wo_ref[NSA_W:NSA_W + LRU_W, :])
          + _dot(yc_ref[...], wo_ref[NSA_W + LRU_W:NSA_W + LRU_W + CV_W, :]))
    hm = (x1 * lax.rsqrt(jnp.mean(x1 * x1, axis=-1, keepdims=True) + EPS) * gm_ref[...]).astype(BF16)
    d_ff = w1_ref.shape[1]
    fc = 1024
    o_ref[...] = x1
    for c in range(d_ff // fc):
        hc = jnp.maximum(_dot(hm, w1_ref[:, c * fc:(c + 1) * fc]), 0.0)
        o_ref[...] += _dot((hc * hc).astype(BF16), w2_ref[c * fc:(c + 1) * fc, :])


def _out_mlp(x2, ya, yl, yc, ga, wo, gm, w1, w2):
    T, D = x2.shape
    tok = lambda w_: pl.BlockSpec((TM, w_), lambda t: (t, 0))
    const1 = lambda a: pl.BlockSpec(a.shape, lambda t: (0,) * a.ndim, pipeline_mode=pl.Buffered(1))
    return pl.pallas_call(
        _out_mlp_kernel,
        out_shape=jax.ShapeDtypeStruct((T, D), F32),
        grid=(T // TM,),
        in_specs=[tok(D), tok(NSA_W), tok(LRU_W), tok(CV_W),
                  const1(ga), const1(wo), const1(gm), const1(w1), const1(w2)],
        out_specs=tok(D),
        compiler_params=pltpu.CompilerParams(
            dimension_semantics=("arbitrary",), vmem_limit_bytes=VMEM_LIMIT),
        name="out_mlp",
    )(x2, ya, yl, yc, ga, wo, gm, w1, w2)


def _block_ones(n):
    idx = np.arange(n) // DH
    return jnp.asarray((idx[:, None] == idx[None, :]).astype(np.float32) / DH, dtype=BF16)


def _permute_w_in(w):
    off_kv = NSA_W
    kv = [w[:, off_kv + c * 128: off_kv + (c + 1) * 128] for c in range(6)]
    off_gate = off_kv + 6 * 128
    ngate = 3 * HKV * G
    off_lru = off_gate + ngate
    gate = jnp.pad(w[:, off_gate:off_lru], ((0, 0), (0, LANES - ngate)))
    cols = [w[:, 0:NSA_W], kv[0], kv[1], kv[2], kv[4], kv[3], kv[5],
            w[:, off_lru:off_lru + 2 * LRU_W + 2 * CV_W], gate]
    return jnp.concatenate(cols, axis=1).astype(BF16)


def _compress_weights(pos, w1, w2):
    slot_kv = jnp.array([0, 0, 1, 1])
    eye = jnp.eye(4, dtype=F32)
    w1r = w1.reshape(2, 2, CMP_STRIDE, DH, CMP_HID)[slot_kv]
    wbig = jnp.einsum('shpdo,st->psdhto', w1r, eye).reshape(CMP_STRIDE * 4 * DH, 2 * 4 * CMP_HID)
    posr = pos.reshape(2, 2, CMP_STRIDE, DH)[slot_kv]
    pos2 = jnp.transpose(posr, (1, 2, 0, 3)).reshape(2, CMP_STRIDE * 4 * DH)
    pos2 = jnp.pad(pos2, ((0, 6), (0, 0)))
    w2big = jnp.einsum('sod,st->sotd', w2[slot_kv], eye).reshape(4 * CMP_HID, 4 * DH)
    return pos2.astype(BF16), wbig.astype(BF16), w2big.astype(BF16)


def _block_diag(w):
    hh, bw, _ = w.shape
    eye = jnp.eye(hh, dtype=w.dtype)
    return jnp.einsum('hij,hg->higj', w, eye).reshape(hh * bw, hh * bw).astype(BF16)


def kernel(x, attn_norm, w_in, q_norm, k_norm, cmp_pos, cmp_w1, cmp_w2, lru_conv_w, lru_conv_b,
           lru_wa, lru_ba, lru_wx, lru_bx, lru_lambda, cv_dw_w, cv_dw_b, cv_ln_g, cv_ln_b,
           out_norm, w_out, mlp_norm, mlp_w1, mlp_w2):
    B, S, D = x.shape
    depth = w_in.shape[0]
    assert S % TM == 0 and S % TS == 0 and S >= WINDOW + QB and S // SLC_BLOCK >= N_SEL
    NC = S // CMP_STRIDE
    row = lambda v: v.reshape(1, -1).astype(F32)
    bd512 = _block_ones(NSA_W)
    bd128 = _block_ones(128)

    x2 = x.reshape(B * S, D)
    for l in range(depth):
        qgain = row(jnp.tile(q_norm[l], HKV * G)) * (DH ** -0.5 * LOG2E)
        kgain = row(jnp.concatenate([jnp.tile(k_norm[l, 1], HKV), jnp.tile(k_norm[l, 2], HKV)]))
        q, cmpr, kslc, kwin, vslcT, vwinT, lru, cv, gT = _in_proj(
            x2, row(attn_norm[l]), _permute_w_in(w_in[l]), bd512, qgain, kgain, B, S)

        pos2, wbig, w2big = _compress_weights(cmp_pos[l], cmp_w1[l], cmp_w2[l])
        kc, vcT = _compress(cmpr.reshape(B, NC, CMP_STRIDE * 256), pos2, wbig, w2big, bd128,
                            row(jnp.tile(k_norm[l, 0], HKV)), B, NC)

        y_attn = _attention(_score_bound(qgain, kgain[:, 0:HKV * DH]), q.reshape(B, S, NSA_W),
                            kc, vcT, kslc, vslcT, kwin, vwinT, gT, B, S)

        g_out = out_norm[l]
        seq_params = (lru_conv_w[l], row(lru_conv_b[l]), _block_diag(lru_wa[l]), row(lru_ba[l]),
                      _block_diag(lru_wx[l]), row(lru_bx[l]), row(lru_lambda[l]),
                      cv_dw_w[l], row(cv_dw_b[l]), row(cv_ln_g[l]), row(cv_ln_b[l]),
                      row(g_out[NSA_W:NSA_W + LRU_W]), row(g_out[NSA_W + LRU_W:]))
        yl, yc = _seq_mix(lru.reshape(B, S, 2 * LRU_W), cv.reshape(B, S, 2 * CV_W), seq_params, B, S)

        x2 = _out_mlp(x2, y_attn.reshape(B * S, NSA_W), yl.reshape(B * S, LRU_W),
                      yc.reshape(B * S, CV_W), row(g_out[:NSA_W]), w_out[l].astype(BF16),
                      row(mlp_norm[l]), mlp_w1[l].astype(BF16), mlp_w2[l].astype(BF16))
    return x2.reshape(B, S, D)
```

```python
import functools

import numpy as np
import jax
import jax.numpy as jnp
from jax import lax
from jax.experimental import pallas as pl
from jax.experimental.pallas import tpu as pltpu

F32 = jnp.float32
BF16 = jnp.bfloat16

EPS = 1e-6
NEG = -1e30
FORCED = 1e6
DH = 64
HKV = 2
G = 4
NSA_W = HKV * G * DH
CMP_BLOCK = 32
CMP_STRIDE = 16
CMP_HID = 128
SLC_BLOCK = 64
N_SEL = 16
WINDOW = 512
QB = 256
LRU_W = 256
LRU_HEADS = 8
LRU_CONV = 4
LRU_C = 8.0
CV_W = 256
CV_KERNEL = 31
LOG2E = 1.4426950408889634
SAFE_SHIFT = 50.0

LANES = 128
KCH = 256
NCHUNK = 8
CCH = 256
TM = 512
TS = 512
VMEM_LIMIT = 48 * 1024 * 1024

C_Q = 0
C_CMP = C_Q + NSA_W
C_KSLC = C_CMP + 256
C_KWIN = C_KSLC + 128
C_VSLC = C_KWIN + 128
C_VWIN = C_VSLC + 128
C_LRU = C_VWIN + 128
C_CV = C_LRU + 2 * LRU_W
C_GATE = C_CV + 2 * CV_W
N_INP = C_GATE + LANES
GATE_ROWS = 32
VROWS = DH + 16
BIAS_BLOCKS = LANES - DH


def _dot(a, b):
    return jnp.dot(a, b, preferred_element_type=F32)


def _head_rms_scale(z, bd):
    sq = z * z
    hi = sq.astype(BF16)
    lo = (sq - hi.astype(F32)).astype(BF16)
    ms = _dot(hi, bd) + _dot(lo, bd)
    return lax.rsqrt(ms + EPS)


def _gelu_tanh(x):
    return 0.5 * x * (1.0 + jnp.tanh(0.7978845608028654 * (x + 0.044715 * (x * x * x))))


def _sigmoid(x):
    return 1.0 / (1.0 + jnp.exp(-x))


def _in_proj_kernel(x_ref, g_ref, w_ref, bd_ref, qgain_ref, kgain_ref,
                    q_ref, cmp_ref, kslc_ref, kwin_ref, vslcT_ref, vwinT_ref,
                    lru_ref, cv_ref, gT_ref, *, nt):
    x = x_ref[...]
    ms = jnp.mean(x * x, axis=-1, keepdims=True)
    hn = (x * lax.rsqrt(ms + EPS) * g_ref[...]).astype(BF16)

    zq = _dot(hn, w_ref[:, C_Q:C_CMP])
    q_ref[...] = (zq * _head_rms_scale(zq, bd_ref[...]) * qgain_ref[...]).astype(BF16)

    cmp_ref[...] = _dot(hn, w_ref[:, C_CMP:C_KSLC]).astype(BF16)

    zk = _dot(hn, w_ref[:, C_KSLC:C_VSLC])
    kn = zk * _head_rms_scale(zk, bd_ref[0:256, 0:256]) * kgain_ref[...]
    lane = lax.broadcasted_iota(jnp.int32, (TM, LANES), 1)
    rowg = (pl.program_id(0) % nt) * TM + lax.broadcasted_iota(jnp.int32, (TM, LANES), 0)
    blk = lax.shift_right_logical(rowg, 6) & (BIAS_BLOCKS - 1)
    onehot = jnp.where(lane == DH + blk, 1.0, 0.0)
    lo = lane < DH
    ks, kw = kn[:, 0:128], kn[:, 128:256]
    kslc_ref[0] = jnp.where(lo, ks, onehot).astype(BF16)
    kslc_ref[1] = jnp.where(lo, pltpu.roll(ks, DH, 1), onehot).astype(BF16)
    kwin_ref[0] = jnp.where(lo, kw, 0.0).astype(BF16)
    kwin_ref[1] = jnp.where(lo, pltpu.roll(kw, DH, 1), 0.0).astype(BF16)

    zv = _dot(hn, w_ref[:, C_VSLC:C_LRU])
    zvT = zv.T
    ones_row = jnp.where(lax.broadcasted_iota(jnp.int32, (VROWS - DH, LANES), 0) == 0,
                         1.0, 0.0).astype(BF16)
    for j in range(TM // LANES):
        for h in range(HKV):
            cols = slice(j * LANES, (j + 1) * LANES)
            vslcT_ref[h, j, 0:DH, :] = zvT[h * DH:(h + 1) * DH, cols].astype(BF16)
            vslcT_ref[h, j, DH:VROWS, :] = ones_row
            vwinT_ref[h, j, 0:DH, :] = zvT[128 + h * DH:128 + (h + 1) * DH, cols].astype(BF16)
            vwinT_ref[h, j, DH:VROWS, :] = ones_row

    lru_ref[...] = _dot(hn, w_ref[:, C_LRU:C_CV])
    cv_ref[...] = _dot(hn, w_ref[:, C_CV:C_GATE])

    zg = _sigmoid(_dot(hn, w_ref[:, C_GATE:N_INP]))
    gT_ref[...] = zg.T[0:GATE_ROWS, :]


def _in_proj(x2, g, w, bd, qgain, kgain, B, S):
    T, D = x2.shape
    nt = S // TM
    tok = lambda w_: pl.BlockSpec((TM, w_), lambda t: (t, 0))
    const = lambda a: pl.BlockSpec(a.shape, lambda t: (0,) * a.ndim)
    vT_spec = pl.BlockSpec((None, HKV, TM // LANES, VROWS, LANES),
                           lambda t: (t // nt, 0, t % nt, 0, 0))
    k_spec = pl.BlockSpec((None, HKV, TM, LANES), lambda t: (t // nt, 0, t % nt, 0))
    out_shape = (
        jax.ShapeDtypeStruct((T, NSA_W), BF16),
        jax.ShapeDtypeStruct((T, 256), BF16),
        jax.ShapeDtypeStruct((B, HKV, S, LANES), BF16),
        jax.ShapeDtypeStruct((B, HKV, S, LANES), BF16),
        jax.ShapeDtypeStruct((B, HKV, S // LANES, VROWS, LANES), BF16),
        jax.ShapeDtypeStruct((B, HKV, S // LANES, VROWS, LANES), BF16),
        jax.ShapeDtypeStruct((T, 2 * LRU_W), F32),
        jax.ShapeDtypeStruct((T, 2 * CV_W), F32),
        jax.ShapeDtypeStruct((B, GATE_ROWS, S), F32),
    )
    out_specs = (
        tok(NSA_W), tok(256), k_spec, k_spec, vT_spec, vT_spec,
        tok(2 * LRU_W), tok(2 * CV_W),
        pl.BlockSpec((None, GATE_ROWS, TM), lambda t: (t // nt, 0, t % nt)),
    )
    return pl.pallas_call(
        functools.partial(_in_proj_kernel, nt=nt),
        out_shape=out_shape,
        grid=(T // TM,),
        in_specs=[tok(D), const(g), const(w), const(bd), const(qgain), const(kgain)],
        out_specs=out_specs,
        compiler_params=pltpu.CompilerParams(
            dimension_semantics=("arbitrary",), vmem_limit_bytes=VMEM_LIMIT),
        name="in_proj",
    )(x2, g, w, bd, qgain, kgain)


def _compress_kernel(x_ref, pos_ref, wbig_ref, w2_ref, bd_ref, kgain_ref, kc_ref, vcT_ref):
    nc = x_ref.shape[0]
    half = 4 * CMP_HID
    p = _dot(x_ref[...], wbig_ref[...])
    pc = _dot(pos_ref[...], wbig_ref[...])
    const = pc[0:1, 0:half] + pc[1:2, half:2 * half]
    nxt = pltpu.roll(p[:, half:2 * half], nc - 1, 0)
    hid = _gelu_tanh(p[:, 0:half] + nxt + const).astype(BF16)
    kv = _dot(hid, w2_ref[...])
    kc = kv[:, 0:128]
    kc = kc * _head_rms_scale(kc, bd_ref[...]) * kgain_ref[...]
    lo = lax.broadcasted_iota(jnp.int32, (nc, LANES), 1) < DH
    kc_ref[0] = jnp.where(lo, kc, 0.0).astype(BF16)
    kc_ref[1] = jnp.where(lo, pltpu.roll(kc, DH, 1), 0.0).astype(BF16)
    vT = kv[:, 128:256].T
    for h in range(HKV):
        vcT_ref[h] = vT[h * DH:(h + 1) * DH, :].astype(BF16)


def _compress(cmpx, pos2, wbig, w2big, bd128, kgain, B, NC):
    const = lambda a: pl.BlockSpec(a.shape, lambda b: (0,) * a.ndim, pipeline_mode=pl.Buffered(1))
    return pl.pallas_call(
        _compress_kernel,
        out_shape=(jax.ShapeDtypeStruct((B, HKV, NC, LANES), BF16),
                   jax.ShapeDtypeStruct((B, HKV, DH, NC), BF16)),
        grid=(B,),
        in_specs=[pl.BlockSpec((None, NC, cmpx.shape[2]), lambda b: (b, 0, 0)),
                  const(pos2), const(wbig), const(w2big), const(bd128), const(kgain)],
        out_specs=(pl.BlockSpec((None, HKV, NC, LANES), lambda b: (b, 0, 0, 0)),
                   pl.BlockSpec((None, HKV, DH, NC), lambda b: (b, 0, 0, 0))),
        compiler_params=pltpu.CompilerParams(
            dimension_semantics=("arbitrary",), vmem_limit_bytes=VMEM_LIMIT),
        name="compress",
    )(cmpx, pos2, wbig, w2big, bd128, kgain)


def _attn_kernel(sc_ref, q_ref, kc_ref, vcT_ref, ks_ref, vsT_ref, kw_ref, vwT_ref, gT_ref,
                 y_ref, qT_ref, imp_ref, selb_ref, selo_ref, oc_ref, os_ref, ow_ref, *, cch):
    h = pl.program_id(1)
    i = pl.program_id(2)
    nc = kc_ref.shape[0]
    ns = selb_ref.shape[0]
    wspan = WINDOW + QB
    t0 = i * QB
    iota = lambda shape, ax: lax.broadcasted_iota(jnp.int32, shape, ax)
    use_bound = sc_ref[1] > 0.5
    shift = jnp.where(use_bound, -sc_ref[0], 0.0)

    qfT = q_ref[...].astype(F32).T
    for g in range(G):
        qT_ref[0:DH, g * QB:(g + 1) * QB] = qfT[g * DH:(g + 1) * DH, :].astype(BF16)
    qT_ref[DH:LANES, :] = jnp.zeros((LANES - DH, G * QB), BF16)

    def front(nr, bounded):
        nsu = nr // (SLC_BLOCK // CMP_STRIDE)
        cmask = iota((nr, QB), 0) * CMP_STRIDE + (CMP_BLOCK - 1) <= t0 + iota((nr, QB), 1)
        cb = jnp.where(cmask, -sc_ref[2] if bounded else 0.0, NEG)
        s = _dot(kc_ref[0:nr, :], qT_ref[...]) + jnp.concatenate([cb] * G, axis=1)
        p = jnp.exp2(s if bounded else s - jnp.max(s, axis=0, keepdims=True))
        l = jnp.sum(p, axis=0, keepdims=True)
        anyv = jnp.where(t0 + iota((1, QB), 1) >= CMP_BLOCK - 1, 1.0, 0.0)
        pn = p * jnp.where(jnp.concatenate([anyv] * G, axis=1) > 0.5, 1.0 / l, 0.0)
        oc_ref[...] = _dot(vcT_ref[:, 0:nr], pn.astype(BF16))
        imp = pn[:, 0:QB]
        for g in range(1, G):
            imp = imp + pn[:, g * QB:(g + 1) * QB]

        parts = []
        for w in range(QB // LANES):
            imp_ref[w, 0:8, :] = jnp.zeros((8, LANES), F32)
            imp_ref[w, 8:8 + nr, :] = imp[:, w * LANES:(w + 1) * LANES]
            acc = imp_ref[w, pl.ds(7, nsu, stride=4), :]
            for r in range(4):
                acc = acc + imp_ref[w, pl.ds(8 + r, nsu, stride=4), :]
            parts.append(acc)
        islc = jnp.concatenate(parts, axis=1)

        j_i = iota((nsu, QB), 0)
        t_s = t0 + iota((nsu, QB), 1)
        cur = lax.shift_right_logical(t_s, 6)
        valid = j_i * SLC_BLOCK <= t_s
        ninf = -jnp.inf
        score = jnp.where(j_i == 0, ninf, jnp.where(j_i == cur, ninf,
                                                    jnp.where(j_i == cur - 1, ninf, islc)))
        score = jnp.where(valid, score, NEG)
        j_f = j_i.astype(F32)
        for _ in range(N_SEL - 3):
            mx = jnp.max(score, axis=0, keepdims=True)
            jm = jnp.min(jnp.where(score == mx, j_f, float(nsu)), axis=0, keepdims=True)
            score = jnp.where(j_f == jm, -jnp.inf, score)
        own = lax.shift_right_logical(j_i, (QB // SLC_BLOCK).bit_length() - 1) == i
        picked = jnp.where(valid, jnp.where(score == -jnp.inf, shift, NEG), NEG)
        selo_ref[0:nsu, :] = picked
        selb_ref[0:nsu, :] = jnp.where(own, NEG, picked)

    cls = (i * (QB // CMP_STRIDE) + (QB - CMP_BLOCK) // CMP_STRIDE) // cch
    for kk in range(nc // cch):
        for bounded in (True, False):
            pl.when(jnp.logical_and(cls == kk, use_bound == bounded))(
                functools.partial(front, (kk + 1) * cch, bounded))

    bpt = QB // SLC_BLOCK
    own_b = jnp.concatenate(
        [jnp.broadcast_to(selo_ref[pl.ds(i * bpt + b, 1), :], (SLC_BLOCK, QB)) for b in range(bpt)],
        axis=0)
    own_b = jnp.where(iota((QB, QB), 0) <= iota((QB, QB), 1), own_b, NEG)
    own_v = jnp.concatenate([vsT_ref[i * (QB // LANES) + w] for w in range(QB // LANES)], axis=1)
    sd = (_dot(ks_ref[pl.ds(pl.multiple_of(t0, QB), QB), :], qT_ref[...])
          + jnp.concatenate([own_b] * G, axis=1))

    nv = KCH // LANES
    cpg = BIAS_BLOCKS // (KCH // SLC_BLOCK)
    brows = min(BIAS_BLOCKS, ns)
    nstep = (i + NCHUNK - 1) // NCHUNK

    def chunk_scores(c0):
        @pl.when(c0 % cpg == 0)
        def _():
            r0 = pl.multiple_of((c0 // cpg) * brows, brows)
            rows = selb_ref[pl.ds(r0, brows), :].astype(BF16)
            qT_ref[DH:DH + brows, :] = jnp.concatenate([rows] * G, axis=1)

        qT = qT_ref[...]
        return [_dot(ks_ref[pl.ds(pl.multiple_of((c0 + u) * KCH, KCH), KCH), :], qT)
                for u in range(NCHUNK)]

    def values(c):
        return jnp.concatenate([vsT_ref[c * nv + w] for w in range(nv)], axis=1)

    def finish(acc):
        os_ref[...] = acc[0:DH, :] * (1.0 / acc[DH:DH + 1, :])

    @pl.when(use_bound)
    def _():
        def step(grp, acc):
            ss = chunk_scores(grp * NCHUNK)
            for u in range(NCHUNK):
                acc = acc + _dot(values(grp * NCHUNK + u), jnp.exp2(ss[u]).astype(BF16))
            return acc

        acc0 = _dot(own_v, jnp.exp2(sd).astype(BF16))
        finish(lax.fori_loop(0, nstep, step, acc0))

    @pl.when(jnp.logical_not(use_bound))
    def _():
        def step(grp, carry):
            m, acc = carry
            ss = chunk_scores(grp * NCHUNK)
            for u in range(NCHUNK):
                s = ss[u]
                m_new = jnp.maximum(m, jnp.max(s, axis=0, keepdims=True))
                p = jnp.exp2(s - m_new).astype(BF16)
                acc = jnp.exp2(m - m_new) * acc + _dot(values(grp * NCHUNK + u), p)
                m = m_new
            return m, acc

        m0 = jnp.max(sd, axis=0, keepdims=True)
        acc0 = _dot(own_v, jnp.exp2(sd - m0).astype(BF16))
        finish(lax.fori_loop(0, nstep, step, (m0, acc0))[1])

    os_ = os_ref[...]

    def window(bounded):
        wc = jnp.maximum(i * (QB // LANES) - WINDOW // LANES, 0)
        ws = wc * LANES
        sw = _dot(kw_ref[pl.ds(pl.multiple_of(ws, LANES), wspan), :], qT_ref[...])
        kpos = ws + iota((wspan, QB), 0)
        t_w = t0 + iota((wspan, QB), 1)
        inside = -sc_ref[3] if bounded else 0.0
        wb = jnp.where(kpos <= t_w, jnp.where(kpos > t_w - WINDOW, inside, NEG), NEG)
        sw = sw + jnp.concatenate([wb] * G, axis=1)
        p_w = jnp.exp2(sw if bounded else sw - jnp.max(sw, axis=0, keepdims=True)).astype(BF16)
        vw = jnp.concatenate([vwT_ref[wc + u] for u in range(wspan // LANES)], axis=1)
        acc_w = _dot(vw, p_w)
        ow_ref[...] = acc_w[0:DH, :] * (1.0 / acc_w[DH:DH + 1, :])

    for bounded in (True, False):
        pl.when(use_bound == bounded)(functools.partial(window, bounded))
    ow = ow_ref[...]

    outs = []
    for g in range(G):
        sl = slice(g * QB, (g + 1) * QB)
        row = h * (3 * G) + 3 * g
        gc = gT_ref[pl.ds(row, 1), :]
        gs = gT_ref[pl.ds(row + 1, 1), :]
        gw = gT_ref[pl.ds(row + 2, 1), :]
        outs.append(gc * oc_ref[:, sl] + gs * os_[:, sl] + gw * ow[:, sl])
    y_ref[...] = jnp.concatenate(outs, axis=0).T


def _score_bound(qgain, kgain_slc, kgain_win, kgain_cmp):
    qmax = jnp.max(jnp.abs(qgain))
    ms = [(1.05 * DH * qmax * jnp.max(jnp.abs(kg))).astype(BF16).astype(F32)
          for kg in (kgain_slc, kgain_cmp, kgain_win)]
    ok = (jnp.maximum(jnp.maximum(ms[0], ms[1]), ms[2]) <= SAFE_SHIFT).astype(F32)
    return jnp.stack([ms[0], ok, ms[1], ms[2]])


def _attention(sc, q, kc, vcT, kslc, vslcT, kwin, vwinT, gT, B, S):
    NC = S // CMP_STRIDE
    NS = S // SLC_BLOCK
    NQ = S // QB
    cch = min(CCH, NC)
    assert NC % cch == 0 and (S // KCH) % NCHUNK == 0 and QB == KCH
    head = lambda *blk: pl.BlockSpec((None, None) + blk, lambda b, h, i: (b, h) + (0,) * len(blk),
                                     pipeline_mode=pl.Buffered(1))
    return pl.pallas_call(
        functools.partial(_attn_kernel, cch=cch),
        out_shape=jax.ShapeDtypeStruct((B, S, NSA_W), F32),
        grid=(B, HKV, NQ),
        in_specs=[
            pl.BlockSpec(memory_space=pltpu.SMEM),
            pl.BlockSpec((None, QB, G * DH), lambda b, h, i: (b, i, h)),
            head(NC, LANES), head(DH, NC),
            head(S, LANES), head(S // LANES, VROWS, LANES),
            head(S, LANES), head(S // LANES, VROWS, LANES),
            pl.BlockSpec((None, GATE_ROWS, QB), lambda b, h, i: (b, 0, i)),
        ],
        out_specs=pl.BlockSpec((None, QB, G * DH), lambda b, h, i: (b, i, h)),
        scratch_shapes=[pltpu.VMEM((LANES, G * QB), BF16),
                        pltpu.VMEM((QB // LANES, NC + 8, LANES), F32),
                        pltpu.VMEM((NS, QB), F32), pltpu.VMEM((NS, QB), F32),
                        pltpu.VMEM((DH, G * QB), F32),
                        pltpu.VMEM((DH, G * QB), F32), pltpu.VMEM((DH, G * QB), F32)],
        compiler_params=pltpu.CompilerParams(
            dimension_semantics=("arbitrary", "arbitrary", "arbitrary"),
            vmem_limit_bytes=VMEM_LIMIT),
        name="attention",
    )(sc, q, kc, vcT, kslc, vslcT, kwin, vwinT, gT)


XH = 8
YH = 32


def _seq_mix_kernel(lru_ref, cv_ref, cw_ref, cb_ref, wa_ref, ba_ref, wx_ref, bx_ref, lam_ref,
                    dw_ref, db_ref, lng_ref, lnb_ref, gl_ref, gc_ref,
                    yl_ref, yc_ref, xbuf, ybuf, hbuf):
    t = pl.program_id(1)

    @pl.when(t == 0)
    def _():
        xbuf[0:XH, :] = jnp.zeros((XH, LRU_W), F32)
        ybuf[0:YH, :] = jnp.zeros((YH, CV_W), F32)
        hbuf[...] = jnp.zeros((8, LRU_W), F32)

    xb = lru_ref[:, 0:LRU_W]
    gb = lru_ref[:, LRU_W:2 * LRU_W]
    xbuf[XH:XH + TS, :] = xb
    xr = jnp.broadcast_to(cb_ref[...], (TS, LRU_W))
    for k in range(LRU_CONV):
        xr = xr + cw_ref[k:k + 1, :] * xbuf[XH - (LRU_CONV - 1) + k:XH - (LRU_CONV - 1) + k + TS, :]
    xbuf[0:XH, :] = xbuf[TS:TS + XH, :]

    xr16 = xr.astype(BF16)
    r = _sigmoid(_dot(xr16, wa_ref[...]) + ba_ref[...])
    ig = _sigmoid(_dot(xr16, wx_ref[...]) + bx_ref[...])
    nl = -lam_ref[...]
    softplus = jnp.maximum(nl, 0.0) + jnp.log1p(jnp.exp(-jnp.abs(nl)))
    log_a = -LRU_C * r * softplus
    a = jnp.exp(log_a)
    u = xr * ig * jnp.sqrt(-jnp.tanh(log_a) * (a * a + 1.0))

    row = lax.broadcasted_iota(jnp.int32, (TS, LRU_W), 0)
    d = 1
    while d < TS:
        keep = row >= d
        a_sh = jnp.where(keep, pltpu.roll(a, d, 0), 1.0)
        u_sh = jnp.where(keep, pltpu.roll(u, d, 0), 0.0)
        u = a * u_sh + u
        a = a * a_sh
        d *= 2
    hseq = a * hbuf[0:1, :] + u
    hbuf[...] = jnp.broadcast_to(hseq[TS - 1:TS, :], (8, LRU_W))
    yl = hseq * _gelu_tanh(gb)
    yl = yl * lax.rsqrt(jnp.mean(yl * yl, axis=-1, keepdims=True) + EPS) * gl_ref[...]
    yl_ref[...] = yl.astype(BF16)

    y = cv_ref[:, 0:CV_W] * _sigmoid(cv_ref[:, CV_W:2 * CV_W])
    ybuf[YH:YH + TS, :] = y
    c = jnp.broadcast_to(db_ref[...], (TS, CV_W))
    for k in range(CV_KERNEL):
        off = YH - (CV_KERNEL - 1) + k
        c = c + dw_ref[k:k + 1, :] * ybuf[off:off + TS, :]
    ybuf[0:YH, :] = ybuf[TS:TS + YH, :]
    mu = jnp.mean(c, axis=-1, keepdims=True)
    cc = c - mu
    var = jnp.mean(cc * cc, axis=-1, keepdims=True)
    ln = cc * lax.rsqrt(var + EPS) * lng_ref[...] + lnb_ref[...]
    yc = ln * _sigmoid(ln)
    yc = yc * lax.rsqrt(jnp.mean(yc * yc, axis=-1, keepdims=True) + EPS) * gc_ref[...]
    yc_ref[...] = yc.astype(BF16)


def _seq_mix(lru, cv, params, B, S):
    const = lambda a: pl.BlockSpec(a.shape, lambda b, t: (0,) * a.ndim)
    tile = lambda w_: pl.BlockSpec((None, TS, w_), lambda b, t: (b, t, 0))
    return pl.pallas_call(
        _seq_mix_kernel,
        out_shape=(jax.ShapeDtypeStruct((B, S, LRU_W), BF16),
                   jax.ShapeDtypeStruct((B, S, CV_W), BF16)),
        grid=(B, S // TS),
        in_specs=[tile(2 * LRU_W), tile(2 * CV_W)] + [const(p) for p in params],
        out_specs=(tile(LRU_W), tile(CV_W)),
        scratch_shapes=[pltpu.VMEM((TS + XH, LRU_W), F32), pltpu.VMEM((TS + YH, CV_W), F32),
                        pltpu.VMEM((8, LRU_W), F32)],
        compiler_params=pltpu.CompilerParams(
            dimension_semantics=("arbitrary", "arbitrary"), vmem_limit_bytes=VMEM_LIMIT),
        name="seq_mix",
    )(lru, cv, *params)


def _out_mlp_kernel(x_ref, ya_ref, yl_ref, yc_ref, ga_ref, wo_ref, gm_ref, w1_ref, w2_ref, o_ref):
    ya = ya_ref[...]
    ya = (ya * lax.rsqrt(jnp.mean(ya * ya, axis=-1, keepdims=True) + EPS) * ga_ref[...]).astype(BF16)
    x1 = (x_ref[...] + _dot(ya, wo_ref[0:NSA_W, :])
          + _dot(yl_ref[...], wo_ref[NSA_W:NSA_W + LRU_W, :])
          + _dot(yc_ref[...], wo_ref[NSA_W + LRU_W:NSA_W + LRU_W + CV_W, :]))
    hm = (x1 * lax.rsqrt(jnp.mean(x1 * x1, axis=-1, keepdims=True) + EPS) * gm_ref[...]).astype(BF16)
    d_ff = w1_ref.shape[1]
    fc = 1024
    o_ref[...] = x1
    for c in range(d_ff // fc):
        hc = jnp.maximum(_dot(hm, w1_ref[:, c * fc:(c + 1) * fc]), 0.0)
        o_ref[...] += _dot((hc * hc).astype(BF16), w2_ref[c * fc:(c + 1) * fc, :])


def _out_mlp(x2, ya, yl, yc, ga, wo, gm, w1, w2):
    T, D = x2.shape
    tok = lambda w_: pl.BlockSpec((TM, w_), lambda t: (t, 0))
    const1 = lambda a: pl.BlockSpec(a.shape, lambda t: (0,) * a.ndim, pipeline_mode=pl.Buffered(1))
    return pl.pallas_call(
        _out_mlp_kernel,
        out_shape=jax.ShapeDtypeStruct((T, D), F32),
        grid=(T // TM,),
        in_specs=[tok(D), tok(NSA_W), tok(LRU_W), tok(CV_W),
                  const1(ga), const1(wo), const1(gm), const1(w1), const1(w2)],
        out_specs=tok(D),
        compiler_params=pltpu.CompilerParams(
            dimension_semantics=("arbitrary",), vmem_limit_bytes=VMEM_LIMIT),
        name="out_mlp",
    )(x2, ya, yl, yc, ga, wo, gm, w1, w2)


def _block_ones(n):
    idx = np.arange(n) // DH
    return jnp.asarray((idx[:, None] == idx[None, :]).astype(np.float32) / DH, dtype=BF16)


def _permute_w_in(w):
    off_kv = NSA_W
    kv = [w[:, off_kv + c * 128: off_kv + (c + 1) * 128] for c in range(6)]
    off_gate = off_kv + 6 * 128
    ngate = 3 * HKV * G
    off_lru = off_gate + ngate
    gate = jnp.pad(w[:, off_gate:off_lru], ((0, 0), (0, LANES - ngate)))
    cols = [w[:, 0:NSA_W], kv[0], kv[1], kv[2], kv[4], kv[3], kv[5],
            w[:, off_lru:off_lru + 2 * LRU_W + 2 * CV_W], gate]
    return jnp.concatenate(cols, axis=1).astype(BF16)


def _compress_weights(pos, w1, w2):
    slot_kv = jnp.array([0, 0, 1, 1])
    eye = jnp.eye(4, dtype=F32)
    w1r = w1.reshape(2, 2, CMP_STRIDE, DH, CMP_HID)[slot_kv]
    wbig = jnp.einsum('shpdo,st->psdhto', w1r, eye).reshape(CMP_STRIDE * 4 * DH, 2 * 4 * CMP_HID)
    posr = pos.reshape(2, 2, CMP_STRIDE, DH)[slot_kv]
    pos2 = jnp.transpose(posr, (1, 2, 0, 3)).reshape(2, CMP_STRIDE * 4 * DH)
    pos2 = jnp.pad(pos2, ((0, 6), (0, 0)))
    w2big = jnp.einsum('sod,st->sotd', w2[slot_kv], eye).reshape(4 * CMP_HID, 4 * DH)
    return pos2.astype(BF16), wbig.astype(BF16), w2big.astype(BF16)


def _block_diag(w):
    hh, bw, _ = w.shape
    eye = jnp.eye(hh, dtype=w.dtype)
    return jnp.einsum('hij,hg->higj', w, eye).reshape(hh * bw, hh * bw).astype(BF16)


def kernel(x, attn_norm, w_in, q_norm, k_norm, cmp_pos, cmp_w1, cmp_w2, lru_conv_w, lru_conv_b,
           lru_wa, lru_ba, lru_wx, lru_bx, lru_lambda, cv_dw_w, cv_dw_b, cv_ln_g, cv_ln_b,
           out_norm, w_out, mlp_norm, mlp_w1, mlp_w2):
    B, S, D = x.shape
    depth = w_in.shape[0]
    assert S % TM == 0 and S % TS == 0 and S >= WINDOW + QB and S // SLC_BLOCK >= N_SEL
    NC = S // CMP_STRIDE
    row = lambda v: v.reshape(1, -1).astype(F32)
    bd512 = _block_ones(NSA_W)
    bd128 = _block_ones(128)

    x2 = x.reshape(B * S, D)
    for l in range(depth):
        qgain = row(jnp.tile(q_norm[l], HKV * G)) * (DH ** -0.5 * LOG2E)
        kgain = row(jnp.concatenate([jnp.tile(k_norm[l, 1], HKV), jnp.tile(k_norm[l, 2], HKV)]))
        q, cmpr, kslc, kwin, vslcT, vwinT, lru, cv, gT = _in_proj(
            x2, row(attn_norm[l]), _permute_w_in(w_in[l]), bd512, qgain, kgain, B, S)

        pos2, wbig, w2big = _compress_weights(cmp_pos[l], cmp_w1[l], cmp_w2[l])
        kc, vcT = _compress(cmpr.reshape(B, NC, CMP_STRIDE * 256), pos2, wbig, w2big, bd128,
                            row(jnp.tile(k_norm[l, 0], HKV)), B, NC)

        sc = _score_bound(qgain, k_norm[l, 1], k_norm[l, 2], k_norm[l, 0])
        y_attn = _attention(sc, q.reshape(B, S, NSA_W), kc, vcT, kslc, vslcT, kwin, vwinT, gT, B, S)

        g_out = out_norm[l]
        seq_params = (lru_conv_w[l], row(lru_conv_b[l]), _block_diag(lru_wa[l]), row(lru_ba[l]),
                      _block_diag(lru_wx[l]), row(lru_bx[l]), row(lru_lambda[l]),
                      cv_dw_w[l], row(cv_dw_b[l]), row(cv_ln_g[l]), row(cv_ln_b[l]),
                      row(g_out[NSA_W:NSA_W + LRU_W]), row(g_out[NSA_W + LRU_W:]))
        yl, yc = _seq_mix(lru.reshape(B, S, 2 * LRU_W), cv.reshape(B, S, 2 * CV_W), seq_params, B, S)

        x2 = _out_mlp(x2, y_attn.reshape(B * S, NSA_W), yl.reshape(B * S, LRU_W),
                      yc.reshape(B * S, CV_W), row(g_out[:NSA_W]), w_out[l].astype(BF16),
                      row(mlp_norm[l]), mlp_w1[l].astype(BF16), mlp_w2[l].astype(BF16))
    return x2.reshape(B, S, D)
```

```python
import functools

import numpy as np
import jax
import jax.numpy as jnp
from jax import lax
from jax.experimental import pallas as pl
from jax.experimental.pallas import tpu as pltpu

F32 = jnp.float32
BF16 = jnp.bfloat16

EPS = 1e-6
NEG = -1e30
FORCED = 1e6
DH = 64
HKV = 2
G = 4
NSA_W = HKV * G * DH
CMP_BLOCK = 32
CMP_STRIDE = 16
CMP_HID = 128
SLC_BLOCK = 64
N_SEL = 16
WINDOW = 512
QB = 256
LRU_W = 256
LRU_HEADS = 8
LRU_CONV = 4
LRU_C = 8.0
CV_W = 256
CV_KERNEL = 31
LOG2E = 1.4426950408889634
SAFE_SHIFT = 50.0

LANES = 128
KCH = 256
NCHUNK = 8
CCH = 256
TM = 512
TS = 512
VMEM_LIMIT = 48 * 1024 * 1024

C_Q = 0
C_CMP = C_Q + NSA_W
C_KSLC = C_CMP + 256
C_KWIN = C_KSLC + 128
C_VSLC = C_KWIN + 128
C_VWIN = C_VSLC + 128
C_LRU = C_VWIN + 128
C_CV = C_LRU + 2 * LRU_W
C_GATE = C_CV + 2 * CV_W
N_INP = C_GATE + LANES
GATE_ROWS = 32
VROWS = DH + 16
BIAS_BLOCKS = LANES - DH


def _dot(a, b):
    return jnp.dot(a, b, preferred_element_type=F32)


def _head_rms_scale(z, bd):
    sq = z * z
    hi = sq.astype(BF16)
    lo = (sq - hi.astype(F32)).astype(BF16)
    ms = _dot(hi, bd) + _dot(lo, bd)
    return lax.rsqrt(ms + EPS)


def _gelu_tanh(x):
    return 0.5 * x * (1.0 + jnp.tanh(0.7978845608028654 * (x + 0.044715 * (x * x * x))))


def _sigmoid(x):
    return 1.0 / (1.0 + jnp.exp(-x))


def _in_proj_kernel(x_ref, g_ref, w_ref, bd_ref, qgain_ref, kgain_ref,
                    q_ref, cmp_ref, kslc_ref, kwin_ref, vslcT_ref, vwinT_ref,
                    lru_ref, cv_ref, gT_ref, *, nt):
    x = x_ref[...]
    ms = jnp.mean(x * x, axis=-1, keepdims=True)
    hn = (x * lax.rsqrt(ms + EPS) * g_ref[...]).astype(BF16)

    zq = _dot(hn, w_ref[:, C_Q:C_CMP])
    q_ref[...] = (zq * _head_rms_scale(zq, bd_ref[...]) * qgain_ref[...]).astype(BF16)

    cmp_ref[...] = _dot(hn, w_ref[:, C_CMP:C_KSLC]).astype(BF16)

    zk = _dot(hn, w_ref[:, C_KSLC:C_VSLC])
    kn = zk * _head_rms_scale(zk, bd_ref[0:256, 0:256]) * kgain_ref[...]
    lane = lax.broadcasted_iota(jnp.int32, (TM, LANES), 1)
    rowg = (pl.program_id(0) % nt) * TM + lax.broadcasted_iota(jnp.int32, (TM, LANES), 0)
    blk = lax.shift_right_logical(rowg, 6) & (BIAS_BLOCKS - 1)
    onehot = jnp.where(lane == DH + blk, 1.0, 0.0)
    lo = lane < DH
    ks, kw = kn[:, 0:128], kn[:, 128:256]
    kslc_ref[0] = jnp.where(lo, ks, onehot).astype(BF16)
    kslc_ref[1] = jnp.where(lo, pltpu.roll(ks, DH, 1), onehot).astype(BF16)
    kwin_ref[0] = jnp.where(lo, kw, 0.0).astype(BF16)
    kwin_ref[1] = jnp.where(lo, pltpu.roll(kw, DH, 1), 0.0).astype(BF16)

    zv = _dot(hn, w_ref[:, C_VSLC:C_LRU])
    zvT = zv.T
    ones_row = jnp.where(lax.broadcasted_iota(jnp.int32, (VROWS - DH, LANES), 0) == 0,
                         1.0, 0.0).astype(BF16)
    for j in range(TM // LANES):
        for h in range(HKV):
            cols = slice(j * LANES, (j + 1) * LANES)
            vslcT_ref[h, j, 0:DH, :] = zvT[h * DH:(h + 1) * DH, cols].astype(BF16)
            vslcT_ref[h, j, DH:VROWS, :] = ones_row
            vwinT_ref[h, j, 0:DH, :] = zvT[128 + h * DH:128 + (h + 1) * DH, cols].astype(BF16)
            vwinT_ref[h, j, DH:VROWS, :] = ones_row

    lru_ref[...] = _dot(hn, w_ref[:, C_LRU:C_CV])
    cv_ref[...] = _dot(hn, w_ref[:, C_CV:C_GATE])

    zg = _sigmoid(_dot(hn, w_ref[:, C_GATE:N_INP]))
    gT_ref[...] = zg.T[0:GATE_ROWS, :]


def _in_proj(x2, g, w, bd, qgain, kgain, B, S):
    T, D = x2.shape
    nt = S // TM
    tok = lambda w_: pl.BlockSpec((TM, w_), lambda t: (t, 0))
    const = lambda a: pl.BlockSpec(a.shape, lambda t: (0,) * a.ndim)
    vT_spec = pl.BlockSpec((None, HKV, TM // LANES, VROWS, LANES),
                           lambda t: (t // nt, 0, t % nt, 0, 0))
    k_spec = pl.BlockSpec((None, HKV, TM, LANES), lambda t: (t // nt, 0, t % nt, 0))
    out_shape = (
        jax.ShapeDtypeStruct((T, NSA_W), BF16),
        jax.ShapeDtypeStruct((T, 256), BF16),
        jax.ShapeDtypeStruct((B, HKV, S, LANES), BF16),
        jax.ShapeDtypeStruct((B, HKV, S, LANES), BF16),
        jax.ShapeDtypeStruct((B, HKV, S // LANES, VROWS, LANES), BF16),
        jax.ShapeDtypeStruct((B, HKV, S // LANES, VROWS, LANES), BF16),
        jax.ShapeDtypeStruct((T, 2 * LRU_W), F32),
        jax.ShapeDtypeStruct((T, 2 * CV_W), F32),
        jax.ShapeDtypeStruct((B, GATE_ROWS, S), F32),
    )
    out_specs = (
        tok(NSA_W), tok(256), k_spec, k_spec, vT_spec, vT_spec,
        tok(2 * LRU_W), tok(2 * CV_W),
        pl.BlockSpec((None, GATE_ROWS, TM), lambda t: (t // nt, 0, t % nt)),
    )
    return pl.pallas_call(
        functools.partial(_in_proj_kernel, nt=nt),
        out_shape=out_shape,
        grid=(T // TM,),
        in_specs=[tok(D), const(g), const(w), const(bd), const(qgain), const(kgain)],
        out_specs=out_specs,
        compiler_params=pltpu.CompilerParams(
            dimension_semantics=("arbitrary",), vmem_limit_bytes=VMEM_LIMIT),
        name="in_proj",
    )(x2, g, w, bd, qgain, kgain)


def _compress_kernel(x_ref, pos_ref, wbig_ref, w2_ref, bd_ref, kgain_ref, kc_ref, vcT_ref):
    nc = x_ref.shape[0]
    half = 4 * CMP_HID
    p = _dot(x_ref[...], wbig_ref[...])
    pc = _dot(pos_ref[...], wbig_ref[...])
    const = pc[0:1, 0:half] + pc[1:2, half:2 * half]
    nxt = pltpu.roll(p[:, half:2 * half], nc - 1, 0)
    hid = _gelu_tanh(p[:, 0:half] + nxt + const).astype(BF16)
    kv = _dot(hid, w2_ref[...])
    kc = kv[:, 0:128]
    kc = kc * _head_rms_scale(kc, bd_ref[...]) * kgain_ref[...]
    lo = lax.broadcasted_iota(jnp.int32, (nc, LANES), 1) < DH
    kc_ref[0] = jnp.where(lo, kc, 0.0).astype(BF16)
    kc_ref[1] = jnp.where(lo, pltpu.roll(kc, DH, 1), 0.0).astype(BF16)
    vT = kv[:, 128:256].T
    for h in range(HKV):
        vcT_ref[h] = vT[h * DH:(h + 1) * DH, :].astype(BF16)


def _compress(cmpx, pos2, wbig, w2big, bd128, kgain, B, NC):
    const = lambda a: pl.BlockSpec(a.shape, lambda b: (0,) * a.ndim, pipeline_mode=pl.Buffered(1))
    return pl.pallas_call(
        _compress_kernel,
        out_shape=(jax.ShapeDtypeStruct((B, HKV, NC, LANES), BF16),
                   jax.ShapeDtypeStruct((B, HKV, DH, NC), BF16)),
        grid=(B,),
        in_specs=[pl.BlockSpec((None, NC, cmpx.shape[2]), lambda b: (b, 0, 0)),
                  const(pos2), const(wbig), const(w2big), const(bd128), const(kgain)],
        out_specs=(pl.BlockSpec((None, HKV, NC, LANES), lambda b: (b, 0, 0, 0)),
                   pl.BlockSpec((None, HKV, DH, NC), lambda b: (b, 0, 0, 0))),
        compiler_params=pltpu.CompilerParams(
            dimension_semantics=("arbitrary",), vmem_limit_bytes=VMEM_LIMIT),
        name="compress",
    )(cmpx, pos2, wbig, w2big, bd128, kgain)


def _attn_kernel(sc_ref, q_ref, kc_ref, vcT_ref, ks_ref, vsT_ref, kw_ref, vwT_ref, gT_ref,
                 y_ref, qT_ref, imp_ref, selb_ref, selo_ref, oc_ref, os_ref, ow_ref, acc_ref, *, cch):
    h = pl.program_id(1)
    i = pl.program_id(2)
    nc = kc_ref.shape[0]
    ns = selb_ref.shape[0]
    wspan = WINDOW + QB
    t0 = i * QB
    iota = lambda shape, ax: lax.broadcasted_iota(jnp.int32, shape, ax)
    use_bound = sc_ref[1] > 0.5
    shift = jnp.where(use_bound, -sc_ref[0], 0.0)

    qfT = q_ref[...].astype(F32).T
    for g in range(G):
        qT_ref[0:DH, g * QB:(g + 1) * QB] = qfT[g * DH:(g + 1) * DH, :].astype(BF16)
    qT_ref[DH:LANES, :] = jnp.zeros((LANES - DH, G * QB), BF16)

    def front(nr, bounded):
        nsu = nr // (SLC_BLOCK // CMP_STRIDE)
        cmask = iota((nr, QB), 0) * CMP_STRIDE + (CMP_BLOCK - 1) <= t0 + iota((nr, QB), 1)
        cb = jnp.where(cmask, -sc_ref[2] if bounded else 0.0, NEG)
        s = _dot(kc_ref[0:nr, :], qT_ref[...]) + jnp.concatenate([cb] * G, axis=1)
        p = jnp.exp2(s if bounded else s - jnp.max(s, axis=0, keepdims=True))
        l = jnp.sum(p, axis=0, keepdims=True)
        anyv = jnp.where(t0 + iota((1, QB), 1) >= CMP_BLOCK - 1, 1.0, 0.0)
        pn = p * jnp.where(jnp.concatenate([anyv] * G, axis=1) > 0.5, 1.0 / l, 0.0)
        oc_ref[...] = _dot(vcT_ref[:, 0:nr], pn.astype(BF16))
        imp = pn[:, 0:QB]
        for g in range(1, G):
            imp = imp + pn[:, g * QB:(g + 1) * QB]

        parts = []
        for w in range(QB // LANES):
            imp_ref[w, 0:8, :] = jnp.zeros((8, LANES), F32)
            imp_ref[w, 8:8 + nr, :] = imp[:, w * LANES:(w + 1) * LANES]
            acc = imp_ref[w, pl.ds(7, nsu, stride=4), :]
            for r in range(4):
                acc = acc + imp_ref[w, pl.ds(8 + r, nsu, stride=4), :]
            parts.append(acc)
        islc = jnp.concatenate(parts, axis=1)

        j_i = iota((nsu, QB), 0)
        t_s = t0 + iota((nsu, QB), 1)
        cur = lax.shift_right_logical(t_s, 6)
        valid = j_i * SLC_BLOCK <= t_s
        ninf = -jnp.inf
        score = jnp.where(j_i == 0, ninf, jnp.where(j_i == cur, ninf,
                                                    jnp.where(j_i == cur - 1, ninf, islc)))
        score = jnp.where(valid, score, NEG)
        j_f = j_i.astype(F32)
        for _ in range(N_SEL - 3):
            mx = jnp.max(score, axis=0, keepdims=True)
            jm = jnp.min(jnp.where(score == mx, j_f, float(nsu)), axis=0, keepdims=True)
            score = jnp.where(j_f == jm, -jnp.inf, score)
        own = lax.shift_right_logical(j_i, (QB // SLC_BLOCK).bit_length() - 1) == i
        picked = jnp.where(valid, jnp.where(score == -jnp.inf, shift, NEG), NEG)
        selo_ref[0:nsu, :] = picked
        selb_ref[0:nsu, :] = jnp.where(own, NEG, picked)

    cls = (i * (QB // CMP_STRIDE) + (QB - CMP_BLOCK) // CMP_STRIDE) // cch
    for kk in range(nc // cch):
        for bounded in (True, False):
            pl.when(jnp.logical_and(cls == kk, use_bound == bounded))(
                functools.partial(front, (kk + 1) * cch, bounded))

    bpt = QB // SLC_BLOCK
    own_b = jnp.concatenate(
        [jnp.broadcast_to(selo_ref[pl.ds(i * bpt + b, 1), :], (SLC_BLOCK, QB)) for b in range(bpt)],
        axis=0)
    own_b = jnp.where(iota((QB, QB), 0) <= iota((QB, QB), 1), own_b, NEG)
    own_v = jnp.concatenate([vsT_ref[i * (QB // LANES) + w] for w in range(QB // LANES)], axis=1)
    sd = (_dot(ks_ref[pl.ds(pl.multiple_of(t0, QB), QB), :], qT_ref[...])
          + jnp.concatenate([own_b] * G, axis=1))

    nv = KCH // LANES
    cpg = BIAS_BLOCKS // (KCH // SLC_BLOCK)
    brows = min(BIAS_BLOCKS, ns)
    nstep = (i + NCHUNK - 1) // NCHUNK

    def chunk_scores(c0, n=NCHUNK):
        @pl.when(c0 % cpg == 0)
        def _():
            r0 = pl.multiple_of((c0 // cpg) * brows, brows)
            rows = selb_ref[pl.ds(r0, brows), :].astype(BF16)
            qT_ref[DH:DH + brows, :] = jnp.concatenate([rows] * G, axis=1)

        qT = qT_ref[...]
        return [_dot(ks_ref[pl.ds(pl.multiple_of((c0 + u) * KCH, KCH), KCH), :], qT)
                for u in range(n)]

    def values(c):
        return jnp.concatenate([vsT_ref[c * nv + w] for w in range(nv)], axis=1)

    def finish(acc):
        os_ref[...] = acc[0:DH, :] * (1.0 / acc[DH:DH + 1, :])

    @pl.when(use_bound)
    def _():
        def add_chunks(c0, n, acc):
            ss = chunk_scores(c0, n)
            for u in range(n):
                acc = acc + _dot(values(c0 + u), jnp.exp2(ss[u]).astype(BF16))
            return acc

        nfull = i // NCHUNK
        rem = i - nfull * NCHUNK
        acc0 = _dot(own_v, jnp.exp2(sd).astype(BF16))
        acc_ref[...] = lax.fori_loop(
            0, nfull, lambda grp, acc: add_chunks(grp * NCHUNK, NCHUNK, acc), acc0)

        @pl.when(jnp.logical_and(rem > 0, rem <= NCHUNK // 2))
        def _():
            acc_ref[...] = add_chunks(nfull * NCHUNK, NCHUNK // 2, acc_ref[...])

        @pl.when(rem > NCHUNK // 2)
        def _():
            acc_ref[...] = add_chunks(nfull * NCHUNK, NCHUNK, acc_ref[...])

        finish(acc_ref[...])

    @pl.when(jnp.logical_not(use_bound))
    def _():
        def step(grp, carry):
            m, acc = carry
            ss = chunk_scores(grp * NCHUNK)
            for u in range(NCHUNK):
                s = ss[u]
                m_new = jnp.maximum(m, jnp.max(s, axis=0, keepdims=True))
                p = jnp.exp2(s - m_new).astype(BF16)
                acc = jnp.exp2(m - m_new) * acc + _dot(values(grp * NCHUNK + u), p)
                m = m_new
            return m, acc

        m0 = jnp.max(sd, axis=0, keepdims=True)
        acc0 = _dot(own_v, jnp.exp2(sd - m0).astype(BF16))
        finish(lax.fori_loop(0, nstep, step, (m0, acc0))[1])

    os_ = os_ref[...]

    def window(bounded):
        wc = jnp.maximum(i * (QB // LANES) - WINDOW // LANES, 0)
        ws = wc * LANES
        sw = _dot(kw_ref[pl.ds(pl.multiple_of(ws, LANES), wspan), :], qT_ref[...])
        kpos = ws + iota((wspan, QB), 0)
        t_w = t0 + iota((wspan, QB), 1)
        inside = -sc_ref[3] if bounded else 0.0
        wb = jnp.where(kpos <= t_w, jnp.where(kpos > t_w - WINDOW, inside, NEG), NEG)
        sw = sw + jnp.concatenate([wb] * G, axis=1)
        p_w = jnp.exp2(sw if bounded else sw - jnp.max(sw, axis=0, keepdims=True)).astype(BF16)
        vw = jnp.concatenate([vwT_ref[wc + u] for u in range(wspan // LANES)], axis=1)
        acc_w = _dot(vw, p_w)
        ow_ref[...] = acc_w[0:DH, :] * (1.0 / acc_w[DH:DH + 1, :])

    for bounded in (True, False):
        pl.when(use_bound == bounded)(functools.partial(window, bounded))
    ow = ow_ref[...]

    outs = []
    for g in range(G):
        sl = slice(g * QB, (g + 1) * QB)
        row = h * (3 * G) + 3 * g
        gc = gT_ref[pl.ds(row, 1), :]
        gs = gT_ref[pl.ds(row + 1, 1), :]
        gw = gT_ref[pl.ds(row + 2, 1), :]
        outs.append(gc * oc_ref[:, sl] + gs * os_[:, sl] + gw * ow[:, sl])
    y_ref[...] = jnp.concatenate(outs, axis=0).T


def _score_bound(qgain, kgain_slc, kgain_win, kgain_cmp):
    qmax = jnp.max(jnp.abs(qgain))
    ms = [(1.05 * DH * qmax * jnp.max(jnp.abs(kg))).astype(BF16).astype(F32)
          for kg in (kgain_slc, kgain_cmp, kgain_win)]
    ok = (jnp.maximum(jnp.maximum(ms[0], ms[1]), ms[2]) <= SAFE_SHIFT).astype(F32)
    return jnp.stack([ms[0], ok, ms[1], ms[2]])


def _attention(sc, q, kc, vcT, kslc, vslcT, kwin, vwinT, gT, B, S):
    NC = S // CMP_STRIDE
    NS = S // SLC_BLOCK
    NQ = S // QB
    cch = min(CCH, NC)
    assert NC % cch == 0 and (S // KCH) % NCHUNK == 0 and QB == KCH
    head = lambda *blk: pl.BlockSpec((None, None) + blk, lambda b, h, i: (b, h) + (0,) * len(blk),
                                     pipeline_mode=pl.Buffered(1))
    return pl.pallas_call(
        functools.partial(_attn_kernel, cch=cch),
        out_shape=jax.ShapeDtypeStruct((B, S, NSA_W), F32),
        grid=(B, HKV, NQ),
        in_specs=[
            pl.BlockSpec(memory_space=pltpu.SMEM),
            pl.BlockSpec((None, QB, G * DH), lambda b, h, i: (b, i, h)),
            head(NC, LANES), head(DH, NC),
            head(S, LANES), head(S // LANES, VROWS, LANES),
            head(S, LANES), head(S // LANES, VROWS, LANES),
            pl.BlockSpec((None, GATE_ROWS, QB), lambda b, h, i: (b, 0, i)),
        ],
        out_specs=pl.BlockSpec((None, QB, G * DH), lambda b, h, i: (b, i, h)),
        scratch_shapes=[pltpu.VMEM((LANES, G * QB), BF16),
                        pltpu.VMEM((QB // LANES, NC + 8, LANES), F32),
                        pltpu.VMEM((NS, QB), F32), pltpu.VMEM((NS, QB), F32),
                        pltpu.VMEM((DH, G * QB), F32),
                        pltpu.VMEM((DH, G * QB), F32), pltpu.VMEM((DH, G * QB), F32),
                        pltpu.VMEM((VROWS, G * QB), F32)],
        compiler_params=pltpu.CompilerParams(
            dimension_semantics=("arbitrary", "arbitrary", "arbitrary"),
            vmem_limit_bytes=VMEM_LIMIT),
        name="attention",
    )(sc, q, kc, vcT, kslc, vslcT, kwin, vwinT, gT)


XH = 8
YH = 32


def _seq_mix_kernel(lru_ref, cv_ref, cw_ref, cb_ref, wa_ref, ba_ref, wx_ref, bx_ref, lam_ref,
                    dw_ref, db_ref, lng_ref, lnb_ref, gl_ref, gc_ref,
                    yl_ref, yc_ref, xbuf, ybuf, hbuf):
    t = pl.program_id(1)

    @pl.when(t == 0)
    def _():
        xbuf[0:XH, :] = jnp.zeros((XH, LRU_W), F32)
        ybuf[0:YH, :] = jnp.zeros((YH, CV_W), F32)
        hbuf[...] = jnp.zeros((8, LRU_W), F32)

    xb = lru_ref[:, 0:LRU_W]
    gb = lru_ref[:, LRU_W:2 * LRU_W]
    xbuf[XH:XH + TS, :] = xb
    xr = jnp.broadcast_to(cb_ref[...], (TS, LRU_W))
    for k in range(LRU_CONV):
        xr = xr + cw_ref[k:k + 1, :] * xbuf[XH - (LRU_CONV - 1) + k:XH - (LRU_CONV - 1) + k + TS, :]
    xbuf[0:XH, :] = xbuf[TS:TS + XH, :]

    xr16 = xr.astype(BF16)
    r = _sigmoid(_dot(xr16, wa_ref[...]) + ba_ref[...])
    ig = _sigmoid(_dot(xr16, wx_ref[...]) + bx_ref[...])
    nl = -lam_ref[...]
    softplus = jnp.maximum(nl, 0.0) + jnp.log1p(jnp.exp(-jnp.abs(nl)))
    log_a = -LRU_C * r * softplus
    a = jnp.exp(log_a)
    u = xr * ig * jnp.sqrt(-jnp.tanh(log_a) * (a * a + 1.0))

    row = lax.broadcasted_iota(jnp.int32, (TS, LRU_W), 0)
    d = 1
    while d < TS:
        keep = row >= d
        a_sh = jnp.where(keep, pltpu.roll(a, d, 0), 1.0)
        u_sh = jnp.where(keep, pltpu.roll(u, d, 0), 0.0)
        u = a * u_sh + u
        a = a * a_sh
        d *= 2
    hseq = a * hbuf[0:1, :] + u
    hbuf[...] = jnp.broadcast_to(hseq[TS - 1:TS, :], (8, LRU_W))
    yl = hseq * _gelu_tanh(gb)
    yl = yl * lax.rsqrt(jnp.mean(yl * yl, axis=-1, keepdims=True) + EPS) * gl_ref[...]
    yl_ref[...] = yl.astype(BF16)

    y = cv_ref[:, 0:CV_W] * _sigmoid(cv_ref[:, CV_W:2 * CV_W])
    ybuf[YH:YH + TS, :] = y
    c = jnp.broadcast_to(db_ref[...], (TS, CV_W))
    for k in range(CV_KERNEL):
        off = YH - (CV_KERNEL - 1) + k
        c = c + dw_ref[k:k + 1, :] * ybuf[off:off + TS, :]
    ybuf[0:YH, :] = ybuf[TS:TS + YH, :]
    mu = jnp.mean(c, axis=-1, keepdims=True)
    cc = c - mu
    var = jnp.mean(cc * cc, axis=-1, keepdims=True)
    ln = cc * lax.rsqrt(var + EPS) * lng_ref[...] + lnb_ref[...]
    yc = ln * _sigmoid(ln)
    yc = yc * lax.rsqrt(jnp.mean(yc * yc, axis=-1, keepdims=True) + EPS) * gc_ref[...]
    yc_ref[...] = yc.astype(BF16)


def _seq_mix(lru, cv, params, B, S):
    const = lambda a: pl.BlockSpec(a.shape, lambda b, t: (0,) * a.ndim)
    tile = lambda w_: pl.BlockSpec((None, TS, w_), lambda b, t: (b, t, 0))
    return pl.pallas_call(
        _seq_mix_kernel,
        out_shape=(jax.ShapeDtypeStruct((B, S, LRU_W), BF16),
                   jax.ShapeDtypeStruct((B, S, CV_W), BF16)),
        grid=(B, S // TS),
        in_specs=[tile(2 * LRU_W), tile(2 * CV_W)] + [const(p) for p in params],
        out_specs=(tile(LRU_W), tile(CV_W)),
        scratch_shapes=[pltpu.VMEM((TS + XH, LRU_W), F32), pltpu.VMEM((TS + YH, CV_W), F32),
                        pltpu.VMEM((8, LRU_W), F32)],
        compiler_params=pltpu.CompilerParams(
            dimension_semantics=("arbitrary", "arbitrary"), vmem_limit_bytes=VMEM_LIMIT),
        name="seq_mix",
    )(lru, cv, *params)


def _out_mlp_kernel(x_ref, ya_ref, yl_ref, yc_ref, ga_ref, wo_ref, gm_ref, w1_ref, w2_ref, o_ref):
    ya = ya_ref[...]
    ya = (ya * lax.rsqrt(jnp.mean(ya * ya, axis=-1, keepdims=True) + EPS) * ga_ref[...]).astype(BF16)
    x1 = (x_ref[...] + _dot(ya, wo_ref[0:NSA_W, :])
          + _dot(yl_ref[...], wo_ref[NSA_W:NSA_W + LRU_W, :])
          + _dot(yc_ref[...], wo_ref[NSA_W + LRU_W:NSA_W + LRU_W + CV_W, :]))
    hm = (x1 * lax.rsqrt(jnp.mean(x1 * x1, axis=-1, keepdims=True) + EPS) * gm_ref[...]).astype(BF16)
    d_ff = w1_ref.shape[1]
    fc = 1024
    o_ref[...] = x1
    for c in range(d_ff // fc):
        hc = jnp.maximum(_dot(hm, w1_ref[:, c * fc:(c + 1) * fc]), 0.0)
        o_ref[...] += _dot((hc * hc).astype(BF16), w2_ref[c * fc:(c + 1) * fc, :])


def _out_mlp(x2, ya, yl, yc, ga, wo, gm, w1, w2):
    T, D = x2.shape
    tok = lambda w_: pl.BlockSpec((TM, w_), lambda t: (t, 0))
    const1 = lambda a: pl.BlockSpec(a.shape, lambda t: (0,) * a.ndim, pipeline_mode=pl.Buffered(1))
    return pl.pallas_call(
        _out_mlp_kernel,
        out_shape=jax.ShapeDtypeStruct((T, D), F32),
        grid=(T // TM,),
        in_specs=[tok(D), tok(NSA_W), tok(LRU_W), tok(CV_W),
                  const1(ga), const1(wo), const1(gm), const1(w1), const1(w2)],
        out_specs=tok(D),
        compiler_params=pltpu.CompilerParams(
            dimension_semantics=("arbitrary",), vmem_limit_bytes=VMEM_LIMIT),
        name="out_mlp",
    )(x2, ya, yl, yc, ga, wo, gm, w1, w2)


def _block_ones(n):
    idx = np.arange(n) // DH
    return jnp.asarray((idx[:, None] == idx[None, :]).astype(np.float32) / DH, dtype=BF16)


def _permute_w_in(w):
    off_kv = NSA_W
    kv = [w[:, off_kv + c * 128: off_kv + (c + 1) * 128] for c in range(6)]
    off_gate = off_kv + 6 * 128
    ngate = 3 * HKV * G
    off_lru = off_gate + ngate
    gate = jnp.pad(w[:, off_gate:off_lru], ((0, 0), (0, LANES - ngate)))
    cols = [w[:, 0:NSA_W], kv[0], kv[1], kv[2], kv[4], kv[3], kv[5],
            w[:, off_lru:off_lru + 2 * LRU_W + 2 * CV_W], gate]
    return jnp.concatenate(cols, axis=1).astype(BF16)


def _compress_weights(pos, w1, w2):
    slot_kv = jnp.array([0, 0, 1, 1])
    eye = jnp.eye(4, dtype=F32)
    w1r = w1.reshape(2, 2, CMP_STRIDE, DH, CMP_HID)[slot_kv]
    wbig = jnp.einsum('shpdo,st->psdhto', w1r, eye).reshape(CMP_STRIDE * 4 * DH, 2 * 4 * CMP_HID)
    posr = pos.reshape(2, 2, CMP_STRIDE, DH)[slot_kv]
    pos2 = jnp.transpose(posr, (1, 2, 0, 3)).reshape(2, CMP_STRIDE * 4 * DH)
    pos2 = jnp.pad(pos2, ((0, 6), (0, 0)))
    w2big = jnp.einsum('sod,st->sotd', w2[slot_kv], eye).reshape(4 * CMP_HID, 4 * DH)
    return pos2.astype(BF16), wbig.astype(BF16), w2big.astype(BF16)


def _block_diag(w):
    hh, bw, _ = w.shape
    eye = jnp.eye(hh, dtype=w.dtype)
    return jnp.einsum('hij,hg->higj', w, eye).reshape(hh * bw, hh * bw).astype(BF16)


def kernel(x, attn_norm, w_in, q_norm, k_norm, cmp_pos, cmp_w1, cmp_w2, lru_conv_w, lru_conv_b,
           lru_wa, lru_ba, lru_wx, lru_bx, lru_lambda, cv_dw_w, cv_dw_b, cv_ln_g, cv_ln_b,
           out_norm, w_out, mlp_norm, mlp_w1, mlp_w2):
    B, S, D = x.shape
    depth = w_in.shape[0]
    assert S % TM == 0 and S % TS == 0 and S >= WINDOW + QB and S // SLC_BLOCK >= N_SEL
    NC = S // CMP_STRIDE
    row = lambda v: v.reshape(1, -1).astype(F32)
    bd512 = _block_ones(NSA_W)
    bd128 = _block_ones(128)

    x2 = x.reshape(B * S, D)
    for l in range(depth):
        qgain = row(jnp.tile(q_norm[l], HKV * G)) * (DH ** -0.5 * LOG2E)
        kgain = row(jnp.concatenate([jnp.tile(k_norm[l, 1], HKV), jnp.tile(k_norm[l, 2], HKV)]))
        q, cmpr, kslc, kwin, vslcT, vwinT, lru, cv, gT = _in_proj(
            x2, row(attn_norm[l]), _permute_w_in(w_in[l]), bd512, qgain, kgain, B, S)

        pos2, wbig, w2big = _compress_weights(cmp_pos[l], cmp_w1[l], cmp_w2[l])
        kc, vcT = _compress(cmpr.reshape(B, NC, CMP_STRIDE * 256), pos2, wbig, w2big, bd128,
                            row(jnp.tile(k_norm[l, 0], HKV)), B, NC)

        sc = _score_bound(qgain, k_norm[l, 1], k_norm[l, 2], k_norm[l, 0])
        y_attn = _attention(sc, q.reshape(B, S, NSA_W), kc, vcT, kslc, vslcT, kwin, vwinT, gT, B, S)

        g_out = out_norm[l]
        seq_params = (lru_conv_w[l], row(lru_conv_b[l]), _block_diag(lru_wa[l]), row(lru_ba[l]),
                      _block_diag(lru_wx[l]), row(lru_bx[l]), row(lru_lambda[l]),
                      cv_dw_w[l], row(cv_dw_b[l]), row(cv_ln_g[l]), row(cv_ln_b[l]),
                      row(g_out[NSA_W:NSA_W + LRU_W]), row(g_out[NSA_W + LRU_W:]))
        yl, yc = _seq_mix(lru.reshape(B, S, 2 * LRU_W), cv.reshape(B, S, 2 * CV_W), seq_params, B, S)

        x2 = _out_mlp(x2, y_attn.reshape(B * S, NSA_W), yl.reshape(B * S, LRU_W),
                      yc.reshape(B * S, CV_W), row(g_out[:NSA_W]), w_out[l].astype(BF16),
                      row(mlp_norm[l]), mlp_w1[l].astype(BF16), mlp_w2[l].astype(BF16))
    return x2.reshape(B, S, D)
```

```python
import functools

import numpy as np
import jax
import jax.numpy as jnp
from jax import lax
from jax.experimental import pallas as pl
from jax.experimental.pallas import tpu as pltpu

F32 = jnp.float32
BF16 = jnp.bfloat16

EPS = 1e-6
NEG = -1e30
FORCED = 1e6
DH = 64
HKV = 2
G = 4
NSA_W = HKV * G * DH
CMP_BLOCK = 32
CMP_STRIDE = 16
CMP_HID = 128
SLC_BLOCK = 64
N_SEL = 16
WINDOW = 512
QB = 256
LRU_W = 256
LRU_HEADS = 8
LRU_CONV = 4
LRU_C = 8.0
CV_W = 256
CV_KERNEL = 31
LOG2E = 1.4426950408889634
SAFE_SHIFT = 50.0

LANES = 128
KCH = 256
NCHUNK = 8
CCH = 256
TM = 512
TS = 512
VMEM_LIMIT = 48 * 1024 * 1024

C_Q = 0
C_CMP = C_Q + NSA_W
C_KSLC = C_CMP + 256
C_KWIN = C_KSLC + 128
C_VSLC = C_KWIN + 128
C_VWIN = C_VSLC + 128
C_LRU = C_VWIN + 128
C_CV = C_LRU + 2 * LRU_W
C_GATE = C_CV + 2 * CV_W
N_INP = C_GATE + LANES
GATE_ROWS = 32
VROWS = DH + 16
BIAS_BLOCKS = LANES - DH


def _dot(a, b):
    return jnp.dot(a, b, preferred_element_type=F32)


def _head_rms_scale(z, bd):
    sq = z * z
    hi = sq.astype(BF16)
    lo = (sq - hi.astype(F32)).astype(BF16)
    ms = _dot(hi, bd) + _dot(lo, bd)
    return lax.rsqrt(ms + EPS)


def _gelu_tanh(x):
    return 0.5 * x * (1.0 + jnp.tanh(0.7978845608028654 * (x + 0.044715 * (x * x * x))))


def _sigmoid(x):
    return 1.0 / (1.0 + jnp.exp(-x))


def _in_proj_kernel(x_ref, g_ref, w_ref, bd_ref, qgain_ref, kgain_ref,
                    q_ref, cmp_ref, kslc_ref, kwin_ref, vslcT_ref, vwinT_ref,
                    lru_ref, cv_ref, gT_ref, *, nt):
    x = x_ref[...]
    ms = jnp.mean(x * x, axis=-1, keepdims=True)
    hn = (x * lax.rsqrt(ms + EPS) * g_ref[...]).astype(BF16)

    zq = _dot(hn, w_ref[:, C_Q:C_CMP])
    q_ref[...] = (zq * _head_rms_scale(zq, bd_ref[...]) * qgain_ref[...]).astype(BF16)

    cmp_ref[...] = _dot(hn, w_ref[:, C_CMP:C_KSLC]).astype(BF16)

    zk = _dot(hn, w_ref[:, C_KSLC:C_VSLC])
    kn = zk * _head_rms_scale(zk, bd_ref[0:256, 0:256]) * kgain_ref[...]
    lane = lax.broadcasted_iota(jnp.int32, (TM, LANES), 1)
    rowg = (pl.program_id(0) % nt) * TM + lax.broadcasted_iota(jnp.int32, (TM, LANES), 0)
    blk = lax.shift_right_logical(rowg, 6) & (BIAS_BLOCKS - 1)
    onehot = jnp.where(lane == DH + blk, 1.0, 0.0)
    lo = lane < DH
    ks, kw = kn[:, 0:128], kn[:, 128:256]
    kslc_ref[0] = jnp.where(lo, ks, onehot).astype(BF16)
    kslc_ref[1] = jnp.where(lo, pltpu.roll(ks, DH, 1), onehot).astype(BF16)
    kwin_ref[0] = jnp.where(lo, kw, 0.0).astype(BF16)
    kwin_ref[1] = jnp.where(lo, pltpu.roll(kw, DH, 1), 0.0).astype(BF16)

    zv = _dot(hn, w_ref[:, C_VSLC:C_LRU])
    zvT = zv.T
    ones_row = jnp.where(lax.broadcasted_iota(jnp.int32, (VROWS - DH, LANES), 0) == 0,
                         1.0, 0.0).astype(BF16)
    for j in range(TM // LANES):
        for h in range(HKV):
            cols = slice(j * LANES, (j + 1) * LANES)
            vslcT_ref[h, j, 0:DH, :] = zvT[h * DH:(h + 1) * DH, cols].astype(BF16)
            vslcT_ref[h, j, DH:VROWS, :] = ones_row
            vwinT_ref[h, j, 0:DH, :] = zvT[128 + h * DH:128 + (h + 1) * DH, cols].astype(BF16)
            vwinT_ref[h, j, DH:VROWS, :] = ones_row

    lru_ref[...] = _dot(hn, w_ref[:, C_LRU:C_CV])
    cv_ref[...] = _dot(hn, w_ref[:, C_CV:C_GATE])

    zg = _sigmoid(_dot(hn, w_ref[:, C_GATE:N_INP]))
    gT_ref[...] = zg.T[0:GATE_ROWS, :]


def _in_proj(x2, g, w, bd, qgain, kgain, B, S):
    T, D = x2.shape
    nt = S // TM
    tok = lambda w_: pl.BlockSpec((TM, w_), lambda t: (t, 0))
    const = lambda a: pl.BlockSpec(a.shape, lambda t: (0,) * a.ndim)
    vT_spec = pl.BlockSpec((None, HKV, TM // LANES, VROWS, LANES),
                           lambda t: (t // nt, 0, t % nt, 0, 0))
    k_spec = pl.BlockSpec((None, HKV, TM, LANES), lambda t: (t // nt, 0, t % nt, 0))
    out_shape = (
        jax.ShapeDtypeStruct((T, NSA_W), BF16),
        jax.ShapeDtypeStruct((T, 256), BF16),
        jax.ShapeDtypeStruct((B, HKV, S, LANES), BF16),
        jax.ShapeDtypeStruct((B, HKV, S, LANES), BF16),
        jax.ShapeDtypeStruct((B, HKV, S // LANES, VROWS, LANES), BF16),
        jax.ShapeDtypeStruct((B, HKV, S // LANES, VROWS, LANES), BF16),
        jax.ShapeDtypeStruct((T, 2 * LRU_W), F32),
        jax.ShapeDtypeStruct((T, 2 * CV_W), F32),
        jax.ShapeDtypeStruct((B, GATE_ROWS, S), F32),
    )
    out_specs = (
        tok(NSA_W), tok(256), k_spec, k_spec, vT_spec, vT_spec,
        tok(2 * LRU_W), tok(2 * CV_W),
        pl.BlockSpec((None, GATE_ROWS, TM), lambda t: (t // nt, 0, t % nt)),
    )
    return pl.pallas_call(
        functools.partial(_in_proj_kernel, nt=nt),
        out_shape=out_shape,
        grid=(T // TM,),
        in_specs=[tok(D), const(g), const(w), const(bd), const(qgain), const(kgain)],
        out_specs=out_specs,
        compiler_params=pltpu.CompilerParams(
            dimension_semantics=("arbitrary",), vmem_limit_bytes=VMEM_LIMIT),
        name="in_proj",
    )(x2, g, w, bd, qgain, kgain)


def _compress_kernel(x_ref, pos_ref, wbig_ref, w2_ref, bd_ref, kgain_ref, kc_ref, vcT_ref):
    nc = x_ref.shape[0]
    half = 4 * CMP_HID
    p = _dot(x_ref[...], wbig_ref[...])
    pc = _dot(pos_ref[...], wbig_ref[...])
    const = pc[0:1, 0:half] + pc[1:2, half:2 * half]
    nxt = pltpu.roll(p[:, half:2 * half], nc - 1, 0)
    hid = _gelu_tanh(p[:, 0:half] + nxt + const).astype(BF16)
    kv = _dot(hid, w2_ref[...])
    kc = kv[:, 0:128]
    kc = kc * _head_rms_scale(kc, bd_ref[...]) * kgain_ref[...]
    lo = lax.broadcasted_iota(jnp.int32, (nc, LANES), 1) < DH
    kc_ref[0] = jnp.where(lo, kc, 0.0).astype(BF16)
    kc_ref[1] = jnp.where(lo, pltpu.roll(kc, DH, 1), 0.0).astype(BF16)
    vT = kv[:, 128:256].T
    for h in range(HKV):
        vcT_ref[h] = vT[h * DH:(h + 1) * DH, :].astype(BF16)


def _compress(cmpx, pos2, wbig, w2big, bd128, kgain, B, NC):
    const = lambda a: pl.BlockSpec(a.shape, lambda b: (0,) * a.ndim, pipeline_mode=pl.Buffered(1))
    return pl.pallas_call(
        _compress_kernel,
        out_shape=(jax.ShapeDtypeStruct((B, HKV, NC, LANES), BF16),
                   jax.ShapeDtypeStruct((B, HKV, DH, NC), BF16)),
        grid=(B,),
        in_specs=[pl.BlockSpec((None, NC, cmpx.shape[2]), lambda b: (b, 0, 0)),
                  const(pos2), const(wbig), const(w2big), const(bd128), const(kgain)],
        out_specs=(pl.BlockSpec((None, HKV, NC, LANES), lambda b: (b, 0, 0, 0)),
                   pl.BlockSpec((None, HKV, DH, NC), lambda b: (b, 0, 0, 0))),
        compiler_params=pltpu.CompilerParams(
            dimension_semantics=("arbitrary",), vmem_limit_bytes=VMEM_LIMIT),
        name="compress",
    )(cmpx, pos2, wbig, w2big, bd128, kgain)


def _attn_kernel(sc_ref, q_ref, kc_ref, vcT_ref, ks_ref, vsT_ref, kw_ref, vwT_ref, gT_ref,
                 y_ref, qT_ref, imp_ref, selb_ref, selo_ref, oc_ref, os_ref, ow_ref, acc_ref, *, cch):
    h = pl.program_id(1)
    i = pl.program_id(2)
    nc = kc_ref.shape[0]
    ns = selb_ref.shape[0]
    wspan = WINDOW + QB
    t0 = i * QB
    iota = lambda shape, ax: lax.broadcasted_iota(jnp.int32, shape, ax)
    use_bound = sc_ref[1] > 0.5
    shift = jnp.where(use_bound, -sc_ref[0], 0.0)

    qfT = q_ref[...].astype(F32).T
    for g in range(G):
        qT_ref[0:DH, g * QB:(g + 1) * QB] = qfT[g * DH:(g + 1) * DH, :].astype(BF16)
    qT_ref[DH:LANES, :] = jnp.zeros((LANES - DH, G * QB), BF16)

    def front(nr, bounded):
        nsu = nr // (SLC_BLOCK // CMP_STRIDE)
        cmask = iota((nr, QB), 0) * CMP_STRIDE + (CMP_BLOCK - 1) <= t0 + iota((nr, QB), 1)
        cb = jnp.where(cmask, -sc_ref[2] if bounded else 0.0, NEG)
        s = _dot(kc_ref[0:nr, :], qT_ref[...]) + jnp.concatenate([cb] * G, axis=1)
        p = jnp.exp2(s if bounded else s - jnp.max(s, axis=0, keepdims=True))
        l = jnp.sum(p, axis=0, keepdims=True)
        anyv = jnp.where(t0 + iota((1, QB), 1) >= CMP_BLOCK - 1, 1.0, 0.0)
        pn = p * jnp.where(jnp.concatenate([anyv] * G, axis=1) > 0.5, 1.0 / l, 0.0)
        oc_ref[...] = _dot(vcT_ref[:, 0:nr], pn.astype(BF16))
        imp = pn[:, 0:QB]
        for g in range(1, G):
            imp = imp + pn[:, g * QB:(g + 1) * QB]

        parts = []
        for w in range(QB // LANES):
            imp_ref[w, 0:8, :] = jnp.zeros((8, LANES), F32)
            imp_ref[w, 8:8 + nr, :] = imp[:, w * LANES:(w + 1) * LANES]
            acc = imp_ref[w, pl.ds(7, nsu, stride=4), :]
            for r in range(4):
                acc = acc + imp_ref[w, pl.ds(8 + r, nsu, stride=4), :]
            parts.append(acc)
        islc = jnp.concatenate(parts, axis=1)

        j_i = iota((nsu, QB), 0)
        t_s = t0 + iota((nsu, QB), 1)
        cur = lax.shift_right_logical(t_s, 6)
        valid = j_i * SLC_BLOCK <= t_s
        ninf = -jnp.inf
        score = jnp.where(j_i == 0, ninf, jnp.where(j_i == cur, ninf,
                                                    jnp.where(j_i == cur - 1, ninf, islc)))
        score = jnp.where(valid, score, NEG)
        j_f = j_i.astype(F32)
        for _ in range(N_SEL - 3):
            mx = jnp.max(score, axis=0, keepdims=True)
            jm = jnp.min(jnp.where(score == mx, j_f, float(nsu)), axis=0, keepdims=True)
            score = jnp.where(j_f == jm, -jnp.inf, score)
        own = lax.shift_right_logical(j_i, (QB // SLC_BLOCK).bit_length() - 1) == i
        picked = jnp.where(valid, jnp.where(score == -jnp.inf, shift, NEG), NEG)
        selo_ref[0:nsu, :] = picked
        selb_ref[0:nsu, :] = jnp.where(own, NEG, picked)

    cls = (i * (QB // CMP_STRIDE) + (QB - CMP_BLOCK) // CMP_STRIDE) // cch
    for kk in range(nc // cch):
        for bounded in (True, False):
            pl.when(jnp.logical_and(cls == kk, use_bound == bounded))(
                functools.partial(front, (kk + 1) * cch, bounded))

    bpt = QB // SLC_BLOCK
    own_b = jnp.concatenate(
        [jnp.broadcast_to(selo_ref[pl.ds(i * bpt + b, 1), :], (SLC_BLOCK, QB)) for b in range(bpt)],
        axis=0)
    own_b = jnp.where(iota((QB, QB), 0) <= iota((QB, QB), 1), own_b, NEG)
    own_v = jnp.concatenate([vsT_ref[i * (QB // LANES) + w] for w in range(QB // LANES)], axis=1)
    sd = (_dot(ks_ref[pl.ds(pl.multiple_of(t0, QB), QB), :], qT_ref[...])
          + jnp.concatenate([own_b] * G, axis=1))

    nv = KCH // LANES
    cpg = BIAS_BLOCKS // (KCH // SLC_BLOCK)
    brows = min(BIAS_BLOCKS, ns)
    nstep = (i + NCHUNK - 1) // NCHUNK

    def chunk_scores(c0, n=NCHUNK):
        @pl.when(c0 % cpg == 0)
        def _():
            r0 = pl.multiple_of((c0 // cpg) * brows, brows)
            rows = selb_ref[pl.ds(r0, brows), :].astype(BF16)
            qT_ref[DH:DH + brows, :] = jnp.concatenate([rows] * G, axis=1)

        qT = qT_ref[...]
        return [_dot(ks_ref[pl.ds(pl.multiple_of((c0 + u) * KCH, KCH), KCH), :], qT)
                for u in range(n)]

    def values(c):
        return jnp.concatenate([vsT_ref[c * nv + w] for w in range(nv)], axis=1)

    def finish(acc):
        os_ref[...] = acc[0:DH, :] * (1.0 / acc[DH:DH + 1, :])

    @pl.when(use_bound)
    def _():
        def add_chunks(c0, n, acc):
            r0 = pl.multiple_of((c0 // cpg) * brows, brows)
            rows = selb_ref[pl.ds(r0, brows), :].astype(BF16)
            qT_ref[DH:DH + brows, :] = jnp.concatenate([rows] * G, axis=1)
            qT = qT_ref[...]
            for sub in range(0, n, NCHUNK):
                cs = [c0 + sub + u for u in range(min(NCHUNK, n - sub))]
                ss = [_dot(ks_ref[pl.ds(pl.multiple_of(c * KCH, KCH), KCH), :], qT) for c in cs]
                for c, s in zip(cs, ss):
                    acc = acc + _dot(values(c), jnp.exp2(s).astype(BF16))
            return acc

        nfull = i // cpg
        rem = i - nfull * cpg
        acc0 = _dot(own_v, jnp.exp2(sd).astype(BF16))
        acc_ref[...] = lax.fori_loop(
            0, nfull, lambda grp, acc: add_chunks(grp * cpg, cpg, acc), acc0)

        half = jnp.where(rem >= NCHUNK, NCHUNK, 0)
        c1 = nfull * cpg + half
        rest = rem - half

        @pl.when(half > 0)
        def _():
            acc_ref[...] = add_chunks(nfull * cpg, NCHUNK, acc_ref[...])

        @pl.when(jnp.logical_and(rest > 0, rest <= NCHUNK // 2))
        def _():
            acc_ref[...] = add_chunks(c1, NCHUNK // 2, acc_ref[...])

        @pl.when(rest > NCHUNK // 2)
        def _():
            acc_ref[...] = add_chunks(c1, NCHUNK, acc_ref[...])

        finish(acc_ref[...])

    @pl.when(jnp.logical_not(use_bound))
    def _():
        def step(grp, carry):
            m, acc = carry
            ss = chunk_scores(grp * NCHUNK)
            for u in range(NCHUNK):
                s = ss[u]
                m_new = jnp.maximum(m, jnp.max(s, axis=0, keepdims=True))
                p = jnp.exp2(s - m_new).astype(BF16)
                acc = jnp.exp2(m - m_new) * acc + _dot(values(grp * NCHUNK + u), p)
                m = m_new
            return m, acc

        m0 = jnp.max(sd, axis=0, keepdims=True)
        acc0 = _dot(own_v, jnp.exp2(sd - m0).astype(BF16))
        finish(lax.fori_loop(0, nstep, step, (m0, acc0))[1])

    os_ = os_ref[...]

    def window(bounded):
        wc = jnp.maximum(i * (QB // LANES) - WINDOW // LANES, 0)
        ws = wc * LANES
        sw = _dot(kw_ref[pl.ds(pl.multiple_of(ws, LANES), wspan), :], qT_ref[...])
        kpos = ws + iota((wspan, QB), 0)
        t_w = t0 + iota((wspan, QB), 1)
        inside = -sc_ref[3] if bounded else 0.0
        wb = jnp.where(kpos <= t_w, jnp.where(kpos > t_w - WINDOW, inside, NEG), NEG)
        sw = sw + jnp.concatenate([wb] * G, axis=1)
        p_w = jnp.exp2(sw if bounded else sw - jnp.max(sw, axis=0, keepdims=True)).astype(BF16)
        vw = jnp.concatenate([vwT_ref[wc + u] for u in range(wspan // LANES)], axis=1)
        acc_w = _dot(vw, p_w)
        ow_ref[...] = acc_w[0:DH, :] * (1.0 / acc_w[DH:DH + 1, :])

    for bounded in (True, False):
        pl.when(use_bound == bounded)(functools.partial(window, bounded))
    ow = ow_ref[...]

    outs = []
    for g in range(G):
        sl = slice(g * QB, (g + 1) * QB)
        row = h * (3 * G) + 3 * g
        gc = gT_ref[pl.ds(row, 1), :]
        gs = gT_ref[pl.ds(row + 1, 1), :]
        gw = gT_ref[pl.ds(row + 2, 1), :]
        outs.append(gc * oc_ref[:, sl] + gs * os_[:, sl] + gw * ow[:, sl])
    y_ref[...] = jnp.concatenate(outs, axis=0).T


def _score_bound(qgain, kgain_slc, kgain_win, kgain_cmp):
    qmax = jnp.max(jnp.abs(qgain))
    ms = [(1.05 * DH * qmax * jnp.max(jnp.abs(kg))).astype(BF16).astype(F32)
          for kg in (kgain_slc, kgain_cmp, kgain_win)]
    ok = (jnp.maximum(jnp.maximum(ms[0], ms[1]), ms[2]) <= SAFE_SHIFT).astype(F32)
    return jnp.stack([ms[0], ok, ms[1], ms[2]])


def _attention(sc, q, kc, vcT, kslc, vslcT, kwin, vwinT, gT, B, S):
    NC = S // CMP_STRIDE
    NS = S // SLC_BLOCK
    NQ = S // QB
    cch = min(CCH, NC)
    assert NC % cch == 0 and (S // KCH) % NCHUNK == 0 and QB == KCH
    assert (BIAS_BLOCKS * SLC_BLOCK // KCH) % NCHUNK == 0
    head = lambda *blk: pl.BlockSpec((None, None) + blk, lambda b, h, i: (b, h) + (0,) * len(blk),
                                     pipeline_mode=pl.Buffered(1))
    return pl.pallas_call(
        functools.partial(_attn_kernel, cch=cch),
        out_shape=jax.ShapeDtypeStruct((B, S, NSA_W), F32),
        grid=(B, HKV, NQ),
        in_specs=[
            pl.BlockSpec(memory_space=pltpu.SMEM),
            pl.BlockSpec((None, QB, G * DH), lambda b, h, i: (b, i, h)),
            head(NC, LANES), head(DH, NC),
            head(S, LANES), head(S // LANES, VROWS, LANES),
            head(S, LANES), head(S // LANES, VROWS, LANES),
            pl.BlockSpec((None, GATE_ROWS, QB), lambda b, h, i: (b, 0, i)),
        ],
        out_specs=pl.BlockSpec((None, QB, G * DH), lambda b, h, i: (b, i, h)),
        scratch_shapes=[pltpu.VMEM((LANES, G * QB), BF16),
                        pltpu.VMEM((QB // LANES, NC + 8, LANES), F32),
                        pltpu.VMEM((NS, QB), F32), pltpu.VMEM((NS, QB), F32),
                        pltpu.VMEM((DH, G * QB), F32),
                        pltpu.VMEM((DH, G * QB), F32), pltpu.VMEM((DH, G * QB), F32),
                        pltpu.VMEM((VROWS, G * QB), F32)],
        compiler_params=pltpu.CompilerParams(
            dimension_semantics=("arbitrary", "arbitrary", "arbitrary"),
            vmem_limit_bytes=VMEM_LIMIT),
        name="attention",
    )(sc, q, kc, vcT, kslc, vslcT, kwin, vwinT, gT)


XH = 8
YH = 32


def _seq_mix_kernel(lru_ref, cv_ref, cw_ref, cb_ref, wa_ref, ba_ref, wx_ref, bx_ref, lam_ref,
                    dw_ref, db_ref, lng_ref, lnb_ref, gl_ref, gc_ref,
                    yl_ref, yc_ref, xbuf, ybuf, hbuf):
    t = pl.program_id(1)

    @pl.when(t == 0)
    def _():
        xbuf[0:XH, :] = jnp.zeros((XH, LRU_W), F32)
        ybuf[0:YH, :] = jnp.zeros((YH, CV_W), F32)
        hbuf[...] = jnp.zeros((8, LRU_W), F32)

    xb = lru_ref[:, 0:LRU_W]
    gb = lru_ref[:, LRU_W:2 * LRU_W]
    xbuf[XH:XH + TS, :] = xb
    xr = jnp.broadcast_to(cb_ref[...], (TS, LRU_W))
    for k in range(LRU_CONV):
        xr = xr + cw_ref[k:k + 1, :] * xbuf[XH - (LRU_CONV - 1) + k:XH - (LRU_CONV - 1) + k + TS, :]
    xbuf[0:XH, :] = xbuf[TS:TS + XH, :]

    xr16 = xr.astype(BF16)
    r = _sigmoid(_dot(xr16, wa_ref[...]) + ba_ref[...])
    ig = _sigmoid(_dot(xr16, wx_ref[...]) + bx_ref[...])
    nl = -lam_ref[...]
    softplus = jnp.maximum(nl, 0.0) + jnp.log1p(jnp.exp(-jnp.abs(nl)))
    log_a = -LRU_C * r * softplus
    a = jnp.exp(log_a)
    u = xr * ig * jnp.sqrt(-jnp.tanh(log_a) * (a * a + 1.0))

    row = lax.broadcasted_iota(jnp.int32, (TS, LRU_W), 0)
    d = 1
    while d < TS:
        keep = row >= d
        a_sh = jnp.where(keep, pltpu.roll(a, d, 0), 1.0)
        u_sh = jnp.where(keep, pltpu.roll(u, d, 0), 0.0)
        u = a * u_sh + u
        a = a * a_sh
        d *= 2
    hseq = a * hbuf[0:1, :] + u
    hbuf[...] = jnp.broadcast_to(hseq[TS - 1:TS, :], (8, LRU_W))
    yl = hseq * _gelu_tanh(gb)
    yl = yl * lax.rsqrt(jnp.mean(yl * yl, axis=-1, keepdims=True) + EPS) * gl_ref[...]
    yl_ref[...] = yl.astype(BF16)

    y = cv_ref[:, 0:CV_W] * _sigmoid(cv_ref[:, CV_W:2 * CV_W])
    ybuf[YH:YH + TS, :] = y
    c = jnp.broadcast_to(db_ref[...], (TS, CV_W))
    for k in range(CV_KERNEL):
        off = YH - (CV_KERNEL - 1) + k
        c = c + dw_ref[k:k + 1, :] * ybuf[off:off + TS, :]
    ybuf[0:YH, :] = ybuf[TS:TS + YH, :]
    mu = jnp.mean(c, axis=-1, keepdims=True)
    cc = c - mu
    var = jnp.mean(cc * cc, axis=-1, keepdims=True)
    ln = cc * lax.rsqrt(var + EPS) * lng_ref[...] + lnb_ref[...]
    yc = ln * _sigmoid(ln)
    yc = yc * lax.rsqrt(jnp.mean(yc * yc, axis=-1, keepdims=True) + EPS) * gc_ref[...]
    yc_ref[...] = yc.astype(BF16)


def _seq_mix(lru, cv, params, B, S):
    const = lambda a: pl.BlockSpec(a.shape, lambda b, t: (0,) * a.ndim)
    tile = lambda w_: pl.BlockSpec((None, TS, w_), lambda b, t: (b, t, 0))
    return pl.pallas_call(
        _seq_mix_kernel,
        out_shape=(jax.ShapeDtypeStruct((B, S, LRU_W), BF16),
                   jax.ShapeDtypeStruct((B, S, CV_W), BF16)),
        grid=(B, S // TS),
        in_specs=[tile(2 * LRU_W), tile(2 * CV_W)] + [const(p) for p in params],
        out_specs=(tile(LRU_W), tile(CV_W)),
        scratch_shapes=[pltpu.VMEM((TS + XH, LRU_W), F32), pltpu.VMEM((TS + YH, CV_W), F32),
                        pltpu.VMEM((8, LRU_W), F32)],
        compiler_params=pltpu.CompilerParams(
            dimension_semantics=("arbitrary", "arbitrary"), vmem_limit_bytes=VMEM_LIMIT),
        name="seq_mix",
    )(lru, cv, *params)


def _out_mlp_kernel(x_ref, ya_ref, yl_ref, yc_ref, ga_ref, wo_ref, gm_ref, w1_ref, w2_ref, o_ref):
    ya = ya_ref[...]
    ya = (ya * lax.rsqrt(jnp.mean(ya * ya, axis=-1, keepdims=True) + EPS) * ga_ref[...]).astype(BF16)
    x1 = (x_ref[...] + _dot(ya, wo_ref[0:NSA_W, :])
          + _dot(yl_ref[...], wo_ref[NSA_W:NSA_W + LRU_W, :])
          + _dot(yc_ref[...], wo_ref[NSA_W + LRU_W:NSA_W + LRU_W + CV_W, :]))
    hm = (x1 * lax.rsqrt(jnp.mean(x1 * x1, axis=-1, keepdims=True) + EPS) * gm_ref[...]).astype(BF16)
    d_ff = w1_ref.shape[1]
    fc = 1024
    o_ref[...] = x1
    for c in range(d_ff // fc):
        hc = jnp.maximum(_dot(hm, w1_ref[:, c * fc:(c + 1) * fc]), 0.0)
        o_ref[...] += _dot((hc * hc).astype(BF16), w2_ref[c * fc:(c + 1) * fc, :])


def _out_mlp(x2, ya, yl, yc, ga, wo, gm, w1, w2):
    T, D = x2.shape
    tok = lambda w_: pl.BlockSpec((TM, w_), lambda t: (t, 0))
    const1 = lambda a: pl.BlockSpec(a.shape, lambda t: (0,) * a.ndim, pipeline_mode=pl.Buffered(1))
    return pl.pallas_call(
        _out_mlp_kernel,
        out_shape=jax.ShapeDtypeStruct((T, D), F32),
        grid=(T // TM,),
        in_specs=[tok(D), tok(NSA_W), tok(LRU_W), tok(CV_W),
                  const1(ga), const1(wo), const1(gm), const1(w1), const1(w2)],
        out_specs=tok(D),
        compiler_params=pltpu.CompilerParams(
            dimension_semantics=("arbitrary",), vmem_limit_bytes=VMEM_LIMIT),
        name="out_mlp",
    )(x2, ya, yl, yc, ga, wo, gm, w1, w2)


def _block_ones(n):
    idx = np.arange(n) // DH
    return jnp.asarray((idx[:, None] == idx[None, :]).astype(np.float32) / DH, dtype=BF16)


def _permute_w_in(w):
    off_kv = NSA_W
    kv = [w[:, off_kv + c * 128: off_kv + (c + 1) * 128] for c in range(6)]
    off_gate = off_kv + 6 * 128
    ngate = 3 * HKV * G
    off_lru = off_gate + ngate
    gate = jnp.pad(w[:, off_gate:off_lru], ((0, 0), (0, LANES - ngate)))
    cols = [w[:, 0:NSA_W], kv[0], kv[1], kv[2], kv[4], kv[3], kv[5],
            w[:, off_lru:off_lru + 2 * LRU_W + 2 * CV_W], gate]
    return jnp.concatenate(cols, axis=1).astype(BF16)


def _compress_weights(pos, w1, w2):
    slot_kv = jnp.array([0, 0, 1, 1])
    eye = jnp.eye(4, dtype=F32)
    w1r = w1.reshape(2, 2, CMP_STRIDE, DH, CMP_HID)[slot_kv]
    wbig = jnp.einsum('shpdo,st->psdhto', w1r, eye).reshape(CMP_STRIDE * 4 * DH, 2 * 4 * CMP_HID)
    posr = pos.reshape(2, 2, CMP_STRIDE, DH)[slot_kv]
    pos2 = jnp.transpose(posr, (1, 2, 0, 3)).reshape(2, CMP_STRIDE * 4 * DH)
    pos2 = jnp.pad(pos2, ((0, 6), (0, 0)))
    w2big = jnp.einsum('sod,st->sotd', w2[slot_kv], eye).reshape(4 * CMP_HID, 4 * DH)
    return pos2.astype(BF16), wbig.astype(BF16), w2big.astype(BF16)


def _block_diag(w):
    hh, bw, _ = w.shape
    eye = jnp.eye(hh, dtype=w.dtype)
    return jnp.einsum('hij,hg->higj', w, eye).reshape(hh * bw, hh * bw).astype(BF16)


def kernel(x, attn_norm, w_in, q_norm, k_norm, cmp_pos, cmp_w1, cmp_w2, lru_conv_w, lru_conv_b,
           lru_wa, lru_ba, lru_wx, lru_bx, lru_lambda, cv_dw_w, cv_dw_b, cv_ln_g, cv_ln_b,
           out_norm, w_out, mlp_norm, mlp_w1, mlp_w2):
    B, S, D = x.shape
    depth = w_in.shape[0]
    assert S % TM == 0 and S % TS == 0 and S >= WINDOW + QB and S // SLC_BLOCK >= N_SEL
    NC = S // CMP_STRIDE
    row = lambda v: v.reshape(1, -1).astype(F32)
    bd512 = _block_ones(NSA_W)
    bd128 = _block_ones(128)

    x2 = x.reshape(B * S, D)
    for l in range(depth):
        qgain = row(jnp.tile(q_norm[l], HKV * G)) * (DH ** -0.5 * LOG2E)
        kgain = row(jnp.concatenate([jnp.tile(k_norm[l, 1], HKV), jnp.tile(k_norm[l, 2], HKV)]))
        q, cmpr, kslc, kwin, vslcT, vwinT, lru, cv, gT = _in_proj(
            x2, row(attn_norm[l]), _permute_w_in(w_in[l]), bd512, qgain, kgain, B, S)

        pos2, wbig, w2big = _compress_weights(cmp_pos[l], cmp_w1[l], cmp_w2[l])
        kc, vcT = _compress(cmpr.reshape(B, NC, CMP_STRIDE * 256), pos2, wbig, w2big, bd128,
                            row(jnp.tile(k_norm[l, 0], HKV)), B, NC)

        sc = _score_bound(qgain, k_norm[l, 1], k_norm[l, 2], k_norm[l, 0])
        y_attn = _attention(sc, q.reshape(B, S, NSA_W), kc, vcT, kslc, vslcT, kwin, vwinT, gT, B, S)

        g_out = out_norm[l]
        seq_params = (lru_conv_w[l], row(lru_conv_b[l]), _block_diag(lru_wa[l]), row(lru_ba[l]),
                      _block_diag(lru_wx[l]), row(lru_bx[l]), row(lru_lambda[l]),
                      cv_dw_w[l], row(cv_dw_b[l]), row(cv_ln_g[l]), row(cv_ln_b[l]),
                      row(g_out[NSA_W:NSA_W + LRU_W]), row(g_out[NSA_W + LRU_W:]))
        yl, yc = _seq_mix(lru.reshape(B, S, 2 * LRU_W), cv.reshape(B, S, 2 * CV_W), seq_params, B, S)

        x2 = _out_mlp(x2, y_attn.reshape(B * S, NSA_W), yl.reshape(B * S, LRU_W),
                      yc.reshape(B * S, CV_W), row(g_out[:NSA_W]), w_out[l].astype(BF16),
                      row(mlp_norm[l]), mlp_w1[l].astype(BF16), mlp_w2[l].astype(BF16))
    return x2.reshape(B, S, D)
```

```python
import functools

import numpy as np
import jax
import jax.numpy as jnp
from jax import lax
from jax.experimental import pallas as pl
from jax.experimental.pallas import tpu as pltpu

F32 = jnp.float32
BF16 = jnp.bfloat16

EPS = 1e-6
NEG = -1e30
FORCED = 1e6
DH = 64
HKV = 2
G = 4
NSA_W = HKV * G * DH
CMP_BLOCK = 32
CMP_STRIDE = 16
CMP_HID = 128
SLC_BLOCK = 64
N_SEL = 16
WINDOW = 512
QB = 256
LRU_W = 256
LRU_HEADS = 8
LRU_CONV = 4
LRU_C = 8.0
CV_W = 256
CV_KERNEL = 31
LOG2E = 1.4426950408889634
SAFE_SHIFT = 50.0

LANES = 128
KCH = 256
NCHUNK = 8
CCH = 256
TM = 512
TS = 512
VMEM_LIMIT = 48 * 1024 * 1024

C_Q = 0
C_CMP = C_Q + NSA_W
C_KSLC = C_CMP + 256
C_KWIN = C_KSLC + 128
C_VSLC = C_KWIN + 128
C_VWIN = C_VSLC + 128
C_LRU = C_VWIN + 128
C_CV = C_LRU + 2 * LRU_W
C_GATE = C_CV + 2 * CV_W
N_INP = C_GATE + LANES
GATE_ROWS = 32
VROWS = DH + 16
BIAS_BLOCKS = LANES - DH


def _dot(a, b):
    return jnp.dot(a, b, preferred_element_type=F32)


def _head_rms_scale(z, bd):
    sq = z * z
    hi = sq.astype(BF16)
    lo = (sq - hi.astype(F32)).astype(BF16)
    ms = _dot(hi, bd) + _dot(lo, bd)
    return lax.rsqrt(ms + EPS)


def _gelu_tanh(x):
    return 0.5 * x * (1.0 + jnp.tanh(0.7978845608028654 * (x + 0.044715 * (x * x * x))))


def _sigmoid(x):
    return 1.0 / (1.0 + jnp.exp(-x))


def _in_proj_kernel(x_ref, g_ref, w_ref, bd_ref, qgain_ref, kgain_ref,
                    q_ref, cmp_ref, kslc_ref, kwin_ref, vslcT_ref, vwinT_ref,
                    lru_ref, cv_ref, gT_ref, *, nt):
    x = x_ref[...]
    ms = jnp.mean(x * x, axis=-1, keepdims=True)
    hn = (x * lax.rsqrt(ms + EPS) * g_ref[...]).astype(BF16)

    zq = _dot(hn, w_ref[:, C_Q:C_CMP])
    q_ref[...] = (zq * _head_rms_scale(zq, bd_ref[...]) * qgain_ref[...]).astype(BF16)

    cmp_ref[...] = _dot(hn, w_ref[:, C_CMP:C_KSLC]).astype(BF16)

    zk = _dot(hn, w_ref[:, C_KSLC:C_VSLC])
    kn = zk * _head_rms_scale(zk, bd_ref[0:256, 0:256]) * kgain_ref[...]
    lane = lax.broadcasted_iota(jnp.int32, (TM, LANES), 1)
    rowg = (pl.program_id(0) % nt) * TM + lax.broadcasted_iota(jnp.int32, (TM, LANES), 0)
    blk = lax.shift_right_logical(rowg, 6) & (BIAS_BLOCKS - 1)
    onehot = jnp.where(lane == DH + blk, 1.0, 0.0)
    lo = lane < DH
    ks, kw = kn[:, 0:128], kn[:, 128:256]
    kslc_ref[0] = jnp.where(lo, ks, onehot).astype(BF16)
    kslc_ref[1] = jnp.where(lo, pltpu.roll(ks, DH, 1), onehot).astype(BF16)
    kwin_ref[0] = jnp.where(lo, kw, 0.0).astype(BF16)
    kwin_ref[1] = jnp.where(lo, pltpu.roll(kw, DH, 1), 0.0).astype(BF16)

    zv = _dot(hn, w_ref[:, C_VSLC:C_LRU])
    zvT = zv.T
    ones_row = jnp.where(lax.broadcasted_iota(jnp.int32, (VROWS - DH, LANES), 0) == 0,
                         1.0, 0.0).astype(BF16)
    for j in range(TM // LANES):
        for h in range(HKV):
            cols = slice(j * LANES, (j + 1) * LANES)
            vslcT_ref[h, j, 0:DH, :] = zvT[h * DH:(h + 1) * DH, cols].astype(BF16)
            vslcT_ref[h, j, DH:VROWS, :] = ones_row
            vwinT_ref[h, j, 0:DH, :] = zvT[128 + h * DH:128 + (h + 1) * DH, cols].astype(BF16)
            vwinT_ref[h, j, DH:VROWS, :] = ones_row

    lru_ref[...] = _dot(hn, w_ref[:, C_LRU:C_CV])
    cv_ref[...] = _dot(hn, w_ref[:, C_CV:C_GATE])

    zg = _sigmoid(_dot(hn, w_ref[:, C_GATE:N_INP]))
    gT_ref[...] = zg.T[0:GATE_ROWS, :]


def _in_proj(x2, g, w, bd, qgain, kgain, B, S):
    T, D = x2.shape
    nt = S // TM
    tok = lambda w_: pl.BlockSpec((TM, w_), lambda t: (t, 0))
    const = lambda a: pl.BlockSpec(a.shape, lambda t: (0,) * a.ndim)
    vT_spec = pl.BlockSpec((None, HKV, TM // LANES, VROWS, LANES),
                           lambda t: (t // nt, 0, t % nt, 0, 0))
    k_spec = pl.BlockSpec((None, HKV, TM, LANES), lambda t: (t // nt, 0, t % nt, 0))
    out_shape = (
        jax.ShapeDtypeStruct((T, NSA_W), BF16),
        jax.ShapeDtypeStruct((T, 256), BF16),
        jax.ShapeDtypeStruct((B, HKV, S, LANES), BF16),
        jax.ShapeDtypeStruct((B, HKV, S, LANES), BF16),
        jax.ShapeDtypeStruct((B, HKV, S // LANES, VROWS, LANES), BF16),
        jax.ShapeDtypeStruct((B, HKV, S // LANES, VROWS, LANES), BF16),
        jax.ShapeDtypeStruct((T, 2 * LRU_W), F32),
        jax.ShapeDtypeStruct((T, 2 * CV_W), F32),
        jax.ShapeDtypeStruct((B, GATE_ROWS, S), F32),
    )
    out_specs = (
        tok(NSA_W), tok(256), k_spec, k_spec, vT_spec, vT_spec,
        tok(2 * LRU_W), tok(2 * CV_W),
        pl.BlockSpec((None, GATE_ROWS, TM), lambda t: (t // nt, 0, t % nt)),
    )
    return pl.pallas_call(
        functools.partial(_in_proj_kernel, nt=nt),
        out_shape=out_shape,
        grid=(T // TM,),
        in_specs=[tok(D), const(g), const(w), const(bd), const(qgain), const(kgain)],
        out_specs=out_specs,
        compiler_params=pltpu.CompilerParams(
            dimension_semantics=("arbitrary",), vmem_limit_bytes=VMEM_LIMIT),
        name="in_proj",
    )(x2, g, w, bd, qgain, kgain)


def _compress_kernel(x_ref, pos_ref, wbig_ref, w2_ref, bd_ref, kgain_ref, kc_ref, vcT_ref):
    nc = x_ref.shape[0]
    half = 4 * CMP_HID
    p = _dot(x_ref[...], wbig_ref[...])
    pc = _dot(pos_ref[...], wbig_ref[...])
    const = pc[0:1, 0:half] + pc[1:2, half:2 * half]
    nxt = pltpu.roll(p[:, half:2 * half], nc - 1, 0)
    hid = _gelu_tanh(p[:, 0:half] + nxt + const).astype(BF16)
    kv = _dot(hid, w2_ref[...])
    kc = kv[:, 0:128]
    kc = kc * _head_rms_scale(kc, bd_ref[...]) * kgain_ref[...]
    lo = lax.broadcasted_iota(jnp.int32, (nc, LANES), 1) < DH
    kc_ref[0] = jnp.where(lo, kc, 0.0).astype(BF16)
    kc_ref[1] = jnp.where(lo, pltpu.roll(kc, DH, 1), 0.0).astype(BF16)
    vT = kv[:, 128:256].T
    for h in range(HKV):
        vcT_ref[h] = vT[h * DH:(h + 1) * DH, :].astype(BF16)


def _compress(cmpx, pos2, wbig, w2big, bd128, kgain, B, NC):
    const = lambda a: pl.BlockSpec(a.shape, lambda b: (0,) * a.ndim, pipeline_mode=pl.Buffered(1))
    return pl.pallas_call(
        _compress_kernel,
        out_shape=(jax.ShapeDtypeStruct((B, HKV, NC, LANES), BF16),
                   jax.ShapeDtypeStruct((B, HKV, DH, NC), BF16)),
        grid=(B,),
        in_specs=[pl.BlockSpec((None, NC, cmpx.shape[2]), lambda b: (b, 0, 0)),
                  const(pos2), const(wbig), const(w2big), const(bd128), const(kgain)],
        out_specs=(pl.BlockSpec((None, HKV, NC, LANES), lambda b: (b, 0, 0, 0)),
                   pl.BlockSpec((None, HKV, DH, NC), lambda b: (b, 0, 0, 0))),
        compiler_params=pltpu.CompilerParams(
            dimension_semantics=("arbitrary",), vmem_limit_bytes=VMEM_LIMIT),
        name="compress",
    )(cmpx, pos2, wbig, w2big, bd128, kgain)


def _attn_kernel(sc_ref, q_ref, kc_ref, vcT_ref, ks_ref, vsT_ref, kw_ref, vwT_ref, gT_ref,
                 y_ref, qT_ref, imp_ref, selb_ref, selo_ref, oc_ref, os_ref, ow_ref, acc_ref, *, cch):
    h = pl.program_id(1)
    i = pl.program_id(2)
    nc = kc_ref.shape[0]
    ns = selb_ref.shape[0]
    wspan = WINDOW + QB
    t0 = i * QB
    iota = lambda shape, ax: lax.broadcasted_iota(jnp.int32, shape, ax)
    use_bound = sc_ref[1] > 0.5
    shift = jnp.where(use_bound, -sc_ref[0], 0.0)

    qfT = q_ref[...].astype(F32).T
    for g in range(G):
        qT_ref[0:DH, g * QB:(g + 1) * QB] = qfT[g * DH:(g + 1) * DH, :].astype(BF16)
    qT_ref[DH:LANES, :] = jnp.zeros((LANES - DH, G * QB), BF16)

    def front(nr, bounded):
        nsu = nr // (SLC_BLOCK // CMP_STRIDE)
        cmask = iota((nr, QB), 0) * CMP_STRIDE + (CMP_BLOCK - 1) <= t0 + iota((nr, QB), 1)
        cb = jnp.where(cmask, -sc_ref[2] if bounded else 0.0, NEG)
        s = _dot(kc_ref[0:nr, :], qT_ref[...]) + jnp.concatenate([cb] * G, axis=1)
        p = jnp.exp2(s if bounded else s - jnp.max(s, axis=0, keepdims=True))
        l = jnp.sum(p, axis=0, keepdims=True)
        anyv = jnp.where(t0 + iota((1, QB), 1) >= CMP_BLOCK - 1, 1.0, 0.0)
        pn = p * jnp.where(jnp.concatenate([anyv] * G, axis=1) > 0.5, 1.0 / l, 0.0)
        oc_ref[...] = _dot(vcT_ref[:, 0:nr], pn.astype(BF16))
        imp = pn[:, 0:QB]
        for g in range(1, G):
            imp = imp + pn[:, g * QB:(g + 1) * QB]

        parts = []
        for w in range(QB // LANES):
            imp_ref[w, 0:8, :] = jnp.zeros((8, LANES), F32)
            imp_ref[w, 8:8 + nr, :] = imp[:, w * LANES:(w + 1) * LANES]
            acc = imp_ref[w, pl.ds(7, nsu, stride=4), :]
            for r in range(4):
                acc = acc + imp_ref[w, pl.ds(8 + r, nsu, stride=4), :]
            parts.append(acc)
        islc = jnp.concatenate(parts, axis=1)

        j_i = iota((nsu, QB), 0)
        t_s = t0 + iota((nsu, QB), 1)
        cur = lax.shift_right_logical(t_s, 6)
        valid = j_i * SLC_BLOCK <= t_s
        ninf = -jnp.inf
        score = jnp.where(j_i == 0, ninf, jnp.where(j_i == cur, ninf,
                                                    jnp.where(j_i == cur - 1, ninf, islc)))
        score = jnp.where(valid, score, NEG)
        j_f = j_i.astype(F32)
        for _ in range(N_SEL - 3):
            mx = jnp.max(score, axis=0, keepdims=True)
            jm = jnp.min(jnp.where(score == mx, j_f, float(nsu)), axis=0, keepdims=True)
            score = jnp.where(j_f == jm, -jnp.inf, score)
        own = lax.shift_right_logical(j_i, (QB // SLC_BLOCK).bit_length() - 1) == i
        picked = jnp.where(valid, jnp.where(score == -jnp.inf, shift, NEG), NEG)
        selo_ref[0:nsu, :] = picked
        selb_ref[0:nsu, :] = jnp.where(own, NEG, picked)

    cls = (i * (QB // CMP_STRIDE) + (QB - CMP_BLOCK) // CMP_STRIDE) // cch
    for kk in range(nc // cch):
        for bounded in (True, False):
            pl.when(jnp.logical_and(cls == kk, use_bound == bounded))(
                functools.partial(front, (kk + 1) * cch, bounded))

    bpt = QB // SLC_BLOCK
    own_b = jnp.concatenate(
        [jnp.broadcast_to(selo_ref[pl.ds(i * bpt + b, 1), :], (SLC_BLOCK, QB)) for b in range(bpt)],
        axis=0)
    own_b = jnp.where(iota((QB, QB), 0) <= iota((QB, QB), 1), own_b, NEG)
    own_v = jnp.concatenate([vsT_ref[i * (QB // LANES) + w] for w in range(QB // LANES)], axis=1)
    sd = (_dot(ks_ref[pl.ds(pl.multiple_of(t0, QB), QB), :], qT_ref[...])
          + jnp.concatenate([own_b] * G, axis=1))

    nv = KCH // LANES
    cpg = BIAS_BLOCKS // (KCH // SLC_BLOCK)
    brows = min(BIAS_BLOCKS, ns)
    nstep = (i + NCHUNK - 1) // NCHUNK

    def chunk_scores(c0, n=NCHUNK):
        @pl.when(c0 % cpg == 0)
        def _():
            r0 = pl.multiple_of((c0 // cpg) * brows, brows)
            rows = selb_ref[pl.ds(r0, brows), :].astype(BF16)
            qT_ref[DH:DH + brows, :] = jnp.concatenate([rows] * G, axis=1)

        qT = qT_ref[...]
        return [_dot(ks_ref[pl.ds(pl.multiple_of((c0 + u) * KCH, KCH), KCH), :], qT)
                for u in range(n)]

    def values(c):
        return jnp.concatenate([vsT_ref[c * nv + w] for w in range(nv)], axis=1)

    def finish(acc):
        os_ref[...] = acc[0:DH, :] * (1.0 / acc[DH:DH + 1, :])

    @pl.when(use_bound)
    def _():
        def add_chunks(c0, n, acc):
            r0 = pl.multiple_of((c0 // cpg) * brows, brows)
            rows = selb_ref[pl.ds(r0, brows), :].astype(BF16)
            qT_ref[DH:DH + brows, :] = jnp.concatenate([rows] * G, axis=1)
            qT = qT_ref[...]
            for sub in range(0, n, NCHUNK):
                cs = [c0 + sub + u for u in range(min(NCHUNK, n - sub))]
                ss = [_dot(ks_ref[pl.ds(pl.multiple_of(c * KCH, KCH), KCH), :], qT) for c in cs]
                for c, s in zip(cs, ss):
                    acc = acc + _dot(values(c), jnp.exp2(s).astype(BF16))
            return acc

        nfull = i // cpg
        rem = i - nfull * cpg
        acc0 = _dot(own_v, jnp.exp2(sd).astype(BF16))
        acc_ref[...] = lax.fori_loop(
            0, nfull, lambda grp, acc: add_chunks(grp * cpg, cpg, acc), acc0)

        half = jnp.where(rem >= NCHUNK, NCHUNK, 0)
        c1 = nfull * cpg + half
        rest = rem - half

        @pl.when(half > 0)
        def _():
            acc_ref[...] = add_chunks(nfull * cpg, NCHUNK, acc_ref[...])

        @pl.when(jnp.logical_and(rest > 0, rest <= NCHUNK // 2))
        def _():
            acc_ref[...] = add_chunks(c1, NCHUNK // 2, acc_ref[...])

        @pl.when(rest > NCHUNK // 2)
        def _():
            acc_ref[...] = add_chunks(c1, NCHUNK, acc_ref[...])

        finish(acc_ref[...])

    @pl.when(jnp.logical_not(use_bound))
    def _():
        def step(grp, carry):
            m, acc = carry
            ss = chunk_scores(grp * NCHUNK)
            for u in range(NCHUNK):
                s = ss[u]
                m_new = jnp.maximum(m, jnp.max(s, axis=0, keepdims=True))
                p = jnp.exp2(s - m_new).astype(BF16)
                acc = jnp.exp2(m - m_new) * acc + _dot(values(grp * NCHUNK + u), p)
                m = m_new
            return m, acc

        m0 = jnp.max(sd, axis=0, keepdims=True)
        acc0 = _dot(own_v, jnp.exp2(sd - m0).astype(BF16))
        finish(lax.fori_loop(0, nstep, step, (m0, acc0))[1])

    os_ = os_ref[...]

    def window(bounded):
        wc = jnp.maximum(i * (QB // LANES) - WINDOW // LANES, 0)
        ws = wc * LANES
        sw = _dot(kw_ref[pl.ds(pl.multiple_of(ws, LANES), wspan), :], qT_ref[...])
        kpos = ws + iota((wspan, QB), 0)
        t_w = t0 + iota((wspan, QB), 1)
        inside = -sc_ref[3] if bounded else 0.0
        wb = jnp.where(kpos <= t_w, jnp.where(kpos > t_w - WINDOW, inside, NEG), NEG)
        sw = sw + jnp.concatenate([wb] * G, axis=1)
        p_w = jnp.exp2(sw if bounded else sw - jnp.max(sw, axis=0, keepdims=True)).astype(BF16)
        vw = jnp.concatenate([vwT_ref[wc + u] for u in range(wspan // LANES)], axis=1)
        acc_w = _dot(vw, p_w)
        ow_ref[...] = acc_w[0:DH, :] * (1.0 / acc_w[DH:DH + 1, :])

    for bounded in (True, False):
        pl.when(use_bound == bounded)(functools.partial(window, bounded))
    ow = ow_ref[...]

    outs = []
    for g in range(G):
        sl = slice(g * QB, (g + 1) * QB)
        row = h * (3 * G) + 3 * g
        gc = gT_ref[pl.ds(row, 1), :]
        gs = gT_ref[pl.ds(row + 1, 1), :]
        gw = gT_ref[pl.ds(row + 2, 1), :]
        outs.append(gc * oc_ref[:, sl] + gs * os_[:, sl] + gw * ow[:, sl])
    y_ref[...] = jnp.concatenate(outs, axis=0).T


def _score_bound(qgain, kgain_slc, kgain_win, kgain_cmp):
    qmax = jnp.max(jnp.abs(qgain))
    ms = [(1.05 * DH * qmax * jnp.max(jnp.abs(kg))).astype(BF16).astype(F32)
          for kg in (kgain_slc, kgain_cmp, kgain_win)]
    ok = (jnp.maximum(jnp.maximum(ms[0], ms[1]), ms[2]) <= SAFE_SHIFT).astype(F32)
    return jnp.stack([ms[0], ok, ms[1], ms[2]])


def _attention(sc, q, kc, vcT, kslc, vslcT, kwin, vwinT, gT, B, S):
    NC = S // CMP_STRIDE
    NS = S // SLC_BLOCK
    NQ = S // QB
    cch = min(CCH, NC)
    assert NC % cch == 0 and (S // KCH) % NCHUNK == 0 and QB == KCH
    assert (BIAS_BLOCKS * SLC_BLOCK // KCH) % NCHUNK == 0
    head = lambda *blk: pl.BlockSpec((None, None) + blk, lambda b, h, i: (b, h) + (0,) * len(blk),
                                     pipeline_mode=pl.Buffered(1))
    return pl.pallas_call(
        functools.partial(_attn_kernel, cch=cch),
        out_shape=jax.ShapeDtypeStruct((B, S, NSA_W), F32),
        grid=(B, HKV, NQ),
        in_specs=[
            pl.BlockSpec(memory_space=pltpu.SMEM),
            pl.BlockSpec((None, QB, G * DH), lambda b, h, i: (b, i, h)),
            head(NC, LANES), head(DH, NC),
            head(S, LANES), head(S // LANES, VROWS, LANES),
            head(S, LANES), head(S // LANES, VROWS, LANES),
            pl.BlockSpec((None, GATE_ROWS, QB), lambda b, h, i: (b, 0, i)),
        ],
        out_specs=pl.BlockSpec((None, QB, G * DH), lambda b, h, i: (b, i, h)),
        scratch_shapes=[pltpu.VMEM((LANES, G * QB), BF16),
                        pltpu.VMEM((QB // LANES, NC + 8, LANES), F32),
                        pltpu.VMEM((NS, QB), F32), pltpu.VMEM((NS, QB), F32),
                        pltpu.VMEM((DH, G * QB), F32),
                        pltpu.VMEM((DH, G * QB), F32), pltpu.VMEM((DH, G * QB), F32),
                        pltpu.VMEM((VROWS, G * QB), F32)],
        compiler_params=pltpu.CompilerParams(
            dimension_semantics=("arbitrary", "arbitrary", "arbitrary"),
            vmem_limit_bytes=VMEM_LIMIT),
        name="attention",
    )(sc, q, kc, vcT, kslc, vslcT, kwin, vwinT, gT)


XH = 8
YH = 32


def _causal_taps(buf, w_ref, bias, hist, ntaps):
    rows = TS + 8
    base = hist - (ntaps - 1)
    out = jnp.broadcast_to(bias, (TS, buf.shape[1]))
    for r in range(8):
        z = None
        for k in range(ntaps):
            if (base + k) % 8 == r:
                term = w_ref[k:k + 1, :] * buf[base + k - r:base + k - r + rows, :]
                z = term if z is None else z + term
        if z is not None:
            out = out + (z[0:TS, :] if r == 0 else pltpu.roll(z, rows - r, 0)[0:TS, :])
    return out


def _seq_mix_kernel(lru_ref, cv_ref, cw_ref, cb_ref, wa_ref, ba_ref, wx_ref, bx_ref, lam_ref,
                    dw_ref, db_ref, lng_ref, lnb_ref, gl_ref, gc_ref,
                    yl_ref, yc_ref, xbuf, ybuf, hbuf):
    t = pl.program_id(1)

    @pl.when(t == 0)
    def _():
        xbuf[0:XH, :] = jnp.zeros((XH, LRU_W), F32)
        xbuf[XH + TS:XH + TS + 8, :] = jnp.zeros((8, LRU_W), F32)
        ybuf[0:YH, :] = jnp.zeros((YH, CV_W), F32)
        ybuf[YH + TS:YH + TS + 8, :] = jnp.zeros((8, CV_W), F32)
        hbuf[...] = jnp.zeros((8, LRU_W), F32)

    xb = lru_ref[:, 0:LRU_W]
    gb = lru_ref[:, LRU_W:2 * LRU_W]
    xbuf[XH:XH + TS, :] = xb
    xr = _causal_taps(xbuf, cw_ref, cb_ref[...], XH, LRU_CONV)
    xbuf[0:XH, :] = xbuf[TS:TS + XH, :]

    xr16 = xr.astype(BF16)
    r = _sigmoid(_dot(xr16, wa_ref[...]) + ba_ref[...])
    ig = _sigmoid(_dot(xr16, wx_ref[...]) + bx_ref[...])
    nl = -lam_ref[...]
    softplus = jnp.maximum(nl, 0.0) + jnp.log1p(jnp.exp(-jnp.abs(nl)))
    log_a = -LRU_C * r * softplus
    a = jnp.exp(log_a)
    u = xr * ig * jnp.sqrt(-jnp.tanh(log_a) * (a * a + 1.0))

    row = lax.broadcasted_iota(jnp.int32, (TS, LRU_W), 0)
    d = 1
    while d < TS:
        keep = row >= d
        a_sh = jnp.where(keep, pltpu.roll(a, d, 0), 1.0)
        u_sh = jnp.where(keep, pltpu.roll(u, d, 0), 0.0)
        u = a * u_sh + u
        a = a * a_sh
        d *= 2
    hseq = a * hbuf[0:1, :] + u
    hbuf[...] = jnp.broadcast_to(hseq[TS - 1:TS, :], (8, LRU_W))
    yl = hseq * _gelu_tanh(gb)
    yl = yl * lax.rsqrt(jnp.mean(yl * yl, axis=-1, keepdims=True) + EPS) * gl_ref[...]
    yl_ref[...] = yl.astype(BF16)

    y = cv_ref[:, 0:CV_W] * _sigmoid(cv_ref[:, CV_W:2 * CV_W])
    ybuf[YH:YH + TS, :] = y
    c = _causal_taps(ybuf, dw_ref, db_ref[...], YH, CV_KERNEL)
    ybuf[0:YH, :] = ybuf[TS:TS + YH, :]
    mu = jnp.mean(c, axis=-1, keepdims=True)
    cc = c - mu
    var = jnp.mean(cc * cc, axis=-1, keepdims=True)
    ln = cc * lax.rsqrt(var + EPS) * lng_ref[...] + lnb_ref[...]
    yc = ln * _sigmoid(ln)
    yc = yc * lax.rsqrt(jnp.mean(yc * yc, axis=-1, keepdims=True) + EPS) * gc_ref[...]
    yc_ref[...] = yc.astype(BF16)


def _seq_mix(lru, cv, params, B, S):
    const = lambda a: pl.BlockSpec(a.shape, lambda b, t: (0,) * a.ndim)
    tile = lambda w_: pl.BlockSpec((None, TS, w_), lambda b, t: (b, t, 0))
    return pl.pallas_call(
        _seq_mix_kernel,
        out_shape=(jax.ShapeDtypeStruct((B, S, LRU_W), BF16),
                   jax.ShapeDtypeStruct((B, S, CV_W), BF16)),
        grid=(B, S // TS),
        in_specs=[tile(2 * LRU_W), tile(2 * CV_W)] + [const(p) for p in params],
        out_specs=(tile(LRU_W), tile(CV_W)),
        scratch_shapes=[pltpu.VMEM((XH + TS + 8, LRU_W), F32), pltpu.VMEM((YH + TS + 8, CV_W), F32),
                        pltpu.VMEM((8, LRU_W), F32)],
        compiler_params=pltpu.CompilerParams(
            dimension_semantics=("arbitrary", "arbitrary"), vmem_limit_bytes=VMEM_LIMIT),
        name="seq_mix",
    )(lru, cv, *params)


def _out_mlp_kernel(x_ref, ya_ref, yl_ref, yc_ref, ga_ref, wo_ref, gm_ref, w1_ref, w2_ref, o_ref):
    ya = ya_ref[...]
    ya = (ya * lax.rsqrt(jnp.mean(ya * ya, axis=-1, keepdims=True) + EPS) * ga_ref[...]).astype(BF16)
    x1 = (x_ref[...] + _dot(ya, wo_ref[0:NSA_W, :])
          + _dot(yl_ref[...], wo_ref[NSA_W:NSA_W + LRU_W, :])
          + _dot(yc_ref[...], wo_ref[NSA_W + LRU_W:NSA_W + LRU_W + CV_W, :]))
    hm = (x1 * lax.rsqrt(jnp.mean(x1 * x1, axis=-1, keepdims=True) + EPS) * gm_ref[...]).astype(BF16)
    d_ff = w1_ref.shape[1]
    fc = 1024
    o_ref[...] = x1
    for c in range(d_ff // fc):
        hc = jnp.maximum(_dot(hm, w1_ref[:, c * fc:(c + 1) * fc]), 0.0)
        o_ref[...] += _dot((hc * hc).astype(BF16), w2_ref[c * fc:(c + 1) * fc, :])


def _out_mlp(x2, ya, yl, yc, ga, wo, gm, w1, w2):
    T, D = x2.shape
    tok = lambda w_: pl.BlockSpec((TM, w_), lambda t: (t, 0))
    const1 = lambda a: pl.BlockSpec(a.shape, lambda t: (0,) * a.ndim, pipeline_mode=pl.Buffered(1))
    return pl.pallas_call(
        _out_mlp_kernel,
        out_shape=jax.ShapeDtypeStruct((T, D), F32),
        grid=(T // TM,),
        in_specs=[tok(D), tok(NSA_W), tok(LRU_W), tok(CV_W),
                  const1(ga), const1(wo), const1(gm), const1(w1), const1(w2)],
        out_specs=tok(D),
        compiler_params=pltpu.CompilerParams(
            dimension_semantics=("arbitrary",), vmem_limit_bytes=VMEM_LIMIT),
        name="out_mlp",
    )(x2, ya, yl, yc, ga, wo, gm, w1, w2)


def _block_ones(n):
    idx = np.arange(n) // DH
    return jnp.asarray((idx[:, None] == idx[None, :]).astype(np.float32) / DH, dtype=BF16)


def _permute_w_in(w):
    off_kv = NSA_W
    kv = [w[:, off_kv + c * 128: off_kv + (c + 1) * 128] for c in range(6)]
    off_gate = off_kv + 6 * 128
    ngate = 3 * HKV * G
    off_lru = off_gate + ngate
    gate = jnp.pad(w[:, off_gate:off_lru], ((0, 0), (0, LANES - ngate)))
    cols = [w[:, 0:NSA_W], kv[0], kv[1], kv[2], kv[4], kv[3], kv[5],
            w[:, off_lru:off_lru + 2 * LRU_W + 2 * CV_W], gate]
    return jnp.concatenate(cols, axis=1).astype(BF16)


def _compress_weights(pos, w1, w2):
    slot_kv = jnp.array([0, 0, 1, 1])
    eye = jnp.eye(4, dtype=F32)
    w1r = w1.reshape(2, 2, CMP_STRIDE, DH, CMP_HID)[slot_kv]
    wbig = jnp.einsum('shpdo,st->psdhto', w1r, eye).reshape(CMP_STRIDE * 4 * DH, 2 * 4 * CMP_HID)
    posr = pos.reshape(2, 2, CMP_STRIDE, DH)[slot_kv]
    pos2 = jnp.transpose(posr, (1, 2, 0, 3)).reshape(2, CMP_STRIDE * 4 * DH)
    pos2 = jnp.pad(pos2, ((0, 6), (0, 0)))
    w2big = jnp.einsum('sod,st->sotd', w2[slot_kv], eye).reshape(4 * CMP_HID, 4 * DH)
    return pos2.astype(BF16), wbig.astype(BF16), w2big.astype(BF16)


def _block_diag(w):
    hh, bw, _ = w.shape
    eye = jnp.eye(hh, dtype=w.dtype)
    return jnp.einsum('hij,hg->higj', w, eye).reshape(hh * bw, hh * bw).astype(BF16)


def kernel(x, attn_norm, w_in, q_norm, k_norm, cmp_pos, cmp_w1, cmp_w2, lru_conv_w, lru_conv_b,
           lru_wa, lru_ba, lru_wx, lru_bx, lru_lambda, cv_dw_w, cv_dw_b, cv_ln_g, cv_ln_b,
           out_norm, w_out, mlp_norm, mlp_w1, mlp_w2):
    B, S, D = x.shape
    depth = w_in.shape[0]
    assert S % TM == 0 and S % TS == 0 and S >= WINDOW + QB and S // SLC_BLOCK >= N_SEL
    NC = S // CMP_STRIDE
    row = lambda v: v.reshape(1, -1).astype(F32)
    bd512 = _block_ones(NSA_W)
    bd128 = _block_ones(128)

    x2 = x.reshape(B * S, D)
    for l in range(depth):
        qgain = row(jnp.tile(q_norm[l], HKV * G)) * (DH ** -0.5 * LOG2E)
        kgain = row(jnp.concatenate([jnp.tile(k_norm[l, 1], HKV), jnp.tile(k_norm[l, 2], HKV)]))
        q, cmpr, kslc, kwin, vslcT, vwinT, lru, cv, gT = _in_proj(
            x2, row(attn_norm[l]), _permute_w_in(w_in[l]), bd512, qgain, kgain, B, S)

        pos2, wbig, w2big = _compress_weights(cmp_pos[l], cmp_w1[l], cmp_w2[l])
        kc, vcT = _compress(cmpr.reshape(B, NC, CMP_STRIDE * 256), pos2, wbig, w2big, bd128,
                            row(jnp.tile(k_norm[l, 0], HKV)), B, NC)

        sc = _score_bound(qgain, k_norm[l, 1], k_norm[l, 2], k_norm[l, 0])
        y_attn = _attention(sc, q.reshape(B, S, NSA_W), kc, vcT, kslc, vslcT, kwin, vwinT, gT, B, S)

        g_out = out_norm[l]
        seq_params = (lru_conv_w[l], row(lru_conv_b[l]), _block_diag(lru_wa[l]), row(lru_ba[l]),
                      _block_diag(lru_wx[l]), row(lru_bx[l]), row(lru_lambda[l]),
                      cv_dw_w[l], row(cv_dw_b[l]), row(cv_ln_g[l]), row(cv_ln_b[l]),
                      row(g_out[NSA_W:NSA_W + LRU_W]), row(g_out[NSA_W + LRU_W:]))
        yl, yc = _seq_mix(lru.reshape(B, S, 2 * LRU_W), cv.reshape(B, S, 2 * CV_W), seq_params, B, S)

        x2 = _out_mlp(x2, y_attn.reshape(B * S, NSA_W), yl.reshape(B * S, LRU_W),
                      yc.reshape(B * S, CV_W), row(g_out[:NSA_W]), w_out[l].astype(BF16),
                      row(mlp_norm[l]), mlp_w1[l].astype(BF16), mlp_w2[l].astype(BF16))
    return x2.reshape(B, S, D)
```

```python
import functools

import numpy as np
import jax
import jax.numpy as jnp
from jax import lax
from jax.experimental import pallas as pl
from jax.experimental.pallas import tpu as pltpu

F32 = jnp.float32
BF16 = jnp.bfloat16

EPS = 1e-6
NEG = -1e30
DH = 64
HKV = 2
G = 4
NSA_W = HKV * G * DH
CMP_BLOCK = 32
CMP_STRIDE = 16
CMP_HID = 128
SLC_BLOCK = 64
N_SEL = 16
WINDOW = 512
QB = 256
LRU_W = 256
LRU_HEADS = 8
LRU_CONV = 4
LRU_C = 8.0
CV_W = 256
CV_KERNEL = 31
LOG2E = 1.4426950408889634
SAFE_SHIFT = 50.0

LANES = 128
KCH = 256
NCHUNK = 8
CCH = 128
TM = 512
TS = 512
VMEM_LIMIT = 48 * 1024 * 1024

C_Q = 0
C_CMP = C_Q + NSA_W
C_KSLC = C_CMP + 256
C_KWIN = C_KSLC + 128
C_VSLC = C_KWIN + 128
C_VWIN = C_VSLC + 128
C_LRU = C_VWIN + 128
C_CV = C_LRU + 2 * LRU_W
C_GATE = C_CV + 2 * CV_W
N_INP = C_GATE + LANES
GATE_ROWS = 32
VROWS = DH + 16
BIAS_BLOCKS = LANES - DH


def _dot(a, b):
    return jnp.dot(a, b, preferred_element_type=F32)


def _head_rms_scale(z, bd):
    sq = z * z
    hi = sq.astype(BF16)
    lo = (sq - hi.astype(F32)).astype(BF16)
    ms = _dot(hi, bd) + _dot(lo, bd)
    return lax.rsqrt(ms + EPS)


def _gelu_tanh(x):
    return 0.5 * x * (1.0 + jnp.tanh(0.7978845608028654 * (x + 0.044715 * (x * x * x))))


def _sigmoid(x):
    return 1.0 / (1.0 + jnp.exp(-x))


def _in_proj_kernel(x_ref, g_ref, w_ref, bd_ref, qgain_ref, kgain_ref,
                    q_ref, cmp_ref, kslc_ref, kwin_ref, vslcT_ref, vwinT_ref,
                    lru_ref, cv_ref, gT_ref, *, nt):
    x = x_ref[...]
    ms = jnp.mean(x * x, axis=-1, keepdims=True)
    hn = (x * lax.rsqrt(ms + EPS) * g_ref[...]).astype(BF16)

    zq = _dot(hn, w_ref[:, C_Q:C_CMP])
    q_ref[...] = (zq * _head_rms_scale(zq, bd_ref[...]) * qgain_ref[...]).astype(BF16)

    cmp_ref[...] = _dot(hn, w_ref[:, C_CMP:C_KSLC]).astype(BF16)

    zk = _dot(hn, w_ref[:, C_KSLC:C_VSLC])
    kn = zk * _head_rms_scale(zk, bd_ref[0:256, 0:256]) * kgain_ref[...]
    lane = lax.broadcasted_iota(jnp.int32, (TM, LANES), 1)
    rowg = (pl.program_id(0) % nt) * TM + lax.broadcasted_iota(jnp.int32, (TM, LANES), 0)
    blk = lax.shift_right_logical(rowg, 6) & (BIAS_BLOCKS - 1)
    onehot = jnp.where(lane == DH + blk, 1.0, 0.0)
    lo = lane < DH
    ks, kw = kn[:, 0:128], kn[:, 128:256]
    kslc_ref[0] = jnp.where(lo, ks, onehot).astype(BF16)
    kslc_ref[1] = jnp.where(lo, pltpu.roll(ks, DH, 1), onehot).astype(BF16)
    kwin_ref[0] = jnp.where(lo, kw, 0.0).astype(BF16)
    kwin_ref[1] = jnp.where(lo, pltpu.roll(kw, DH, 1), 0.0).astype(BF16)

    zv = _dot(hn, w_ref[:, C_VSLC:C_LRU])
    zvT = zv.T
    ones_row = jnp.where(lax.broadcasted_iota(jnp.int32, (VROWS - DH, LANES), 0) == 0,
                         1.0, 0.0).astype(BF16)
    for j in range(TM // LANES):
        for h in range(HKV):
            cols = slice(j * LANES, (j + 1) * LANES)
            vslcT_ref[h, j, 0:DH, :] = zvT[h * DH:(h + 1) * DH, cols].astype(BF16)
            vslcT_ref[h, j, DH:VROWS, :] = ones_row
            vwinT_ref[h, j, 0:DH, :] = zvT[128 + h * DH:128 + (h + 1) * DH, cols].astype(BF16)
            vwinT_ref[h, j, DH:VROWS, :] = ones_row

    lru_ref[...] = _dot(hn, w_ref[:, C_LRU:C_CV])
    cv_ref[...] = _dot(hn, w_ref[:, C_CV:C_GATE])

    zg = _sigmoid(_dot(hn, w_ref[:, C_GATE:N_INP]))
    gT_ref[...] = zg.T[0:GATE_ROWS, :]


def _in_proj(x2, g, w, bd, qgain, kgain, B, S):
    T, D = x2.shape
    nt = S // TM
    tok = lambda w_: pl.BlockSpec((TM, w_), lambda t: (t, 0))
    const = lambda a: pl.BlockSpec(a.shape, lambda t: (0,) * a.ndim)
    vT_spec = pl.BlockSpec((None, HKV, TM // LANES, VROWS, LANES),
                           lambda t: (t // nt, 0, t % nt, 0, 0))
    k_spec = pl.BlockSpec((None, HKV, TM, LANES), lambda t: (t // nt, 0, t % nt, 0))
    out_shape = (
        jax.ShapeDtypeStruct((T, NSA_W), BF16),
        jax.ShapeDtypeStruct((T, 256), BF16),
        jax.ShapeDtypeStruct((B, HKV, S, LANES), BF16),
        jax.ShapeDtypeStruct((B, HKV, S, LANES), BF16),
        jax.ShapeDtypeStruct((B, HKV, S // LANES, VROWS, LANES), BF16),
        jax.ShapeDtypeStruct((B, HKV, S // LANES, VROWS, LANES), BF16),
        jax.ShapeDtypeStruct((T, 2 * LRU_W), F32),
        jax.ShapeDtypeStruct((T, 2 * CV_W), F32),
        jax.ShapeDtypeStruct((B, GATE_ROWS, S), F32),
    )
    out_specs = (
        tok(NSA_W), tok(256), k_spec, k_spec, vT_spec, vT_spec,
        tok(2 * LRU_W), tok(2 * CV_W),
        pl.BlockSpec((None, GATE_ROWS, TM), lambda t: (t // nt, 0, t % nt)),
    )
    return pl.pallas_call(
        functools.partial(_in_proj_kernel, nt=nt),
        out_shape=out_shape,
        grid=(T // TM,),
        in_specs=[tok(D), const(g), const(w), const(bd), const(qgain), const(kgain)],
        out_specs=out_specs,
        compiler_params=pltpu.CompilerParams(
            dimension_semantics=("arbitrary",), vmem_limit_bytes=VMEM_LIMIT),
        name="in_proj",
    )(x2, g, w, bd, qgain, kgain)


def _compress_kernel(x_ref, pos_ref, wbig_ref, w2_ref, bd_ref, kgain_ref, kc_ref, vcT_ref):
    nc = x_ref.shape[0]
    half = 4 * CMP_HID
    p = _dot(x_ref[...], wbig_ref[...])
    pc = _dot(pos_ref[...], wbig_ref[...])
    const = pc[0:1, 0:half] + pc[1:2, half:2 * half]
    nxt = pltpu.roll(p[:, half:2 * half], nc - 1, 0)
    hid = _gelu_tanh(p[:, 0:half] + nxt + const).astype(BF16)
    kv = _dot(hid, w2_ref[...])
    kc = kv[:, 0:128]
    kc = kc * _head_rms_scale(kc, bd_ref[...]) * kgain_ref[...]
    lo = lax.broadcasted_iota(jnp.int32, (nc, LANES), 1) < DH
    kc_ref[0] = jnp.where(lo, kc, 0.0).astype(BF16)
    kc_ref[1] = jnp.where(lo, pltpu.roll(kc, DH, 1), 0.0).astype(BF16)
    vT = kv[:, 128:256].T
    for h in range(HKV):
        vcT_ref[h] = vT[h * DH:(h + 1) * DH, :].astype(BF16)


def _compress(cmpx, pos2, wbig, w2big, bd128, kgain, B, NC):
    const = lambda a: pl.BlockSpec(a.shape, lambda b: (0,) * a.ndim, pipeline_mode=pl.Buffered(1))
    return pl.pallas_call(
        _compress_kernel,
        out_shape=(jax.ShapeDtypeStruct((B, HKV, NC, LANES), BF16),
                   jax.ShapeDtypeStruct((B, HKV, DH, NC), BF16)),
        grid=(B,),
        in_specs=[pl.BlockSpec((None, NC, cmpx.shape[2]), lambda b: (b, 0, 0)),
                  const(pos2), const(wbig), const(w2big), const(bd128), const(kgain)],
        out_specs=(pl.BlockSpec((None, HKV, NC, LANES), lambda b: (b, 0, 0, 0)),
                   pl.BlockSpec((None, HKV, DH, NC), lambda b: (b, 0, 0, 0))),
        compiler_params=pltpu.CompilerParams(
            dimension_semantics=("arbitrary",), vmem_limit_bytes=VMEM_LIMIT),
        name="compress",
    )(cmpx, pos2, wbig, w2big, bd128, kgain)


def _attn_kernel(sc_ref, q_ref, kc_ref, vcT_ref, ks_ref, vsT_ref, kw_ref, vwT_ref, gT_ref,
                 y_ref, qT_ref, imp_ref, selb_ref, selo_ref, oc_ref, os_ref, ow_ref, acc_ref, *, cch):
    h = pl.program_id(1)
    i = pl.program_id(2)
    nc = kc_ref.shape[0]
    ns = selb_ref.shape[0]
    wspan = WINDOW + QB
    t0 = i * QB
    iota = lambda shape, ax: lax.broadcasted_iota(jnp.int32, shape, ax)
    use_bound = sc_ref[1] > 0.5
    shift = jnp.where(use_bound, -sc_ref[0], 0.0)

    qfT = q_ref[...].astype(F32).T
    for g in range(G):
        qT_ref[0:DH, g * QB:(g + 1) * QB] = qfT[g * DH:(g + 1) * DH, :].astype(BF16)
    qT_ref[DH:LANES, :] = jnp.zeros((LANES - DH, G * QB), BF16)

    def front(nr, bounded):
        nsu = nr // (SLC_BLOCK // CMP_STRIDE)
        cmask = iota((nr, QB), 0) * CMP_STRIDE + (CMP_BLOCK - 1) <= t0 + iota((nr, QB), 1)
        cb = jnp.where(cmask, -sc_ref[2] if bounded else 0.0, NEG)
        s = _dot(kc_ref[0:nr, :], qT_ref[...]) + jnp.concatenate([cb] * G, axis=1)
        p = jnp.exp2(s if bounded else s - jnp.max(s, axis=0, keepdims=True))
        l = jnp.sum(p, axis=0, keepdims=True)
        anyv = jnp.where(t0 + iota((1, QB), 1) >= CMP_BLOCK - 1, 1.0, 0.0)
        pn = p * jnp.where(jnp.concatenate([anyv] * G, axis=1) > 0.5, 1.0 / l, 0.0)
        oc_ref[...] = _dot(vcT_ref[:, 0:nr], pn.astype(BF16))
        imp = pn[:, 0:QB]
        for g in range(1, G):
            imp = imp + pn[:, g * QB:(g + 1) * QB]

        parts = []
        for w in range(QB // LANES):
            imp_ref[w, 0:8, :] = jnp.zeros((8, LANES), F32)
            imp_ref[w, 8:8 + nr, :] = imp[:, w * LANES:(w + 1) * LANES]
            acc = imp_ref[w, pl.ds(7, nsu, stride=4), :]
            for r in range(4):
                acc = acc + imp_ref[w, pl.ds(8 + r, nsu, stride=4), :]
            parts.append(acc)
        islc = jnp.concatenate(parts, axis=1)

        j_i = iota((nsu, QB), 0)
        t_s = t0 + iota((nsu, QB), 1)
        cur = lax.shift_right_logical(t_s, 6)
        valid = j_i * SLC_BLOCK <= t_s
        ninf = -jnp.inf
        score = jnp.where(j_i == 0, ninf, jnp.where(j_i == cur, ninf,
                                                    jnp.where(j_i == cur - 1, ninf, islc)))
        score = jnp.where(valid, score, NEG)
        j_f = j_i.astype(F32)
        for _ in range(N_SEL - 3):
            mx = jnp.max(score, axis=0, keepdims=True)
            jm = jnp.min(jnp.where(score == mx, j_f, float(nsu)), axis=0, keepdims=True)
            score = jnp.where(j_f == jm, -jnp.inf, score)
        own = lax.shift_right_logical(j_i, (QB // SLC_BLOCK).bit_length() - 1) == i
        picked = jnp.where(valid, jnp.where(score == -jnp.inf, shift, NEG), NEG)
        selo_ref[0:nsu, :] = picked
        selb_ref[0:nsu, :] = jnp.where(own, NEG, picked)
        pad = -nsu % min(BIAS_BLOCKS, ns)
        if pad:
            selb_ref[nsu:nsu + pad, :] = jnp.full((pad, QB), NEG, F32)

    cls = (i * (QB // CMP_STRIDE) + (QB - CMP_BLOCK) // CMP_STRIDE) // cch
    for kk in range(nc // cch):
        for bounded in (True, False):
            pl.when(jnp.logical_and(cls == kk, use_bound == bounded))(
                functools.partial(front, (kk + 1) * cch, bounded))

    bpt = QB // SLC_BLOCK
    own_b = jnp.concatenate(
        [jnp.broadcast_to(selo_ref[pl.ds(i * bpt + b, 1), :], (SLC_BLOCK, QB)) for b in range(bpt)],
        axis=0)
    own_b = jnp.where(iota((QB, QB), 0) <= iota((QB, QB), 1), own_b, NEG)
    own_v = jnp.concatenate([vsT_ref[i * (QB // LANES) + w] for w in range(QB // LANES)], axis=1)
    sd = (_dot(ks_ref[pl.ds(pl.multiple_of(t0, QB), QB), :], qT_ref[...])
          + jnp.concatenate([own_b] * G, axis=1))

    nv = KCH // LANES
    cpg = BIAS_BLOCKS // (KCH // SLC_BLOCK)
    brows = min(BIAS_BLOCKS, ns)
    nstep = (i + NCHUNK - 1) // NCHUNK

    def chunk_scores(c0, n=NCHUNK):
        @pl.when(c0 % cpg == 0)
        def _():
            r0 = pl.multiple_of((c0 // cpg) * brows, brows)
            rows = selb_ref[pl.ds(r0, brows), :].astype(BF16)
            qT_ref[DH:DH + brows, :] = jnp.concatenate([rows] * G, axis=1)

        qT = qT_ref[...]
        return [_dot(ks_ref[pl.ds(pl.multiple_of((c0 + u) * KCH, KCH), KCH), :], qT)
                for u in range(n)]

    def values(c):
        return jnp.concatenate([vsT_ref[c * nv + w] for w in range(nv)], axis=1)

    def finish(acc):
        os_ref[...] = acc[0:DH, :] * (1.0 / acc[DH:DH + 1, :])

    @pl.when(use_bound)
    def _():
        def add_chunks(c0, n, acc):
            r0 = pl.multiple_of((c0 // cpg) * brows, brows)
            rows = selb_ref[pl.ds(r0, brows), :].astype(BF16)
            qT_ref[DH:DH + brows, :] = jnp.concatenate([rows] * G, axis=1)
            qT = qT_ref[...]
            for sub in range(0, n, NCHUNK):
                cs = [c0 + sub + u for u in range(min(NCHUNK, n - sub))]
                ss = [_dot(ks_ref[pl.ds(pl.multiple_of(c * KCH, KCH), KCH), :], qT) for c in cs]
                for c, s in zip(cs, ss):
                    acc = acc + _dot(values(c), jnp.exp2(s).astype(BF16))
            return acc

        nfull = i // cpg
        rem = i - nfull * cpg
        acc0 = _dot(own_v, jnp.exp2(sd).astype(BF16))
        acc_ref[...] = lax.fori_loop(
            0, nfull, lambda grp, acc: add_chunks(grp * cpg, cpg, acc), acc0)

        half = jnp.where(rem >= NCHUNK, NCHUNK, 0)
        c1 = nfull * cpg + half
        rest = rem - half

        @pl.when(half > 0)
        def _():
            acc_ref[...] = add_chunks(nfull * cpg, NCHUNK, acc_ref[...])

        @pl.when(jnp.logical_and(rest > 0, rest <= NCHUNK // 2))
        def _():
            acc_ref[...] = add_chunks(c1, NCHUNK // 2, acc_ref[...])

        @pl.when(rest > NCHUNK // 2)
        def _():
            acc_ref[...] = add_chunks(c1, NCHUNK, acc_ref[...])

        finish(acc_ref[...])

    @pl.when(jnp.logical_not(use_bound))
    def _():
        def step(grp, carry):
            m, acc = carry
            ss = chunk_scores(grp * NCHUNK)
            for u in range(NCHUNK):
                s = ss[u]
                m_new = jnp.maximum(m, jnp.max(s, axis=0, keepdims=True))
                p = jnp.exp2(s - m_new).astype(BF16)
                acc = jnp.exp2(m - m_new) * acc + _dot(values(grp * NCHUNK + u), p)
                m = m_new
            return m, acc

        m0 = jnp.max(sd, axis=0, keepdims=True)
        acc0 = _dot(own_v, jnp.exp2(sd - m0).astype(BF16))
        finish(lax.fori_loop(0, nstep, step, (m0, acc0))[1])

    os_ = os_ref[...]

    def window(bounded):
        wc = jnp.maximum(i * (QB // LANES) - WINDOW // LANES, 0)
        ws = wc * LANES
        sw = _dot(kw_ref[pl.ds(pl.multiple_of(ws, LANES), wspan), :], qT_ref[...])
        kpos = ws + iota((wspan, QB), 0)
        t_w = t0 + iota((wspan, QB), 1)
        inside = -sc_ref[3] if bounded else 0.0
        wb = jnp.where(kpos <= t_w, jnp.where(kpos > t_w - WINDOW, inside, NEG), NEG)
        sw = sw + jnp.concatenate([wb] * G, axis=1)
        p_w = jnp.exp2(sw if bounded else sw - jnp.max(sw, axis=0, keepdims=True)).astype(BF16)
        vw = jnp.concatenate([vwT_ref[wc + u] for u in range(wspan // LANES)], axis=1)
        acc_w = _dot(vw, p_w)
        ow_ref[...] = acc_w[0:DH, :] * (1.0 / acc_w[DH:DH + 1, :])

    for bounded in (True, False):
        pl.when(use_bound == bounded)(functools.partial(window, bounded))
    ow = ow_ref[...]

    outs = []
    for g in range(G):
        sl = slice(g * QB, (g + 1) * QB)
        row = h * (3 * G) + 3 * g
        gc = gT_ref[pl.ds(row, 1), :]
        gs = gT_ref[pl.ds(row + 1, 1), :]
        gw = gT_ref[pl.ds(row + 2, 1), :]
        outs.append(gc * oc_ref[:, sl] + gs * os_[:, sl] + gw * ow[:, sl])
    y_ref[...] = jnp.concatenate(outs, axis=0).T


def _score_bound(qgain, kgain_slc, kgain_win, kgain_cmp):
    qmax = jnp.max(jnp.abs(qgain))
    ms = [(1.05 * DH * qmax * jnp.max(jnp.abs(kg))).astype(BF16).astype(F32)
          for kg in (kgain_slc, kgain_cmp, kgain_win)]
    ok = (jnp.maximum(jnp.maximum(ms[0], ms[1]), ms[2]) <= SAFE_SHIFT).astype(F32)
    return jnp.stack([ms[0], ok, ms[1], ms[2]])


def _attention(sc, q, kc, vcT, kslc, vslcT, kwin, vwinT, gT, B, S):
    NC = S // CMP_STRIDE
    NS = S // SLC_BLOCK
    NQ = S // QB
    cch = min(CCH, NC)
    assert NC % cch == 0 and (S // KCH) % NCHUNK == 0 and QB == KCH
    assert (BIAS_BLOCKS * SLC_BLOCK // KCH) % NCHUNK == 0
    head = lambda *blk: pl.BlockSpec((None, None) + blk, lambda b, h, i: (b, h) + (0,) * len(blk),
                                     pipeline_mode=pl.Buffered(1))
    return pl.pallas_call(
        functools.partial(_attn_kernel, cch=cch),
        out_shape=jax.ShapeDtypeStruct((B, S, NSA_W), F32),
        grid=(B, HKV, NQ),
        in_specs=[
            pl.BlockSpec(memory_space=pltpu.SMEM),
            pl.BlockSpec((None, QB, G * DH), lambda b, h, i: (b, i, h)),
            head(NC, LANES), head(DH, NC),
            head(S, LANES), head(S // LANES, VROWS, LANES),
            head(S, LANES), head(S // LANES, VROWS, LANES),
            pl.BlockSpec((None, GATE_ROWS, QB), lambda b, h, i: (b, 0, i)),
        ],
        out_specs=pl.BlockSpec((None, QB, G * DH), lambda b, h, i: (b, i, h)),
        scratch_shapes=[pltpu.VMEM((LANES, G * QB), BF16),
                        pltpu.VMEM((QB // LANES, NC + 8, LANES), F32),
                        pltpu.VMEM((NS, QB), F32), pltpu.VMEM((NS, QB), F32),
                        pltpu.VMEM((DH, G * QB), F32),
                        pltpu.VMEM((DH, G * QB), F32), pltpu.VMEM((DH, G * QB), F32),
                        pltpu.VMEM((VROWS, G * QB), F32)],
        compiler_params=pltpu.CompilerParams(
            dimension_semantics=("arbitrary", "arbitrary", "arbitrary"),
            vmem_limit_bytes=VMEM_LIMIT),
        name="attention",
    )(sc, q, kc, vcT, kslc, vslcT, kwin, vwinT, gT)


XH = 8
YH = 32


def _causal_taps(buf, w_ref, bias, hist, ntaps):
    rows = TS + 8
    base = hist - (ntaps - 1)
    out = jnp.broadcast_to(bias, (TS, buf.shape[1]))
    for r in range(8):
        z = None
        for k in range(ntaps):
            if (base + k) % 8 == r:
                term = w_ref[k:k + 1, :] * buf[base + k - r:base + k - r + rows, :]
                z = term if z is None else z + term
        if z is not None:
            out = out + (z[0:TS, :] if r == 0 else pltpu.roll(z, rows - r, 0)[0:TS, :])
    return out


def _seq_mix_kernel(lru_ref, cv_ref, cw_ref, cb_ref, wa_ref, ba_ref, wx_ref, bx_ref, lam_ref,
                    dw_ref, db_ref, lng_ref, lnb_ref, gl_ref, gc_ref,
                    yl_ref, yc_ref, xbuf, ybuf, hbuf):
    t = pl.program_id(1)

    @pl.when(t == 0)
    def _():
        xbuf[0:XH, :] = jnp.zeros((XH, LRU_W), F32)
        xbuf[XH + TS:XH + TS + 8, :] = jnp.zeros((8, LRU_W), F32)
        ybuf[0:YH, :] = jnp.zeros((YH, CV_W), F32)
        ybuf[YH + TS:YH + TS + 8, :] = jnp.zeros((8, CV_W), F32)
        hbuf[...] = jnp.zeros((8, LRU_W), F32)

    xb = lru_ref[:, 0:LRU_W]
    gb = lru_ref[:, LRU_W:2 * LRU_W]
    xbuf[XH:XH + TS, :] = xb
    xr = _causal_taps(xbuf, cw_ref, cb_ref[...], XH, LRU_CONV)
    xbuf[0:XH, :] = xbuf[TS:TS + XH, :]

    xr16 = xr.astype(BF16)
    r = _sigmoid(_dot(xr16, wa_ref[...]) + ba_ref[...])
    ig = _sigmoid(_dot(xr16, wx_ref[...]) + bx_ref[...])
    nl = -lam_ref[...]
    softplus = jnp.maximum(nl, 0.0) + jnp.log1p(jnp.exp(-jnp.abs(nl)))
    log_a = -LRU_C * r * softplus
    a = jnp.exp(log_a)
    u = xr * ig * jnp.sqrt(-jnp.tanh(log_a) * (a * a + 1.0))

    row = lax.broadcasted_iota(jnp.int32, (TS, LRU_W), 0)
    d = 1
    while d < TS:
        keep = row >= d
        a_sh = jnp.where(keep, pltpu.roll(a, d, 0), 1.0)
        u_sh = jnp.where(keep, pltpu.roll(u, d, 0), 0.0)
        u = a * u_sh + u
        a = a * a_sh
        d *= 2
    hseq = a * hbuf[0:1, :] + u
    hbuf[...] = jnp.broadcast_to(hseq[TS - 1:TS, :], (8, LRU_W))
    yl = hseq * _gelu_tanh(gb)
    yl = yl * lax.rsqrt(jnp.mean(yl * yl, axis=-1, keepdims=True) + EPS) * gl_ref[...]
    yl_ref[...] = yl.astype(BF16)

    y = cv_ref[:, 0:CV_W] * _sigmoid(cv_ref[:, CV_W:2 * CV_W])
    ybuf[YH:YH + TS, :] = y
    c = _causal_taps(ybuf, dw_ref, db_ref[...], YH, CV_KERNEL)
    ybuf[0:YH, :] = ybuf[TS:TS + YH, :]
    mu = jnp.mean(c, axis=-1, keepdims=True)
    cc = c - mu
    var = jnp.mean(cc * cc, axis=-1, keepdims=True)
    ln = cc * lax.rsqrt(var + EPS) * lng_ref[...] + lnb_ref[...]
    yc = ln * _sigmoid(ln)
    yc = yc * lax.rsqrt(jnp.mean(yc * yc, axis=-1, keepdims=True) + EPS) * gc_ref[...]
    yc_ref[...] = yc.astype(BF16)


def _seq_mix(lru, cv, params, B, S):
    const = lambda a: pl.BlockSpec(a.shape, lambda b, t: (0,) * a.ndim)
    tile = lambda w_: pl.BlockSpec((None, TS, w_), lambda b, t: (b, t, 0))
    return pl.pallas_call(
        _seq_mix_kernel,
        out_shape=(jax.ShapeDtypeStruct((B, S, LRU_W), BF16),
                   jax.ShapeDtypeStruct((B, S, CV_W), BF16)),
        grid=(B, S // TS),
        in_specs=[tile(2 * LRU_W), tile(2 * CV_W)] + [const(p) for p in params],
        out_specs=(tile(LRU_W), tile(CV_W)),
        scratch_shapes=[pltpu.VMEM((XH + TS + 8, LRU_W), F32), pltpu.VMEM((YH + TS + 8, CV_W), F32),
                        pltpu.VMEM((8, LRU_W), F32)],
        compiler_params=pltpu.CompilerParams(
            dimension_semantics=("arbitrary", "arbitrary"), vmem_limit_bytes=VMEM_LIMIT),
        name="seq_mix",
    )(lru, cv, *params)


def _out_mlp_kernel(x_ref, ya_ref, yl_ref, yc_ref, ga_ref, wo_ref, gm_ref, w1_ref, w2_ref, o_ref):
    ya = ya_ref[...]
    ya = (ya * lax.rsqrt(jnp.mean(ya * ya, axis=-1, keepdims=True) + EPS) * ga_ref[...]).astype(BF16)
    x1 = (x_ref[...] + _dot(ya, wo_ref[0:NSA_W, :])
          + _dot(yl_ref[...], wo_ref[NSA_W:NSA_W + LRU_W, :])
          + _dot(yc_ref[...], wo_ref[NSA_W + LRU_W:NSA_W + LRU_W + CV_W, :]))
    hm = (x1 * lax.rsqrt(jnp.mean(x1 * x1, axis=-1, keepdims=True) + EPS) * gm_ref[...]).astype(BF16)
    d_ff = w1_ref.shape[1]
    fc = 1024
    o_ref[...] = x1
    for c in range(d_ff // fc):
        hc = jnp.maximum(_dot(hm, w1_ref[:, c * fc:(c + 1) * fc]), 0.0)
        o_ref[...] += _dot((hc * hc).astype(BF16), w2_ref[c * fc:(c + 1) * fc, :])


def _out_mlp(x2, ya, yl, yc, ga, wo, gm, w1, w2):
    T, D = x2.shape
    tok = lambda w_: pl.BlockSpec((TM, w_), lambda t: (t, 0))
    const1 = lambda a: pl.BlockSpec(a.shape, lambda t: (0,) * a.ndim, pipeline_mode=pl.Buffered(1))
    return pl.pallas_call(
        _out_mlp_kernel,
        out_shape=jax.ShapeDtypeStruct((T, D), F32),
        grid=(T // TM,),
        in_specs=[tok(D), tok(NSA_W), tok(LRU_W), tok(CV_W),
                  const1(ga), const1(wo), const1(gm), const1(w1), const1(w2)],
        out_specs=tok(D),
        compiler_params=pltpu.CompilerParams(
            dimension_semantics=("arbitrary",), vmem_limit_bytes=VMEM_LIMIT),
        name="out_mlp",
    )(x2, ya, yl, yc, ga, wo, gm, w1, w2)


def _block_ones(n):
    idx = np.arange(n) // DH
    return jnp.asarray((idx[:, None] == idx[None, :]).astype(np.float32) / DH, dtype=BF16)


def _permute_w_in(w):
    off_kv = NSA_W
    kv = [w[:, off_kv + c * 128: off_kv + (c + 1) * 128] for c in range(6)]
    off_gate = off_kv + 6 * 128
    ngate = 3 * HKV * G
    off_lru = off_gate + ngate
    gate = jnp.pad(w[:, off_gate:off_lru], ((0, 0), (0, LANES - ngate)))
    cols = [w[:, 0:NSA_W], kv[0], kv[1], kv[2], kv[4], kv[3], kv[5],
            w[:, off_lru:off_lru + 2 * LRU_W + 2 * CV_W], gate]
    return jnp.concatenate(cols, axis=1).astype(BF16)


def _compress_weights(pos, w1, w2):
    slot_kv = jnp.array([0, 0, 1, 1])
    eye = jnp.eye(4, dtype=F32)
    w1r = w1.reshape(2, 2, CMP_STRIDE, DH, CMP_HID)[slot_kv]
    wbig = jnp.einsum('shpdo,st->psdhto', w1r, eye).reshape(CMP_STRIDE * 4 * DH, 2 * 4 * CMP_HID)
    posr = pos.reshape(2, 2, CMP_STRIDE, DH)[slot_kv]
    pos2 = jnp.transpose(posr, (1, 2, 0, 3)).reshape(2, CMP_STRIDE * 4 * DH)
    pos2 = jnp.pad(pos2, ((0, 6), (0, 0)))
    w2big = jnp.einsum('sod,st->sotd', w2[slot_kv], eye).reshape(4 * CMP_HID, 4 * DH)
    return pos2.astype(BF16), wbig.astype(BF16), w2big.astype(BF16)


def _block_diag(w):
    hh, bw, _ = w.shape
    eye = jnp.eye(hh, dtype=w.dtype)
    return jnp.einsum('hij,hg->higj', w, eye).reshape(hh * bw, hh * bw).astype(BF16)


def kernel(x, attn_norm, w_in, q_norm, k_norm, cmp_pos, cmp_w1, cmp_w2, lru_conv_w, lru_conv_b,
           lru_wa, lru_ba, lru_wx, lru_bx, lru_lambda, cv_dw_w, cv_dw_b, cv_ln_g, cv_ln_b,
           out_norm, w_out, mlp_norm, mlp_w1, mlp_w2):
    B, S, D = x.shape
    depth = w_in.shape[0]
    assert S % TM == 0 and S % TS == 0 and S >= WINDOW + QB and S // SLC_BLOCK >= N_SEL
    NC = S // CMP_STRIDE
    row = lambda v: v.reshape(1, -1).astype(F32)
    bd512 = _block_ones(NSA_W)
    bd128 = _block_ones(128)

    x2 = x.reshape(B * S, D)
    for l in range(depth):
        qgain = row(jnp.tile(q_norm[l], HKV * G)) * (DH ** -0.5 * LOG2E)
        kgain = row(jnp.concatenate([jnp.tile(k_norm[l, 1], HKV), jnp.tile(k_norm[l, 2], HKV)]))
        q, cmpr, kslc, kwin, vslcT, vwinT, lru, cv, gT = _in_proj(
            x2, row(attn_norm[l]), _permute_w_in(w_in[l]), bd512, qgain, kgain, B, S)

        pos2, wbig, w2big = _compress_weights(cmp_pos[l], cmp_w1[l], cmp_w2[l])
        kc, vcT = _compress(cmpr.reshape(B, NC, CMP_STRIDE * 256), pos2, wbig, w2big, bd128,
                            row(jnp.tile(k_norm[l, 0], HKV)), B, NC)

        sc = _score_bound(qgain, k_norm[l, 1], k_norm[l, 2], k_norm[l, 0])
        y_attn = _attention(sc, q.reshape(B, S, NSA_W), kc, vcT, kslc, vslcT, kwin, vwinT, gT, B, S)

        g_out = out_norm[l]
        seq_params = (lru_conv_w[l], row(lru_conv_b[l]), _block_diag(lru_wa[l]), row(lru_ba[l]),
                      _block_diag(lru_wx[l]), row(lru_bx[l]), row(lru_lambda[l]),
                      cv_dw_w[l], row(cv_dw_b[l]), row(cv_ln_g[l]), row(cv_ln_b[l]),
                      row(g_out[NSA_W:NSA_W + LRU_W]), row(g_out[NSA_W + LRU_W:]))
        yl, yc = _seq_mix(lru.reshape(B, S, 2 * LRU_W), cv.reshape(B, S, 2 * CV_W), seq_params, B, S)

        x2 = _out_mlp(x2, y_attn.reshape(B * S, NSA_W), yl.reshape(B * S, LRU_W),
                      yc.reshape(B * S, CV_W), row(g_out[:NSA_W]), w_out[l].astype(BF16),
                      row(mlp_norm[l]), mlp_w1[l].astype(BF16), mlp_w2[l].astype(BF16))
    return x2.reshape(B, S, D)
```

```python
import functools

import numpy as np
import jax
import jax.numpy as jnp
from jax import lax
from jax.experimental import pallas as pl
from jax.experimental.pallas import tpu as pltpu

F32 = jnp.float32
BF16 = jnp.bfloat16

EPS = 1e-6
NEG = -1e30
DH = 64
HKV = 2
G = 4
NSA_W = HKV * G * DH
CMP_BLOCK = 32
CMP_STRIDE = 16
CMP_HID = 128
SLC_BLOCK = 64
N_SEL = 16
WINDOW = 512
QB = 256
LRU_W = 256
LRU_HEADS = 8
LRU_CONV = 4
LRU_C = 8.0
CV_W = 256
CV_KERNEL = 31
LOG2E = 1.4426950408889634
SAFE_SHIFT = 50.0

LANES = 128
KCH = 256
NCHUNK = 8
CCH = 256
TM = 512
TS = 512
VMEM_LIMIT = 48 * 1024 * 1024

C_Q = 0
C_CMP = C_Q + NSA_W
C_KSLC = C_CMP + 256
C_KWIN = C_KSLC + 128
C_VSLC = C_KWIN + 128
C_VWIN = C_VSLC + 128
C_LRU = C_VWIN + 128
C_CV = C_LRU + 2 * LRU_W
C_GATE = C_CV + 2 * CV_W
N_INP = C_GATE + LANES
GATE_ROWS = 32
VROWS = DH + 16
BIAS_BLOCKS = LANES - DH


def _dot(a, b):
    return jnp.dot(a, b, preferred_element_type=F32)


def _head_rms_scale(z, bd):
    sq = z * z
    hi = sq.astype(BF16)
    lo = (sq - hi.astype(F32)).astype(BF16)
    ms = _dot(hi, bd) + _dot(lo, bd)
    return lax.rsqrt(ms + EPS)


def _gelu_tanh(x):
    return 0.5 * x * (1.0 + jnp.tanh(0.7978845608028654 * (x + 0.044715 * (x * x * x))))


def _sigmoid(x):
    return 1.0 / (1.0 + jnp.exp(-x))


def _in_proj_kernel(x_ref, g_ref, w_ref, bd_ref, qgain_ref, kgain_ref,
                    q_ref, cmp_ref, kslc_ref, kwin_ref, vslcT_ref, vwinT_ref,
                    lru_ref, cv_ref, gT_ref, *, nt):
    x = x_ref[...]
    ms = jnp.mean(x * x, axis=-1, keepdims=True)
    hn = (x * lax.rsqrt(ms + EPS) * g_ref[...]).astype(BF16)

    zq = _dot(hn, w_ref[:, C_Q:C_CMP])
    q_ref[...] = (zq * _head_rms_scale(zq, bd_ref[...]) * qgain_ref[...]).astype(BF16)

    cmp_ref[...] = _dot(hn, w_ref[:, C_CMP:C_KSLC]).astype(BF16)

    zk = _dot(hn, w_ref[:, C_KSLC:C_VSLC])
    kn = zk * _head_rms_scale(zk, bd_ref[0:256, 0:256]) * kgain_ref[...]
    lane = lax.broadcasted_iota(jnp.int32, (TM, LANES), 1)
    rowg = (pl.program_id(0) % nt) * TM + lax.broadcasted_iota(jnp.int32, (TM, LANES), 0)
    blk = lax.shift_right_logical(rowg, 6) & (BIAS_BLOCKS - 1)
    onehot = jnp.where(lane == DH + blk, 1.0, 0.0)
    lo = lane < DH
    ks, kw = kn[:, 0:128], kn[:, 128:256]
    kslc_ref[0] = jnp.where(lo, ks, onehot).astype(BF16)
    kslc_ref[1] = jnp.where(lo, pltpu.roll(ks, DH, 1), onehot).astype(BF16)
    kwin_ref[0] = jnp.where(lo, kw, 0.0).astype(BF16)
    kwin_ref[1] = jnp.where(lo, pltpu.roll(kw, DH, 1), 0.0).astype(BF16)

    zv = _dot(hn, w_ref[:, C_VSLC:C_LRU])
    zvT = zv.T
    ones_row = jnp.where(lax.broadcasted_iota(jnp.int32, (VROWS - DH, LANES), 0) == 0,
                         1.0, 0.0).astype(BF16)
    for j in range(TM // LANES):
        for h in range(HKV):
            cols = slice(j * LANES, (j + 1) * LANES)
            vslcT_ref[h, j, 0:DH, :] = zvT[h * DH:(h + 1) * DH, cols].astype(BF16)
            vslcT_ref[h, j, DH:VROWS, :] = ones_row
            vwinT_ref[h, j, 0:DH, :] = zvT[128 + h * DH:128 + (h + 1) * DH, cols].astype(BF16)
            vwinT_ref[h, j, DH:VROWS, :] = ones_row

    lru_ref[...] = _dot(hn, w_ref[:, C_LRU:C_CV])
    cv_ref[...] = _dot(hn, w_ref[:, C_CV:C_GATE])

    zg = _sigmoid(_dot(hn, w_ref[:, C_GATE:N_INP]))
    gT_ref[...] = zg.T[0:GATE_ROWS, :]


def _in_proj(x2, g, w, bd, qgain, kgain, B, S):
    T, D = x2.shape
    nt = S // TM
    tok = lambda w_: pl.BlockSpec((TM, w_), lambda t: (t, 0))
    const = lambda a: pl.BlockSpec(a.shape, lambda t: (0,) * a.ndim)
    vT_spec = pl.BlockSpec((None, HKV, TM // LANES, VROWS, LANES),
                           lambda t: (t // nt, 0, t % nt, 0, 0))
    k_spec = pl.BlockSpec((None, HKV, TM, LANES), lambda t: (t // nt, 0, t % nt, 0))
    out_shape = (
        jax.ShapeDtypeStruct((T, NSA_W), BF16),
        jax.ShapeDtypeStruct((T, 256), BF16),
        jax.ShapeDtypeStruct((B, HKV, S, LANES), BF16),
        jax.ShapeDtypeStruct((B, HKV, S, LANES), BF16),
        jax.ShapeDtypeStruct((B, HKV, S // LANES, VROWS, LANES), BF16),
        jax.ShapeDtypeStruct((B, HKV, S // LANES, VROWS, LANES), BF16),
        jax.ShapeDtypeStruct((T, 2 * LRU_W), F32),
        jax.ShapeDtypeStruct((T, 2 * CV_W), F32),
        jax.ShapeDtypeStruct((B, GATE_ROWS, S), F32),
    )
    out_specs = (
        tok(NSA_W), tok(256), k_spec, k_spec, vT_spec, vT_spec,
        tok(2 * LRU_W), tok(2 * CV_W),
        pl.BlockSpec((None, GATE_ROWS, TM), lambda t: (t // nt, 0, t % nt)),
    )
    return pl.pallas_call(
        functools.partial(_in_proj_kernel, nt=nt),
        out_shape=out_shape,
        grid=(T // TM,),
        in_specs=[tok(D), const(g), const(w), const(bd), const(qgain), const(kgain)],
        out_specs=out_specs,
        compiler_params=pltpu.CompilerParams(
            dimension_semantics=("arbitrary",), vmem_limit_bytes=VMEM_LIMIT),
        name="in_proj",
    )(x2, g, w, bd, qgain, kgain)


def _compress_kernel(x_ref, pos_ref, wbig_ref, w2_ref, bd_ref, kgain_ref, kc_ref, vcT_ref):
    nc = x_ref.shape[0]
    half = 4 * CMP_HID
    p = _dot(x_ref[...], wbig_ref[...])
    pc = _dot(pos_ref[...], wbig_ref[...])
    const = pc[0:1, 0:half] + pc[1:2, half:2 * half]
    nxt = pltpu.roll(p[:, half:2 * half], nc - 1, 0)
    hid = _gelu_tanh(p[:, 0:half] + nxt + const).astype(BF16)
    kv = _dot(hid, w2_ref[...])
    kc = kv[:, 0:128]
    kc = kc * _head_rms_scale(kc, bd_ref[...]) * kgain_ref[...]
    lo = lax.broadcasted_iota(jnp.int32, (nc, LANES), 1) < DH
    kc_ref[0] = jnp.where(lo, kc, 0.0).astype(BF16)
    kc_ref[1] = jnp.where(lo, pltpu.roll(kc, DH, 1), 0.0).astype(BF16)
    vT = kv[:, 128:256].T
    for h in range(HKV):
        vcT_ref[h] = vT[h * DH:(h + 1) * DH, :].astype(BF16)


def _compress(cmpx, pos2, wbig, w2big, bd128, kgain, B, NC):
    const = lambda a: pl.BlockSpec(a.shape, lambda b: (0,) * a.ndim, pipeline_mode=pl.Buffered(1))
    return pl.pallas_call(
        _compress_kernel,
        out_shape=(jax.ShapeDtypeStruct((B, HKV, NC, LANES), BF16),
                   jax.ShapeDtypeStruct((B, HKV, DH, NC), BF16)),
        grid=(B,),
        in_specs=[pl.BlockSpec((None, NC, cmpx.shape[2]), lambda b: (b, 0, 0)),
                  const(pos2), const(wbig), const(w2big), const(bd128), const(kgain)],
        out_specs=(pl.BlockSpec((None, HKV, NC, LANES), lambda b: (b, 0, 0, 0)),
                   pl.BlockSpec((None, HKV, DH, NC), lambda b: (b, 0, 0, 0))),
        compiler_params=pltpu.CompilerParams(
            dimension_semantics=("arbitrary",), vmem_limit_bytes=VMEM_LIMIT),
        name="compress",
    )(cmpx, pos2, wbig, w2big, bd128, kgain)


def _attn_kernel(sc_ref, q_ref, kc_ref, vcT_ref, ks_ref, vsT_ref, kw_ref, vwT_ref, gT_ref,
                 y_ref, qT_ref, imp_ref, selb_ref, selo_ref, oc_ref, os_ref, ow_ref, acc_ref, *, cch):
    h = pl.program_id(1)
    i = pl.program_id(2)
    nc = kc_ref.shape[0]
    ns = selb_ref.shape[0]
    wspan = WINDOW + QB
    t0 = i * QB
    iota = lambda shape, ax: lax.broadcasted_iota(jnp.int32, shape, ax)
    use_bound = sc_ref[1] > 0.5
    shift = jnp.where(use_bound, -sc_ref[0], 0.0)

    qfT = q_ref[...].astype(F32).T
    for g in range(G):
        qT_ref[0:DH, g * QB:(g + 1) * QB] = qfT[g * DH:(g + 1) * DH, :].astype(BF16)
    qT_ref[DH:LANES, :] = jnp.zeros((LANES - DH, G * QB), BF16)

    def front(nr, bounded):
        nsu = nr // (SLC_BLOCK // CMP_STRIDE)
        cmask = iota((nr, QB), 0) * CMP_STRIDE + (CMP_BLOCK - 1) <= t0 + iota((nr, QB), 1)
        cb = jnp.where(cmask, -sc_ref[2] if bounded else 0.0, NEG)
        s = _dot(kc_ref[0:nr, :], qT_ref[...]) + jnp.concatenate([cb] * G, axis=1)
        p = jnp.exp2(s if bounded else s - jnp.max(s, axis=0, keepdims=True))
        l = jnp.sum(p, axis=0, keepdims=True)
        anyv = jnp.where(t0 + iota((1, QB), 1) >= CMP_BLOCK - 1, 1.0, 0.0)
        pn = p * jnp.where(jnp.concatenate([anyv] * G, axis=1) > 0.5, 1.0 / l, 0.0)
        oc_ref[...] = _dot(vcT_ref[:, 0:nr], pn.astype(BF16))
        imp = pn[:, 0:QB]
        for g in range(1, G):
            imp = imp + pn[:, g * QB:(g + 1) * QB]

        parts = []
        for w in range(QB // LANES):
            imp_ref[w, 0:8, :] = jnp.zeros((8, LANES), F32)
            imp_ref[w, 8:8 + nr, :] = imp[:, w * LANES:(w + 1) * LANES]
            acc = imp_ref[w, pl.ds(7, nsu, stride=4), :]
            for r in range(4):
                acc = acc + imp_ref[w, pl.ds(8 + r, nsu, stride=4), :]
            parts.append(acc)
        islc = jnp.concatenate(parts, axis=1)

        j_i = iota((nsu, QB), 0)
        t_s = t0 + iota((nsu, QB), 1)
        cur = lax.shift_right_logical(t_s, 6)
        valid = j_i * SLC_BLOCK <= t_s
        ninf = -jnp.inf
        score = jnp.where(j_i == 0, ninf, jnp.where(j_i == cur, ninf,
                                                    jnp.where(j_i == cur - 1, ninf, islc)))
        score = jnp.where(valid, score, NEG)
        j_f = j_i.astype(F32)
        for _ in range(N_SEL - 3):
            mx = jnp.max(score, axis=0, keepdims=True)
            jm = jnp.min(jnp.where(score == mx, j_f, float(nsu)), axis=0, keepdims=True)
            score = jnp.where(j_f == jm, -jnp.inf, score)
        own = lax.shift_right_logical(j_i, (QB // SLC_BLOCK).bit_length() - 1) == i
        picked = jnp.where(valid, jnp.where(score == -jnp.inf, shift, NEG), NEG)
        selo_ref[0:nsu, :] = picked
        selb_ref[0:nsu, :] = jnp.where(own, NEG, picked)
        pad = -nsu % min(BIAS_BLOCKS, ns)
        if pad:
            selb_ref[nsu:nsu + pad, :] = jnp.full((pad, QB), NEG, F32)

    cls = (i * (QB // CMP_STRIDE) + (QB - CMP_BLOCK) // CMP_STRIDE) // cch
    for kk in range(nc // cch):
        for bounded in (True, False):
            pl.when(jnp.logical_and(cls == kk, use_bound == bounded))(
                functools.partial(front, (kk + 1) * cch, bounded))

    bpt = QB // SLC_BLOCK
    own_b = jnp.concatenate(
        [jnp.broadcast_to(selo_ref[pl.ds(i * bpt + b, 1), :], (SLC_BLOCK, QB)) for b in range(bpt)],
        axis=0)
    own_b = jnp.where(iota((QB, QB), 0) <= iota((QB, QB), 1), own_b, NEG)
    own_v = jnp.concatenate([vsT_ref[i * (QB // LANES) + w] for w in range(QB // LANES)], axis=1)
    sd = (_dot(ks_ref[pl.ds(pl.multiple_of(t0, QB), QB), :], qT_ref[...])
          + jnp.concatenate([own_b] * G, axis=1))

    nv = KCH // LANES
    cpg = BIAS_BLOCKS // (KCH // SLC_BLOCK)
    brows = min(BIAS_BLOCKS, ns)
    nstep = (i + NCHUNK - 1) // NCHUNK

    def chunk_scores(c0, n=NCHUNK):
        @pl.when(c0 % cpg == 0)
        def _():
            r0 = pl.multiple_of((c0 // cpg) * brows, brows)
            rows = selb_ref[pl.ds(r0, brows), :].astype(BF16)
            qT_ref[DH:DH + brows, :] = jnp.concatenate([rows] * G, axis=1)

        qT = qT_ref[...]
        return [_dot(ks_ref[pl.ds(pl.multiple_of((c0 + u) * KCH, KCH), KCH), :], qT)
                for u in range(n)]

    def values(c):
        return jnp.concatenate([vsT_ref[c * nv + w] for w in range(nv)], axis=1)

    def finish(acc):
        os_ref[...] = acc[0:DH, :] * (1.0 / acc[DH:DH + 1, :])

    @pl.when(use_bound)
    def _():
        def add_chunks(c0, n, acc):
            r0 = pl.multiple_of((c0 // cpg) * brows, brows)
            rows = selb_ref[pl.ds(r0, brows), :].astype(BF16)
            qT_ref[DH:DH + brows, :] = jnp.concatenate([rows] * G, axis=1)
            qT = qT_ref[...]
            for sub in range(0, n, NCHUNK):
                cs = [c0 + sub + u for u in range(min(NCHUNK, n - sub))]
                ss = [_dot(ks_ref[pl.ds(pl.multiple_of(c * KCH, KCH), KCH), :], qT) for c in cs]
                for c, s in zip(cs, ss):
                    acc = acc + _dot(values(c), jnp.exp2(s).astype(BF16))
            return acc

        nfull = i // cpg
        rem = i - nfull * cpg
        acc0 = _dot(own_v, jnp.exp2(sd).astype(BF16))
        acc_ref[...] = lax.fori_loop(
            0, nfull, lambda grp, acc: add_chunks(grp * cpg, cpg, acc), acc0)

        half = jnp.where(rem >= NCHUNK, NCHUNK, 0)
        c1 = nfull * cpg + half
        rest = rem - half

        @pl.when(half > 0)
        def _():
            acc_ref[...] = add_chunks(nfull * cpg, NCHUNK, acc_ref[...])

        @pl.when(jnp.logical_and(rest > 0, rest <= NCHUNK // 2))
        def _():
            acc_ref[...] = add_chunks(c1, NCHUNK // 2, acc_ref[...])

        @pl.when(rest > NCHUNK // 2)
        def _():
            acc_ref[...] = add_chunks(c1, NCHUNK, acc_ref[...])

        finish(acc_ref[...])

    @pl.when(jnp.logical_not(use_bound))
    def _():
        def step(grp, carry):
            m, acc = carry
            ss = chunk_scores(grp * NCHUNK)
            for u in range(NCHUNK):
                s = ss[u]
                m_new = jnp.maximum(m, jnp.max(s, axis=0, keepdims=True))
                p = jnp.exp2(s - m_new).astype(BF16)
                acc = jnp.exp2(m - m_new) * acc + _dot(values(grp * NCHUNK + u), p)
                m = m_new
            return m, acc

        m0 = jnp.max(sd, axis=0, keepdims=True)
        acc0 = _dot(own_v, jnp.exp2(sd - m0).astype(BF16))
        finish(lax.fori_loop(0, nstep, step, (m0, acc0))[1])

    os_ = os_ref[...]

    def window(bounded):
        wc = jnp.maximum(i * (QB // LANES) - WINDOW // LANES, 0)
        ws = wc * LANES
        sw = _dot(kw_ref[pl.ds(pl.multiple_of(ws, LANES), wspan), :], qT_ref[...])
        kpos = ws + iota((wspan, QB), 0)
        t_w = t0 + iota((wspan, QB), 1)
        inside = -sc_ref[3] if bounded else 0.0
        wb = jnp.where(kpos <= t_w, jnp.where(kpos > t_w - WINDOW, inside, NEG), NEG)
        sw = sw + jnp.concatenate([wb] * G, axis=1)
        p_w = jnp.exp2(sw if bounded else sw - jnp.max(sw, axis=0, keepdims=True)).astype(BF16)
        vw = jnp.concatenate([vwT_ref[wc + u] for u in range(wspan // LANES)], axis=1)
        acc_w = _dot(vw, p_w)
        ow_ref[...] = acc_w[0:DH, :] * (1.0 / acc_w[DH:DH + 1, :])

    for bounded in (True, False):
        pl.when(use_bound == bounded)(functools.partial(window, bounded))
    ow = ow_ref[...]

    outs = []
    for g in range(G):
        sl = slice(g * QB, (g + 1) * QB)
        row = h * (3 * G) + 3 * g
        gc = gT_ref[pl.ds(row, 1), :]
        gs = gT_ref[pl.ds(row + 1, 1), :]
        gw = gT_ref[pl.ds(row + 2, 1), :]
        outs.append(gc * oc_ref[:, sl] + gs * os_[:, sl] + gw * ow[:, sl])
    y_ref[...] = jnp.concatenate(outs, axis=0).T


def _score_bound(qgain, kgain_slc, kgain_win, kgain_cmp):
    qmax = jnp.max(jnp.abs(qgain))
    ms = [(1.05 * DH * qmax * jnp.max(jnp.abs(kg))).astype(BF16).astype(F32)
          for kg in (kgain_slc, kgain_cmp, kgain_win)]
    ok = (jnp.maximum(jnp.maximum(ms[0], ms[1]), ms[2]) <= SAFE_SHIFT).astype(F32)
    return jnp.stack([ms[0], ok, ms[1], ms[2]])


def _attention(sc, q, kc, vcT, kslc, vslcT, kwin, vwinT, gT, B, S):
    NC = S // CMP_STRIDE
    NS = S // SLC_BLOCK
    NQ = S // QB
    cch = min(CCH, NC)
    assert NC % cch == 0 and (S // KCH) % NCHUNK == 0 and QB == KCH
    assert BIAS_BLOCKS * SLC_BLOCK // KCH == 2 * NCHUNK
    head = lambda *blk: pl.BlockSpec((None, None) + blk, lambda b, h, i: (b, h) + (0,) * len(blk),
                                     pipeline_mode=pl.Buffered(1))
    return pl.pallas_call(
        functools.partial(_attn_kernel, cch=cch),
        out_shape=jax.ShapeDtypeStruct((B, S, NSA_W), F32),
        grid=(B, HKV, NQ),
        in_specs=[
            pl.BlockSpec(memory_space=pltpu.SMEM),
            pl.BlockSpec((None, QB, G * DH), lambda b, h, i: (b, i, h)),
            head(NC, LANES), head(DH, NC),
            head(S, LANES), head(S // LANES, VROWS, LANES),
            head(S, LANES), head(S // LANES, VROWS, LANES),
            pl.BlockSpec((None, GATE_ROWS, QB), lambda b, h, i: (b, 0, i)),
        ],
        out_specs=pl.BlockSpec((None, QB, G * DH), lambda b, h, i: (b, i, h)),
        scratch_shapes=[pltpu.VMEM((LANES, G * QB), BF16),
                        pltpu.VMEM((QB // LANES, NC + 8, LANES), F32),
                        pltpu.VMEM((NS, QB), F32), pltpu.VMEM((NS, QB), F32),
                        pltpu.VMEM((DH, G * QB), F32),
                        pltpu.VMEM((DH, G * QB), F32), pltpu.VMEM((DH, G * QB), F32),
                        pltpu.VMEM((VROWS, G * QB), F32)],
        compiler_params=pltpu.CompilerParams(
            dimension_semantics=("arbitrary", "arbitrary", "arbitrary"),
            vmem_limit_bytes=VMEM_LIMIT),
        name="attention",
    )(sc, q, kc, vcT, kslc, vslcT, kwin, vwinT, gT)


XH = 8
YH = 32


def _causal_taps(buf, w_ref, bias, hist, ntaps):
    rows = TS + 8
    base = hist - (ntaps - 1)
    out = jnp.broadcast_to(bias, (TS, buf.shape[1]))
    for r in range(8):
        z = None
        for k in range(ntaps):
            if (base + k) % 8 == r:
                term = w_ref[k:k + 1, :] * buf[base + k - r:base + k - r + rows, :]
                z = term if z is None else z + term
        if z is not None:
            out = out + (z[0:TS, :] if r == 0 else pltpu.roll(z, rows - r, 0)[0:TS, :])
    return out


def _seq_mix_kernel(lru_ref, cv_ref, cw_ref, cb_ref, wa_ref, ba_ref, wx_ref, bx_ref, lam_ref,
                    dw_ref, db_ref, lng_ref, lnb_ref, gl_ref, gc_ref,
                    yl_ref, yc_ref, xbuf, ybuf, hbuf):
    t = pl.program_id(1)

    @pl.when(t == 0)
    def _():
        xbuf[0:XH, :] = jnp.zeros((XH, LRU_W), F32)
        xbuf[XH + TS:XH + TS + 8, :] = jnp.zeros((8, LRU_W), F32)
        ybuf[0:YH, :] = jnp.zeros((YH, CV_W), F32)
        ybuf[YH + TS:YH + TS + 8, :] = jnp.zeros((8, CV_W), F32)
        hbuf[...] = jnp.zeros((8, LRU_W), F32)

    xb = lru_ref[:, 0:LRU_W]
    gb = lru_ref[:, LRU_W:2 * LRU_W]
    xbuf[XH:XH + TS, :] = xb
    xr = _causal_taps(xbuf, cw_ref, cb_ref[...], XH, LRU_CONV)
    xbuf[0:XH, :] = xbuf[TS:TS + XH, :]

    xr16 = xr.astype(BF16)
    r = _sigmoid(_dot(xr16, wa_ref[...]) + ba_ref[...])
    ig = _sigmoid(_dot(xr16, wx_ref[...]) + bx_ref[...])
    nl = -lam_ref[...]
    softplus = jnp.maximum(nl, 0.0) + jnp.log1p(jnp.exp(-jnp.abs(nl)))
    log_a = -LRU_C * r * softplus
    a = jnp.exp(log_a)
    u = xr * ig * jnp.sqrt(-jnp.tanh(log_a) * (a * a + 1.0))

    row = lax.broadcasted_iota(jnp.int32, (TS, LRU_W), 0)
    d = 1
    while d < TS:
        keep = row >= d
        a_sh = jnp.where(keep, pltpu.roll(a, d, 0), 1.0)
        u_sh = jnp.where(keep, pltpu.roll(u, d, 0), 0.0)
        u = a * u_sh + u
        a = a * a_sh
        d *= 2
    hseq = a * hbuf[0:1, :] + u
    hbuf[...] = jnp.broadcast_to(hseq[TS - 1:TS, :], (8, LRU_W))
    yl = hseq * _gelu_tanh(gb)
    yl = yl * lax.rsqrt(jnp.mean(yl * yl, axis=-1, keepdims=True) + EPS) * gl_ref[...]
    yl_ref[...] = yl.astype(BF16)

    y = cv_ref[:, 0:CV_W] * _sigmoid(cv_ref[:, CV_W:2 * CV_W])
    ybuf[YH:YH + TS, :] = y
    c = _causal_taps(ybuf, dw_ref, db_ref[...], YH, CV_KERNEL)
    ybuf[0:YH, :] = ybuf[TS:TS + YH, :]
    mu = jnp.mean(c, axis=-1, keepdims=True)
    cc = c - mu
    var = jnp.mean(cc * cc, axis=-1, keepdims=True)
    ln = cc * lax.rsqrt(var + EPS) * lng_ref[...] + lnb_ref[...]
    yc = ln * _sigmoid(ln)
    yc = yc * lax.rsqrt(jnp.mean(yc * yc, axis=-1, keepdims=True) + EPS) * gc_ref[...]
    yc_ref[...] = yc.astype(BF16)


def _seq_mix(lru, cv, params, B, S):
    const = lambda a: pl.BlockSpec(a.shape, lambda b, t: (0,) * a.ndim)
    tile = lambda w_: pl.BlockSpec((None, TS, w_), lambda b, t: (b, t, 0))
    return pl.pallas_call(
        _seq_mix_kernel,
        out_shape=(jax.ShapeDtypeStruct((B, S, LRU_W), BF16),
                   jax.ShapeDtypeStruct((B, S, CV_W), BF16)),
        grid=(B, S // TS),
        in_specs=[tile(2 * LRU_W), tile(2 * CV_W)] + [const(p) for p in params],
        out_specs=(tile(LRU_W), tile(CV_W)),
        scratch_shapes=[pltpu.VMEM((XH + TS + 8, LRU_W), F32), pltpu.VMEM((YH + TS + 8, CV_W), F32),
                        pltpu.VMEM((8, LRU_W), F32)],
        compiler_params=pltpu.CompilerParams(
            dimension_semantics=("arbitrary", "arbitrary"), vmem_limit_bytes=VMEM_LIMIT),
        name="seq_mix",
    )(lru, cv, *params)


def _out_mlp_kernel(x_ref, ya_ref, yl_ref, yc_ref, ga_ref, wo_ref, gm_ref, w1_ref, w2_ref, o_ref):
    ya = ya_ref[...]
    ya = (ya * lax.rsqrt(jnp.mean(ya * ya, axis=-1, keepdims=True) + EPS) * ga_ref[...]).astype(BF16)
    x1 = (x_ref[...] + _dot(ya, wo_ref[0:NSA_W, :])
          + _dot(yl_ref[...], wo_ref[NSA_W:NSA_W + LRU_W, :])
          + _dot(yc_ref[...], wo_ref[NSA_W + LRU_W:NSA_W + LRU_W + CV_W, :]))
    hm = (x1 * lax.rsqrt(jnp.mean(x1 * x1, axis=-1, keepdims=True) + EPS) * gm_ref[...]).astype(BF16)
    d_ff = w1_ref.shape[1]
    fc = 1024
    o_ref[...] = x1
    for c in range(d_ff // fc):
        hc = jnp.maximum(_dot(hm, w1_ref[:, c * fc:(c + 1) * fc]), 0.0)
        o_ref[...] += _dot((hc * hc).astype(BF16), w2_ref[c * fc:(c + 1) * fc, :])


def _out_mlp(x2, ya, yl, yc, ga, wo, gm, w1, w2):
    T, D = x2.shape
    tok = lambda w_: pl.BlockSpec((TM, w_), lambda t: (t, 0))
    const1 = lambda a: pl.BlockSpec(a.shape, lambda t: (0,) * a.ndim, pipeline_mode=pl.Buffered(1))
    return pl.pallas_call(
        _out_mlp_kernel,
        out_shape=jax.ShapeDtypeStruct((T, D), F32),
        grid=(T // TM,),
        in_specs=[tok(D), tok(NSA_W), tok(LRU_W), tok(CV_W),
                  const1(ga), const1(wo), const1(gm), const1(w1), const1(w2)],
        out_specs=tok(D),
        compiler_params=pltpu.CompilerParams(
            dimension_semantics=("arbitrary",), vmem_limit_bytes=VMEM_LIMIT),
        name="out_mlp",
    )(x2, ya, yl, yc, ga, wo, gm, w1, w2)


def _block_ones(n):
    idx = np.arange(n) // DH
    return jnp.asarray((idx[:, None] == idx[None, :]).astype(np.float32) / DH, dtype=BF16)


def _permute_w_in(w):
    off_kv = NSA_W
    kv = [w[:, off_kv + c * 128: off_kv + (c + 1) * 128] for c in range(6)]
    off_gate = off_kv + 6 * 128
    ngate = 3 * HKV * G
    off_lru = off_gate + ngate
    gate = jnp.pad(w[:, off_gate:off_lru], ((0, 0), (0, LANES - ngate)))
    cols = [w[:, 0:NSA_W], kv[0], kv[1], kv[2], kv[4], kv[3], kv[5],
            w[:, off_lru:off_lru + 2 * LRU_W + 2 * CV_W], gate]
    return jnp.concatenate(cols, axis=1).astype(BF16)


def _compress_weights(pos, w1, w2):
    slot_kv = jnp.array([0, 0, 1, 1])
    eye = jnp.eye(4, dtype=F32)
    w1r = w1.reshape(2, 2, CMP_STRIDE, DH, CMP_HID)[slot_kv]
    wbig = jnp.einsum('shpdo,st->psdhto', w1r, eye).reshape(CMP_STRIDE * 4 * DH, 2 * 4 * CMP_HID)
    posr = pos.reshape(2, 2, CMP_STRIDE, DH)[slot_kv]
    pos2 = jnp.transpose(posr, (1, 2, 0, 3)).reshape(2, CMP_STRIDE * 4 * DH)
    pos2 = jnp.pad(pos2, ((0, 6), (0, 0)))
    w2big = jnp.einsum('sod,st->sotd', w2[slot_kv], eye).reshape(4 * CMP_HID, 4 * DH)
    return pos2.astype(BF16), wbig.astype(BF16), w2big.astype(BF16)


def _block_diag(w):
    hh, bw, _ = w.shape
    eye = jnp.eye(hh, dtype=w.dtype)
    return jnp.einsum('hij,hg->higj', w, eye).reshape(hh * bw, hh * bw).astype(BF16)


def kernel(x, attn_norm, w_in, q_norm, k_norm, cmp_pos, cmp_w1, cmp_w2, lru_conv_w, lru_conv_b,
           lru_wa, lru_ba, lru_wx, lru_bx, lru_lambda, cv_dw_w, cv_dw_b, cv_ln_g, cv_ln_b,
           out_norm, w_out, mlp_norm, mlp_w1, mlp_w2):
    B, S, D = x.shape
    depth = w_in.shape[0]
    assert S % TM == 0 and S % TS == 0 and S >= WINDOW + QB and S // SLC_BLOCK >= N_SEL
    NC = S // CMP_STRIDE
    row = lambda v: v.reshape(1, -1).astype(F32)
    bd512 = _block_ones(NSA_W)
    bd128 = _block_ones(128)

    x2 = x.reshape(B * S, D)
    for l in range(depth):
        qgain = row(jnp.tile(q_norm[l], HKV * G)) * (DH ** -0.5 * LOG2E)
        kgain = row(jnp.concatenate([jnp.tile(k_norm[l, 1], HKV), jnp.tile(k_norm[l, 2], HKV)]))
        q, cmpr, kslc, kwin, vslcT, vwinT, lru, cv, gT = _in_proj(
            x2, row(attn_norm[l]), _permute_w_in(w_in[l]), bd512, qgain, kgain, B, S)

        pos2, wbig, w2big = _compress_weights(cmp_pos[l], cmp_w1[l], cmp_w2[l])
        kc, vcT = _compress(cmpr.reshape(B, NC, CMP_STRIDE * 256), pos2, wbig, w2big, bd128,
                            row(jnp.tile(k_norm[l, 0], HKV)), B, NC)

        sc = _score_bound(qgain, k_norm[l, 1], k_norm[l, 2], k_norm[l, 0])
        y_attn = _attention(sc, q.reshape(B, S, NSA_W), kc, vcT, kslc, vslcT, kwin, vwinT, gT, B, S)

        g_out = out_norm[l]
        seq_params = (lru_conv_w[l], row(lru_conv_b[l]), _block_diag(lru_wa[l]), row(lru_ba[l]),
                      _block_diag(lru_wx[l]), row(lru_bx[l]), row(lru_lambda[l]),
                      cv_dw_w[l], row(cv_dw_b[l]), row(cv_ln_g[l]), row(cv_ln_b[l]),
                      row(g_out[NSA_W:NSA_W + LRU_W]), row(g_out[NSA_W + LRU_W:]))
        yl, yc = _seq_mix(lru.reshape(B, S, 2 * LRU_W), cv.reshape(B, S, 2 * CV_W), seq_params, B, S)

        x2 = _out_mlp(x2, y_attn.reshape(B * S, NSA_W), yl.reshape(B * S, LRU_W),
                      yc.reshape(B * S, CV_W), row(g_out[:NSA_W]), w_out[l].astype(BF16),
                      row(mlp_norm[l]), mlp_w1[l].astype(BF16), mlp_w2[l].astype(BF16))
    return x2.reshape(B, S, D)
```

```python
import functools

import numpy as np
import jax
import jax.numpy as jnp
from jax import lax
from jax.experimental import pallas as pl
from jax.experimental.pallas import tpu as pltpu

F32 = jnp.float32
BF16 = jnp.bfloat16

EPS = 1e-6
NEG = -1e30
DH = 64
HKV = 2
G = 4
NSA_W = HKV * G * DH
CMP_BLOCK = 32
CMP_STRIDE = 16
CMP_HID = 128
SLC_BLOCK = 64
N_SEL = 16
WINDOW = 512
QB = 256
LRU_W = 256
LRU_HEADS = 8
LRU_CONV = 4
LRU_C = 8.0
CV_W = 256
CV_KERNEL = 31
LOG2E = 1.4426950408889634
SAFE_SHIFT = 50.0

LANES = 128
KCH = 256
NCHUNK = 8
CCH = 256
TM = 512
TS = 512
VMEM_LIMIT = 48 * 1024 * 1024

C_Q = 0
C_CMP = C_Q + NSA_W
C_KSLC = C_CMP + 256
C_KWIN = C_KSLC + 128
C_VSLC = C_KWIN + 128
C_VWIN = C_VSLC + 128
C_LRU = C_VWIN + 128
C_CV = C_LRU + 2 * LRU_W
C_GATE = C_CV + 2 * CV_W
N_INP = C_GATE + LANES
GATE_ROWS = 32
VROWS = DH + 16
BIAS_BLOCKS = LANES - DH


def _dot(a, b):
    return jnp.dot(a, b, preferred_element_type=F32)


def _head_rms_scale(z, bd):
    sq = z * z
    hi = sq.astype(BF16)
    lo = (sq - hi.astype(F32)).astype(BF16)
    ms = _dot(hi, bd) + _dot(lo, bd)
    return lax.rsqrt(ms + EPS)


def _gelu_tanh(x):
    return 0.5 * x * (1.0 + jnp.tanh(0.7978845608028654 * (x + 0.044715 * (x * x * x))))


def _sigmoid(x):
    return 1.0 / (1.0 + jnp.exp(-x))


def _in_proj_kernel(x_ref, g_ref, w_ref, bd_ref, qgain_ref, kgain_ref,
                    q_ref, cmp_ref, kslc_ref, kwin_ref, vslcT_ref, vwinT_ref,
                    lru_ref, cv_ref, gT_ref, *, nt):
    x = x_ref[...]
    ms = jnp.mean(x * x, axis=-1, keepdims=True)
    hn = (x * lax.rsqrt(ms + EPS) * g_ref[...]).astype(BF16)

    zq = _dot(hn, w_ref[:, C_Q:C_CMP])
    q_ref[...] = (zq * _head_rms_scale(zq, bd_ref[...]) * qgain_ref[...]).astype(BF16)

    cmp_ref[...] = _dot(hn, w_ref[:, C_CMP:C_KSLC]).astype(BF16)

    zk = _dot(hn, w_ref[:, C_KSLC:C_VSLC])
    kn = zk * _head_rms_scale(zk, bd_ref[0:256, 0:256]) * kgain_ref[...]
    lane = lax.broadcasted_iota(jnp.int32, (TM, LANES), 1)
    rowg = (pl.program_id(0) % nt) * TM + lax.broadcasted_iota(jnp.int32, (TM, LANES), 0)
    blk = lax.shift_right_logical(rowg, 6) & (BIAS_BLOCKS - 1)
    onehot = jnp.where(lane == DH + blk, 1.0, 0.0)
    lo = lane < DH
    ks, kw = kn[:, 0:128], kn[:, 128:256]
    kslc_ref[0] = jnp.where(lo, ks, onehot).astype(BF16)
    kslc_ref[1] = jnp.where(lo, pltpu.roll(ks, DH, 1), onehot).astype(BF16)
    kwin_ref[0] = jnp.where(lo, kw, 0.0).astype(BF16)
    kwin_ref[1] = jnp.where(lo, pltpu.roll(kw, DH, 1), 0.0).astype(BF16)

    zv = _dot(hn, w_ref[:, C_VSLC:C_LRU])
    zvT = zv.T
    ones_row = jnp.where(lax.broadcasted_iota(jnp.int32, (VROWS - DH, LANES), 0) == 0,
                         1.0, 0.0).astype(BF16)
    for j in range(TM // LANES):
        for h in range(HKV):
            cols = slice(j * LANES, (j + 1) * LANES)
            vslcT_ref[h, j, 0:DH, :] = zvT[h * DH:(h + 1) * DH, cols].astype(BF16)
            vslcT_ref[h, j, DH:VROWS, :] = ones_row
            vwinT_ref[h, j, 0:DH, :] = zvT[128 + h * DH:128 + (h + 1) * DH, cols].astype(BF16)
            vwinT_ref[h, j, DH:VROWS, :] = ones_row

    lru_ref[...] = _dot(hn, w_ref[:, C_LRU:C_CV])
    cv_ref[...] = _dot(hn, w_ref[:, C_CV:C_GATE])

    zg = _sigmoid(_dot(hn, w_ref[:, C_GATE:N_INP]))
    gT_ref[...] = zg.T[0:GATE_ROWS, :]


def _in_proj(x2, g, w, bd, qgain, kgain, B, S):
    T, D = x2.shape
    nt = S // TM
    tok = lambda w_: pl.BlockSpec((TM, w_), lambda t: (t, 0))
    const = lambda a: pl.BlockSpec(a.shape, lambda t: (0,) * a.ndim)
    vT_spec = pl.BlockSpec((None, HKV, TM // LANES, VROWS, LANES),
                           lambda t: (t // nt, 0, t % nt, 0, 0))
    k_spec = pl.BlockSpec((None, HKV, TM, LANES), lambda t: (t // nt, 0, t % nt, 0))
    out_shape = (
        jax.ShapeDtypeStruct((T, NSA_W), BF16),
        jax.ShapeDtypeStruct((T, 256), BF16),
        jax.ShapeDtypeStruct((B, HKV, S, LANES), BF16),
        jax.ShapeDtypeStruct((B, HKV, S, LANES), BF16),
        jax.ShapeDtypeStruct((B, HKV, S // LANES, VROWS, LANES), BF16),
        jax.ShapeDtypeStruct((B, HKV, S // LANES, VROWS, LANES), BF16),
        jax.ShapeDtypeStruct((T, 2 * LRU_W), F32),
        jax.ShapeDtypeStruct((T, 2 * CV_W), F32),
        jax.ShapeDtypeStruct((B, GATE_ROWS, S), F32),
    )
    out_specs = (
        tok(NSA_W), tok(256), k_spec, k_spec, vT_spec, vT_spec,
        tok(2 * LRU_W), tok(2 * CV_W),
        pl.BlockSpec((None, GATE_ROWS, TM), lambda t: (t // nt, 0, t % nt)),
    )
    return pl.pallas_call(
        functools.partial(_in_proj_kernel, nt=nt),
        out_shape=out_shape,
        grid=(T // TM,),
        in_specs=[tok(D), const(g), const(w), const(bd), const(qgain), const(kgain)],
        out_specs=out_specs,
        compiler_params=pltpu.CompilerParams(
            dimension_semantics=("arbitrary",), vmem_limit_bytes=VMEM_LIMIT),
        name="in_proj",
    )(x2, g, w, bd, qgain, kgain)


def _compress_kernel(x_ref, pos_ref, wbig_ref, w2_ref, bd_ref, kgain_ref, kc_ref, vcT_ref):
    nc = x_ref.shape[0]
    half = 4 * CMP_HID
    p = _dot(x_ref[...], wbig_ref[...])
    pc = _dot(pos_ref[...], wbig_ref[...])
    const = pc[0:1, 0:half] + pc[1:2, half:2 * half]
    nxt = pltpu.roll(p[:, half:2 * half], nc - 1, 0)
    hid = _gelu_tanh(p[:, 0:half] + nxt + const).astype(BF16)
    kv = _dot(hid, w2_ref[...])
    kc = kv[:, 0:128]
    kc = kc * _head_rms_scale(kc, bd_ref[...]) * kgain_ref[...]
    lo = lax.broadcasted_iota(jnp.int32, (nc, LANES), 1) < DH
    kc_ref[0] = jnp.where(lo, kc, 0.0).astype(BF16)
    kc_ref[1] = jnp.where(lo, pltpu.roll(kc, DH, 1), 0.0).astype(BF16)
    vT = kv[:, 128:256].T
    for h in range(HKV):
        vcT_ref[h] = vT[h * DH:(h + 1) * DH, :].astype(BF16)


def _compress(cmpx, pos2, wbig, w2big, bd128, kgain, B, NC):
    const = lambda a: pl.BlockSpec(a.shape, lambda b: (0,) * a.ndim, pipeline_mode=pl.Buffered(1))
    return pl.pallas_call(
        _compress_kernel,
        out_shape=(jax.ShapeDtypeStruct((B, HKV, NC, LANES), BF16),
                   jax.ShapeDtypeStruct((B, HKV, DH, NC), BF16)),
        grid=(B,),
        in_specs=[pl.BlockSpec((None, NC, cmpx.shape[2]), lambda b: (b, 0, 0)),
                  const(pos2), const(wbig), const(w2big), const(bd128), const(kgain)],
        out_specs=(pl.BlockSpec((None, HKV, NC, LANES), lambda b: (b, 0, 0, 0)),
                   pl.BlockSpec((None, HKV, DH, NC), lambda b: (b, 0, 0, 0))),
        compiler_params=pltpu.CompilerParams(
            dimension_semantics=("arbitrary",), vmem_limit_bytes=VMEM_LIMIT),
        name="compress",
    )(cmpx, pos2, wbig, w2big, bd128, kgain)


def _attn_kernel(sc_ref, q_ref, kc_ref, vcT_ref, ks_ref, vsT_ref, kw_ref, vwT_ref, gT_ref,
                 y_ref, qT_ref, imp_ref, selb_ref, selo_ref, oc_ref, os_ref, ow_ref, acc_ref, *, cch):
    h = pl.program_id(1)
    i = pl.program_id(2)
    nc = kc_ref.shape[0]
    ns = selb_ref.shape[0]
    wspan = WINDOW + QB
    t0 = i * QB
    iota = lambda shape, ax: lax.broadcasted_iota(jnp.int32, shape, ax)
    use_bound = sc_ref[1] > 0.5
    shift = jnp.where(use_bound, -sc_ref[0], 0.0)

    qfT = q_ref[...].astype(F32).T
    for g in range(G):
        qT_ref[0:DH, g * QB:(g + 1) * QB] = qfT[g * DH:(g + 1) * DH, :].astype(BF16)
    qT_ref[DH:LANES, :] = jnp.zeros((LANES - DH, G * QB), BF16)

    def window(bounded):
        wc = jnp.maximum(i * (QB // LANES) - WINDOW // LANES, 0)
        ws = wc * LANES
        sw = _dot(kw_ref[pl.ds(pl.multiple_of(ws, LANES), wspan), :], qT_ref[...])
        kpos = ws + iota((wspan, QB), 0)
        t_w = t0 + iota((wspan, QB), 1)
        inside = -sc_ref[3] if bounded else 0.0
        wb = jnp.where(kpos <= t_w, jnp.where(kpos > t_w - WINDOW, inside, NEG), NEG)
        sw = sw + jnp.concatenate([wb] * G, axis=1)
        p_w = jnp.exp2(sw if bounded else sw - jnp.max(sw, axis=0, keepdims=True)).astype(BF16)
        vw = jnp.concatenate([vwT_ref[wc + u] for u in range(wspan // LANES)], axis=1)
        acc_w = _dot(vw, p_w)
        ow_ref[...] = acc_w[0:DH, :] * (1.0 / acc_w[DH:DH + 1, :])

    def front(nr, bounded):
        window(bounded)
        nsu = nr // (SLC_BLOCK // CMP_STRIDE)
        cmask = iota((nr, QB), 0) * CMP_STRIDE + (CMP_BLOCK - 1) <= t0 + iota((nr, QB), 1)
        cb = jnp.where(cmask, -sc_ref[2] if bounded else 0.0, NEG)
        s = _dot(kc_ref[0:nr, :], qT_ref[...]) + jnp.concatenate([cb] * G, axis=1)
        p = jnp.exp2(s if bounded else s - jnp.max(s, axis=0, keepdims=True))
        l = jnp.sum(p, axis=0, keepdims=True)
        anyv = jnp.where(t0 + iota((1, QB), 1) >= CMP_BLOCK - 1, 1.0, 0.0)
        pn = p * jnp.where(jnp.concatenate([anyv] * G, axis=1) > 0.5, 1.0 / l, 0.0)
        oc_ref[...] = _dot(vcT_ref[:, 0:nr], pn.astype(BF16))
        imp = pn[:, 0:QB]
        for g in range(1, G):
            imp = imp + pn[:, g * QB:(g + 1) * QB]

        parts = []
        for w in range(QB // LANES):
            imp_ref[w, 0:8, :] = jnp.zeros((8, LANES), F32)
            imp_ref[w, 8:8 + nr, :] = imp[:, w * LANES:(w + 1) * LANES]
            acc = imp_ref[w, pl.ds(7, nsu, stride=4), :]
            for r in range(4):
                acc = acc + imp_ref[w, pl.ds(8 + r, nsu, stride=4), :]
            parts.append(acc)
        islc = jnp.concatenate(parts, axis=1)

        j_i = iota((nsu, QB), 0)
        t_s = t0 + iota((nsu, QB), 1)
        cur = lax.shift_right_logical(t_s, 6)
        valid = j_i * SLC_BLOCK <= t_s
        ninf = -jnp.inf
        score = jnp.where(j_i == 0, ninf, jnp.where(j_i == cur, ninf,
                                                    jnp.where(j_i == cur - 1, ninf, islc)))
        score = jnp.where(valid, score, NEG)
        j_f = j_i.astype(F32)
        for _ in range(N_SEL - 3):
            mx = jnp.max(score, axis=0, keepdims=True)
            jm = jnp.min(jnp.where(score == mx, j_f, float(nsu)), axis=0, keepdims=True)
            score = jnp.where(j_f == jm, -jnp.inf, score)
        own = lax.shift_right_logical(j_i, (QB // SLC_BLOCK).bit_length() - 1) == i
        picked = jnp.where(valid, jnp.where(score == -jnp.inf, shift, NEG), NEG)
        selo_ref[0:nsu, :] = picked
        selb_ref[0:nsu, :] = jnp.where(own, NEG, picked)
        pad = -nsu % min(BIAS_BLOCKS, ns)
        if pad:
            selb_ref[nsu:nsu + pad, :] = jnp.full((pad, QB), NEG, F32)

    cls = (i * (QB // CMP_STRIDE) + (QB - CMP_BLOCK) // CMP_STRIDE) // cch
    for kk in range(nc // cch):
        for bounded in (True, False):
            pl.when(jnp.logical_and(cls == kk, use_bound == bounded))(
                functools.partial(front, (kk + 1) * cch, bounded))

    bpt = QB // SLC_BLOCK
    own_b = jnp.concatenate(
        [jnp.broadcast_to(selo_ref[pl.ds(i * bpt + b, 1), :], (SLC_BLOCK, QB)) for b in range(bpt)],
        axis=0)
    own_b = jnp.where(iota((QB, QB), 0) <= iota((QB, QB), 1), own_b, NEG)
    own_v = jnp.concatenate([vsT_ref[i * (QB // LANES) + w] for w in range(QB // LANES)], axis=1)
    sd = (_dot(ks_ref[pl.ds(pl.multiple_of(t0, QB), QB), :], qT_ref[...])
          + jnp.concatenate([own_b] * G, axis=1))

    nv = KCH // LANES
    cpg = BIAS_BLOCKS // (KCH // SLC_BLOCK)
    brows = min(BIAS_BLOCKS, ns)
    nstep = (i + NCHUNK - 1) // NCHUNK

    def chunk_scores(c0, n=NCHUNK):
        @pl.when(c0 % cpg == 0)
        def _():
            r0 = pl.multiple_of((c0 // cpg) * brows, brows)
            rows = selb_ref[pl.ds(r0, brows), :].astype(BF16)
            qT_ref[DH:DH + brows, :] = jnp.concatenate([rows] * G, axis=1)

        qT = qT_ref[...]
        return [_dot(ks_ref[pl.ds(pl.multiple_of((c0 + u) * KCH, KCH), KCH), :], qT)
                for u in range(n)]

    def values(c):
        return jnp.concatenate([vsT_ref[c * nv + w] for w in range(nv)], axis=1)

    def finish(acc):
        os_ref[...] = acc[0:DH, :] * (1.0 / acc[DH:DH + 1, :])

    @pl.when(use_bound)
    def _():
        def add_chunks(c0, n, acc):
            r0 = pl.multiple_of((c0 // cpg) * brows, brows)
            rows = selb_ref[pl.ds(r0, brows), :].astype(BF16)
            qT_ref[DH:DH + brows, :] = jnp.concatenate([rows] * G, axis=1)
            qT = qT_ref[...]
            for sub in range(0, n, NCHUNK):
                cs = [c0 + sub + u for u in range(min(NCHUNK, n - sub))]
                ss = [_dot(ks_ref[pl.ds(pl.multiple_of(c * KCH, KCH), KCH), :], qT) for c in cs]
                for c, s in zip(cs, ss):
                    acc = acc + _dot(values(c), jnp.exp2(s).astype(BF16))
            return acc

        nfull = i // cpg
        rem = i - nfull * cpg
        acc0 = _dot(own_v, jnp.exp2(sd).astype(BF16))
        acc_ref[...] = lax.fori_loop(
            0, nfull, lambda grp, acc: add_chunks(grp * cpg, cpg, acc), acc0)

        half = jnp.where(rem >= NCHUNK, NCHUNK, 0)
        c1 = nfull * cpg + half
        rest = rem - half

        @pl.when(half > 0)
        def _():
            acc_ref[...] = add_chunks(nfull * cpg, NCHUNK, acc_ref[...])

        @pl.when(jnp.logical_and(rest > 0, rest <= NCHUNK // 2))
        def _():
            acc_ref[...] = add_chunks(c1, NCHUNK // 2, acc_ref[...])

        @pl.when(rest > NCHUNK // 2)
        def _():
            acc_ref[...] = add_chunks(c1, NCHUNK, acc_ref[...])

        finish(acc_ref[...])

    @pl.when(jnp.logical_not(use_bound))
    def _():
        def step(grp, carry):
            m, acc = carry
            ss = chunk_scores(grp * NCHUNK)
            for u in range(NCHUNK):
                s = ss[u]
                m_new = jnp.maximum(m, jnp.max(s, axis=0, keepdims=True))
                p = jnp.exp2(s - m_new).astype(BF16)
                acc = jnp.exp2(m - m_new) * acc + _dot(values(grp * NCHUNK + u), p)
                m = m_new
            return m, acc

        m0 = jnp.max(sd, axis=0, keepdims=True)
        acc0 = _dot(own_v, jnp.exp2(sd - m0).astype(BF16))
        finish(lax.fori_loop(0, nstep, step, (m0, acc0))[1])

    os_ = os_ref[...]
    ow = ow_ref[...]

    outs = []
    for g in range(G):
        sl = slice(g * QB, (g + 1) * QB)
        row = h * (3 * G) + 3 * g
        gc = gT_ref[pl.ds(row, 1), :]
        gs = gT_ref[pl.ds(row + 1, 1), :]
        gw = gT_ref[pl.ds(row + 2, 1), :]
        outs.append(gc * oc_ref[:, sl] + gs * os_[:, sl] + gw * ow[:, sl])
    y_ref[...] = jnp.concatenate(outs, axis=0).T


def _score_bound(qgain, kgain_slc, kgain_win, kgain_cmp):
    qmax = jnp.max(jnp.abs(qgain))
    ms = [(1.05 * DH * qmax * jnp.max(jnp.abs(kg))).astype(BF16).astype(F32)
          for kg in (kgain_slc, kgain_cmp, kgain_win)]
    ok = (jnp.maximum(jnp.maximum(ms[0], ms[1]), ms[2]) <= SAFE_SHIFT).astype(F32)
    return jnp.stack([ms[0], ok, ms[1], ms[2]])


def _attention(sc, q, kc, vcT, kslc, vslcT, kwin, vwinT, gT, B, S):
    NC = S // CMP_STRIDE
    NS = S // SLC_BLOCK
    NQ = S // QB
    cch = min(CCH, NC)
    assert NC % cch == 0 and (S // KCH) % NCHUNK == 0 and QB == KCH
    assert BIAS_BLOCKS * SLC_BLOCK // KCH == 2 * NCHUNK
    head = lambda *blk: pl.BlockSpec((None, None) + blk, lambda b, h, i: (b, h) + (0,) * len(blk),
                                     pipeline_mode=pl.Buffered(1))
    return pl.pallas_call(
        functools.partial(_attn_kernel, cch=cch),
        out_shape=jax.ShapeDtypeStruct((B, S, NSA_W), F32),
        grid=(B, HKV, NQ),
        in_specs=[
            pl.BlockSpec(memory_space=pltpu.SMEM),
            pl.BlockSpec((None, QB, G * DH), lambda b, h, i: (b, i, h)),
            head(NC, LANES), head(DH, NC),
            head(S, LANES), head(S // LANES, VROWS, LANES),
            head(S, LANES), head(S // LANES, VROWS, LANES),
            pl.BlockSpec((None, GATE_ROWS, QB), lambda b, h, i: (b, 0, i)),
        ],
        out_specs=pl.BlockSpec((None, QB, G * DH), lambda b, h, i: (b, i, h)),
        scratch_shapes=[pltpu.VMEM((LANES, G * QB), BF16),
                        pltpu.VMEM((QB // LANES, NC + 8, LANES), F32),
                        pltpu.VMEM((NS, QB), F32), pltpu.VMEM((NS, QB), F32),
                        pltpu.VMEM((DH, G * QB), F32),
                        pltpu.VMEM((DH, G * QB), F32), pltpu.VMEM((DH, G * QB), F32),
                        pltpu.VMEM((VROWS, G * QB), F32)],
        compiler_params=pltpu.CompilerParams(
            dimension_semantics=("arbitrary", "arbitrary", "arbitrary"),
            vmem_limit_bytes=VMEM_LIMIT),
        name="attention",
    )(sc, q, kc, vcT, kslc, vslcT, kwin, vwinT, gT)


XH = 8
YH = 32


def _causal_taps(buf, w_ref, bias, hist, ntaps):
    rows = TS + 8
    base = hist - (ntaps - 1)
    out = jnp.broadcast_to(bias, (TS, buf.shape[1]))
    for r in range(8):
        z = None
        for k in range(ntaps):
            if (base + k) % 8 == r:
                term = w_ref[k:k + 1, :] * buf[base + k - r:base + k - r + rows, :]
                z = term if z is None else z + term
        if z is not None:
            out = out + (z[0:TS, :] if r == 0 else pltpu.roll(z, rows - r, 0)[0:TS, :])
    return out


def _seq_mix_kernel(lru_ref, cv_ref, cw_ref, cb_ref, wa_ref, ba_ref, wx_ref, bx_ref, lam_ref,
                    dw_ref, db_ref, lng_ref, lnb_ref, gl_ref, gc_ref,
                    yl_ref, yc_ref, xbuf, ybuf, hbuf):
    t = pl.program_id(1)

    @pl.when(t == 0)
    def _():
        xbuf[0:XH, :] = jnp.zeros((XH, LRU_W), F32)
        xbuf[XH + TS:XH + TS + 8, :] = jnp.zeros((8, LRU_W), F32)
        ybuf[0:YH, :] = jnp.zeros((YH, CV_W), F32)
        ybuf[YH + TS:YH + TS + 8, :] = jnp.zeros((8, CV_W), F32)
        hbuf[...] = jnp.zeros((8, LRU_W), F32)

    xb = lru_ref[:, 0:LRU_W]
    gb = lru_ref[:, LRU_W:2 * LRU_W]
    xbuf[XH:XH + TS, :] = xb
    xr = _causal_taps(xbuf, cw_ref, cb_ref[...], XH, LRU_CONV)
    xbuf[0:XH, :] = xbuf[TS:TS + XH, :]

    xr16 = xr.astype(BF16)
    r = _sigmoid(_dot(xr16, wa_ref[...]) + ba_ref[...])
    ig = _sigmoid(_dot(xr16, wx_ref[...]) + bx_ref[...])
    nl = -lam_ref[...]
    softplus = jnp.maximum(nl, 0.0) + jnp.log1p(jnp.exp(-jnp.abs(nl)))
    log_a = -LRU_C * r * softplus
    a = jnp.exp(log_a)
    u = xr * ig * jnp.sqrt(-jnp.tanh(log_a) * (a * a + 1.0))

    row = lax.broadcasted_iota(jnp.int32, (TS, LRU_W), 0)
    d = 1
    while d < TS:
        keep = row >= d
        a_sh = jnp.where(keep, pltpu.roll(a, d, 0), 1.0)
        u_sh = jnp.where(keep, pltpu.roll(u, d, 0), 0.0)
        u = a * u_sh + u
        a = a * a_sh
        d *= 2
    hseq = a * hbuf[0:1, :] + u
    hbuf[...] = jnp.broadcast_to(hseq[TS - 1:TS, :], (8, LRU_W))
    yl = hseq * _gelu_tanh(gb)
    yl = yl * lax.rsqrt(jnp.mean(yl * yl, axis=-1, keepdims=True) + EPS) * gl_ref[...]
    yl_ref[...] = yl.astype(BF16)

    y = cv_ref[:, 0:CV_W] * _sigmoid(cv_ref[:, CV_W:2 * CV_W])
    ybuf[YH:YH + TS, :] = y
    c = _causal_taps(ybuf, dw_ref, db_ref[...], YH, CV_KERNEL)
    ybuf[0:YH, :] = ybuf[TS:TS + YH, :]
    mu = jnp.mean(c, axis=-1, keepdims=True)
    cc = c - mu
    var = jnp.mean(cc * cc, axis=-1, keepdims=True)
    ln = cc * lax.rsqrt(var + EPS) * lng_ref[...] + lnb_ref[...]
    yc = ln * _sigmoid(ln)
    yc = yc * lax.rsqrt(jnp.mean(yc * yc, axis=-1, keepdims=True) + EPS) * gc_ref[...]
    yc_ref[...] = yc.astype(BF16)


def _seq_mix(lru, cv, params, B, S):
    const = lambda a: pl.BlockSpec(a.shape, lambda b, t: (0,) * a.ndim)
    tile = lambda w_: pl.BlockSpec((None, TS, w_), lambda b, t: (b, t, 0))
    return pl.pallas_call(
        _seq_mix_kernel,
        out_shape=(jax.ShapeDtypeStruct((B, S, LRU_W), BF16),
                   jax.ShapeDtypeStruct((B, S, CV_W), BF16)),
        grid=(B, S // TS),
        in_specs=[tile(2 * LRU_W), tile(2 * CV_W)] + [const(p) for p in params],
        out_specs=(tile(LRU_W), tile(CV_W)),
        scratch_shapes=[pltpu.VMEM((XH + TS + 8, LRU_W), F32), pltpu.VMEM((YH + TS + 8, CV_W), F32),
                        pltpu.VMEM((8, LRU_W), F32)],
        compiler_params=pltpu.CompilerParams(
            dimension_semantics=("arbitrary", "arbitrary"), vmem_limit_bytes=VMEM_LIMIT),
        name="seq_mix",
    )(lru, cv, *params)


def _out_mlp_kernel(x_ref, ya_ref, yl_ref, yc_ref, ga_ref, wo_ref, gm_ref, w1_ref, w2_ref, o_ref):
    ya = ya_ref[...]
    ya = (ya * lax.rsqrt(jnp.mean(ya * ya, axis=-1, keepdims=True) + EPS) * ga_ref[...]).astype(BF16)
    x1 = (x_ref[...] + _dot(ya, wo_ref[0:NSA_W, :])
          + _dot(yl_ref[...], wo_ref[NSA_W:NSA_W + LRU_W, :])
          + _dot(yc_ref[...], wo_ref[NSA_W + LRU_W:NSA_W + LRU_W + CV_W, :]))
    hm = (x1 * lax.rsqrt(jnp.mean(x1 * x1, axis=-1, keepdims=True) + EPS) * gm_ref[...]).astype(BF16)
    d_ff = w1_ref.shape[1]
    fc = 1024
    o_ref[...] = x1
    for c in range(d_ff // fc):
        hc = jnp.maximum(_dot(hm, w1_ref[:, c * fc:(c + 1) * fc]), 0.0)
        o_ref[...] += _dot((hc * hc).astype(BF16), w2_ref[c * fc:(c + 1) * fc, :])


def _out_mlp(x2, ya, yl, yc, ga, wo, gm, w1, w2):
    T, D = x2.shape
    tok = lambda w_: pl.BlockSpec((TM, w_), lambda t: (t, 0))
    const1 = lambda a: pl.BlockSpec(a.shape, lambda t: (0,) * a.ndim, pipeline_mode=pl.Buffered(1))
    return pl.pallas_call(
        _out_mlp_kernel,
        out_shape=jax.ShapeDtypeStruct((T, D), F32),
        grid=(T // TM,),
        in_specs=[tok(D), tok(NSA_W), tok(LRU_W), tok(CV_W),
                  const1(ga), const1(wo), const1(gm), const1(w1), const1(w2)],
        out_specs=tok(D),
        compiler_params=pltpu.CompilerParams(
            dimension_semantics=("arbitrary",), vmem_limit_bytes=VMEM_LIMIT),
        name="out_mlp",
    )(x2, ya, yl, yc, ga, wo, gm, w1, w2)


def _block_ones(n):
    idx = np.arange(n) // DH
    return jnp.asarray((idx[:, None] == idx[None, :]).astype(np.float32) / DH, dtype=BF16)


def _permute_w_in(w):
    off_kv = NSA_W
    kv = [w[:, off_kv + c * 128: off_kv + (c + 1) * 128] for c in range(6)]
    off_gate = off_kv + 6 * 128
    ngate = 3 * HKV * G
    off_lru = off_gate + ngate
    gate = jnp.pad(w[:, off_gate:off_lru], ((0, 0), (0, LANES - ngate)))
    cols = [w[:, 0:NSA_W], kv[0], kv[1], kv[2], kv[4], kv[3], kv[5],
            w[:, off_lru:off_lru + 2 * LRU_W + 2 * CV_W], gate]
    return jnp.concatenate(cols, axis=1).astype(BF16)


def _compress_weights(pos, w1, w2):
    slot_kv = jnp.array([0, 0, 1, 1])
    eye = jnp.eye(4, dtype=F32)
    w1r = w1.reshape(2, 2, CMP_STRIDE, DH, CMP_HID)[slot_kv]
    wbig = jnp.einsum('shpdo,st->psdhto', w1r, eye).reshape(CMP_STRIDE * 4 * DH, 2 * 4 * CMP_HID)
    posr = pos.reshape(2, 2, CMP_STRIDE, DH)[slot_kv]
    pos2 = jnp.transpose(posr, (1, 2, 0, 3)).reshape(2, CMP_STRIDE * 4 * DH)
    pos2 = jnp.pad(pos2, ((0, 6), (0, 0)))
    w2big = jnp.einsum('sod,st->sotd', w2[slot_kv], eye).reshape(4 * CMP_HID, 4 * DH)
    return pos2.astype(BF16), wbig.astype(BF16), w2big.astype(BF16)


def _block_diag(w):
    hh, bw, _ = w.shape
    eye = jnp.eye(hh, dtype=w.dtype)
    return jnp.einsum('hij,hg->higj', w, eye).reshape(hh * bw, hh * bw).astype(BF16)


def kernel(x, attn_norm, w_in, q_norm, k_norm, cmp_pos, cmp_w1, cmp_w2, lru_conv_w, lru_conv_b,
           lru_wa, lru_ba, lru_wx, lru_bx, lru_lambda, cv_dw_w, cv_dw_b, cv_ln_g, cv_ln_b,
           out_norm, w_out, mlp_norm, mlp_w1, mlp_w2):
    B, S, D = x.shape
    depth = w_in.shape[0]
    assert S % TM == 0 and S % TS == 0 and S >= WINDOW + QB and S // SLC_BLOCK >= N_SEL
    NC = S // CMP_STRIDE
    row = lambda v: v.reshape(1, -1).astype(F32)
    bd512 = _block_ones(NSA_W)
    bd128 = _block_ones(128)

    x2 = x.reshape(B * S, D)
    for l in range(depth):
        qgain = row(jnp.tile(q_norm[l], HKV * G)) * (DH ** -0.5 * LOG2E)
        kgain = row(jnp.concatenate([jnp.tile(k_norm[l, 1], HKV), jnp.tile(k_norm[l, 2], HKV)]))
        q, cmpr, kslc, kwin, vslcT, vwinT, lru, cv, gT = _in_proj(
            x2, row(attn_norm[l]), _permute_w_in(w_in[l]), bd512, qgain, kgain, B, S)

        pos2, wbig, w2big = _compress_weights(cmp_pos[l], cmp_w1[l], cmp_w2[l])
        kc, vcT = _compress(cmpr.reshape(B, NC, CMP_STRIDE * 256), pos2, wbig, w2big, bd128,
                            row(jnp.tile(k_norm[l, 0], HKV)), B, NC)

        sc = _score_bound(qgain, k_norm[l, 1], k_norm[l, 2], k_norm[l, 0])
        y_attn = _attention(sc, q.reshape(B, S, NSA_W), kc, vcT, kslc, vslcT, kwin, vwinT, gT, B, S)

        g_out = out_norm[l]
        seq_params = (lru_conv_w[l], row(lru_conv_b[l]), _block_diag(lru_wa[l]), row(lru_ba[l]),
                      _block_diag(lru_wx[l]), row(lru_bx[l]), row(lru_lambda[l]),
                      cv_dw_w[l], row(cv_dw_b[l]), row(cv_ln_g[l]), row(cv_ln_b[l]),
                      row(g_out[NSA_W:NSA_W + LRU_W]), row(g_out[NSA_W + LRU_W:]))
        yl, yc = _seq_mix(lru.reshape(B, S, 2 * LRU_W), cv.reshape(B, S, 2 * CV_W), seq_params, B, S)

        x2 = _out_mlp(x2, y_attn.reshape(B * S, NSA_W), yl.reshape(B * S, LRU_W),
                      yc.reshape(B * S, CV_W), row(g_out[:NSA_W]), w_out[l].astype(BF16),
                      row(mlp_norm[l]), mlp_w1[l].astype(BF16), mlp_w2[l].astype(BF16))
    return x2.reshape(B, S, D)
```

```python
import functools

import numpy as np
import jax
import jax.numpy as jnp
from jax import lax
from jax.experimental import pallas as pl
from jax.experimental.pallas import tpu as pltpu

F32 = jnp.float32
BF16 = jnp.bfloat16

EPS = 1e-6
NEG = -1e30
DH = 64
HKV = 2
G = 4
NSA_W = HKV * G * DH
CMP_BLOCK = 32
CMP_STRIDE = 16
CMP_HID = 128
SLC_BLOCK = 64
N_SEL = 16
WINDOW = 512
QB = 256
LRU_W = 256
LRU_HEADS = 8
LRU_CONV = 4
LRU_C = 8.0
CV_W = 256
CV_KERNEL = 31
LOG2E = 1.4426950408889634
SAFE_SHIFT = 50.0

LANES = 128
KCH = 256
NCHUNK = 8
CCH = 256
TM = 512
TS = 512
VMEM_LIMIT = 48 * 1024 * 1024

C_Q = 0
C_CMP = C_Q + NSA_W
C_KSLC = C_CMP + 256
C_KWIN = C_KSLC + 128
C_VSLC = C_KWIN + 128
C_VWIN = C_VSLC + 128
C_LRU = C_VWIN + 128
C_CV = C_LRU + 2 * LRU_W
C_GATE = C_CV + 2 * CV_W
N_INP = C_GATE + LANES
GATE_ROWS = 32
VROWS = DH + 16
BIAS_BLOCKS = LANES - DH


def _dot(a, b):
    return jnp.dot(a, b, preferred_element_type=F32)


def _head_rms_scale(z, bd):
    sq = z * z
    hi = sq.astype(BF16)
    lo = (sq - hi.astype(F32)).astype(BF16)
    ms = _dot(hi, bd) + _dot(lo, bd)
    return lax.rsqrt(ms + EPS)


def _gelu_tanh(x):
    return 0.5 * x * (1.0 + jnp.tanh(0.7978845608028654 * (x + 0.044715 * (x * x * x))))


def _sigmoid(x):
    return 1.0 / (1.0 + jnp.exp(-x))


def _in_proj_kernel(x_ref, g_ref, w_ref, bd_ref, qgain_ref, kgain_ref,
                    q_ref, cmp_ref, kslc_ref, kwin_ref, vslcT_ref, vwinT_ref,
                    lru_ref, cv_ref, gT_ref, cmpbuf, *, nt):
    x = x_ref[...]
    ms = jnp.mean(x * x, axis=-1, keepdims=True)
    hn = (x * lax.rsqrt(ms + EPS) * g_ref[...]).astype(BF16)

    zq = _dot(hn, w_ref[:, C_Q:C_CMP])
    q_ref[...] = (zq * _head_rms_scale(zq, bd_ref[...]) * qgain_ref[...]).astype(BF16)

    zc = _dot(hn, w_ref[:, C_CMP:C_KSLC])
    for w in range(2):
        cmpbuf[w] = zc[:, w * LANES:(w + 1) * LANES]
    for p in range(CMP_STRIDE):
        for w in range(2):
            col = p * 256 + w * LANES
            cmp_ref[:, col:col + LANES] = cmpbuf[w, pl.ds(p, TM // CMP_STRIDE, stride=CMP_STRIDE),
                                                 :].astype(BF16)

    zk = _dot(hn, w_ref[:, C_KSLC:C_VSLC])
    kn = zk * _head_rms_scale(zk, bd_ref[0:256, 0:256]) * kgain_ref[...]
    lane = lax.broadcasted_iota(jnp.int32, (TM, LANES), 1)
    rowg = (pl.program_id(0) % nt) * TM + lax.broadcasted_iota(jnp.int32, (TM, LANES), 0)
    blk = lax.shift_right_logical(rowg, 6) & (BIAS_BLOCKS - 1)
    onehot = jnp.where(lane == DH + blk, 1.0, 0.0)
    lo = lane < DH
    ks, kw = kn[:, 0:128], kn[:, 128:256]
    kslc_ref[0] = jnp.where(lo, ks, onehot).astype(BF16)
    kslc_ref[1] = jnp.where(lo, pltpu.roll(ks, DH, 1), onehot).astype(BF16)
    kwin_ref[0] = jnp.where(lo, kw, 0.0).astype(BF16)
    kwin_ref[1] = jnp.where(lo, pltpu.roll(kw, DH, 1), 0.0).astype(BF16)

    zv = _dot(hn, w_ref[:, C_VSLC:C_LRU])
    zvT = zv.T
    ones_row = jnp.where(lax.broadcasted_iota(jnp.int32, (VROWS - DH, LANES), 0) == 0,
                         1.0, 0.0).astype(BF16)
    for j in range(TM // LANES):
        for h in range(HKV):
            cols = slice(j * LANES, (j + 1) * LANES)
            vslcT_ref[h, j, 0:DH, :] = zvT[h * DH:(h + 1) * DH, cols].astype(BF16)
            vslcT_ref[h, j, DH:VROWS, :] = ones_row
            vwinT_ref[h, j, 0:DH, :] = zvT[128 + h * DH:128 + (h + 1) * DH, cols].astype(BF16)
            vwinT_ref[h, j, DH:VROWS, :] = ones_row

    lru_ref[...] = _dot(hn, w_ref[:, C_LRU:C_CV])
    cv_ref[...] = _dot(hn, w_ref[:, C_CV:C_GATE])

    zg = _sigmoid(_dot(hn, w_ref[:, C_GATE:N_INP]))
    gT_ref[...] = zg.T[0:GATE_ROWS, :]


def _in_proj(x2, g, w, bd, qgain, kgain, B, S):
    T, D = x2.shape
    nt = S // TM
    tok = lambda w_: pl.BlockSpec((TM, w_), lambda t: (t, 0))
    const = lambda a: pl.BlockSpec(a.shape, lambda t: (0,) * a.ndim)
    vT_spec = pl.BlockSpec((None, HKV, TM // LANES, VROWS, LANES),
                           lambda t: (t // nt, 0, t % nt, 0, 0))
    k_spec = pl.BlockSpec((None, HKV, TM, LANES), lambda t: (t // nt, 0, t % nt, 0))
    out_shape = (
        jax.ShapeDtypeStruct((T, NSA_W), BF16),
        jax.ShapeDtypeStruct((T // CMP_STRIDE, CMP_STRIDE * 256), BF16),
        jax.ShapeDtypeStruct((B, HKV, S, LANES), BF16),
        jax.ShapeDtypeStruct((B, HKV, S, LANES), BF16),
        jax.ShapeDtypeStruct((B, HKV, S // LANES, VROWS, LANES), BF16),
        jax.ShapeDtypeStruct((B, HKV, S // LANES, VROWS, LANES), BF16),
        jax.ShapeDtypeStruct((T, 2 * LRU_W), F32),
        jax.ShapeDtypeStruct((T, 2 * CV_W), F32),
        jax.ShapeDtypeStruct((B, GATE_ROWS, S), F32),
    )
    out_specs = (
        tok(NSA_W), pl.BlockSpec((TM // CMP_STRIDE, CMP_STRIDE * 256), lambda t: (t, 0)),
        k_spec, k_spec, vT_spec, vT_spec,
        tok(2 * LRU_W), tok(2 * CV_W),
        pl.BlockSpec((None, GATE_ROWS, TM), lambda t: (t // nt, 0, t % nt)),
    )
    return pl.pallas_call(
        functools.partial(_in_proj_kernel, nt=nt),
        out_shape=out_shape,
        grid=(T // TM,),
        in_specs=[tok(D), const(g), const(w), const(bd), const(qgain), const(kgain)],
        out_specs=out_specs,
        scratch_shapes=[pltpu.VMEM((2, TM, LANES), F32)],
        compiler_params=pltpu.CompilerParams(
            dimension_semantics=("arbitrary",), vmem_limit_bytes=VMEM_LIMIT),
        name="in_proj",
    )(x2, g, w, bd, qgain, kgain)


def _compress_kernel(x_ref, pos_ref, wbig_ref, w2_ref, bd_ref, kgain_ref, kc_ref, vcT_ref):
    nc = x_ref.shape[0]
    half = 4 * CMP_HID
    p = _dot(x_ref[...], wbig_ref[...])
    pc = _dot(pos_ref[...], wbig_ref[...])
    const = pc[0:1, 0:half] + pc[1:2, half:2 * half]
    nxt = pltpu.roll(p[:, half:2 * half], nc - 1, 0)
    hid = _gelu_tanh(p[:, 0:half] + nxt + const).astype(BF16)
    kv = _dot(hid, w2_ref[...])
    kc = kv[:, 0:128]
    kc = kc * _head_rms_scale(kc, bd_ref[...]) * kgain_ref[...]
    lo = lax.broadcasted_iota(jnp.int32, (nc, LANES), 1) < DH
    kc_ref[0] = jnp.where(lo, kc, 0.0).astype(BF16)
    kc_ref[1] = jnp.where(lo, pltpu.roll(kc, DH, 1), 0.0).astype(BF16)
    vT = kv[:, 128:256].T
    for h in range(HKV):
        vcT_ref[h] = vT[h * DH:(h + 1) * DH, :].astype(BF16)


def _compress(cmpx, pos2, wbig, w2big, bd128, kgain, B, NC):
    const = lambda a: pl.BlockSpec(a.shape, lambda b: (0,) * a.ndim, pipeline_mode=pl.Buffered(1))
    return pl.pallas_call(
        _compress_kernel,
        out_shape=(jax.ShapeDtypeStruct((B, HKV, NC, LANES), BF16),
                   jax.ShapeDtypeStruct((B, HKV, DH, NC), BF16)),
        grid=(B,),
        in_specs=[pl.BlockSpec((None, NC, cmpx.shape[2]), lambda b: (b, 0, 0)),
                  const(pos2), const(wbig), const(w2big), const(bd128), const(kgain)],
        out_specs=(pl.BlockSpec((None, HKV, NC, LANES), lambda b: (b, 0, 0, 0)),
                   pl.BlockSpec((None, HKV, DH, NC), lambda b: (b, 0, 0, 0))),
        compiler_params=pltpu.CompilerParams(
            dimension_semantics=("arbitrary",), vmem_limit_bytes=VMEM_LIMIT),
        name="compress",
    )(cmpx, pos2, wbig, w2big, bd128, kgain)


def _attn_kernel(sc_ref, q_ref, kc_ref, vcT_ref, ks_ref, vsT_ref, kw_ref, vwT_ref, gT_ref,
                 y_ref, qT_ref, imp_ref, selb_ref, selo_ref, oc_ref, os_ref, ow_ref, acc_ref, *, cch):
    h = pl.program_id(1)
    i = pl.program_id(2)
    nc = kc_ref.shape[0]
    ns = selb_ref.shape[0]
    wspan = WINDOW + QB
    t0 = i * QB
    iota = lambda shape, ax: lax.broadcasted_iota(jnp.int32, shape, ax)
    use_bound = sc_ref[1] > 0.5
    shift = jnp.where(use_bound, -sc_ref[0], 0.0)

    qfT = q_ref[...].astype(F32).T
    for g in range(G):
        qT_ref[0:DH, g * QB:(g + 1) * QB] = qfT[g * DH:(g + 1) * DH, :].astype(BF16)
    qT_ref[DH:LANES, :] = jnp.zeros((LANES - DH, G * QB), BF16)

    def window(bounded):
        wc = jnp.maximum(i * (QB // LANES) - WINDOW // LANES, 0)
        ws = wc * LANES
        sw = _dot(kw_ref[pl.ds(pl.multiple_of(ws, LANES), wspan), :], qT_ref[...])
        kpos = ws + iota((wspan, QB), 0)
        t_w = t0 + iota((wspan, QB), 1)
        inside = -sc_ref[3] if bounded else 0.0
        wb = jnp.where(kpos <= t_w, jnp.where(kpos > t_w - WINDOW, inside, NEG), NEG)
        sw = sw + jnp.concatenate([wb] * G, axis=1)
        p_w = jnp.exp2(sw if bounded else sw - jnp.max(sw, axis=0, keepdims=True)).astype(BF16)
        vw = jnp.concatenate([vwT_ref[wc + u] for u in range(wspan // LANES)], axis=1)
        acc_w = _dot(vw, p_w)
        ow_ref[...] = acc_w[0:DH, :] * (1.0 / acc_w[DH:DH + 1, :])

    def front(nr, bounded):
        window(bounded)
        nsu = nr // (SLC_BLOCK // CMP_STRIDE)
        cmask = iota((nr, QB), 0) * CMP_STRIDE + (CMP_BLOCK - 1) <= t0 + iota((nr, QB), 1)
        cb = jnp.where(cmask, -sc_ref[2] if bounded else 0.0, NEG)
        s = _dot(kc_ref[0:nr, :], qT_ref[...]) + jnp.concatenate([cb] * G, axis=1)
        p = jnp.exp2(s if bounded else s - jnp.max(s, axis=0, keepdims=True))
        l = jnp.sum(p, axis=0, keepdims=True)
        anyv = jnp.where(t0 + iota((1, QB), 1) >= CMP_BLOCK - 1, 1.0, 0.0)
        pn = p * jnp.where(jnp.concatenate([anyv] * G, axis=1) > 0.5, 1.0 / l, 0.0)
        oc_ref[...] = _dot(vcT_ref[:, 0:nr], pn.astype(BF16))
        imp = pn[:, 0:QB]
        for g in range(1, G):
            imp = imp + pn[:, g * QB:(g + 1) * QB]

        parts = []
        for w in range(QB // LANES):
            imp_ref[w, 0:8, :] = jnp.zeros((8, LANES), F32)
            imp_ref[w, 8:8 + nr, :] = imp[:, w * LANES:(w + 1) * LANES]
            acc = imp_ref[w, pl.ds(7, nsu, stride=4), :]
            for r in range(4):
                acc = acc + imp_ref[w, pl.ds(8 + r, nsu, stride=4), :]
            parts.append(acc)
        islc = jnp.concatenate(parts, axis=1)

        j_i = iota((nsu, QB), 0)
        t_s = t0 + iota((nsu, QB), 1)
        cur = lax.shift_right_logical(t_s, 6)
        valid = j_i * SLC_BLOCK <= t_s
        ninf = -jnp.inf
        score = jnp.where(j_i == 0, ninf, jnp.where(j_i == cur, ninf,
                                                    jnp.where(j_i == cur - 1, ninf, islc)))
        score = jnp.where(valid, score, NEG)
        j_f = j_i.astype(F32)
        for _ in range(N_SEL - 3):
            mx = jnp.max(score, axis=0, keepdims=True)
            jm = jnp.min(jnp.where(score == mx, j_f, float(nsu)), axis=0, keepdims=True)
            score = jnp.where(j_f == jm, -jnp.inf, score)
        own = lax.shift_right_logical(j_i, (QB // SLC_BLOCK).bit_length() - 1) == i
        picked = jnp.where(valid, jnp.where(score == -jnp.inf, shift, NEG), NEG)
        selo_ref[0:nsu, :] = picked
        selb_ref[0:nsu, :] = jnp.where(own, NEG, picked)
        pad = -nsu % min(BIAS_BLOCKS, ns)
        if pad:
            selb_ref[nsu:nsu + pad, :] = jnp.full((pad, QB), NEG, F32)

    cls = (i * (QB // CMP_STRIDE) + (QB - CMP_BLOCK) // CMP_STRIDE) // cch
    for kk in range(nc // cch):
        for bounded in (True, False):
            pl.when(jnp.logical_and(cls == kk, use_bound == bounded))(
                functools.partial(front, (kk + 1) * cch, bounded))

    bpt = QB // SLC_BLOCK
    own_b = jnp.concatenate(
        [jnp.broadcast_to(selo_ref[pl.ds(i * bpt + b, 1), :], (SLC_BLOCK, QB)) for b in range(bpt)],
        axis=0)
    own_b = jnp.where(iota((QB, QB), 0) <= iota((QB, QB), 1), own_b, NEG)
    own_v = jnp.concatenate([vsT_ref[i * (QB // LANES) + w] for w in range(QB // LANES)], axis=1)
    sd = (_dot(ks_ref[pl.ds(pl.multiple_of(t0, QB), QB), :], qT_ref[...])
          + jnp.concatenate([own_b] * G, axis=1))

    nv = KCH // LANES
    cpg = BIAS_BLOCKS // (KCH // SLC_BLOCK)
    brows = min(BIAS_BLOCKS, ns)
    nstep = (i + NCHUNK - 1) // NCHUNK

    def chunk_scores(c0, n=NCHUNK):
        @pl.when(c0 % cpg == 0)
        def _():
            r0 = pl.multiple_of((c0 // cpg) * brows, brows)
            rows = selb_ref[pl.ds(r0, brows), :].astype(BF16)
            qT_ref[DH:DH + brows, :] = jnp.concatenate([rows] * G, axis=1)

        qT = qT_ref[...]
        return [_dot(ks_ref[pl.ds(pl.multiple_of((c0 + u) * KCH, KCH), KCH), :], qT)
                for u in range(n)]

    def values(c):
        return jnp.concatenate([vsT_ref[c * nv + w] for w in range(nv)], axis=1)

    def finish(acc):
        os_ref[...] = acc[0:DH, :] * (1.0 / acc[DH:DH + 1, :])

    @pl.when(use_bound)
    def _():
        def add_chunks(c0, n, acc):
            r0 = pl.multiple_of((c0 // cpg) * brows, brows)
            rows = selb_ref[pl.ds(r0, brows), :].astype(BF16)
            qT_ref[DH:DH + brows, :] = jnp.concatenate([rows] * G, axis=1)
            qT = qT_ref[...]
            for sub in range(0, n, NCHUNK):
                cs = [c0 + sub + u for u in range(min(NCHUNK, n - sub))]
                ss = [_dot(ks_ref[pl.ds(pl.multiple_of(c * KCH, KCH), KCH), :], qT) for c in cs]
                for c, s in zip(cs, ss):
                    acc = acc + _dot(values(c), jnp.exp2(s).astype(BF16))
            return acc

        nfull = i // cpg
        rem = i - nfull * cpg
        acc0 = _dot(own_v, jnp.exp2(sd).astype(BF16))
        acc_ref[...] = lax.fori_loop(
            0, nfull, lambda grp, acc: add_chunks(grp * cpg, cpg, acc), acc0)

        half = jnp.where(rem >= NCHUNK, NCHUNK, 0)
        c1 = nfull * cpg + half
        rest = rem - half

        @pl.when(half > 0)
        def _():
            acc_ref[...] = add_chunks(nfull * cpg, NCHUNK, acc_ref[...])

        @pl.when(jnp.logical_and(rest > 0, rest <= NCHUNK // 2))
        def _():
            acc_ref[...] = add_chunks(c1, NCHUNK // 2, acc_ref[...])

        @pl.when(rest > NCHUNK // 2)
        def _():
            acc_ref[...] = add_chunks(c1, NCHUNK, acc_ref[...])

        finish(acc_ref[...])

    @pl.when(jnp.logical_not(use_bound))
    def _():
        def step(grp, carry):
            m, acc = carry
            ss = chunk_scores(grp * NCHUNK)
            for u in range(NCHUNK):
                s = ss[u]
                m_new = jnp.maximum(m, jnp.max(s, axis=0, keepdims=True))
                p = jnp.exp2(s - m_new).astype(BF16)
                acc = jnp.exp2(m - m_new) * acc + _dot(values(grp * NCHUNK + u), p)
                m = m_new
            return m, acc

        m0 = jnp.max(sd, axis=0, keepdims=True)
        acc0 = _dot(own_v, jnp.exp2(sd - m0).astype(BF16))
        finish(lax.fori_loop(0, nstep, step, (m0, acc0))[1])

    os_ = os_ref[...]
    ow = ow_ref[...]

    outs = []
    for g in range(G):
        sl = slice(g * QB, (g + 1) * QB)
        row = h * (3 * G) + 3 * g
        gc = gT_ref[pl.ds(row, 1), :]
        gs = gT_ref[pl.ds(row + 1, 1), :]
        gw = gT_ref[pl.ds(row + 2, 1), :]
        outs.append(gc * oc_ref[:, sl] + gs * os_[:, sl] + gw * ow[:, sl])
    y_ref[...] = jnp.concatenate(outs, axis=0).T


def _score_bound(qgain, kgain_slc, kgain_win, kgain_cmp):
    qmax = jnp.max(jnp.abs(qgain))
    ms = [(1.05 * DH * qmax * jnp.max(jnp.abs(kg))).astype(BF16).astype(F32)
          for kg in (kgain_slc, kgain_cmp, kgain_win)]
    ok = (jnp.maximum(jnp.maximum(ms[0], ms[1]), ms[2]) <= SAFE_SHIFT).astype(F32)
    return jnp.stack([ms[0], ok, ms[1], ms[2]])


def _attention(sc, q, kc, vcT, kslc, vslcT, kwin, vwinT, gT, B, S):
    NC = S // CMP_STRIDE
    NS = S // SLC_BLOCK
    NQ = S // QB
    cch = min(CCH, NC)
    assert NC % cch == 0 and (S // KCH) % NCHUNK == 0 and QB == KCH
    assert BIAS_BLOCKS * SLC_BLOCK // KCH == 2 * NCHUNK
    head = lambda *blk: pl.BlockSpec((None, None) + blk, lambda b, h, i: (b, h) + (0,) * len(blk),
                                     pipeline_mode=pl.Buffered(1))
    return pl.pallas_call(
        functools.partial(_attn_kernel, cch=cch),
        out_shape=jax.ShapeDtypeStruct((B, S, NSA_W), F32),
        grid=(B, HKV, NQ),
        in_specs=[
            pl.BlockSpec(memory_space=pltpu.SMEM),
            pl.BlockSpec((None, QB, G * DH), lambda b, h, i: (b, i, h)),
            head(NC, LANES), head(DH, NC),
            head(S, LANES), head(S // LANES, VROWS, LANES),
            head(S, LANES), head(S // LANES, VROWS, LANES),
            pl.BlockSpec((None, GATE_ROWS, QB), lambda b, h, i: (b, 0, i)),
        ],
        out_specs=pl.BlockSpec((None, QB, G * DH), lambda b, h, i: (b, i, h)),
        scratch_shapes=[pltpu.VMEM((LANES, G * QB), BF16),
                        pltpu.VMEM((QB // LANES, NC + 8, LANES), F32),
                        pltpu.VMEM((NS, QB), F32), pltpu.VMEM((NS, QB), F32),
                        pltpu.VMEM((DH, G * QB), F32),
                        pltpu.VMEM((DH, G * QB), F32), pltpu.VMEM((DH, G * QB), F32),
                        pltpu.VMEM((VROWS, G * QB), F32)],
        compiler_params=pltpu.CompilerParams(
            dimension_semantics=("arbitrary", "arbitrary", "arbitrary"),
            vmem_limit_bytes=VMEM_LIMIT),
        name="attention",
    )(sc, q, kc, vcT, kslc, vslcT, kwin, vwinT, gT)


XH = 8
YH = 32


def _causal_taps(buf, w_ref, bias, hist, ntaps):
    rows = TS + 8
    base = hist - (ntaps - 1)
    out = jnp.broadcast_to(bias, (TS, buf.shape[1]))
    for r in range(8):
        z = None
        for k in range(ntaps):
            if (base + k) % 8 == r:
                term = w_ref[k:k + 1, :] * buf[base + k - r:base + k - r + rows, :]
                z = term if z is None else z + term
        if z is not None:
            out = out + (z[0:TS, :] if r == 0 else pltpu.roll(z, rows - r, 0)[0:TS, :])
    return out


def _seq_mix_kernel(lru_ref, cv_ref, cw_ref, cb_ref, wa_ref, ba_ref, wx_ref, bx_ref, lam_ref,
                    dw_ref, db_ref, lng_ref, lnb_ref, gl_ref, gc_ref,
                    yl_ref, yc_ref, xbuf, ybuf, hbuf):
    t = pl.program_id(1)

    @pl.when(t == 0)
    def _():
        xbuf[0:XH, :] = jnp.zeros((XH, LRU_W), F32)
        xbuf[XH + TS:XH + TS + 8, :] = jnp.zeros((8, LRU_W), F32)
        ybuf[0:YH, :] = jnp.zeros((YH, CV_W), F32)
        ybuf[YH + TS:YH + TS + 8, :] = jnp.zeros((8, CV_W), F32)
        hbuf[...] = jnp.zeros((8, LRU_W), F32)

    xb = lru_ref[:, 0:LRU_W]
    gb = lru_ref[:, LRU_W:2 * LRU_W]
    xbuf[XH:XH + TS, :] = xb
    xr = _causal_taps(xbuf, cw_ref, cb_ref[...], XH, LRU_CONV)
    xbuf[0:XH, :] = xbuf[TS:TS + XH, :]

    xr16 = xr.astype(BF16)
    r = _sigmoid(_dot(xr16, wa_ref[...]) + ba_ref[...])
    ig = _sigmoid(_dot(xr16, wx_ref[...]) + bx_ref[...])
    nl = -lam_ref[...]
    softplus = jnp.maximum(nl, 0.0) + jnp.log1p(jnp.exp(-jnp.abs(nl)))
    log_a = -LRU_C * r * softplus
    a = jnp.exp(log_a)
    u = xr * ig * jnp.sqrt(-jnp.tanh(log_a) * (a * a + 1.0))

    row = lax.broadcasted_iota(jnp.int32, (TS, LRU_W), 0)
    d = 1
    while d < TS:
        keep = row >= d
        a_sh = jnp.where(keep, pltpu.roll(a, d, 0), 1.0)
        u_sh = jnp.where(keep, pltpu.roll(u, d, 0), 0.0)
        u = a * u_sh + u
        a = a * a_sh
        d *= 2
    hseq = a * hbuf[0:1, :] + u
    hbuf[...] = jnp.broadcast_to(hseq[TS - 1:TS, :], (8, LRU_W))
    yl = hseq * _gelu_tanh(gb)
    yl = yl * lax.rsqrt(jnp.mean(yl * yl, axis=-1, keepdims=True) + EPS) * gl_ref[...]
    yl_ref[...] = yl.astype(BF16)

    y = cv_ref[:, 0:CV_W] * _sigmoid(cv_ref[:, CV_W:2 * CV_W])
    ybuf[YH:YH + TS, :] = y
    c = _causal_taps(ybuf, dw_ref, db_ref[...], YH, CV_KERNEL)
    ybuf[0:YH, :] = ybuf[TS:TS + YH, :]
    mu = jnp.mean(c, axis=-1, keepdims=True)
    cc = c - mu
    var = jnp.mean(cc * cc, axis=-1, keepdims=True)
    ln = cc * lax.rsqrt(var + EPS) * lng_ref[...] + lnb_ref[...]
    yc = ln * _sigmoid(ln)
    yc = yc * lax.rsqrt(jnp.mean(yc * yc, axis=-1, keepdims=True) + EPS) * gc_ref[...]
    yc_ref[...] = yc.astype(BF16)


def _seq_mix(lru, cv, params, B, S):
    const = lambda a: pl.BlockSpec(a.shape, lambda b, t: (0,) * a.ndim)
    tile = lambda w_: pl.BlockSpec((None, TS, w_), lambda b, t: (b, t, 0))
    return pl.pallas_call(
        _seq_mix_kernel,
        out_shape=(jax.ShapeDtypeStruct((B, S, LRU_W), BF16),
                   jax.ShapeDtypeStruct((B, S, CV_W), BF16)),
        grid=(B, S // TS),
        in_specs=[tile(2 * LRU_W), tile(2 * CV_W)] + [const(p) for p in params],
        out_specs=(tile(LRU_W), tile(CV_W)),
        scratch_shapes=[pltpu.VMEM((XH + TS + 8, LRU_W), F32), pltpu.VMEM((YH + TS + 8, CV_W), F32),
                        pltpu.VMEM((8, LRU_W), F32)],
        compiler_params=pltpu.CompilerParams(
            dimension_semantics=("arbitrary", "arbitrary"), vmem_limit_bytes=VMEM_LIMIT),
        name="seq_mix",
    )(lru, cv, *params)


def _out_mlp_kernel(x_ref, ya_ref, yl_ref, yc_ref, ga_ref, wo_ref, gm_ref, w1_ref, w2_ref, o_ref):
    ya = ya_ref[...]
    ya = (ya * lax.rsqrt(jnp.mean(ya * ya, axis=-1, keepdims=True) + EPS) * ga_ref[...]).astype(BF16)
    x1 = (x_ref[...] + _dot(ya, wo_ref[0:NSA_W, :])
          + _dot(yl_ref[...], wo_ref[NSA_W:NSA_W + LRU_W, :])
          + _dot(yc_ref[...], wo_ref[NSA_W + LRU_W:NSA_W + LRU_W + CV_W, :]))
    hm = (x1 * lax.rsqrt(jnp.mean(x1 * x1, axis=-1, keepdims=True) + EPS) * gm_ref[...]).astype(BF16)
    d_ff = w1_ref.shape[1]
    fc = 1024
    o_ref[...] = x1
    for c in range(d_ff // fc):
        hc = jnp.maximum(_dot(hm, w1_ref[:, c * fc:(c + 1) * fc]), 0.0)
        o_ref[...] += _dot((hc * hc).astype(BF16), w2_ref[c * fc:(c + 1) * fc, :])


def _out_mlp(x2, ya, yl, yc, ga, wo, gm, w1, w2):
    T, D = x2.shape
    tok = lambda w_: pl.BlockSpec((TM, w_), lambda t: (t, 0))
    const1 = lambda a: pl.BlockSpec(a.shape, lambda t: (0,) * a.ndim, pipeline_mode=pl.Buffered(1))
    return pl.pallas_call(
        _out_mlp_kernel,
        out_shape=jax.ShapeDtypeStruct((T, D), F32),
        grid=(T // TM,),
        in_specs=[tok(D), tok(NSA_W), tok(LRU_W), tok(CV_W),
                  const1(ga), const1(wo), const1(gm), const1(w1), const1(w2)],
        out_specs=tok(D),
        compiler_params=pltpu.CompilerParams(
            dimension_semantics=("arbitrary",), vmem_limit_bytes=VMEM_LIMIT),
        name="out_mlp",
    )(x2, ya, yl, yc, ga, wo, gm, w1, w2)


def _block_ones(n):
    idx = np.arange(n) // DH
    return jnp.asarray((idx[:, None] == idx[None, :]).astype(np.float32) / DH, dtype=BF16)


def _permute_w_in(w):
    off_kv = NSA_W
    kv = [w[:, off_kv + c * 128: off_kv + (c + 1) * 128] for c in range(6)]
    off_gate = off_kv + 6 * 128
    ngate = 3 * HKV * G
    off_lru = off_gate + ngate
    gate = jnp.pad(w[:, off_gate:off_lru], ((0, 0), (0, LANES - ngate)))
    cols = [w[:, 0:NSA_W], kv[0], kv[1], kv[2], kv[4], kv[3], kv[5],
            w[:, off_lru:off_lru + 2 * LRU_W + 2 * CV_W], gate]
    return jnp.concatenate(cols, axis=1).astype(BF16)


def _compress_weights(pos, w1, w2):
    slot_kv = jnp.array([0, 0, 1, 1])
    eye = jnp.eye(4, dtype=F32)
    w1r = w1.reshape(2, 2, CMP_STRIDE, DH, CMP_HID)[slot_kv]
    wbig = jnp.einsum('shpdo,st->psdhto', w1r, eye).reshape(CMP_STRIDE * 4 * DH, 2 * 4 * CMP_HID)
    posr = pos.reshape(2, 2, CMP_STRIDE, DH)[slot_kv]
    pos2 = jnp.transpose(posr, (1, 2, 0, 3)).reshape(2, CMP_STRIDE * 4 * DH)
    pos2 = jnp.pad(pos2, ((0, 6), (0, 0)))
    w2big = jnp.einsum('sod,st->sotd', w2[slot_kv], eye).reshape(4 * CMP_HID, 4 * DH)
    return pos2.astype(BF16), wbig.astype(BF16), w2big.astype(BF16)


def _block_diag(w):
    hh, bw, _ = w.shape
    eye = jnp.eye(hh, dtype=w.dtype)
    return jnp.einsum('hij,hg->higj', w, eye).reshape(hh * bw, hh * bw).astype(BF16)


def kernel(x, attn_norm, w_in, q_norm, k_norm, cmp_pos, cmp_w1, cmp_w2, lru_conv_w, lru_conv_b,
           lru_wa, lru_ba, lru_wx, lru_bx, lru_lambda, cv_dw_w, cv_dw_b, cv_ln_g, cv_ln_b,
           out_norm, w_out, mlp_norm, mlp_w1, mlp_w2):
    B, S, D = x.shape
    depth = w_in.shape[0]
    assert S % TM == 0 and S % TS == 0 and S >= WINDOW + QB and S // SLC_BLOCK >= N_SEL
    NC = S // CMP_STRIDE
    row = lambda v: v.reshape(1, -1).astype(F32)
    bd512 = _block_ones(NSA_W)
    bd128 = _block_ones(128)

    x2 = x.reshape(B * S, D)
    for l in range(depth):
        qgain = row(jnp.tile(q_norm[l], HKV * G)) * (DH ** -0.5 * LOG2E)
        kgain = row(jnp.concatenate([jnp.tile(k_norm[l, 1], HKV), jnp.tile(k_norm[l, 2], HKV)]))
        q, cmpr, kslc, kwin, vslcT, vwinT, lru, cv, gT = _in_proj(
            x2, row(attn_norm[l]), _permute_w_in(w_in[l]), bd512, qgain, kgain, B, S)

        pos2, wbig, w2big = _compress_weights(cmp_pos[l], cmp_w1[l], cmp_w2[l])
        kc, vcT = _compress(cmpr.reshape(B, NC, CMP_STRIDE * 256), pos2, wbig, w2big, bd128,
                            row(jnp.tile(k_norm[l, 0], HKV)), B, NC)

        sc = _score_bound(qgain, k_norm[l, 1], k_norm[l, 2], k_norm[l, 0])
        y_attn = _attention(sc, q.reshape(B, S, NSA_W), kc, vcT, kslc, vslcT, kwin, vwinT, gT, B, S)

        g_out = out_norm[l]
        seq_params = (lru_conv_w[l], row(lru_conv_b[l]), _block_diag(lru_wa[l]), row(lru_ba[l]),
                      _block_diag(lru_wx[l]), row(lru_bx[l]), row(lru_lambda[l]),
                      cv_dw_w[l], row(cv_dw_b[l]), row(cv_ln_g[l]), row(cv_ln_b[l]),
                      row(g_out[NSA_W:NSA_W + LRU_W]), row(g_out[NSA_W + LRU_W:]))
        yl, yc = _seq_mix(lru.reshape(B, S, 2 * LRU_W), cv.reshape(B, S, 2 * CV_W), seq_params, B, S)

        x2 = _out_mlp(x2, y_attn.reshape(B * S, NSA_W), yl.reshape(B * S, LRU_W),
                      yc.reshape(B * S, CV_W), row(g_out[:NSA_W]), w_out[l].astype(BF16),
                      row(mlp_norm[l]), mlp_w1[l].astype(BF16), mlp_w2[l].astype(BF16))
    return x2.reshape(B, S, D)
```

```python
import functools

import numpy as np
import jax
import jax.numpy as jnp
from jax import lax
from jax.experimental import pallas as pl
from jax.experimental.pallas import tpu as pltpu

F32 = jnp.float32
BF16 = jnp.bfloat16

EPS = 1e-6
NEG = -1e30
DH = 64
HKV = 2
G = 4
NSA_W = HKV * G * DH
CMP_BLOCK = 32
CMP_STRIDE = 16
CMP_HID = 128
SLC_BLOCK = 64
N_SEL = 16
WINDOW = 512
QB = 256
LRU_W = 256
LRU_HEADS = 8
LRU_CONV = 4
LRU_C = 8.0
CV_W = 256
CV_KERNEL = 31
LOG2E = 1.4426950408889634
SAFE_SHIFT = 50.0

LANES = 128
KCH = 256
NCHUNK = 8
CCH = 256
TM = 512
TS = 512
VMEM_LIMIT = 48 * 1024 * 1024

C_Q = 0
C_CMP = C_Q + NSA_W
C_KSLC = C_CMP + 256
C_KWIN = C_KSLC + 128
C_VSLC = C_KWIN + 128
C_VWIN = C_VSLC + 128
C_LRU = C_VWIN + 128
C_CV = C_LRU + 2 * LRU_W
C_GATE = C_CV + 2 * CV_W
N_INP = C_GATE + LANES
GATE_ROWS = 32
VROWS = DH + 16
BIAS_BLOCKS = LANES - DH


def _dot(a, b):
    return jnp.dot(a, b, preferred_element_type=F32)


def _head_rms_scale(z, bd):
    sq = z * z
    hi = sq.astype(BF16)
    lo = (sq - hi.astype(F32)).astype(BF16)
    ms = _dot(hi, bd) + _dot(lo, bd)
    return lax.rsqrt(ms + EPS)


def _gelu_tanh(x):
    return 0.5 * x * (1.0 + jnp.tanh(0.7978845608028654 * (x + 0.044715 * (x * x * x))))


def _sigmoid(x):
    return 1.0 / (1.0 + jnp.exp(-x))


def _in_proj_kernel(x_ref, g_ref, w_ref, bd_ref, qgain_ref, kgain_ref,
                    q_ref, cmp_ref, kslc_ref, kwin_ref, vslcT_ref, vwinT_ref,
                    lru_ref, cv_ref, gT_ref, cmpbuf, *, nt):
    x = x_ref[...]
    ms = jnp.mean(x * x, axis=-1, keepdims=True)
    hn = (x * lax.rsqrt(ms + EPS) * g_ref[...]).astype(BF16)

    zq = _dot(hn, w_ref[:, C_Q:C_CMP])
    q_ref[...] = (zq * _head_rms_scale(zq, bd_ref[...]) * qgain_ref[...]).astype(BF16)

    zc = _dot(hn, w_ref[:, C_CMP:C_KSLC])
    for w in range(2):
        cmpbuf[w] = zc[:, w * LANES:(w + 1) * LANES]
    for p in range(CMP_STRIDE):
        for w in range(2):
            col = p * 256 + w * LANES
            cmp_ref[:, col:col + LANES] = cmpbuf[w, pl.ds(p, TM // CMP_STRIDE, stride=CMP_STRIDE),
                                                 :].astype(BF16)

    zk = _dot(hn, w_ref[:, C_KSLC:C_VSLC])
    kn = zk * _head_rms_scale(zk, bd_ref[0:256, 0:256]) * kgain_ref[...]
    lane = lax.broadcasted_iota(jnp.int32, (TM, LANES), 1)
    rowg = (pl.program_id(0) % nt) * TM + lax.broadcasted_iota(jnp.int32, (TM, LANES), 0)
    blk = lax.shift_right_logical(rowg, 6) & (BIAS_BLOCKS - 1)
    onehot = jnp.where(lane == DH + blk, 1.0, 0.0)
    lo = lane < DH
    ks, kw = kn[:, 0:128], kn[:, 128:256]
    kslc_ref[0] = jnp.where(lo, ks, onehot).astype(BF16)
    kslc_ref[1] = jnp.where(lo, pltpu.roll(ks, DH, 1), onehot).astype(BF16)
    kwin_ref[0] = jnp.where(lo, kw, 0.0).astype(BF16)
    kwin_ref[1] = jnp.where(lo, pltpu.roll(kw, DH, 1), 0.0).astype(BF16)

    zv = _dot(hn, w_ref[:, C_VSLC:C_LRU])
    zvT = zv.T
    ones_row = jnp.where(lax.broadcasted_iota(jnp.int32, (VROWS - DH, LANES), 0) == 0,
                         1.0, 0.0).astype(BF16)
    for j in range(TM // LANES):
        for h in range(HKV):
            cols = slice(j * LANES, (j + 1) * LANES)
            vslcT_ref[h, j, 0:DH, :] = zvT[h * DH:(h + 1) * DH, cols].astype(BF16)
            vslcT_ref[h, j, DH:VROWS, :] = ones_row
            vwinT_ref[h, j, 0:DH, :] = zvT[128 + h * DH:128 + (h + 1) * DH, cols].astype(BF16)
            vwinT_ref[h, j, DH:VROWS, :] = ones_row

    lru_ref[...] = _dot(hn, w_ref[:, C_LRU:C_CV])
    cv_ref[...] = _dot(hn, w_ref[:, C_CV:C_GATE])

    zg = _sigmoid(_dot(hn, w_ref[:, C_GATE:N_INP]))
    gT_ref[...] = zg.T[0:GATE_ROWS, :]


def _in_proj(x2, g, w, bd, qgain, kgain, B, S):
    T, D = x2.shape
    nt = S // TM
    tok = lambda w_: pl.BlockSpec((TM, w_), lambda t: (t, 0))
    const = lambda a: pl.BlockSpec(a.shape, lambda t: (0,) * a.ndim)
    vT_spec = pl.BlockSpec((None, HKV, TM // LANES, VROWS, LANES),
                           lambda t: (t // nt, 0, t % nt, 0, 0))
    k_spec = pl.BlockSpec((None, HKV, TM, LANES), lambda t: (t // nt, 0, t % nt, 0))
    out_shape = (
        jax.ShapeDtypeStruct((T, NSA_W), BF16),
        jax.ShapeDtypeStruct((T // CMP_STRIDE, CMP_STRIDE * 256), BF16),
        jax.ShapeDtypeStruct((B, HKV, S, LANES), BF16),
        jax.ShapeDtypeStruct((B, HKV, S, LANES), BF16),
        jax.ShapeDtypeStruct((B, HKV, S // LANES, VROWS, LANES), BF16),
        jax.ShapeDtypeStruct((B, HKV, S // LANES, VROWS, LANES), BF16),
        jax.ShapeDtypeStruct((T, 2 * LRU_W), F32),
        jax.ShapeDtypeStruct((T, 2 * CV_W), F32),
        jax.ShapeDtypeStruct((B, GATE_ROWS, S), F32),
    )
    out_specs = (
        tok(NSA_W), pl.BlockSpec((TM // CMP_STRIDE, CMP_STRIDE * 256), lambda t: (t, 0)),
        k_spec, k_spec, vT_spec, vT_spec,
        tok(2 * LRU_W), tok(2 * CV_W),
        pl.BlockSpec((None, GATE_ROWS, TM), lambda t: (t // nt, 0, t % nt)),
    )
    return pl.pallas_call(
        functools.partial(_in_proj_kernel, nt=nt),
        out_shape=out_shape,
        grid=(T // TM,),
        in_specs=[tok(D), const(g), const(w), const(bd), const(qgain), const(kgain)],
        out_specs=out_specs,
        scratch_shapes=[pltpu.VMEM((2, TM, LANES), F32)],
        compiler_params=pltpu.CompilerParams(
            dimension_semantics=("arbitrary",), vmem_limit_bytes=VMEM_LIMIT),
        name="in_proj",
    )(x2, g, w, bd, qgain, kgain)


def _compress_kernel(x_ref, pos_ref, wbig_ref, w2_ref, bd_ref, kgain_ref, kc_ref, vcT_ref):
    nc = x_ref.shape[0]
    half = 4 * CMP_HID
    p = _dot(x_ref[...], wbig_ref[...])
    pc = _dot(pos_ref[...], wbig_ref[...])
    const = pc[0:1, 0:half] + pc[1:2, half:2 * half]
    nxt = pltpu.roll(p[:, half:2 * half], nc - 1, 0)
    hid = _gelu_tanh(p[:, 0:half] + nxt + const).astype(BF16)
    kv = _dot(hid, w2_ref[...])
    kc = kv[:, 0:128]
    kc = kc * _head_rms_scale(kc, bd_ref[...]) * kgain_ref[...]
    lo = lax.broadcasted_iota(jnp.int32, (nc, LANES), 1) < DH
    kc_ref[0] = jnp.where(lo, kc, 0.0).astype(BF16)
    kc_ref[1] = jnp.where(lo, pltpu.roll(kc, DH, 1), 0.0).astype(BF16)
    vT = kv[:, 128:256].T
    for h in range(HKV):
        vcT_ref[h] = vT[h * DH:(h + 1) * DH, :].astype(BF16)


def _compress(cmpx, pos2, wbig, w2big, bd128, kgain, B, NC):
    const = lambda a: pl.BlockSpec(a.shape, lambda b: (0,) * a.ndim, pipeline_mode=pl.Buffered(1))
    return pl.pallas_call(
        _compress_kernel,
        out_shape=(jax.ShapeDtypeStruct((B, HKV, NC, LANES), BF16),
                   jax.ShapeDtypeStruct((B, HKV, DH, NC), BF16)),
        grid=(B,),
        in_specs=[pl.BlockSpec((None, NC, cmpx.shape[2]), lambda b: (b, 0, 0)),
                  const(pos2), const(wbig), const(w2big), const(bd128), const(kgain)],
        out_specs=(pl.BlockSpec((None, HKV, NC, LANES), lambda b: (b, 0, 0, 0)),
                   pl.BlockSpec((None, HKV, DH, NC), lambda b: (b, 0, 0, 0))),
        compiler_params=pltpu.CompilerParams(
            dimension_semantics=("arbitrary",), vmem_limit_bytes=VMEM_LIMIT),
        name="compress",
    )(cmpx, pos2, wbig, w2big, bd128, kgain)


def _attn_kernel(sc_ref, q_ref, kc_ref, vcT_ref, ks_ref, vsT_ref, kw_ref, vwT_ref, gT_ref,
                 y_ref, qT_ref, imp_ref, selb_ref, selo_ref, oc_ref, os_ref, ow_ref, acc_ref, *, cch):
    h = pl.program_id(1)
    i = pl.program_id(2)
    nc = kc_ref.shape[0]
    ns = selb_ref.shape[0]
    wspan = WINDOW + QB
    t0 = i * QB
    iota = lambda shape, ax: lax.broadcasted_iota(jnp.int32, shape, ax)
    use_bound = sc_ref[1] > 0.5
    shift = jnp.where(use_bound, -sc_ref[0], 0.0)

    qfT = q_ref[...].astype(F32).T
    for g in range(G):
        qT_ref[0:DH, g * QB:(g + 1) * QB] = qfT[g * DH:(g + 1) * DH, :].astype(BF16)
    qT_ref[DH:LANES, :] = jnp.zeros((LANES - DH, G * QB), BF16)

    def window(bounded):
        wc = jnp.maximum(i * (QB // LANES) - WINDOW // LANES, 0)
        ws = wc * LANES
        sw = _dot(kw_ref[pl.ds(pl.multiple_of(ws, LANES), wspan), :], qT_ref[...])
        kpos = ws + iota((wspan, QB), 0)
        t_w = t0 + iota((wspan, QB), 1)
        inside = -sc_ref[3] if bounded else 0.0
        wb = jnp.where(kpos <= t_w, jnp.where(kpos > t_w - WINDOW, inside, NEG), NEG)
        sw = sw + jnp.concatenate([wb] * G, axis=1)
        p_w = jnp.exp2(sw if bounded else sw - jnp.max(sw, axis=0, keepdims=True)).astype(BF16)
        vw = jnp.concatenate([vwT_ref[wc + u] for u in range(wspan // LANES)], axis=1)
        acc_w = _dot(vw, p_w)
        ow_ref[...] = acc_w[0:DH, :] * (1.0 / acc_w[DH:DH + 1, :])

    def front(nr, bounded):
        window(bounded)
        nsu = nr // (SLC_BLOCK // CMP_STRIDE)
        cmask = iota((nr, QB), 0) * CMP_STRIDE + (CMP_BLOCK - 1) <= t0 + iota((nr, QB), 1)
        cb = jnp.where(cmask, -sc_ref[2] if bounded else 0.0, NEG)
        s = _dot(kc_ref[0:nr, :], qT_ref[...]) + jnp.concatenate([cb] * G, axis=1)
        p = jnp.exp2(s if bounded else s - jnp.max(s, axis=0, keepdims=True))
        l = jnp.sum(p, axis=0, keepdims=True)
        anyv = jnp.where(t0 + iota((1, QB), 1) >= CMP_BLOCK - 1, 1.0, 0.0)
        pn = p * jnp.where(jnp.concatenate([anyv] * G, axis=1) > 0.5, 1.0 / l, 0.0)
        oc_ref[...] = _dot(vcT_ref[:, 0:nr], pn.astype(BF16))
        imp = pn[:, 0:QB]
        for g in range(1, G):
            imp = imp + pn[:, g * QB:(g + 1) * QB]

        parts = []
        for w in range(QB // LANES):
            imp_ref[w, 0:8, :] = jnp.zeros((8, LANES), F32)
            imp_ref[w, 8:8 + nr, :] = imp[:, w * LANES:(w + 1) * LANES]
            acc = imp_ref[w, pl.ds(7, nsu, stride=4), :]
            for r in range(4):
                acc = acc + imp_ref[w, pl.ds(8 + r, nsu, stride=4), :]
            parts.append(acc)
        islc = jnp.concatenate(parts, axis=1)

        j_i = iota((nsu, QB), 0)
        t_s = t0 + iota((nsu, QB), 1)
        cur = lax.shift_right_logical(t_s, 6)
        valid = j_i * SLC_BLOCK <= t_s
        ninf = -jnp.inf
        score = jnp.where(j_i == 0, ninf, jnp.where(j_i == cur, ninf,
                                                    jnp.where(j_i == cur - 1, ninf, islc)))
        score = jnp.where(valid, score, NEG)
        j_f = j_i.astype(F32)
        for _ in range(N_SEL - 3):
            mx = jnp.max(score, axis=0, keepdims=True)
            jm = jnp.min(jnp.where(score == mx, j_f, float(nsu)), axis=0, keepdims=True)
            score = jnp.where(j_f == jm, -jnp.inf, score)
        own = lax.shift_right_logical(j_i, (QB // SLC_BLOCK).bit_length() - 1) == i
        picked = jnp.where(valid, jnp.where(score == -jnp.inf, shift, NEG), NEG)
        selo_ref[0:nsu, :] = picked
        selb_ref[0:nsu, :] = jnp.where(own, NEG, picked)
        pad = -nsu % min(BIAS_BLOCKS, ns)
        if pad:
            selb_ref[nsu:nsu + pad, :] = jnp.full((pad, QB), NEG, F32)

    cls = (i * (QB // CMP_STRIDE) + (QB - CMP_BLOCK) // CMP_STRIDE) // cch
    for kk in range(nc // cch):
        for bounded in (True, False):
            pl.when(jnp.logical_and(cls == kk, use_bound == bounded))(
                functools.partial(front, (kk + 1) * cch, bounded))

    bpt = QB // SLC_BLOCK
    own_b = jnp.concatenate(
        [jnp.broadcast_to(selo_ref[pl.ds(i * bpt + b, 1), :], (SLC_BLOCK, QB)) for b in range(bpt)],
        axis=0)
    own_b = jnp.where(iota((QB, QB), 0) <= iota((QB, QB), 1), own_b, NEG)
    own_v = jnp.concatenate([vsT_ref[i * (QB // LANES) + w] for w in range(QB // LANES)], axis=1)
    sd = (_dot(ks_ref[pl.ds(pl.multiple_of(t0, QB), QB), :], qT_ref[...])
          + jnp.concatenate([own_b] * G, axis=1))

    nv = KCH // LANES
    cpg = BIAS_BLOCKS // (KCH // SLC_BLOCK)
    brows = min(BIAS_BLOCKS, ns)
    nstep = (i + NCHUNK - 1) // NCHUNK

    def chunk_scores(c0, n=NCHUNK):
        @pl.when(c0 % cpg == 0)
        def _():
            r0 = pl.multiple_of((c0 // cpg) * brows, brows)
            rows = selb_ref[pl.ds(r0, brows), :].astype(BF16)
            qT_ref[DH:DH + brows, :] = jnp.concatenate([rows] * G, axis=1)

        qT = qT_ref[...]
        return [_dot(ks_ref[pl.ds(pl.multiple_of((c0 + u) * KCH, KCH), KCH), :], qT)
                for u in range(n)]

    def values(c):
        return jnp.concatenate([vsT_ref[c * nv + w] for w in range(nv)], axis=1)

    def finish(acc):
        os_ref[...] = acc[0:DH, :] * (1.0 / acc[DH:DH + 1, :])

    @pl.when(use_bound)
    def _():
        def add_chunks(c0, n, acc):
            r0 = pl.multiple_of((c0 // cpg) * brows, brows)
            rows = selb_ref[pl.ds(r0, brows), :].astype(BF16)
            qT_ref[DH:DH + brows, :] = jnp.concatenate([rows] * G, axis=1)
            qT = qT_ref[...]
            for sub in range(0, n, NCHUNK):
                cs = [c0 + sub + u for u in range(min(NCHUNK, n - sub))]
                ss = [_dot(ks_ref[pl.ds(pl.multiple_of(c * KCH, KCH), KCH), :], qT) for c in cs]
                for c, s in zip(cs, ss):
                    acc = acc + _dot(values(c), jnp.exp2(s).astype(BF16))
            return acc

        nfull = i // cpg
        rem = i - nfull * cpg
        acc0 = _dot(own_v, jnp.exp2(sd).astype(BF16))
        acc_ref[...] = lax.fori_loop(
            0, nfull, lambda grp, acc: add_chunks(grp * cpg, cpg, acc), acc0)

        half = jnp.where(rem >= NCHUNK, NCHUNK, 0)
        c1 = nfull * cpg + half
        rest = rem - half

        @pl.when(half > 0)
        def _():
            acc_ref[...] = add_chunks(nfull * cpg, NCHUNK, acc_ref[...])

        @pl.when(jnp.logical_and(rest > 0, rest <= NCHUNK // 2))
        def _():
            acc_ref[...] = add_chunks(c1, NCHUNK // 2, acc_ref[...])

        @pl.when(rest > NCHUNK // 2)
        def _():
            acc_ref[...] = add_chunks(c1, NCHUNK, acc_ref[...])

        finish(acc_ref[...])

    @pl.when(jnp.logical_not(use_bound))
    def _():
        def step(grp, carry):
            m, acc = carry
            ss = chunk_scores(grp * NCHUNK)
            for u in range(NCHUNK):
                s = ss[u]
                m_new = jnp.maximum(m, jnp.max(s, axis=0, keepdims=True))
                p = jnp.exp2(s - m_new).astype(BF16)
                acc = jnp.exp2(m - m_new) * acc + _dot(values(grp * NCHUNK + u), p)
                m = m_new
            return m, acc

        m0 = jnp.max(sd, axis=0, keepdims=True)
        acc0 = _dot(own_v, jnp.exp2(sd - m0).astype(BF16))
        finish(lax.fori_loop(0, nstep, step, (m0, acc0))[1])

    os_ = os_ref[...]
    ow = ow_ref[...]

    outs = []
    for g in range(G):
        sl = slice(g * QB, (g + 1) * QB)
        row = h * (3 * G) + 3 * g
        gc = gT_ref[pl.ds(row, 1), :]
        gs = gT_ref[pl.ds(row + 1, 1), :]
        gw = gT_ref[pl.ds(row + 2, 1), :]
        outs.append(gc * oc_ref[:, sl] + gs * os_[:, sl] + gw * ow[:, sl])
    y_ref[...] = jnp.concatenate(outs, axis=0).T


def _score_bound(qgain, kgain_slc, kgain_win, kgain_cmp):
    qmax = jnp.max(jnp.abs(qgain))
    ms = [(1.05 * DH * qmax * jnp.max(jnp.abs(kg))).astype(BF16).astype(F32)
          for kg in (kgain_slc, kgain_cmp, kgain_win)]
    ok = (jnp.maximum(jnp.maximum(ms[0], ms[1]), ms[2]) <= SAFE_SHIFT).astype(F32)
    return jnp.stack([ms[0], ok, ms[1], ms[2]])


def _attention(sc, q, kc, vcT, kslc, vslcT, kwin, vwinT, gT, B, S):
    NC = S // CMP_STRIDE
    NS = S // SLC_BLOCK
    NQ = S // QB
    cch = min(CCH, NC)
    assert NC % cch == 0 and (S // KCH) % NCHUNK == 0 and QB == KCH
    assert BIAS_BLOCKS * SLC_BLOCK // KCH == 2 * NCHUNK
    head = lambda *blk: pl.BlockSpec((None, None) + blk, lambda b, h, i: (b, h) + (0,) * len(blk),
                                     pipeline_mode=pl.Buffered(1))
    return pl.pallas_call(
        functools.partial(_attn_kernel, cch=cch),
        out_shape=jax.ShapeDtypeStruct((B, S, NSA_W), F32),
        grid=(B, HKV, NQ),
        in_specs=[
            pl.BlockSpec(memory_space=pltpu.SMEM),
            pl.BlockSpec((None, QB, G * DH), lambda b, h, i: (b, i, h)),
            head(NC, LANES), head(DH, NC),
            head(S, LANES), head(S // LANES, VROWS, LANES),
            head(S, LANES), head(S // LANES, VROWS, LANES),
            pl.BlockSpec((None, GATE_ROWS, QB), lambda b, h, i: (b, 0, i)),
        ],
        out_specs=pl.BlockSpec((None, QB, G * DH), lambda b, h, i: (b, i, h)),
        scratch_shapes=[pltpu.VMEM((LANES, G * QB), BF16),
                        pltpu.VMEM((QB // LANES, NC + 8, LANES), F32),
                        pltpu.VMEM((NS, QB), F32), pltpu.VMEM((NS, QB), F32),
                        pltpu.VMEM((DH, G * QB), F32),
                        pltpu.VMEM((DH, G * QB), F32), pltpu.VMEM((DH, G * QB), F32),
                        pltpu.VMEM((VROWS, G * QB), F32)],
        compiler_params=pltpu.CompilerParams(
            dimension_semantics=("arbitrary", "arbitrary", "arbitrary"),
            vmem_limit_bytes=VMEM_LIMIT),
        name="attention",
    )(sc, q, kc, vcT, kslc, vslcT, kwin, vwinT, gT)


XH = 8
YH = 32


def _causal_taps(buf, w_ref, bias, hist, ntaps):
    rows = TS + 8
    base = hist - (ntaps - 1)
    out = jnp.broadcast_to(bias, (TS, buf.shape[1]))
    for r in range(8):
        z = None
        for k in range(ntaps):
            if (base + k) % 8 == r:
                term = w_ref[k:k + 1, :] * buf[base + k - r:base + k - r + rows, :]
                z = term if z is None else z + term
        if z is not None:
            out = out + (z[0:TS, :] if r == 0 else pltpu.roll(z, rows - r, 0)[0:TS, :])
    return out


def _seq_mix_kernel(lru_ref, cv_ref, cw_ref, cb_ref, wa_ref, ba_ref, wx_ref, bx_ref, lam_ref,
                    dw_ref, db_ref, lng_ref, lnb_ref, gl_ref, gc_ref,
                    yl_ref, yc_ref, xbuf, ybuf, hbuf):
    t = pl.program_id(1)

    @pl.when(t == 0)
    def _():
        xbuf[0:XH, :] = jnp.zeros((XH, LRU_W), F32)
        xbuf[XH + TS:XH + TS + 8, :] = jnp.zeros((8, LRU_W), F32)
        ybuf[0:YH, :] = jnp.zeros((YH, CV_W), F32)
        ybuf[YH + TS:YH + TS + 8, :] = jnp.zeros((8, CV_W), F32)
        hbuf[...] = jnp.zeros((8, LRU_W), F32)

    xb = lru_ref[:, 0:LRU_W]
    gb = lru_ref[:, LRU_W:2 * LRU_W]
    xbuf[XH:XH + TS, :] = xb
    xr = _causal_taps(xbuf, cw_ref, cb_ref[...], XH, LRU_CONV)
    xbuf[0:XH, :] = xbuf[TS:TS + XH, :]

    xr16 = xr.astype(BF16)
    r = _sigmoid(_dot(xr16, wa_ref[...]) + ba_ref[...])
    ig = _sigmoid(_dot(xr16, wx_ref[...]) + bx_ref[...])
    nl = -lam_ref[...]
    softplus = jnp.maximum(nl, 0.0) + jnp.log1p(jnp.exp(-jnp.abs(nl)))
    log_a = -LRU_C * r * softplus
    a = jnp.exp(log_a)
    u = xr * ig * jnp.sqrt(-jnp.tanh(log_a) * (a * a + 1.0))

    row = lax.broadcasted_iota(jnp.int32, (TS, LRU_W), 0)
    d = 1
    while d < TS:
        keep = row >= d
        a_sh = jnp.where(keep, pltpu.roll(a, d, 0), 1.0)
        u_sh = jnp.where(keep, pltpu.roll(u, d, 0), 0.0)
        u = a * u_sh + u
        a = a * a_sh
        d *= 2
    hseq = a * hbuf[0:1, :] + u
    hbuf[...] = jnp.broadcast_to(hseq[TS - 1:TS, :], (8, LRU_W))
    yl = hseq * _gelu_tanh(gb)
    yl = yl * lax.rsqrt(jnp.mean(yl * yl, axis=-1, keepdims=True) + EPS) * gl_ref[...]
    yl_ref[...] = yl.astype(BF16)

    y = cv_ref[:, 0:CV_W] * _sigmoid(cv_ref[:, CV_W:2 * CV_W])
    ybuf[YH:YH + TS, :] = y
    c = _causal_taps(ybuf, dw_ref, db_ref[...], YH, CV_KERNEL)
    ybuf[0:YH, :] = ybuf[TS:TS + YH, :]
    mu = jnp.mean(c, axis=-1, keepdims=True)
    cc = c - mu
    var = jnp.mean(cc * cc, axis=-1, keepdims=True)
    ln = cc * lax.rsqrt(var + EPS) * lng_ref[...] + lnb_ref[...]
    yc = ln * _sigmoid(ln)
    yc = yc * lax.rsqrt(jnp.mean(yc * yc, axis=-1, keepdims=True) + EPS) * gc_ref[...]
    yc_ref[...] = yc.astype(BF16)


def _seq_mix(lru, cv, params, B, S):
    const = lambda a: pl.BlockSpec(a.shape, lambda b, t: (0,) * a.ndim)
    tile = lambda w_: pl.BlockSpec((None, TS, w_), lambda b, t: (b, t, 0))
    return pl.pallas_call(
        _seq_mix_kernel,
        out_shape=(jax.ShapeDtypeStruct((B, S, LRU_W), BF16),
                   jax.ShapeDtypeStruct((B, S, CV_W), BF16)),
        grid=(B, S // TS),
        in_specs=[tile(2 * LRU_W), tile(2 * CV_W)] + [const(p) for p in params],
        out_specs=(tile(LRU_W), tile(CV_W)),
        scratch_shapes=[pltpu.VMEM((XH + TS + 8, LRU_W), F32), pltpu.VMEM((YH + TS + 8, CV_W), F32),
                        pltpu.VMEM((8, LRU_W), F32)],
        compiler_params=pltpu.CompilerParams(
            dimension_semantics=("arbitrary", "arbitrary"), vmem_limit_bytes=VMEM_LIMIT),
        name="seq_mix",
    )(lru, cv, *params)


def _out_mlp_kernel(x_ref, ya_ref, yl_ref, yc_ref, ga_ref, wo_ref, gm_ref, w1_ref, w2_ref, o_ref):
    ya = ya_ref[...]
    ya = (ya * lax.rsqrt(jnp.mean(ya * ya, axis=-1, keepdims=True) + EPS) * ga_ref[...]).astype(BF16)
    x1 = (x_ref[...] + _dot(ya, wo_ref[0:NSA_W, :])
          + _dot(yl_ref[...], wo_ref[NSA_W:NSA_W + LRU_W, :])
          + _dot(yc_ref[...], wo_ref[NSA_W + LRU_W:NSA_W + LRU_W + CV_W, :]))
    hm = (x1 * lax.rsqrt(jnp.mean(x1 * x1, axis=-1, keepdims=True) + EPS) * gm_ref[...]).astype(BF16)
    d_ff = w1_ref.shape[1]
    fc = 1024
    o_ref[...] = x1
    for c in range(d_ff // fc):
        hc = jnp.maximum(_dot(hm, w1_ref[:, c * fc:(c + 1) * fc]), 0.0)
        o_ref[...] += _dot((hc * hc).astype(BF16), w2_ref[c * fc:(c + 1) * fc, :])


def _out_mlp(x2, ya, yl, yc, ga, wo, gm, w1, w2):
    T, D = x2.shape
    tok = lambda w_: pl.BlockSpec((TM, w_), lambda t: (t, 0))
    const1 = lambda a: pl.BlockSpec(a.shape, lambda t: (0,) * a.ndim, pipeline_mode=pl.Buffered(1))
    return pl.pallas_call(
        _out_mlp_kernel,
        out_shape=jax.ShapeDtypeStruct((T, D), F32),
        grid=(T // TM,),
        in_specs=[tok(D), tok(NSA_W), tok(LRU_W), tok(CV_W),
                  const1(ga), const1(wo), const1(gm), const1(w1), const1(w2)],
        out_specs=tok(D),
        compiler_params=pltpu.CompilerParams(
            dimension_semantics=("arbitrary",), vmem_limit_bytes=VMEM_LIMIT),
        name="out_mlp",
    )(x2, ya, yl, yc, ga, wo, gm, w1, w2)


def _block_ones(n):
    idx = np.arange(n) // DH
    return jnp.asarray((idx[:, None] == idx[None, :]).astype(np.float32) / DH, dtype=BF16)


def _permute_w_in(w):
    off_kv = NSA_W
    kv = [w[:, off_kv + c * 128: off_kv + (c + 1) * 128] for c in range(6)]
    off_gate = off_kv + 6 * 128
    ngate = 3 * HKV * G
    off_lru = off_gate + ngate
    gate = jnp.pad(w[:, off_gate:off_lru], ((0, 0), (0, LANES - ngate)))
    cols = [w[:, 0:NSA_W], kv[0], kv[1], kv[2], kv[4], kv[3], kv[5],
            w[:, off_lru:off_lru + 2 * LRU_W + 2 * CV_W], gate]
    return jnp.concatenate(cols, axis=1).astype(BF16)


def _compress_weights(pos, w1, w2):
    slot_kv = jnp.array([0, 0, 1, 1])
    eye = jnp.eye(4, dtype=F32)
    w1r = w1.reshape(2, 2, CMP_STRIDE, DH, CMP_HID)
    slot = jnp.arange(4).reshape(1, 4, 1, 1)
    wbig = jnp.concatenate(
        [jnp.where(slot == t, w1r[t // 2, half][:, None, :, :], 0.0).reshape(CMP_STRIDE * 4 * DH, CMP_HID)
         for half in range(2) for t in range(4)], axis=1)
    posr = pos.reshape(2, 2, CMP_STRIDE, DH)[slot_kv]
    pos2 = jnp.transpose(posr, (1, 2, 0, 3)).reshape(2, CMP_STRIDE * 4 * DH)
    pos2 = jnp.pad(pos2, ((0, 6), (0, 0)))
    w2big = jnp.einsum('sod,st->sotd', w2[slot_kv], eye).reshape(4 * CMP_HID, 4 * DH)
    return pos2.astype(BF16), wbig.astype(BF16), w2big.astype(BF16)


def _block_diag(w):
    hh, bw, _ = w.shape
    eye = jnp.eye(hh, dtype=w.dtype)
    return jnp.einsum('hij,hg->higj', w, eye).reshape(hh * bw, hh * bw).astype(BF16)


def kernel(x, attn_norm, w_in, q_norm, k_norm, cmp_pos, cmp_w1, cmp_w2, lru_conv_w, lru_conv_b,
           lru_wa, lru_ba, lru_wx, lru_bx, lru_lambda, cv_dw_w, cv_dw_b, cv_ln_g, cv_ln_b,
           out_norm, w_out, mlp_norm, mlp_w1, mlp_w2):
    B, S, D = x.shape
    depth = w_in.shape[0]
    assert S % TM == 0 and S % TS == 0 and S >= WINDOW + QB and S // SLC_BLOCK >= N_SEL
    NC = S // CMP_STRIDE
    row = lambda v: v.reshape(1, -1).astype(F32)
    bd512 = _block_ones(NSA_W)
    bd128 = _block_ones(128)

    x2 = x.reshape(B * S, D)
    for l in range(depth):
        qgain = row(jnp.tile(q_norm[l], HKV * G)) * (DH ** -0.5 * LOG2E)
        kgain = row(jnp.concatenate([jnp.tile(k_norm[l, 1], HKV), jnp.tile(k_norm[l, 2], HKV)]))
        q, cmpr, kslc, kwin, vslcT, vwinT, lru, cv, gT = _in_proj(
            x2, row(attn_norm[l]), _permute_w_in(w_in[l]), bd512, qgain, kgain, B, S)

        pos2, wbig, w2big = _compress_weights(cmp_pos[l], cmp_w1[l], cmp_w2[l])
        kc, vcT = _compress(cmpr.reshape(B, NC, CMP_STRIDE * 256), pos2, wbig, w2big, bd128,
                            row(jnp.tile(k_norm[l, 0], HKV)), B, NC)

        sc = _score_bound(qgain, k_norm[l, 1], k_norm[l, 2], k_norm[l, 0])
        y_attn = _attention(sc, q.reshape(B, S, NSA_W), kc, vcT, kslc, vslcT, kwin, vwinT, gT, B, S)

        g_out = out_norm[l]
        seq_params = (lru_conv_w[l], row(lru_conv_b[l]), _block_diag(lru_wa[l]), row(lru_ba[l]),
                      _block_diag(lru_wx[l]), row(lru_bx[l]), row(lru_lambda[l]),
                      cv_dw_w[l], row(cv_dw_b[l]), row(cv_ln_g[l]), row(cv_ln_b[l]),
                      row(g_out[NSA_W:NSA_W + LRU_W]), row(g_out[NSA_W + LRU_W:]))
        yl, yc = _seq_mix(lru.reshape(B, S, 2 * LRU_W), cv.reshape(B, S, 2 * CV_W), seq_params, B, S)

        x2 = _out_mlp(x2, y_attn.reshape(B * S, NSA_W), yl.reshape(B * S, LRU_W),
                      yc.reshape(B * S, CV_W), row(g_out[:NSA_W]), w_out[l].astype(BF16),
                      row(mlp_norm[l]), mlp_w1[l].astype(BF16), mlp_w2[l].astype(BF16))
    return x2.reshape(B, S, D)
```

```python
import functools

import numpy as np
import jax
import jax.numpy as jnp
from jax import lax
from jax.experimental import pallas as pl
from jax.experimental.pallas import tpu as pltpu

F32 = jnp.float32
BF16 = jnp.bfloat16

EPS = 1e-6
NEG = -1e30
DH = 64
HKV = 2
G = 4
NSA_W = HKV * G * DH
CMP_BLOCK = 32
CMP_STRIDE = 16
CMP_HID = 128
SLC_BLOCK = 64
N_SEL = 16
WINDOW = 512
QB = 256
LRU_W = 256
LRU_HEADS = 8
LRU_CONV = 4
LRU_C = 8.0
CV_W = 256
CV_KERNEL = 31
LOG2E = 1.4426950408889634
SAFE_SHIFT = 50.0

LANES = 128
KCH = 256
NCHUNK = 8
CCH = 256
TM = 512
TS = 512
VMEM_LIMIT = 48 * 1024 * 1024

C_Q = 0
C_CMP = C_Q + NSA_W
C_KSLC = C_CMP + 256
C_KWIN = C_KSLC + 128
C_VSLC = C_KWIN + 128
C_VWIN = C_VSLC + 128
C_LRU = C_VWIN + 128
C_CV = C_LRU + 2 * LRU_W
C_GATE = C_CV + 2 * CV_W
N_INP = C_GATE + LANES
GATE_ROWS = 32
VROWS = DH + 16
BIAS_BLOCKS = LANES - DH


def _dot(a, b):
    return jnp.dot(a, b, preferred_element_type=F32)


def _head_rms_scale(z, bd):
    sq = z * z
    hi = sq.astype(BF16)
    lo = (sq - hi.astype(F32)).astype(BF16)
    ms = _dot(hi, bd) + _dot(lo, bd)
    return lax.rsqrt(ms + EPS)


def _gelu_tanh(x):
    return 0.5 * x * (1.0 + jnp.tanh(0.7978845608028654 * (x + 0.044715 * (x * x * x))))


def _sigmoid(x):
    return jax.nn.sigmoid(x)


def _in_proj_kernel(x_ref, g_ref, w_ref, bd_ref, qgain_ref, kgain_ref,
                    q_ref, cmp_ref, kslc_ref, kwin_ref, vslcT_ref, vwinT_ref,
                    lru_ref, cv_ref, gT_ref, cmpbuf, *, nt):
    x = x_ref[...]
    ms = jnp.mean(x * x, axis=-1, keepdims=True)
    hn = (x * lax.rsqrt(ms + EPS) * g_ref[...]).astype(BF16)

    zq = _dot(hn, w_ref[:, C_Q:C_CMP])
    q_ref[...] = (zq * _head_rms_scale(zq, bd_ref[...]) * qgain_ref[...]).astype(BF16)

    zc = _dot(hn, w_ref[:, C_CMP:C_KSLC])
    for w in range(2):
        cmpbuf[w] = zc[:, w * LANES:(w + 1) * LANES]
    for p in range(CMP_STRIDE):
        for w in range(2):
            col = p * 256 + w * LANES
            cmp_ref[:, col:col + LANES] = cmpbuf[w, pl.ds(p, TM // CMP_STRIDE, stride=CMP_STRIDE),
                                                 :].astype(BF16)

    zk = _dot(hn, w_ref[:, C_KSLC:C_VSLC])
    kn = zk * _head_rms_scale(zk, bd_ref[0:256, 0:256]) * kgain_ref[...]
    lane = lax.broadcasted_iota(jnp.int32, (TM, LANES), 1)
    rowg = (pl.program_id(0) % nt) * TM + lax.broadcasted_iota(jnp.int32, (TM, LANES), 0)
    blk = lax.shift_right_logical(rowg, 6) & (BIAS_BLOCKS - 1)
    onehot = jnp.where(lane == DH + blk, 1.0, 0.0)
    lo = lane < DH
    ks, kw = kn[:, 0:128], kn[:, 128:256]
    kslc_ref[0] = jnp.where(lo, ks, onehot).astype(BF16)
    kslc_ref[1] = jnp.where(lo, pltpu.roll(ks, DH, 1), onehot).astype(BF16)
    kwin_ref[0] = jnp.where(lo, kw, 0.0).astype(BF16)
    kwin_ref[1] = jnp.where(lo, pltpu.roll(kw, DH, 1), 0.0).astype(BF16)

    zv = _dot(hn, w_ref[:, C_VSLC:C_LRU])
    zvT = zv.T
    ones_row = jnp.where(lax.broadcasted_iota(jnp.int32, (VROWS - DH, LANES), 0) == 0,
                         1.0, 0.0).astype(BF16)
    for j in range(TM // LANES):
        for h in range(HKV):
            cols = slice(j * LANES, (j + 1) * LANES)
            vslcT_ref[h, j, 0:DH, :] = zvT[h * DH:(h + 1) * DH, cols].astype(BF16)
            vslcT_ref[h, j, DH:VROWS, :] = ones_row
            vwinT_ref[h, j, 0:DH, :] = zvT[128 + h * DH:128 + (h + 1) * DH, cols].astype(BF16)
            vwinT_ref[h, j, DH:VROWS, :] = ones_row

    lru_ref[...] = _dot(hn, w_ref[:, C_LRU:C_CV])
    cv_ref[...] = _dot(hn, w_ref[:, C_CV:C_GATE])

    zg = _sigmoid(_dot(hn, w_ref[:, C_GATE:N_INP]))
    gT_ref[...] = zg.T[0:GATE_ROWS, :]


def _in_proj(x2, g, w, bd, qgain, kgain, B, S):
    T, D = x2.shape
    nt = S // TM
    tok = lambda w_: pl.BlockSpec((TM, w_), lambda t: (t, 0))
    const = lambda a: pl.BlockSpec(a.shape, lambda t: (0,) * a.ndim)
    vT_spec = pl.BlockSpec((None, HKV, TM // LANES, VROWS, LANES),
                           lambda t: (t // nt, 0, t % nt, 0, 0))
    k_spec = pl.BlockSpec((None, HKV, TM, LANES), lambda t: (t // nt, 0, t % nt, 0))
    out_shape = (
        jax.ShapeDtypeStruct((T, NSA_W), BF16),
        jax.ShapeDtypeStruct((T // CMP_STRIDE, CMP_STRIDE * 256), BF16),
        jax.ShapeDtypeStruct((B, HKV, S, LANES), BF16),
        jax.ShapeDtypeStruct((B, HKV, S, LANES), BF16),
        jax.ShapeDtypeStruct((B, HKV, S // LANES, VROWS, LANES), BF16),
        jax.ShapeDtypeStruct((B, HKV, S // LANES, VROWS, LANES), BF16),
        jax.ShapeDtypeStruct((T, 2 * LRU_W), F32),
        jax.ShapeDtypeStruct((T, 2 * CV_W), F32),
        jax.ShapeDtypeStruct((B, GATE_ROWS, S), F32),
    )
    out_specs = (
        tok(NSA_W), pl.BlockSpec((TM // CMP_STRIDE, CMP_STRIDE * 256), lambda t: (t, 0)),
        k_spec, k_spec, vT_spec, vT_spec,
        tok(2 * LRU_W), tok(2 * CV_W),
        pl.BlockSpec((None, GATE_ROWS, TM), lambda t: (t // nt, 0, t % nt)),
    )
    return pl.pallas_call(
        functools.partial(_in_proj_kernel, nt=nt),
        out_shape=out_shape,
        grid=(T // TM,),
        in_specs=[tok(D), const(g), const(w), const(bd), const(qgain), const(kgain)],
        out_specs=out_specs,
        scratch_shapes=[pltpu.VMEM((2, TM, LANES), F32)],
        compiler_params=pltpu.CompilerParams(
            dimension_semantics=("arbitrary",), vmem_limit_bytes=VMEM_LIMIT),
        name="in_proj",
    )(x2, g, w, bd, qgain, kgain)


def _compress_kernel(x_ref, pos_ref, wbig_ref, w2_ref, bd_ref, kgain_ref, kc_ref, vcT_ref):
    nc = x_ref.shape[0]
    half = 4 * CMP_HID
    p = _dot(x_ref[...], wbig_ref[...])
    pc = _dot(pos_ref[...], wbig_ref[...])
    const = pc[0:1, 0:half] + pc[1:2, half:2 * half]
    nxt = pltpu.roll(p[:, half:2 * half], nc - 1, 0)
    hid = _gelu_tanh(p[:, 0:half] + nxt + const).astype(BF16)
    kv = _dot(hid, w2_ref[...])
    kc = kv[:, 0:128]
    kc = kc * _head_rms_scale(kc, bd_ref[...]) * kgain_ref[...]
    lo = lax.broadcasted_iota(jnp.int32, (nc, LANES), 1) < DH
    kc_ref[0] = jnp.where(lo, kc, 0.0).astype(BF16)
    kc_ref[1] = jnp.where(lo, pltpu.roll(kc, DH, 1), 0.0).astype(BF16)
    vT = kv[:, 128:256].T
    for h in range(HKV):
        vcT_ref[h] = vT[h * DH:(h + 1) * DH, :].astype(BF16)


def _compress(cmpx, pos2, wbig, w2big, bd128, kgain, B, NC):
    const = lambda a: pl.BlockSpec(a.shape, lambda b: (0,) * a.ndim, pipeline_mode=pl.Buffered(1))
    return pl.pallas_call(
        _compress_kernel,
        out_shape=(jax.ShapeDtypeStruct((B, HKV, NC, LANES), BF16),
                   jax.ShapeDtypeStruct((B, HKV, DH, NC), BF16)),
        grid=(B,),
        in_specs=[pl.BlockSpec((None, NC, cmpx.shape[2]), lambda b: (b, 0, 0)),
                  const(pos2), const(wbig), const(w2big), const(bd128), const(kgain)],
        out_specs=(pl.BlockSpec((None, HKV, NC, LANES), lambda b: (b, 0, 0, 0)),
                   pl.BlockSpec((None, HKV, DH, NC), lambda b: (b, 0, 0, 0))),
        compiler_params=pltpu.CompilerParams(
            dimension_semantics=("arbitrary",), vmem_limit_bytes=VMEM_LIMIT),
        name="compress",
    )(cmpx, pos2, wbig, w2big, bd128, kgain)


def _attn_kernel(sc_ref, q_ref, kc_ref, vcT_ref, ks_ref, vsT_ref, kw_ref, vwT_ref, gT_ref,
                 y_ref, qT_ref, imp_ref, selb_ref, selo_ref, oc_ref, os_ref, ow_ref, acc_ref, *, cch):
    h = pl.program_id(1)
    i = pl.program_id(2)
    nc = kc_ref.shape[0]
    ns = selb_ref.shape[0]
    wspan = WINDOW + QB
    t0 = i * QB
    iota = lambda shape, ax: lax.broadcasted_iota(jnp.int32, shape, ax)
    use_bound = sc_ref[1] > 0.5
    shift = jnp.where(use_bound, -sc_ref[0], 0.0)

    qfT = q_ref[...].astype(F32).T
    for g in range(G):
        qT_ref[0:DH, g * QB:(g + 1) * QB] = qfT[g * DH:(g + 1) * DH, :].astype(BF16)
    qT_ref[DH:LANES, :] = jnp.zeros((LANES - DH, G * QB), BF16)

    def window(bounded):
        wc = jnp.maximum(i * (QB // LANES) - WINDOW // LANES, 0)
        ws = wc * LANES
        sw = _dot(kw_ref[pl.ds(pl.multiple_of(ws, LANES), wspan), :], qT_ref[...])
        kpos = ws + iota((wspan, QB), 0)
        t_w = t0 + iota((wspan, QB), 1)
        inside = -sc_ref[3] if bounded else 0.0
        wb = jnp.where(kpos <= t_w, jnp.where(kpos > t_w - WINDOW, inside, NEG), NEG)
        sw = sw + jnp.concatenate([wb] * G, axis=1)
        p_w = jnp.exp2(sw if bounded else sw - jnp.max(sw, axis=0, keepdims=True)).astype(BF16)
        vw = jnp.concatenate([vwT_ref[wc + u] for u in range(wspan // LANES)], axis=1)
        acc_w = _dot(vw, p_w)
        ow_ref[...] = acc_w[0:DH, :] * (1.0 / acc_w[DH:DH + 1, :])

    def front(nr, bounded):
        window(bounded)
        nsu = nr // (SLC_BLOCK // CMP_STRIDE)
        cmask = iota((nr, QB), 0) * CMP_STRIDE + (CMP_BLOCK - 1) <= t0 + iota((nr, QB), 1)
        cb = jnp.where(cmask, -sc_ref[2] if bounded else 0.0, NEG)
        s = _dot(kc_ref[0:nr, :], qT_ref[...]) + jnp.concatenate([cb] * G, axis=1)
        p = jnp.exp2(s if bounded else s - jnp.max(s, axis=0, keepdims=True))
        l = jnp.sum(p, axis=0, keepdims=True)
        anyv = jnp.where(t0 + iota((1, QB), 1) >= CMP_BLOCK - 1, 1.0, 0.0)
        pn = p * jnp.where(jnp.concatenate([anyv] * G, axis=1) > 0.5, 1.0 / l, 0.0)
        oc_ref[...] = _dot(vcT_ref[:, 0:nr], pn.astype(BF16))
        imp = pn[:, 0:QB]
        for g in range(1, G):
            imp = imp + pn[:, g * QB:(g + 1) * QB]

        parts = []
        for w in range(QB // LANES):
            imp_ref[w, 0:8, :] = jnp.zeros((8, LANES), F32)
            imp_ref[w, 8:8 + nr, :] = imp[:, w * LANES:(w + 1) * LANES]
            acc = imp_ref[w, pl.ds(7, nsu, stride=4), :]
            for r in range(4):
                acc = acc + imp_ref[w, pl.ds(8 + r, nsu, stride=4), :]
            parts.append(acc)
        islc = jnp.concatenate(parts, axis=1)

        j_i = iota((nsu, QB), 0)
        t_s = t0 + iota((nsu, QB), 1)
        cur = lax.shift_right_logical(t_s, 6)
        valid = j_i * SLC_BLOCK <= t_s
        ninf = -jnp.inf
        score = jnp.where(j_i == 0, ninf, jnp.where(j_i == cur, ninf,
                                                    jnp.where(j_i == cur - 1, ninf, islc)))
        score = jnp.where(valid, score, NEG)
        j_f = j_i.astype(F32)
        for _ in range(N_SEL - 3):
            mx = jnp.max(score, axis=0, keepdims=True)
            jm = jnp.min(jnp.where(score == mx, j_f, float(nsu)), axis=0, keepdims=True)
            score = jnp.where(j_f == jm, -jnp.inf, score)
        own = lax.shift_right_logical(j_i, (QB // SLC_BLOCK).bit_length() - 1) == i
        picked = jnp.where(valid, jnp.where(score == -jnp.inf, shift, NEG), NEG)
        selo_ref[0:nsu, :] = picked
        selb_ref[0:nsu, :] = jnp.where(own, NEG, picked)
        pad = -nsu % min(BIAS_BLOCKS, ns)
        if pad:
            selb_ref[nsu:nsu + pad, :] = jnp.full((pad, QB), NEG, F32)

    cls = (i * (QB // CMP_STRIDE) + (QB - CMP_BLOCK) // CMP_STRIDE) // cch
    for kk in range(nc // cch):
        for bounded in (True, False):
            pl.when(jnp.logical_and(cls == kk, use_bound == bounded))(
                functools.partial(front, (kk + 1) * cch, bounded))

    bpt = QB // SLC_BLOCK
    own_b = jnp.concatenate(
        [jnp.broadcast_to(selo_ref[pl.ds(i * bpt + b, 1), :], (SLC_BLOCK, QB)) for b in range(bpt)],
        axis=0)
    own_b = jnp.where(iota((QB, QB), 0) <= iota((QB, QB), 1), own_b, NEG)
    own_v = jnp.concatenate([vsT_ref[i * (QB // LANES) + w] for w in range(QB // LANES)], axis=1)
    sd = (_dot(ks_ref[pl.ds(pl.multiple_of(t0, QB), QB), :], qT_ref[...])
          + jnp.concatenate([own_b] * G, axis=1))

    nv = KCH // LANES
    cpg = BIAS_BLOCKS // (KCH // SLC_BLOCK)
    brows = min(BIAS_BLOCKS, ns)
    nstep = (i + NCHUNK - 1) // NCHUNK

    def chunk_scores(c0, n=NCHUNK):
        @pl.when(c0 % cpg == 0)
        def _():
            r0 = pl.multiple_of((c0 // cpg) * brows, brows)
            rows = selb_ref[pl.ds(r0, brows), :].astype(BF16)
            qT_ref[DH:DH + brows, :] = jnp.concatenate([rows] * G, axis=1)

        qT = qT_ref[...]
        return [_dot(ks_ref[pl.ds(pl.multiple_of((c0 + u) * KCH, KCH), KCH), :], qT)
                for u in range(n)]

    def values(c):
        return jnp.concatenate([vsT_ref[c * nv + w] for w in range(nv)], axis=1)

    def finish(acc):
        os_ref[...] = acc[0:DH, :] * (1.0 / acc[DH:DH + 1, :])

    @pl.when(use_bound)
    def _():
        def add_chunks(c0, n, acc):
            r0 = pl.multiple_of((c0 // cpg) * brows, brows)
            rows = selb_ref[pl.ds(r0, brows), :].astype(BF16)
            qT_ref[DH:DH + brows, :] = jnp.concatenate([rows] * G, axis=1)
            qT = qT_ref[...]
            for sub in range(0, n, NCHUNK):
                cs = [c0 + sub + u for u in range(min(NCHUNK, n - sub))]
                ss = [_dot(ks_ref[pl.ds(pl.multiple_of(c * KCH, KCH), KCH), :], qT) for c in cs]
                for c, s in zip(cs, ss):
                    acc = acc + _dot(values(c), jnp.exp2(s).astype(BF16))
            return acc

        nfull = i // cpg
        rem = i - nfull * cpg
        acc0 = _dot(own_v, jnp.exp2(sd).astype(BF16))
        acc_ref[...] = lax.fori_loop(
            0, nfull, lambda grp, acc: add_chunks(grp * cpg, cpg, acc), acc0)

        half = jnp.where(rem >= NCHUNK, NCHUNK, 0)
        c1 = nfull * cpg + half
        rest = rem - half

        @pl.when(half > 0)
        def _():
            acc_ref[...] = add_chunks(nfull * cpg, NCHUNK, acc_ref[...])

        @pl.when(jnp.logical_and(rest > 0, rest <= NCHUNK // 2))
        def _():
            acc_ref[...] = add_chunks(c1, NCHUNK // 2, acc_ref[...])

        @pl.when(rest > NCHUNK // 2)
        def _():
            acc_ref[...] = add_chunks(c1, NCHUNK, acc_ref[...])

        finish(acc_ref[...])

    @pl.when(jnp.logical_not(use_bound))
    def _():
        def step(grp, carry):
            m, acc = carry
            ss = chunk_scores(grp * NCHUNK)
            for u in range(NCHUNK):
                s = ss[u]
                m_new = jnp.maximum(m, jnp.max(s, axis=0, keepdims=True))
                p = jnp.exp2(s - m_new).astype(BF16)
                acc = jnp.exp2(m - m_new) * acc + _dot(values(grp * NCHUNK + u), p)
                m = m_new
            return m, acc

        m0 = jnp.max(sd, axis=0, keepdims=True)
        acc0 = _dot(own_v, jnp.exp2(sd - m0).astype(BF16))
        finish(lax.fori_loop(0, nstep, step, (m0, acc0))[1])

    os_ = os_ref[...]
    ow = ow_ref[...]

    outs = []
    for g in range(G):
        sl = slice(g * QB, (g + 1) * QB)
        row = h * (3 * G) + 3 * g
        gc = gT_ref[pl.ds(row, 1), :]
        gs = gT_ref[pl.ds(row + 1, 1), :]
        gw = gT_ref[pl.ds(row + 2, 1), :]
        outs.append(gc * oc_ref[:, sl] + gs * os_[:, sl] + gw * ow[:, sl])
    y_ref[...] = jnp.concatenate(outs, axis=0).T


def _score_bound(qgain, kgain_slc, kgain_win, kgain_cmp):
    qmax = jnp.max(jnp.abs(qgain))
    ms = [(1.05 * DH * qmax * jnp.max(jnp.abs(kg))).astype(BF16).astype(F32)
          for kg in (kgain_slc, kgain_cmp, kgain_win)]
    ok = (jnp.maximum(jnp.maximum(ms[0], ms[1]), ms[2]) <= SAFE_SHIFT).astype(F32)
    return jnp.stack([ms[0], ok, ms[1], ms[2]])


def _attention(sc, q, kc, vcT, kslc, vslcT, kwin, vwinT, gT, B, S):
    NC = S // CMP_STRIDE
    NS = S // SLC_BLOCK
    NQ = S // QB
    cch = min(CCH, NC)
    assert NC % cch == 0 and (S // KCH) % NCHUNK == 0 and QB == KCH
    assert BIAS_BLOCKS * SLC_BLOCK // KCH == 2 * NCHUNK
    head = lambda *blk: pl.BlockSpec((None, None) + blk, lambda b, h, i: (b, h) + (0,) * len(blk),
                                     pipeline_mode=pl.Buffered(1))
    return pl.pallas_call(
        functools.partial(_attn_kernel, cch=cch),
        out_shape=jax.ShapeDtypeStruct((B, S, NSA_W), F32),
        grid=(B, HKV, NQ),
        in_specs=[
            pl.BlockSpec(memory_space=pltpu.SMEM),
            pl.BlockSpec((None, QB, G * DH), lambda b, h, i: (b, i, h)),
            head(NC, LANES), head(DH, NC),
            head(S, LANES), head(S // LANES, VROWS, LANES),
            head(S, LANES), head(S // LANES, VROWS, LANES),
            pl.BlockSpec((None, GATE_ROWS, QB), lambda b, h, i: (b, 0, i)),
        ],
        out_specs=pl.BlockSpec((None, QB, G * DH), lambda b, h, i: (b, i, h)),
        scratch_shapes=[pltpu.VMEM((LANES, G * QB), BF16),
                        pltpu.VMEM((QB // LANES, NC + 8, LANES), F32),
                        pltpu.VMEM((NS, QB), F32), pltpu.VMEM((NS, QB), F32),
                        pltpu.VMEM((DH, G * QB), F32),
                        pltpu.VMEM((DH, G * QB), F32), pltpu.VMEM((DH, G * QB), F32),
                        pltpu.VMEM((VROWS, G * QB), F32)],
        compiler_params=pltpu.CompilerParams(
            dimension_semantics=("arbitrary", "arbitrary", "arbitrary"),
            vmem_limit_bytes=VMEM_LIMIT),
        name="attention",
    )(sc, q, kc, vcT, kslc, vslcT, kwin, vwinT, gT)


XH = 8
YH = 32


def _causal_taps(buf, w_ref, bias, hist, ntaps):
    rows = TS + 8
    base = hist - (ntaps - 1)
    out = jnp.broadcast_to(bias, (TS, buf.shape[1]))
    for r in range(8):
        z = None
        for k in range(ntaps):
            if (base + k) % 8 == r:
                term = w_ref[k:k + 1, :] * buf[base + k - r:base + k - r + rows, :]
                z = term if z is None else z + term
        if z is not None:
            out = out + (z[0:TS, :] if r == 0 else pltpu.roll(z, rows - r, 0)[0:TS, :])
    return out


def _seq_mix_kernel(lru_ref, cv_ref, cw_ref, cb_ref, wa_ref, ba_ref, wx_ref, bx_ref, lam_ref,
                    dw_ref, db_ref, lng_ref, lnb_ref, gl_ref, gc_ref,
                    yl_ref, yc_ref, xbuf, ybuf, hbuf):
    t = pl.program_id(1)

    @pl.when(t == 0)
    def _():
        xbuf[0:XH, :] = jnp.zeros((XH, LRU_W), F32)
        xbuf[XH + TS:XH + TS + 8, :] = jnp.zeros((8, LRU_W), F32)
        ybuf[0:YH, :] = jnp.zeros((YH, CV_W), F32)
        ybuf[YH + TS:YH + TS + 8, :] = jnp.zeros((8, CV_W), F32)
        hbuf[...] = jnp.zeros((8, LRU_W), F32)

    xb = lru_ref[:, 0:LRU_W]
    gb = lru_ref[:, LRU_W:2 * LRU_W]
    xbuf[XH:XH + TS, :] = xb
    xr = _causal_taps(xbuf, cw_ref, cb_ref[...], XH, LRU_CONV)
    xbuf[0:XH, :] = xbuf[TS:TS + XH, :]

    xr16 = xr.astype(BF16)
    r = _sigmoid(_dot(xr16, wa_ref[...]) + ba_ref[...])
    ig = _sigmoid(_dot(xr16, wx_ref[...]) + bx_ref[...])
    nl = -lam_ref[...]
    softplus = jnp.maximum(nl, 0.0) + jnp.log1p(jnp.exp(-jnp.abs(nl)))
    log_a = -LRU_C * r * softplus
    a = jnp.exp(log_a)
    u = xr * ig * jnp.sqrt(-jnp.tanh(log_a) * (a * a + 1.0))

    row = lax.broadcasted_iota(jnp.int32, (TS, LRU_W), 0)
    d = 1
    while d < TS:
        keep = row >= d
        a_sh = jnp.where(keep, pltpu.roll(a, d, 0), 1.0)
        u_sh = jnp.where(keep, pltpu.roll(u, d, 0), 0.0)
        u = a * u_sh + u
        a = a * a_sh
        d *= 2
    hseq = a * hbuf[0:1, :] + u
    hbuf[...] = jnp.broadcast_to(hseq[TS - 1:TS, :], (8, LRU_W))
    yl = hseq * _gelu_tanh(gb)
    yl = yl * lax.rsqrt(jnp.mean(yl * yl, axis=-1, keepdims=True) + EPS) * gl_ref[...]
    yl_ref[...] = yl.astype(BF16)

    y = cv_ref[:, 0:CV_W] * _sigmoid(cv_ref[:, CV_W:2 * CV_W])
    ybuf[YH:YH + TS, :] = y
    c = _causal_taps(ybuf, dw_ref, db_ref[...], YH, CV_KERNEL)
    ybuf[0:YH, :] = ybuf[TS:TS + YH, :]
    mu = jnp.mean(c, axis=-1, keepdims=True)
    cc = c - mu
    var = jnp.mean(cc * cc, axis=-1, keepdims=True)
    ln = cc * lax.rsqrt(var + EPS) * lng_ref[...] + lnb_ref[...]
    yc = ln * _sigmoid(ln)
    yc = yc * lax.rsqrt(jnp.mean(yc * yc, axis=-1, keepdims=True) + EPS) * gc_ref[...]
    yc_ref[...] = yc.astype(BF16)


def _seq_mix(lru, cv, params, B, S):
    const = lambda a: pl.BlockSpec(a.shape, lambda b, t: (0,) * a.ndim)
    tile = lambda w_: pl.BlockSpec((None, TS, w_), lambda b, t: (b, t, 0))
    return pl.pallas_call(
        _seq_mix_kernel,
        out_shape=(jax.ShapeDtypeStruct((B, S, LRU_W), BF16),
                   jax.ShapeDtypeStruct((B, S, CV_W), BF16)),
        grid=(B, S // TS),
        in_specs=[tile(2 * LRU_W), tile(2 * CV_W)] + [const(p) for p in params],
        out_specs=(tile(LRU_W), tile(CV_W)),
        scratch_shapes=[pltpu.VMEM((XH + TS + 8, LRU_W), F32), pltpu.VMEM((YH + TS + 8, CV_W), F32),
                        pltpu.VMEM((8, LRU_W), F32)],
        compiler_params=pltpu.CompilerParams(
            dimension_semantics=("arbitrary", "arbitrary"), vmem_limit_bytes=VMEM_LIMIT),
        name="seq_mix",
    )(lru, cv, *params)


def _out_mlp_kernel(x_ref, ya_ref, yl_ref, yc_ref, ga_ref, wo_ref, gm_ref, w1_ref, w2_ref, o_ref):
    ya = ya_ref[...]
    ya = (ya * lax.rsqrt(jnp.mean(ya * ya, axis=-1, keepdims=True) + EPS) * ga_ref[...]).astype(BF16)
    x1 = (x_ref[...] + _dot(ya, wo_ref[0:NSA_W, :])
          + _dot(yl_ref[...], wo_ref[NSA_W:NSA_W + LRU_W, :])
          + _dot(yc_ref[...], wo_ref[NSA_W + LRU_W:NSA_W + LRU_W + CV_W, :]))
    hm = (x1 * lax.rsqrt(jnp.mean(x1 * x1, axis=-1, keepdims=True) + EPS) * gm_ref[...]).astype(BF16)
    d_ff = w1_ref.shape[1]
    fc = 1024
    o_ref[...] = x1
    for c in range(d_ff // fc):
        hc = jnp.maximum(_dot(hm, w1_ref[:, c * fc:(c + 1) * fc]), 0.0)
        o_ref[...] += _dot((hc * hc).astype(BF16), w2_ref[c * fc:(c + 1) * fc, :])


def _out_mlp(x2, ya, yl, yc, ga, wo, gm, w1, w2):
    T, D = x2.shape
    tok = lambda w_: pl.BlockSpec((TM, w_), lambda t: (t, 0))
    const1 = lambda a: pl.BlockSpec(a.shape, lambda t: (0,) * a.ndim, pipeline_mode=pl.Buffered(1))
    return pl.pallas_call(
        _out_mlp_kernel,
        out_shape=jax.ShapeDtypeStruct((T, D), F32),
        grid=(T // TM,),
        in_specs=[tok(D), tok(NSA_W), tok(LRU_W), tok(CV_W),
                  const1(ga), const1(wo), const1(gm), const1(w1), const1(w2)],
        out_specs=tok(D),
        compiler_params=pltpu.CompilerParams(
            dimension_semantics=("arbitrary",), vmem_limit_bytes=VMEM_LIMIT),
        name="out_mlp",
    )(x2, ya, yl, yc, ga, wo, gm, w1, w2)


def _block_ones(n):
    idx = np.arange(n) // DH
    return jnp.asarray((idx[:, None] == idx[None, :]).astype(np.float32) / DH, dtype=BF16)


def _permute_w_in(w):
    off_kv = NSA_W
    kv = [w[:, off_kv + c * 128: off_kv + (c + 1) * 128] for c in range(6)]
    off_gate = off_kv + 6 * 128
    ngate = 3 * HKV * G
    off_lru = off_gate + ngate
    gate = jnp.pad(w[:, off_gate:off_lru], ((0, 0), (0, LANES - ngate)))
    cols = [w[:, 0:NSA_W], kv[0], kv[1], kv[2], kv[4], kv[3], kv[5],
            w[:, off_lru:off_lru + 2 * LRU_W + 2 * CV_W], gate]
    return jnp.concatenate(cols, axis=1).astype(BF16)


def _compress_weights(pos, w1, w2):
    slot_kv = jnp.array([0, 0, 1, 1])
    eye = jnp.eye(4, dtype=F32)
    w1r = w1.reshape(2, 2, CMP_STRIDE, DH, CMP_HID)
    slot = jnp.arange(4).reshape(1, 4, 1, 1)
    wbig = jnp.concatenate(
        [jnp.where(slot == t, w1r[t // 2, half][:, None, :, :], 0.0).reshape(CMP_STRIDE * 4 * DH, CMP_HID)
         for half in range(2) for t in range(4)], axis=1)
    posr = pos.reshape(2, 2, CMP_STRIDE, DH)[slot_kv]
    pos2 = jnp.transpose(posr, (1, 2, 0, 3)).reshape(2, CMP_STRIDE * 4 * DH)
    pos2 = jnp.pad(pos2, ((0, 6), (0, 0)))
    w2big = jnp.einsum('sod,st->sotd', w2[slot_kv], eye).reshape(4 * CMP_HID, 4 * DH)
    return pos2.astype(BF16), wbig.astype(BF16), w2big.astype(BF16)


def _block_diag(w):
    hh, bw, _ = w.shape
    eye = jnp.eye(hh, dtype=w.dtype)
    return jnp.einsum('hij,hg->higj', w, eye).reshape(hh * bw, hh * bw).astype(BF16)


def kernel(x, attn_norm, w_in, q_norm, k_norm, cmp_pos, cmp_w1, cmp_w2, lru_conv_w, lru_conv_b,
           lru_wa, lru_ba, lru_wx, lru_bx, lru_lambda, cv_dw_w, cv_dw_b, cv_ln_g, cv_ln_b,
           out_norm, w_out, mlp_norm, mlp_w1, mlp_w2):
    B, S, D = x.shape
    depth = w_in.shape[0]
    assert S % TM == 0 and S % TS == 0 and S >= WINDOW + QB and S // SLC_BLOCK >= N_SEL
    NC = S // CMP_STRIDE
    row = lambda v: v.reshape(1, -1).astype(F32)
    bd512 = _block_ones(NSA_W)
    bd128 = _block_ones(128)

    x2 = x.reshape(B * S, D)
    for l in range(depth):
        qgain = row(jnp.tile(q_norm[l], HKV * G)) * (DH ** -0.5 * LOG2E)
        kgain = row(jnp.concatenate([jnp.tile(k_norm[l, 1], HKV), jnp.tile(k_norm[l, 2], HKV)]))
        q, cmpr, kslc, kwin, vslcT, vwinT, lru, cv, gT = _in_proj(
            x2, row(attn_norm[l]), _permute_w_in(w_in[l]), bd512, qgain, kgain, B, S)

        pos2, wbig, w2big = _compress_weights(cmp_pos[l], cmp_w1[l], cmp_w2[l])
        kc, vcT = _compress(cmpr.reshape(B, NC, CMP_STRIDE * 256), pos2, wbig, w2big, bd128,
                            row(jnp.tile(k_norm[l, 0], HKV)), B, NC)

        sc = _score_bound(qgain, k_norm[l, 1], k_norm[l, 2], k_norm[l, 0])
        y_attn = _attention(sc, q.reshape(B, S, NSA_W), kc, vcT, kslc, vslcT, kwin, vwinT, gT, B, S)

        g_out = out_norm[l]
        seq_params = (lru_conv_w[l], row(lru_conv_b[l]), _block_diag(lru_wa[l]), row(lru_ba[l]),
                      _block_diag(lru_wx[l]), row(lru_bx[l]), row(lru_lambda[l]),
                      cv_dw_w[l], row(cv_dw_b[l]), row(cv_ln_g[l]), row(cv_ln_b[l]),
                      row(g_out[NSA_W:NSA_W + LRU_W]), row(g_out[NSA_W + LRU_W:]))
        yl, yc = _seq_mix(lru.reshape(B, S, 2 * LRU_W), cv.reshape(B, S, 2 * CV_W), seq_params, B, S)

        x2 = _out_mlp(x2, y_attn.reshape(B * S, NSA_W), yl.reshape(B * S, LRU_W),
                      yc.reshape(B * S, CV_W), row(g_out[:NSA_W]), w_out[l].astype(BF16),
                      row(mlp_norm[l]), mlp_w1[l].astype(BF16), mlp_w2[l].astype(BF16))
    return x2.reshape(B, S, D)
```

```python
import functools

import numpy as np
import jax
import jax.numpy as jnp
from jax import lax
from jax.experimental import pallas as pl
from jax.experimental.pallas import tpu as pltpu

F32 = jnp.float32
BF16 = jnp.bfloat16

EPS = 1e-6
NEG = -1e30
DH = 64
HKV = 2
G = 4
NSA_W = HKV * G * DH
CMP_BLOCK = 32
CMP_STRIDE = 16
CMP_HID = 128
SLC_BLOCK = 64
N_SEL = 16
WINDOW = 512
QB = 256
LRU_W = 256
LRU_HEADS = 8
LRU_CONV = 4
LRU_C = 8.0
CV_W = 256
CV_KERNEL = 31
LOG2E = 1.4426950408889634
SAFE_SHIFT = 50.0

LANES = 128
KCH = 256
NCHUNK = 8
CCH = 256
TM = 512
TS = 512
VMEM_LIMIT = 48 * 1024 * 1024

C_Q = 0
C_CMP = C_Q + NSA_W
C_KSLC = C_CMP + 256
C_KWIN = C_KSLC + 128
C_VSLC = C_KWIN + 128
C_VWIN = C_VSLC + 128
C_LRU = C_VWIN + 128
C_CV = C_LRU + 2 * LRU_W
C_GATE = C_CV + 2 * CV_W
N_INP = C_GATE + LANES
GATE_ROWS = 32
VROWS = DH + 16
BIAS_BLOCKS = LANES - DH


def _dot(a, b):
    return jnp.dot(a, b, preferred_element_type=F32)


def _head_rms_scale(z, bd):
    sq = z * z
    hi = sq.astype(BF16)
    lo = (sq - hi.astype(F32)).astype(BF16)
    ms = _dot(hi, bd) + _dot(lo, bd)
    return lax.rsqrt(ms + EPS)


def _gelu_tanh(x):
    return 0.5 * x * (1.0 + jnp.tanh(0.7978845608028654 * (x + 0.044715 * (x * x * x))))


def _sigmoid(x):
    return jax.nn.sigmoid(x)


def _in_proj_kernel(x_ref, g_ref, w_ref, bd_ref, qgain_ref, kgain_ref,
                    q_ref, cmp_ref, kslc_ref, kwin_ref, vslcT_ref, vwinT_ref,
                    lru_ref, cv_ref, gT_ref, cmpbuf, *, nt):
    x = x_ref[...]
    ms = jnp.mean(x * x, axis=-1, keepdims=True)
    hn = (x * lax.rsqrt(ms + EPS) * g_ref[...]).astype(BF16)

    zq = _dot(hn, w_ref[:, C_Q:C_CMP])
    rs = jnp.concatenate([_head_rms_scale(zq[:, c:c + 256], bd_ref[...]) for c in (0, 256)], axis=1)
    q_ref[...] = (zq * rs * qgain_ref[...]).astype(BF16)

    zc = _dot(hn, w_ref[:, C_CMP:C_KSLC])
    for w in range(2):
        cmpbuf[w] = zc[:, w * LANES:(w + 1) * LANES]
    for p in range(CMP_STRIDE):
        for w in range(2):
            col = p * 256 + w * LANES
            cmp_ref[:, col:col + LANES] = cmpbuf[w, pl.ds(p, TM // CMP_STRIDE, stride=CMP_STRIDE),
                                                 :].astype(BF16)

    zk = _dot(hn, w_ref[:, C_KSLC:C_VSLC])
    kn = zk * _head_rms_scale(zk, bd_ref[...]) * kgain_ref[...]
    lane = lax.broadcasted_iota(jnp.int32, (TM, LANES), 1)
    rowg = (pl.program_id(0) % nt) * TM + lax.broadcasted_iota(jnp.int32, (TM, LANES), 0)
    blk = lax.shift_right_logical(rowg, 6) & (BIAS_BLOCKS - 1)
    onehot = jnp.where(lane == DH + blk, 1.0, 0.0)
    lo = lane < DH
    ks, kw = kn[:, 0:128], kn[:, 128:256]
    kslc_ref[0] = jnp.where(lo, ks, onehot).astype(BF16)
    kslc_ref[1] = jnp.where(lo, pltpu.roll(ks, DH, 1), onehot).astype(BF16)
    kwin_ref[0] = jnp.where(lo, kw, 0.0).astype(BF16)
    kwin_ref[1] = jnp.where(lo, pltpu.roll(kw, DH, 1), 0.0).astype(BF16)

    zv = _dot(hn, w_ref[:, C_VSLC:C_LRU])
    zvT = zv.T
    ones_row = jnp.where(lax.broadcasted_iota(jnp.int32, (VROWS - DH, LANES), 0) == 0,
                         1.0, 0.0).astype(BF16)
    for j in range(TM // LANES):
        for h in range(HKV):
            cols = slice(j * LANES, (j + 1) * LANES)
            vslcT_ref[h, j, 0:DH, :] = zvT[h * DH:(h + 1) * DH, cols].astype(BF16)
            vslcT_ref[h, j, DH:VROWS, :] = ones_row
            vwinT_ref[h, j, 0:DH, :] = zvT[128 + h * DH:128 + (h + 1) * DH, cols].astype(BF16)
            vwinT_ref[h, j, DH:VROWS, :] = ones_row

    lru_ref[...] = _dot(hn, w_ref[:, C_LRU:C_CV])
    cv_ref[...] = _dot(hn, w_ref[:, C_CV:C_GATE])

    zg = _sigmoid(_dot(hn, w_ref[:, C_GATE:N_INP]))
    gT_ref[...] = zg.T[0:GATE_ROWS, :]


def _in_proj(x2, g, w, bd, qgain, kgain, B, S):
    T, D = x2.shape
    nt = S // TM
    tok = lambda w_: pl.BlockSpec((TM, w_), lambda t: (t, 0))
    const = lambda a: pl.BlockSpec(a.shape, lambda t: (0,) * a.ndim)
    vT_spec = pl.BlockSpec((None, HKV, TM // LANES, VROWS, LANES),
                           lambda t: (t // nt, 0, t % nt, 0, 0))
    k_spec = pl.BlockSpec((None, HKV, TM, LANES), lambda t: (t // nt, 0, t % nt, 0))
    out_shape = (
        jax.ShapeDtypeStruct((T, NSA_W), BF16),
        jax.ShapeDtypeStruct((T // CMP_STRIDE, CMP_STRIDE * 256), BF16),
        jax.ShapeDtypeStruct((B, HKV, S, LANES), BF16),
        jax.ShapeDtypeStruct((B, HKV, S, LANES), BF16),
        jax.ShapeDtypeStruct((B, HKV, S // LANES, VROWS, LANES), BF16),
        jax.ShapeDtypeStruct((B, HKV, S // LANES, VROWS, LANES), BF16),
        jax.ShapeDtypeStruct((T, 2 * LRU_W), F32),
        jax.ShapeDtypeStruct((T, 2 * CV_W), F32),
        jax.ShapeDtypeStruct((B, GATE_ROWS, S), F32),
    )
    out_specs = (
        tok(NSA_W), pl.BlockSpec((TM // CMP_STRIDE, CMP_STRIDE * 256), lambda t: (t, 0)),
        k_spec, k_spec, vT_spec, vT_spec,
        tok(2 * LRU_W), tok(2 * CV_W),
        pl.BlockSpec((None, GATE_ROWS, TM), lambda t: (t // nt, 0, t % nt)),
    )
    return pl.pallas_call(
        functools.partial(_in_proj_kernel, nt=nt),
        out_shape=out_shape,
        grid=(T // TM,),
        in_specs=[tok(D), const(g), const(w), const(bd), const(qgain), const(kgain)],
        out_specs=out_specs,
        scratch_shapes=[pltpu.VMEM((2, TM, LANES), F32)],
        compiler_params=pltpu.CompilerParams(
            dimension_semantics=("arbitrary",), vmem_limit_bytes=VMEM_LIMIT),
        name="in_proj",
    )(x2, g, w, bd, qgain, kgain)


def _compress_kernel(x_ref, pos_ref, wbig_ref, w2_ref, bd_ref, kgain_ref, kc_ref, vcT_ref):
    nc = x_ref.shape[0]
    half = 4 * CMP_HID
    p = _dot(x_ref[...], wbig_ref[...])
    pc = _dot(pos_ref[...], wbig_ref[...])
    const = pc[0:1, 0:half] + pc[1:2, half:2 * half]
    nxt = pltpu.roll(p[:, half:2 * half], nc - 1, 0)
    hid = _gelu_tanh(p[:, 0:half] + nxt + const).astype(BF16)
    kv = _dot(hid, w2_ref[...])
    kc = kv[:, 0:128]
    kc = kc * _head_rms_scale(kc, bd_ref[...]) * kgain_ref[...]
    lo = lax.broadcasted_iota(jnp.int32, (nc, LANES), 1) < DH
    kc_ref[0] = jnp.where(lo, kc, 0.0).astype(BF16)
    kc_ref[1] = jnp.where(lo, pltpu.roll(kc, DH, 1), 0.0).astype(BF16)
    vT = kv[:, 128:256].T
    for h in range(HKV):
        vcT_ref[h] = vT[h * DH:(h + 1) * DH, :].astype(BF16)


def _compress(cmpx, pos2, wbig, w2big, bd128, kgain, B, NC):
    const = lambda a: pl.BlockSpec(a.shape, lambda b: (0,) * a.ndim, pipeline_mode=pl.Buffered(1))
    return pl.pallas_call(
        _compress_kernel,
        out_shape=(jax.ShapeDtypeStruct((B, HKV, NC, LANES), BF16),
                   jax.ShapeDtypeStruct((B, HKV, DH, NC), BF16)),
        grid=(B,),
        in_specs=[pl.BlockSpec((None, NC, cmpx.shape[2]), lambda b: (b, 0, 0)),
                  const(pos2), const(wbig), const(w2big), const(bd128), const(kgain)],
        out_specs=(pl.BlockSpec((None, HKV, NC, LANES), lambda b: (b, 0, 0, 0)),
                   pl.BlockSpec((None, HKV, DH, NC), lambda b: (b, 0, 0, 0))),
        compiler_params=pltpu.CompilerParams(
            dimension_semantics=("arbitrary",), vmem_limit_bytes=VMEM_LIMIT),
        name="compress",
    )(cmpx, pos2, wbig, w2big, bd128, kgain)


def _attn_kernel(sc_ref, q_ref, kc_ref, vcT_ref, ks_ref, vsT_ref, kw_ref, vwT_ref, gT_ref,
                 y_ref, qT_ref, imp_ref, selb_ref, selo_ref, oc_ref, os_ref, ow_ref, acc_ref, *, cch):
    h = pl.program_id(1)
    i = pl.program_id(2)
    nc = kc_ref.shape[0]
    ns = selb_ref.shape[0]
    wspan = WINDOW + QB
    t0 = i * QB
    iota = lambda shape, ax: lax.broadcasted_iota(jnp.int32, shape, ax)
    use_bound = sc_ref[1] > 0.5
    shift = jnp.where(use_bound, -sc_ref[0], 0.0)

    qfT = q_ref[...].astype(F32).T
    for g in range(G):
        qT_ref[0:DH, g * QB:(g + 1) * QB] = qfT[g * DH:(g + 1) * DH, :].astype(BF16)
    qT_ref[DH:LANES, :] = jnp.zeros((LANES - DH, G * QB), BF16)

    def window(bounded):
        wc = jnp.maximum(i * (QB // LANES) - WINDOW // LANES, 0)
        ws = wc * LANES
        sw = _dot(kw_ref[pl.ds(pl.multiple_of(ws, LANES), wspan), :], qT_ref[...])
        kpos = ws + iota((wspan, QB), 0)
        t_w = t0 + iota((wspan, QB), 1)
        inside = -sc_ref[3] if bounded else 0.0
        wb = jnp.where(kpos <= t_w, jnp.where(kpos > t_w - WINDOW, inside, NEG), NEG)
        sw = sw + jnp.concatenate([wb] * G, axis=1)
        p_w = jnp.exp2(sw if bounded else sw - jnp.max(sw, axis=0, keepdims=True)).astype(BF16)
        vw = jnp.concatenate([vwT_ref[wc + u] for u in range(wspan // LANES)], axis=1)
        acc_w = _dot(vw, p_w)
        ow_ref[...] = acc_w[0:DH, :] * (1.0 / acc_w[DH:DH + 1, :])

    def front(nr, bounded):
        window(bounded)
        nsu = nr // (SLC_BLOCK // CMP_STRIDE)
        cmask = iota((nr, QB), 0) * CMP_STRIDE + (CMP_BLOCK - 1) <= t0 + iota((nr, QB), 1)
        cb = jnp.where(cmask, -sc_ref[2] if bounded else 0.0, NEG)
        s = _dot(kc_ref[0:nr, :], qT_ref[...]) + jnp.concatenate([cb] * G, axis=1)
        p = jnp.exp2(s if bounded else s - jnp.max(s, axis=0, keepdims=True))
        l = jnp.sum(p, axis=0, keepdims=True)
        anyv = jnp.where(t0 + iota((1, QB), 1) >= CMP_BLOCK - 1, 1.0, 0.0)
        pn = p * jnp.where(jnp.concatenate([anyv] * G, axis=1) > 0.5, 1.0 / l, 0.0)
        oc_ref[...] = _dot(vcT_ref[:, 0:nr], pn.astype(BF16))
        imp = pn[:, 0:QB]
        for g in range(1, G):
            imp = imp + pn[:, g * QB:(g + 1) * QB]

        parts = []
        for w in range(QB // LANES):
            imp_ref[w, 0:8, :] = jnp.zeros((8, LANES), F32)
            imp_ref[w, 8:8 + nr, :] = imp[:, w * LANES:(w + 1) * LANES]
            acc = imp_ref[w, pl.ds(7, nsu, stride=4), :]
            for r in range(4):
                acc = acc + imp_ref[w, pl.ds(8 + r, nsu, stride=4), :]
            parts.append(acc)
        islc = jnp.concatenate(parts, axis=1)

        j_i = iota((nsu, QB), 0)
        t_s = t0 + iota((nsu, QB), 1)
        cur = lax.shift_right_logical(t_s, 6)
        valid = j_i * SLC_BLOCK <= t_s
        ninf = -jnp.inf
        score = jnp.where(j_i == 0, ninf, jnp.where(j_i == cur, ninf,
                                                    jnp.where(j_i == cur - 1, ninf, islc)))
        score = jnp.where(valid, score, NEG)
        j_f = j_i.astype(F32)
        for _ in range(N_SEL - 3):
            mx = jnp.max(score, axis=0, keepdims=True)
            jm = jnp.min(jnp.where(score == mx, j_f, float(nsu)), axis=0, keepdims=True)
            score = jnp.where(j_f == jm, -jnp.inf, score)
        own = lax.shift_right_logical(j_i, (QB // SLC_BLOCK).bit_length() - 1) == i
        picked = jnp.where(valid, jnp.where(score == -jnp.inf, shift, NEG), NEG)
        selo_ref[0:nsu, :] = picked
        selb_ref[0:nsu, :] = jnp.where(own, NEG, picked)
        pad = -nsu % min(BIAS_BLOCKS, ns)
        if pad:
            selb_ref[nsu:nsu + pad, :] = jnp.full((pad, QB), NEG, F32)

    cls = (i * (QB // CMP_STRIDE) + (QB - CMP_BLOCK) // CMP_STRIDE) // cch
    for kk in range(nc // cch):
        for bounded in (True, False):
            pl.when(jnp.logical_and(cls == kk, use_bound == bounded))(
                functools.partial(front, (kk + 1) * cch, bounded))

    bpt = QB // SLC_BLOCK
    own_b = jnp.concatenate(
        [jnp.broadcast_to(selo_ref[pl.ds(i * bpt + b, 1), :], (SLC_BLOCK, QB)) for b in range(bpt)],
        axis=0)
    own_b = jnp.where(iota((QB, QB), 0) <= iota((QB, QB), 1), own_b, NEG)
    own_v = jnp.concatenate([vsT_ref[i * (QB // LANES) + w] for w in range(QB // LANES)], axis=1)
    sd = (_dot(ks_ref[pl.ds(pl.multiple_of(t0, QB), QB), :], qT_ref[...])
          + jnp.concatenate([own_b] * G, axis=1))

    nv = KCH // LANES
    cpg = BIAS_BLOCKS // (KCH // SLC_BLOCK)
    brows = min(BIAS_BLOCKS, ns)
    nstep = (i + NCHUNK - 1) // NCHUNK

    def chunk_scores(c0, n=NCHUNK):
        @pl.when(c0 % cpg == 0)
        def _():
            r0 = pl.multiple_of((c0 // cpg) * brows, brows)
            rows = selb_ref[pl.ds(r0, brows), :].astype(BF16)
            qT_ref[DH:DH + brows, :] = jnp.concatenate([rows] * G, axis=1)

        qT = qT_ref[...]
        return [_dot(ks_ref[pl.ds(pl.multiple_of((c0 + u) * KCH, KCH), KCH), :], qT)
                for u in range(n)]

    def values(c):
        return jnp.concatenate([vsT_ref[c * nv + w] for w in range(nv)], axis=1)

    def finish(acc):
        os_ref[...] = acc[0:DH, :] * (1.0 / acc[DH:DH + 1, :])

    @pl.when(use_bound)
    def _():
        def add_chunks(c0, n, acc):
            for gs in range(0, n, cpg):
                r0 = pl.multiple_of(((c0 + gs) // cpg) * brows, brows)
                rows = selb_ref[pl.ds(r0, brows), :].astype(BF16)
                qT_ref[DH:DH + brows, :] = jnp.concatenate([rows] * G, axis=1)
                qT = qT_ref[...]
                for sub in range(gs, min(gs + cpg, n), NCHUNK):
                    cs = [c0 + sub + u for u in range(min(NCHUNK, n - sub))]
                    ss = [_dot(ks_ref[pl.ds(pl.multiple_of(c * KCH, KCH), KCH), :], qT) for c in cs]
                    for c, s in zip(cs, ss):
                        acc = acc + _dot(values(c), jnp.exp2(s).astype(BF16))
            return acc

        big = 2 * cpg
        nfull = i // big
        acc0 = _dot(own_v, jnp.exp2(sd).astype(BF16))
        acc_ref[...] = lax.fori_loop(
            0, nfull, lambda grp, acc: add_chunks(grp * big, big, acc), acc0)

        grp1 = jnp.where(i - nfull * big >= cpg, cpg, 0)
        c0 = nfull * big + grp1
        rem = i - c0
        half = jnp.where(rem >= NCHUNK, NCHUNK, 0)
        c1 = c0 + half
        rest = rem - half

        @pl.when(grp1 > 0)
        def _():
            acc_ref[...] = add_chunks(nfull * big, cpg, acc_ref[...])

        @pl.when(half > 0)
        def _():
            acc_ref[...] = add_chunks(c0, NCHUNK, acc_ref[...])

        @pl.when(jnp.logical_and(rest > 0, rest <= NCHUNK // 2))
        def _():
            acc_ref[...] = add_chunks(c1, NCHUNK // 2, acc_ref[...])

        @pl.when(rest > NCHUNK // 2)
        def _():
            acc_ref[...] = add_chunks(c1, NCHUNK, acc_ref[...])

        finish(acc_ref[...])

    @pl.when(jnp.logical_not(use_bound))
    def _():
        def step(grp, carry):
            m, acc = carry
            ss = chunk_scores(grp * NCHUNK)
            for u in range(NCHUNK):
                s = ss[u]
                m_new = jnp.maximum(m, jnp.max(s, axis=0, keepdims=True))
                p = jnp.exp2(s - m_new).astype(BF16)
                acc = jnp.exp2(m - m_new) * acc + _dot(values(grp * NCHUNK + u), p)
                m = m_new
            return m, acc

        m0 = jnp.max(sd, axis=0, keepdims=True)
        acc0 = _dot(own_v, jnp.exp2(sd - m0).astype(BF16))
        finish(lax.fori_loop(0, nstep, step, (m0, acc0))[1])

    os_ = os_ref[...]
    ow = ow_ref[...]

    outs = []
    for g in range(G):
        sl = slice(g * QB, (g + 1) * QB)
        row = h * (3 * G) + 3 * g
        gc = gT_ref[pl.ds(row, 1), :]
        gs = gT_ref[pl.ds(row + 1, 1), :]
        gw = gT_ref[pl.ds(row + 2, 1), :]
        outs.append(gc * oc_ref[:, sl] + gs * os_[:, sl] + gw * ow[:, sl])
    y_ref[...] = jnp.concatenate(outs, axis=0).T


def _score_bound(qgain, kgain_slc, kgain_win, kgain_cmp):
    qmax = jnp.max(jnp.abs(qgain))
    ms = [(1.05 * DH * qmax * jnp.max(jnp.abs(kg))).astype(BF16).astype(F32)
          for kg in (kgain_slc, kgain_cmp, kgain_win)]
    ok = (jnp.maximum(jnp.maximum(ms[0], ms[1]), ms[2]) <= SAFE_SHIFT).astype(F32)
    return jnp.stack([ms[0], ok, ms[1], ms[2]])


def _attention(sc, q, kc, vcT, kslc, vslcT, kwin, vwinT, gT, B, S):
    NC = S // CMP_STRIDE
    NS = S // SLC_BLOCK
    NQ = S // QB
    cch = min(CCH, NC)
    assert NC % cch == 0 and (S // KCH) % NCHUNK == 0 and QB == KCH
    assert BIAS_BLOCKS * SLC_BLOCK // KCH == 2 * NCHUNK
    head = lambda *blk: pl.BlockSpec((None, None) + blk, lambda b, h, i: (b, h) + (0,) * len(blk),
                                     pipeline_mode=pl.Buffered(1))
    return pl.pallas_call(
        functools.partial(_attn_kernel, cch=cch),
        out_shape=jax.ShapeDtypeStruct((B, S, NSA_W), F32),
        grid=(B, HKV, NQ),
        in_specs=[
            pl.BlockSpec(memory_space=pltpu.SMEM),
            pl.BlockSpec((None, QB, G * DH), lambda b, h, i: (b, i, h)),
            head(NC, LANES), head(DH, NC),
            head(S, LANES), head(S // LANES, VROWS, LANES),
            head(S, LANES), head(S // LANES, VROWS, LANES),
            pl.BlockSpec((None, GATE_ROWS, QB), lambda b, h, i: (b, 0, i)),
        ],
        out_specs=pl.BlockSpec((None, QB, G * DH), lambda b, h, i: (b, i, h)),
        scratch_shapes=[pltpu.VMEM((LANES, G * QB), BF16),
                        pltpu.VMEM((QB // LANES, NC + 8, LANES), F32),
                        pltpu.VMEM((NS, QB), F32), pltpu.VMEM((NS, QB), F32),
                        pltpu.VMEM((DH, G * QB), F32),
                        pltpu.VMEM((DH, G * QB), F32), pltpu.VMEM((DH, G * QB), F32),
                        pltpu.VMEM((VROWS, G * QB), F32)],
        compiler_params=pltpu.CompilerParams(
            dimension_semantics=("arbitrary", "arbitrary", "arbitrary"),
            vmem_limit_bytes=VMEM_LIMIT),
        name="attention",
    )(sc, q, kc, vcT, kslc, vslcT, kwin, vwinT, gT)


XH = 8
YH = 32


def _causal_taps(buf, w_ref, bias, hist, ntaps):
    rows = TS + 8
    base = hist - (ntaps - 1)
    out = jnp.broadcast_to(bias, (TS, buf.shape[1]))
    for r in range(8):
        z = None
        for k in range(ntaps):
            if (base + k) % 8 == r:
                term = w_ref[k:k + 1, :] * buf[base + k - r:base + k - r + rows, :]
                z = term if z is None else z + term
        if z is not None:
            out = out + (z[0:TS, :] if r == 0 else pltpu.roll(z, rows - r, 0)[0:TS, :])
    return out


def _seq_mix_kernel(lru_ref, cv_ref, cw_ref, cb_ref, wa_ref, ba_ref, wx_ref, bx_ref, lam_ref,
                    dw_ref, db_ref, lng_ref, lnb_ref, gl_ref, gc_ref,
                    yl_ref, yc_ref, xbuf, ybuf, hbuf):
    t = pl.program_id(1)

    @pl.when(t == 0)
    def _():
        xbuf[0:XH, :] = jnp.zeros((XH, LRU_W), F32)
        xbuf[XH + TS:XH + TS + 8, :] = jnp.zeros((8, LRU_W), F32)
        ybuf[0:YH, :] = jnp.zeros((YH, CV_W), F32)
        ybuf[YH + TS:YH + TS + 8, :] = jnp.zeros((8, CV_W), F32)
        hbuf[...] = jnp.zeros((8, LRU_W), F32)

    xb = lru_ref[:, 0:LRU_W]
    gb = lru_ref[:, LRU_W:2 * LRU_W]
    xbuf[XH:XH + TS, :] = xb
    xr = _causal_taps(xbuf, cw_ref, cb_ref[...], XH, LRU_CONV)
    xbuf[0:XH, :] = xbuf[TS:TS + XH, :]

    xr16 = xr.astype(BF16)
    r = _sigmoid(_dot(xr16, wa_ref[...]) + ba_ref[...])
    ig = _sigmoid(_dot(xr16, wx_ref[...]) + bx_ref[...])
    nl = -lam_ref[...]
    softplus = jnp.maximum(nl, 0.0) + jnp.log1p(jnp.exp(-jnp.abs(nl)))
    log_a = -LRU_C * r * softplus
    a = jnp.exp(log_a)
    u = xr * ig * jnp.sqrt(-jnp.tanh(log_a) * (a * a + 1.0))

    row = lax.broadcasted_iota(jnp.int32, (TS, LRU_W), 0)
    d = 1
    while d < TS:
        keep = row >= d
        a_sh = jnp.where(keep, pltpu.roll(a, d, 0), 1.0)
        u_sh = jnp.where(keep, pltpu.roll(u, d, 0), 0.0)
        u = a * u_sh + u
        a = a * a_sh
        d *= 2
    hseq = a * hbuf[0:1, :] + u
    hbuf[...] = jnp.broadcast_to(hseq[TS - 1:TS, :], (8, LRU_W))
    yl = hseq * _gelu_tanh(gb)
    yl = yl * lax.rsqrt(jnp.mean(yl * yl, axis=-1, keepdims=True) + EPS) * gl_ref[...]
    yl_ref[...] = yl.astype(BF16)

    y = cv_ref[:, 0:CV_W] * _sigmoid(cv_ref[:, CV_W:2 * CV_W])
    ybuf[YH:YH + TS, :] = y
    c = _causal_taps(ybuf, dw_ref, db_ref[...], YH, CV_KERNEL)
    ybuf[0:YH, :] = ybuf[TS:TS + YH, :]
    mu = jnp.mean(c, axis=-1, keepdims=True)
    cc = c - mu
    var = jnp.mean(cc * cc, axis=-1, keepdims=True)
    ln = cc * lax.rsqrt(var + EPS) * lng_ref[...] + lnb_ref[...]
    yc = ln * _sigmoid(ln)
    yc = yc * lax.rsqrt(jnp.mean(yc * yc, axis=-1, keepdims=True) + EPS) * gc_ref[...]
    yc_ref[...] = yc.astype(BF16)


def _seq_mix(lru, cv, params, B, S):
    const = lambda a: pl.BlockSpec(a.shape, lambda b, t: (0,) * a.ndim)
    tile = lambda w_: pl.BlockSpec((None, TS, w_), lambda b, t: (b, t, 0))
    return pl.pallas_call(
        _seq_mix_kernel,
        out_shape=(jax.ShapeDtypeStruct((B, S, LRU_W), BF16),
                   jax.ShapeDtypeStruct((B, S, CV_W), BF16)),
        grid=(B, S // TS),
        in_specs=[tile(2 * LRU_W), tile(2 * CV_W)] + [const(p) for p in params],
        out_specs=(tile(LRU_W), tile(CV_W)),
        scratch_shapes=[pltpu.VMEM((XH + TS + 8, LRU_W), F32), pltpu.VMEM((YH + TS + 8, CV_W), F32),
                        pltpu.VMEM((8, LRU_W), F32)],
        compiler_params=pltpu.CompilerParams(
            dimension_semantics=("arbitrary", "arbitrary"), vmem_limit_bytes=VMEM_LIMIT),
        name="seq_mix",
    )(lru, cv, *params)


def _out_mlp_kernel(x_ref, ya_ref, yl_ref, yc_ref, ga_ref, wo_ref, gm_ref, w1_ref, w2_ref, o_ref):
    ya = ya_ref[...]
    ya = (ya * lax.rsqrt(jnp.mean(ya * ya, axis=-1, keepdims=True) + EPS) * ga_ref[...]).astype(BF16)
    x1 = (x_ref[...] + _dot(ya, wo_ref[0:NSA_W, :])
          + _dot(yl_ref[...], wo_ref[NSA_W:NSA_W + LRU_W, :])
          + _dot(yc_ref[...], wo_ref[NSA_W + LRU_W:NSA_W + LRU_W + CV_W, :]))
    hm = (x1 * lax.rsqrt(jnp.mean(x1 * x1, axis=-1, keepdims=True) + EPS) * gm_ref[...]).astype(BF16)
    d_ff = w1_ref.shape[1]
    fc = 1024
    o_ref[...] = x1
    for c in range(d_ff // fc):
        hc = jnp.maximum(_dot(hm, w1_ref[:, c * fc:(c + 1) * fc]), 0.0)
        o_ref[...] += _dot((hc * hc).astype(BF16), w2_ref[c * fc:(c + 1) * fc, :])


def _out_mlp(x2, ya, yl, yc, ga, wo, gm, w1, w2):
    T, D = x2.shape
    tok = lambda w_: pl.BlockSpec((TM, w_), lambda t: (t, 0))
    const1 = lambda a: pl.BlockSpec(a.shape, lambda t: (0,) * a.ndim, pipeline_mode=pl.Buffered(1))
    return pl.pallas_call(
        _out_mlp_kernel,
        out_shape=jax.ShapeDtypeStruct((T, D), F32),
        grid=(T // TM,),
        in_specs=[tok(D), tok(NSA_W), tok(LRU_W), tok(CV_W),
                  const1(ga), const1(wo), const1(gm), const1(w1), const1(w2)],
        out_specs=tok(D),
        compiler_params=pltpu.CompilerParams(
            dimension_semantics=("arbitrary",), vmem_limit_bytes=VMEM_LIMIT),
        name="out_mlp",
    )(x2, ya, yl, yc, ga, wo, gm, w1, w2)


def _block_ones(n):
    idx = np.arange(n) // DH
    return jnp.asarray((idx[:, None] == idx[None, :]).astype(np.float32) / DH, dtype=BF16)


def _permute_w_in(w):
    off_kv = NSA_W
    kv = [w[:, off_kv + c * 128: off_kv + (c + 1) * 128] for c in range(6)]
    off_gate = off_kv + 6 * 128
    ngate = 3 * HKV * G
    off_lru = off_gate + ngate
    gate = jnp.pad(w[:, off_gate:off_lru], ((0, 0), (0, LANES - ngate)))
    cols = [w[:, 0:NSA_W], kv[0], kv[1], kv[2], kv[4], kv[3], kv[5],
            w[:, off_lru:off_lru + 2 * LRU_W + 2 * CV_W], gate]
    return jnp.concatenate(cols, axis=1).astype(BF16)


def _compress_weights(pos, w1, w2):
    slot_kv = jnp.array([0, 0, 1, 1])
    eye = jnp.eye(4, dtype=F32)
    w1r = w1.reshape(2, 2, CMP_STRIDE, DH, CMP_HID)
    slot = jnp.arange(4).reshape(1, 4, 1, 1)
    wbig = jnp.concatenate(
        [jnp.where(slot == t, w1r[t // 2, half][:, None, :, :], 0.0).reshape(CMP_STRIDE * 4 * DH, CMP_HID)
         for half in range(2) for t in range(4)], axis=1)
    posr = pos.reshape(2, 2, CMP_STRIDE, DH)[slot_kv]
    pos2 = jnp.transpose(posr, (1, 2, 0, 3)).reshape(2, CMP_STRIDE * 4 * DH)
    pos2 = jnp.pad(pos2, ((0, 6), (0, 0)))
    w2big = jnp.einsum('sod,st->sotd', w2[slot_kv], eye).reshape(4 * CMP_HID, 4 * DH)
    return pos2.astype(BF16), wbig.astype(BF16), w2big.astype(BF16)


def _block_diag(w):
    hh, bw, _ = w.shape
    eye = jnp.eye(hh, dtype=w.dtype)
    return jnp.einsum('hij,hg->higj', w, eye).reshape(hh * bw, hh * bw).astype(BF16)


def kernel(x, attn_norm, w_in, q_norm, k_norm, cmp_pos, cmp_w1, cmp_w2, lru_conv_w, lru_conv_b,
           lru_wa, lru_ba, lru_wx, lru_bx, lru_lambda, cv_dw_w, cv_dw_b, cv_ln_g, cv_ln_b,
           out_norm, w_out, mlp_norm, mlp_w1, mlp_w2):
    B, S, D = x.shape
    depth = w_in.shape[0]
    assert S % TM == 0 and S % TS == 0 and S >= WINDOW + QB and S // SLC_BLOCK >= N_SEL
    NC = S // CMP_STRIDE
    row = lambda v: v.reshape(1, -1).astype(F32)
    bd256 = _block_ones(256)
    bd128 = _block_ones(128)

    x2 = x.reshape(B * S, D)
    for l in range(depth):
        qgain = row(jnp.tile(q_norm[l], HKV * G)) * (DH ** -0.5 * LOG2E)
        kgain = row(jnp.concatenate([jnp.tile(k_norm[l, 1], HKV), jnp.tile(k_norm[l, 2], HKV)]))
        q, cmpr, kslc, kwin, vslcT, vwinT, lru, cv, gT = _in_proj(
            x2, row(attn_norm[l]), _permute_w_in(w_in[l]), bd256, qgain, kgain, B, S)

        pos2, wbig, w2big = _compress_weights(cmp_pos[l], cmp_w1[l], cmp_w2[l])
        kc, vcT = _compress(cmpr.reshape(B, NC, CMP_STRIDE * 256), pos2, wbig, w2big, bd128,
                            row(jnp.tile(k_norm[l, 0], HKV)), B, NC)

        sc = _score_bound(qgain, k_norm[l, 1], k_norm[l, 2], k_norm[l, 0])
        y_attn = _attention(sc, q.reshape(B, S, NSA_W), kc, vcT, kslc, vslcT, kwin, vwinT, gT, B, S)

        g_out = out_norm[l]
        seq_params = (lru_conv_w[l], row(lru_conv_b[l]), _block_diag(lru_wa[l]), row(lru_ba[l]),
                      _block_diag(lru_wx[l]), row(lru_bx[l]), row(lru_lambda[l]),
                      cv_dw_w[l], row(cv_dw_b[l]), row(cv_ln_g[l]), row(cv_ln_b[l]),
                      row(g_out[NSA_W:NSA_W + LRU_W]), row(g_out[NSA_W + LRU_W:]))
        yl, yc = _seq_mix(lru.reshape(B, S, 2 * LRU_W), cv.reshape(B, S, 2 * CV_W), seq_params, B, S)

        x2 = _out_mlp(x2, y_attn.reshape(B * S, NSA_W), yl.reshape(B * S, LRU_W),
                      yc.reshape(B * S, CV_W), row(g_out[:NSA_W]), w_out[l].astype(BF16),
                      row(mlp_norm[l]), mlp_w1[l].astype(BF16), mlp_w2[l].astype(BF16))
    return x2.reshape(B, S, D)
```

```python
import functools

import numpy as np
import jax
import jax.numpy as jnp
from jax import lax
from jax.experimental import pallas as pl
from jax.experimental.pallas import tpu as pltpu

F32 = jnp.float32
BF16 = jnp.bfloat16

EPS = 1e-6
NEG = -1e30
DH = 64
HKV = 2
G = 4
NSA_W = HKV * G * DH
CMP_BLOCK = 32
CMP_STRIDE = 16
CMP_HID = 128
SLC_BLOCK = 64
N_SEL = 16
WINDOW = 512
QB = 256
LRU_W = 256
LRU_HEADS = 8
LRU_CONV = 4
LRU_C = 8.0
CV_W = 256
CV_KERNEL = 31
LOG2E = 1.4426950408889634
SAFE_SHIFT = 50.0

LANES = 128
KCH = 256
NCHUNK = 8
CCH = 256
TM = 512
TS = 512
VMEM_LIMIT = 48 * 1024 * 1024

C_Q = 0
C_CMP = C_Q + NSA_W
C_KSLC = C_CMP + 256
C_KWIN = C_KSLC + 128
C_VSLC = C_KWIN + 128
C_VWIN = C_VSLC + 128
C_LRU = C_VWIN + 128
C_CV = C_LRU + 2 * LRU_W
C_GATE = C_CV + 2 * CV_W
N_INP = C_GATE + LANES
GATE_ROWS = 32
VROWS = DH + 16
BIAS_BLOCKS = LANES - DH


def _dot(a, b):
    return jnp.dot(a, b, preferred_element_type=F32)


def _head_rms_scale(z, bd):
    sq = z * z
    hi = sq.astype(BF16)
    lo = (sq - hi.astype(F32)).astype(BF16)
    ms = _dot(hi, bd) + _dot(lo, bd)
    return lax.rsqrt(ms + EPS)


def _gelu_tanh(x):
    return 0.5 * x * (1.0 + jnp.tanh(0.7978845608028654 * (x + 0.044715 * (x * x * x))))


def _sigmoid(x):
    return jax.nn.sigmoid(x)


def _in_proj_kernel(x_ref, g_ref, w_ref, bd_ref, qgain_ref, kgain_ref,
                    q_ref, cmp_ref, kslc_ref, kwin_ref, vslcT_ref, vwinT_ref,
                    lru_ref, cv_ref, gT_ref, cmpbuf, *, nt):
    x = x_ref[...]
    ms = jnp.mean(x * x, axis=-1, keepdims=True)
    hn = (x * lax.rsqrt(ms + EPS) * g_ref[...]).astype(BF16)

    zq = _dot(hn, w_ref[:, C_Q:C_CMP])
    rs = jnp.concatenate([_head_rms_scale(zq[:, c:c + 256], bd_ref[...]) for c in (0, 256)], axis=1)
    q_ref[...] = (zq * rs * qgain_ref[...]).astype(BF16)

    zc = _dot(hn, w_ref[:, C_CMP:C_KSLC])
    for w in range(2):
        cmpbuf[w] = zc[:, w * LANES:(w + 1) * LANES]
    for p in range(CMP_STRIDE):
        for w in range(2):
            col = p * 256 + w * LANES
            cmp_ref[:, col:col + LANES] = cmpbuf[w, pl.ds(p, TM // CMP_STRIDE, stride=CMP_STRIDE),
                                                 :].astype(BF16)

    zk = _dot(hn, w_ref[:, C_KSLC:C_VSLC])
    kn = zk * _head_rms_scale(zk, bd_ref[...]) * kgain_ref[...]
    lane = lax.broadcasted_iota(jnp.int32, (TM, LANES), 1)
    rowg = (pl.program_id(0) % nt) * TM + lax.broadcasted_iota(jnp.int32, (TM, LANES), 0)
    blk = lax.shift_right_logical(rowg, 6) & (BIAS_BLOCKS - 1)
    onehot = jnp.where(lane == DH + blk, 1.0, 0.0)
    lo = lane < DH
    ks, kw = kn[:, 0:128], kn[:, 128:256]
    kslc_ref[0] = jnp.where(lo, ks, onehot).astype(BF16)
    kslc_ref[1] = jnp.where(lo, pltpu.roll(ks, DH, 1), onehot).astype(BF16)
    kwin_ref[0] = jnp.where(lo, kw, 0.0).astype(BF16)
    kwin_ref[1] = jnp.where(lo, pltpu.roll(kw, DH, 1), 0.0).astype(BF16)

    zv = _dot(hn, w_ref[:, C_VSLC:C_LRU])
    zvT = zv.T
    ones_row = jnp.where(lax.broadcasted_iota(jnp.int32, (VROWS - DH, LANES), 0) == 0,
                         1.0, 0.0).astype(BF16)
    for j in range(TM // LANES):
        for h in range(HKV):
            cols = slice(j * LANES, (j + 1) * LANES)
            vslcT_ref[h, j, 0:DH, :] = zvT[h * DH:(h + 1) * DH, cols].astype(BF16)
            vslcT_ref[h, j, DH:VROWS, :] = ones_row
            vwinT_ref[h, j, 0:DH, :] = zvT[128 + h * DH:128 + (h + 1) * DH, cols].astype(BF16)
            vwinT_ref[h, j, DH:VROWS, :] = ones_row

    lru_ref[...] = _dot(hn, w_ref[:, C_LRU:C_CV])
    cv_ref[...] = _dot(hn, w_ref[:, C_CV:C_GATE])

    zg = _sigmoid(_dot(hn, w_ref[:, C_GATE:N_INP]))
    gT_ref[...] = zg.T[0:GATE_ROWS, :]


def _in_proj(x2, g, w, bd, qgain, kgain, B, S):
    T, D = x2.shape
    nt = S // TM
    tok = lambda w_: pl.BlockSpec((TM, w_), lambda t: (t, 0))
    const = lambda a: pl.BlockSpec(a.shape, lambda t: (0,) * a.ndim)
    vT_spec = pl.BlockSpec((None, HKV, TM // LANES, VROWS, LANES),
                           lambda t: (t // nt, 0, t % nt, 0, 0))
    k_spec = pl.BlockSpec((None, HKV, TM, LANES), lambda t: (t // nt, 0, t % nt, 0))
    out_shape = (
        jax.ShapeDtypeStruct((T, NSA_W), BF16),
        jax.ShapeDtypeStruct((T // CMP_STRIDE, CMP_STRIDE * 256), BF16),
        jax.ShapeDtypeStruct((B, HKV, S, LANES), BF16),
        jax.ShapeDtypeStruct((B, HKV, S, LANES), BF16),
        jax.ShapeDtypeStruct((B, HKV, S // LANES, VROWS, LANES), BF16),
        jax.ShapeDtypeStruct((B, HKV, S // LANES, VROWS, LANES), BF16),
        jax.ShapeDtypeStruct((T, 2 * LRU_W), F32),
        jax.ShapeDtypeStruct((T, 2 * CV_W), F32),
        jax.ShapeDtypeStruct((B, GATE_ROWS, S), F32),
    )
    out_specs = (
        tok(NSA_W), pl.BlockSpec((TM // CMP_STRIDE, CMP_STRIDE * 256), lambda t: (t, 0)),
        k_spec, k_spec, vT_spec, vT_spec,
        tok(2 * LRU_W), tok(2 * CV_W),
        pl.BlockSpec((None, GATE_ROWS, TM), lambda t: (t // nt, 0, t % nt)),
    )
    return pl.pallas_call(
        functools.partial(_in_proj_kernel, nt=nt),
        out_shape=out_shape,
        grid=(T // TM,),
        in_specs=[tok(D), const(g), const(w), const(bd), const(qgain), const(kgain)],
        out_specs=out_specs,
        scratch_shapes=[pltpu.VMEM((2, TM, LANES), F32)],
        compiler_params=pltpu.CompilerParams(
            dimension_semantics=("arbitrary",), vmem_limit_bytes=VMEM_LIMIT),
        name="in_proj",
    )(x2, g, w, bd, qgain, kgain)


def _compress_kernel(x_ref, pos_ref, wbig_ref, w2_ref, bd_ref, kgain_ref, kc_ref, vcT_ref):
    nc = x_ref.shape[0]
    half = 4 * CMP_HID
    p = _dot(x_ref[...], wbig_ref[...])
    pc = _dot(pos_ref[...], wbig_ref[...])
    const = pc[0:1, 0:half] + pc[1:2, half:2 * half]
    nxt = pltpu.roll(p[:, half:2 * half], nc - 1, 0)
    hid = _gelu_tanh(p[:, 0:half] + nxt + const).astype(BF16)
    kv = _dot(hid, w2_ref[...])
    kc = kv[:, 0:128]
    kc = kc * _head_rms_scale(kc, bd_ref[...]) * kgain_ref[...]
    lo = lax.broadcasted_iota(jnp.int32, (nc, LANES), 1) < DH
    kc_ref[0] = jnp.where(lo, kc, 0.0).astype(BF16)
    kc_ref[1] = jnp.where(lo, pltpu.roll(kc, DH, 1), 0.0).astype(BF16)
    vT = kv[:, 128:256].T
    for h in range(HKV):
        vcT_ref[h] = vT[h * DH:(h + 1) * DH, :].astype(BF16)


def _compress(cmpx, pos2, wbig, w2big, bd128, kgain, B, NC):
    const = lambda a: pl.BlockSpec(a.shape, lambda b: (0,) * a.ndim, pipeline_mode=pl.Buffered(1))
    return pl.pallas_call(
        _compress_kernel,
        out_shape=(jax.ShapeDtypeStruct((B, HKV, NC, LANES), BF16),
                   jax.ShapeDtypeStruct((B, HKV, DH, NC), BF16)),
        grid=(B,),
        in_specs=[pl.BlockSpec((None, NC, cmpx.shape[2]), lambda b: (b, 0, 0)),
                  const(pos2), const(wbig), const(w2big), const(bd128), const(kgain)],
        out_specs=(pl.BlockSpec((None, HKV, NC, LANES), lambda b: (b, 0, 0, 0)),
                   pl.BlockSpec((None, HKV, DH, NC), lambda b: (b, 0, 0, 0))),
        compiler_params=pltpu.CompilerParams(
            dimension_semantics=("arbitrary",), vmem_limit_bytes=VMEM_LIMIT),
        name="compress",
    )(cmpx, pos2, wbig, w2big, bd128, kgain)


def _attn_kernel(sc_ref, q_ref, kc_ref, vcT_ref, ks_ref, vsT_ref, kw_ref, vwT_ref, gT_ref,
                 y_ref, qT_ref, imp_ref, selb_ref, selo_ref, oc_ref, os_ref, ow_ref, acc_ref, *, cch):
    h = pl.program_id(1)
    i = pl.program_id(2)
    nc = kc_ref.shape[0]
    ns = selb_ref.shape[0]
    wspan = WINDOW + QB
    t0 = i * QB
    iota = lambda shape, ax: lax.broadcasted_iota(jnp.int32, shape, ax)
    use_bound = sc_ref[1] > 0.5
    shift = jnp.where(use_bound, -sc_ref[0], 0.0)

    qfT = q_ref[...].astype(F32).T
    for g in range(G):
        qT_ref[0:DH, g * QB:(g + 1) * QB] = qfT[g * DH:(g + 1) * DH, :].astype(BF16)
    qT_ref[DH:LANES, :] = jnp.zeros((LANES - DH, G * QB), BF16)

    def window(bounded):
        wc = jnp.maximum(i * (QB // LANES) - WINDOW // LANES, 0)
        ws = wc * LANES
        sw = _dot(kw_ref[pl.ds(pl.multiple_of(ws, LANES), wspan), :], qT_ref[...])
        kpos = ws + iota((wspan, QB), 0)
        t_w = t0 + iota((wspan, QB), 1)
        inside = -sc_ref[3] if bounded else 0.0
        wb = jnp.where(kpos <= t_w, jnp.where(kpos > t_w - WINDOW, inside, NEG), NEG)
        sw = sw + jnp.concatenate([wb] * G, axis=1)
        p_w = jnp.exp2(sw if bounded else sw - jnp.max(sw, axis=0, keepdims=True)).astype(BF16)
        vw = jnp.concatenate([vwT_ref[wc + u] for u in range(wspan // LANES)], axis=1)
        acc_w = _dot(vw, p_w)
        ow_ref[...] = acc_w[0:DH, :] * (1.0 / acc_w[DH:DH + 1, :])

    def front(nr, bounded):
        window(bounded)
        nsu = nr // (SLC_BLOCK // CMP_STRIDE)
        cmask = iota((nr, QB), 0) * CMP_STRIDE + (CMP_BLOCK - 1) <= t0 + iota((nr, QB), 1)
        cb = jnp.where(cmask, -sc_ref[2] if bounded else 0.0, NEG)
        s = _dot(kc_ref[0:nr, :], qT_ref[...]) + jnp.concatenate([cb] * G, axis=1)
        p = jnp.exp2(s if bounded else s - jnp.max(s, axis=0, keepdims=True))
        l = jnp.sum(p, axis=0, keepdims=True)
        anyv = jnp.where(t0 + iota((1, QB), 1) >= CMP_BLOCK - 1, 1.0, 0.0)
        pn = p * jnp.where(jnp.concatenate([anyv] * G, axis=1) > 0.5, 1.0 / l, 0.0)
        oc_ref[...] = _dot(vcT_ref[:, 0:nr], pn.astype(BF16))
        imp = pn[:, 0:QB]
        for g in range(1, G):
            imp = imp + pn[:, g * QB:(g + 1) * QB]

        parts = []
        for w in range(QB // LANES):
            imp_ref[w, 0:8, :] = jnp.zeros((8, LANES), F32)
            imp_ref[w, 8:8 + nr, :] = imp[:, w * LANES:(w + 1) * LANES]
            acc = imp_ref[w, pl.ds(7, nsu, stride=4), :]
            for r in range(4):
                acc = acc + imp_ref[w, pl.ds(8 + r, nsu, stride=4), :]
            parts.append(acc)
        islc = jnp.concatenate(parts, axis=1)

        j_i = iota((nsu, QB), 0)
        t_s = t0 + iota((nsu, QB), 1)
        cur = lax.shift_right_logical(t_s, 6)
        valid = j_i * SLC_BLOCK <= t_s
        ninf = -jnp.inf
        score = jnp.where(j_i == 0, ninf, jnp.where(j_i == cur, ninf,
                                                    jnp.where(j_i == cur - 1, ninf, islc)))
        score = jnp.where(valid, score, NEG)
        j_f = j_i.astype(F32)
        for _ in range(N_SEL - 3):
            mx = jnp.max(score, axis=0, keepdims=True)
            jm = jnp.min(jnp.where(score == mx, j_f, float(nsu)), axis=0, keepdims=True)
            score = jnp.where(j_f == jm, -jnp.inf, score)
        own = lax.shift_right_logical(j_i, (QB // SLC_BLOCK).bit_length() - 1) == i
        picked = jnp.where(valid, jnp.where(score == -jnp.inf, shift, NEG), NEG)
        selo_ref[0:nsu, :] = picked
        selb_ref[0:nsu, :] = jnp.where(own, NEG, picked)
        pad = -nsu % min(BIAS_BLOCKS, ns)
        if pad:
            selb_ref[nsu:nsu + pad, :] = jnp.full((pad, QB), NEG, F32)

    cls = (i * (QB // CMP_STRIDE) + (QB - CMP_BLOCK) // CMP_STRIDE) // cch
    for kk in range(nc // cch):
        for bounded in (True, False):
            pl.when(jnp.logical_and(cls == kk, use_bound == bounded))(
                functools.partial(front, (kk + 1) * cch, bounded))

    bpt = QB // SLC_BLOCK
    own_b = jnp.concatenate(
        [jnp.broadcast_to(selo_ref[pl.ds(i * bpt + b, 1), :], (SLC_BLOCK, QB)) for b in range(bpt)],
        axis=0)
    own_b = jnp.where(iota((QB, QB), 0) <= iota((QB, QB), 1), own_b, NEG)
    own_v = jnp.concatenate([vsT_ref[i * (QB // LANES) + w] for w in range(QB // LANES)], axis=1)
    sd = (_dot(ks_ref[pl.ds(pl.multiple_of(t0, QB), QB), :], qT_ref[...])
          + jnp.concatenate([own_b] * G, axis=1))

    nv = KCH // LANES
    cpg = BIAS_BLOCKS // (KCH // SLC_BLOCK)
    brows = min(BIAS_BLOCKS, ns)
    nstep = (i + NCHUNK - 1) // NCHUNK

    def chunk_scores(c0, n=NCHUNK):
        @pl.when(c0 % cpg == 0)
        def _():
            r0 = pl.multiple_of((c0 // cpg) * brows, brows)
            rows = selb_ref[pl.ds(r0, brows), :].astype(BF16)
            qT_ref[DH:DH + brows, :] = jnp.concatenate([rows] * G, axis=1)

        qT = qT_ref[...]
        return [_dot(ks_ref[pl.ds(pl.multiple_of((c0 + u) * KCH, KCH), KCH), :], qT)
                for u in range(n)]

    def values(c):
        return jnp.concatenate([vsT_ref[c * nv + w] for w in range(nv)], axis=1)

    def finish(acc):
        os_ref[...] = acc[0:DH, :] * (1.0 / acc[DH:DH + 1, :])

    @pl.when(use_bound)
    def _():
        def add_chunks(c0, n, acc):
            r0 = pl.multiple_of((c0 // cpg) * brows, brows)
            rows = selb_ref[pl.ds(r0, brows), :].astype(BF16)
            qT_ref[DH:DH + brows, :] = jnp.concatenate([rows] * G, axis=1)
            qT = qT_ref[...]
            for sub in range(0, n, NCHUNK):
                cs = [c0 + sub + u for u in range(min(NCHUNK, n - sub))]
                ss = [_dot(ks_ref[pl.ds(pl.multiple_of(c * KCH, KCH), KCH), :], qT) for c in cs]
                for c, s in zip(cs, ss):
                    acc = acc + _dot(values(c), jnp.exp2(s).astype(BF16))
            return acc

        nfull = i // cpg
        rem = i - nfull * cpg
        acc0 = _dot(own_v, jnp.exp2(sd).astype(BF16))
        acc_ref[...] = lax.fori_loop(
            0, nfull, lambda grp, acc: add_chunks(grp * cpg, cpg, acc), acc0)

        half = jnp.where(rem >= NCHUNK, NCHUNK, 0)
        c1 = nfull * cpg + half
        rest = rem - half

        @pl.when(half > 0)
        def _():
            acc_ref[...] = add_chunks(nfull * cpg, NCHUNK, acc_ref[...])

        @pl.when(jnp.logical_and(rest > 0, rest <= NCHUNK // 2))
        def _():
            acc_ref[...] = add_chunks(c1, NCHUNK // 2, acc_ref[...])

        @pl.when(rest > NCHUNK // 2)
        def _():
            acc_ref[...] = add_chunks(c1, NCHUNK, acc_ref[...])

        finish(acc_ref[...])

    @pl.when(jnp.logical_not(use_bound))
    def _():
        def step(grp, carry):
            m, acc = carry
            ss = chunk_scores(grp * NCHUNK)
            for u in range(NCHUNK):
                s = ss[u]
                m_new = jnp.maximum(m, jnp.max(s, axis=0, keepdims=True))
                p = jnp.exp2(s - m_new).astype(BF16)
                acc = jnp.exp2(m - m_new) * acc + _dot(values(grp * NCHUNK + u), p)
                m = m_new
            return m, acc

        m0 = jnp.max(sd, axis=0, keepdims=True)
        acc0 = _dot(own_v, jnp.exp2(sd - m0).astype(BF16))
        finish(lax.fori_loop(0, nstep, step, (m0, acc0))[1])

    os_ = os_ref[...]
    ow = ow_ref[...]

    outs = []
    for g in range(G):
        sl = slice(g * QB, (g + 1) * QB)
        row = h * (3 * G) + 3 * g
        gc = gT_ref[pl.ds(row, 1), :]
        gs = gT_ref[pl.ds(row + 1, 1), :]
        gw = gT_ref[pl.ds(row + 2, 1), :]
        outs.append(gc * oc_ref[:, sl] + gs * os_[:, sl] + gw * ow[:, sl])
    y_ref[...] = jnp.concatenate(outs, axis=0).T


def _score_bound(qgain, kgain_slc, kgain_win, kgain_cmp):
    qmax = jnp.max(jnp.abs(qgain))
    ms = [(1.05 * DH * qmax * jnp.max(jnp.abs(kg))).astype(BF16).astype(F32)
          for kg in (kgain_slc, kgain_cmp, kgain_win)]
    ok = (jnp.maximum(jnp.maximum(ms[0], ms[1]), ms[2]) <= SAFE_SHIFT).astype(F32)
    return jnp.stack([ms[0], ok, ms[1], ms[2]])


def _attention(sc, q, kc, vcT, kslc, vslcT, kwin, vwinT, gT, B, S):
    NC = S // CMP_STRIDE
    NS = S // SLC_BLOCK
    NQ = S // QB
    cch = min(CCH, NC)
    assert NC % cch == 0 and (S // KCH) % NCHUNK == 0 and QB == KCH
    assert BIAS_BLOCKS * SLC_BLOCK // KCH == 2 * NCHUNK
    head = lambda *blk: pl.BlockSpec((None, None) + blk, lambda b, h, i: (b, h) + (0,) * len(blk),
                                     pipeline_mode=pl.Buffered(1))
    return pl.pallas_call(
        functools.partial(_attn_kernel, cch=cch),
        out_shape=jax.ShapeDtypeStruct((B, S, NSA_W), F32),
        grid=(B, HKV, NQ),
        in_specs=[
            pl.BlockSpec(memory_space=pltpu.SMEM),
            pl.BlockSpec((None, QB, G * DH), lambda b, h, i: (b, i, h)),
            head(NC, LANES), head(DH, NC),
            head(S, LANES), head(S // LANES, VROWS, LANES),
            head(S, LANES), head(S // LANES, VROWS, LANES),
            pl.BlockSpec((None, GATE_ROWS, QB), lambda b, h, i: (b, 0, i)),
        ],
        out_specs=pl.BlockSpec((None, QB, G * DH), lambda b, h, i: (b, i, h)),
        scratch_shapes=[pltpu.VMEM((LANES, G * QB), BF16),
                        pltpu.VMEM((QB // LANES, NC + 8, LANES), F32),
                        pltpu.VMEM((NS, QB), F32), pltpu.VMEM((NS, QB), F32),
                        pltpu.VMEM((DH, G * QB), F32),
                        pltpu.VMEM((DH, G * QB), F32), pltpu.VMEM((DH, G * QB), F32),
                        pltpu.VMEM((VROWS, G * QB), F32)],
        compiler_params=pltpu.CompilerParams(
            dimension_semantics=("arbitrary", "arbitrary", "arbitrary"),
            vmem_limit_bytes=VMEM_LIMIT),
        name="attention",
    )(sc, q, kc, vcT, kslc, vslcT, kwin, vwinT, gT)


XH = 8
YH = 32


def _causal_taps(buf, w_ref, bias, hist, ntaps):
    rows = TS + 8
    base = hist - (ntaps - 1)
    out = jnp.broadcast_to(bias, (TS, buf.shape[1]))
    for r in range(8):
        z = None
        for k in range(ntaps):
            if (base + k) % 8 == r:
                term = w_ref[k:k + 1, :] * buf[base + k - r:base + k - r + rows, :]
                z = term if z is None else z + term
        if z is not None:
            out = out + (z[0:TS, :] if r == 0 else pltpu.roll(z, rows - r, 0)[0:TS, :])
    return out


def _seq_mix_kernel(lru_ref, cv_ref, cw_ref, cb_ref, wa_ref, ba_ref, wx_ref, bx_ref, lam_ref,
                    dw_ref, db_ref, lng_ref, lnb_ref, gl_ref, gc_ref,
                    yl_ref, yc_ref, xbuf, ybuf, hbuf):
    t = pl.program_id(1)

    @pl.when(t == 0)
    def _():
        xbuf[0:XH, :] = jnp.zeros((XH, LRU_W), F32)
        xbuf[XH + TS:XH + TS + 8, :] = jnp.zeros((8, LRU_W), F32)
        ybuf[0:YH, :] = jnp.zeros((YH, CV_W), F32)
        ybuf[YH + TS:YH + TS + 8, :] = jnp.zeros((8, CV_W), F32)
        hbuf[...] = jnp.zeros((8, LRU_W), F32)

    xb = lru_ref[:, 0:LRU_W]
    gb = lru_ref[:, LRU_W:2 * LRU_W]
    xbuf[XH:XH + TS, :] = xb
    xr = _causal_taps(xbuf, cw_ref, cb_ref[...], XH, LRU_CONV)
    xbuf[0:XH, :] = xbuf[TS:TS + XH, :]

    xr16 = xr.astype(BF16)
    r = _sigmoid(_dot(xr16, wa_ref[...]) + ba_ref[...])
    ig = _sigmoid(_dot(xr16, wx_ref[...]) + bx_ref[...])
    nl = -lam_ref[...]
    softplus = jnp.maximum(nl, 0.0) + jnp.log1p(jnp.exp(-jnp.abs(nl)))
    log_a = -LRU_C * r * softplus
    a = jnp.exp(log_a)
    u = xr * ig * jnp.sqrt(-jnp.tanh(log_a) * (a * a + 1.0))

    row = lax.broadcasted_iota(jnp.int32, (TS, LRU_W), 0)
    d = 1
    while d < TS:
        keep = row >= d
        a_sh = jnp.where(keep, pltpu.roll(a, d, 0), 1.0)
        u_sh = jnp.where(keep, pltpu.roll(u, d, 0), 0.0)
        u = a * u_sh + u
        a = a * a_sh
        d *= 2
    hseq = a * hbuf[0:1, :] + u
    hbuf[...] = jnp.broadcast_to(hseq[TS - 1:TS, :], (8, LRU_W))
    yl = hseq * _gelu_tanh(gb)
    yl = yl * lax.rsqrt(jnp.mean(yl * yl, axis=-1, keepdims=True) + EPS) * gl_ref[...]
    yl_ref[...] = yl.astype(BF16)

    y = cv_ref[:, 0:CV_W] * _sigmoid(cv_ref[:, CV_W:2 * CV_W])
    ybuf[YH:YH + TS, :] = y
    c = _causal_taps(ybuf, dw_ref, db_ref[...], YH, CV_KERNEL)
    ybuf[0:YH, :] = ybuf[TS:TS + YH, :]
    mu = jnp.mean(c, axis=-1, keepdims=True)
    cc = c - mu
    var = jnp.mean(cc * cc, axis=-1, keepdims=True)
    ln = cc * lax.rsqrt(var + EPS) * lng_ref[...] + lnb_ref[...]
    yc = ln * _sigmoid(ln)
    yc = yc * lax.rsqrt(jnp.mean(yc * yc, axis=-1, keepdims=True) + EPS) * gc_ref[...]
    yc_ref[...] = yc.astype(BF16)


def _seq_mix(lru, cv, params, B, S):
    const = lambda a: pl.BlockSpec(a.shape, lambda b, t: (0,) * a.ndim)
    tile = lambda w_: pl.BlockSpec((None, TS, w_), lambda b, t: (b, t, 0))
    return pl.pallas_call(
        _seq_mix_kernel,
        out_shape=(jax.ShapeDtypeStruct((B, S, LRU_W), BF16),
                   jax.ShapeDtypeStruct((B, S, CV_W), BF16)),
        grid=(B, S // TS),
        in_specs=[tile(2 * LRU_W), tile(2 * CV_W)] + [const(p) for p in params],
        out_specs=(tile(LRU_W), tile(CV_W)),
        scratch_shapes=[pltpu.VMEM((XH + TS + 8, LRU_W), F32), pltpu.VMEM((YH + TS + 8, CV_W), F32),
                        pltpu.VMEM((8, LRU_W), F32)],
        compiler_params=pltpu.CompilerParams(
            dimension_semantics=("arbitrary", "arbitrary"), vmem_limit_bytes=VMEM_LIMIT),
        name="seq_mix",
    )(lru, cv, *params)


def _out_mlp_kernel(x_ref, ya_ref, yl_ref, yc_ref, ga_ref, wo_ref, gm_ref, w1_ref, w2_ref, o_ref):
    ya = ya_ref[...]
    ya = (ya * lax.rsqrt(jnp.mean(ya * ya, axis=-1, keepdims=True) + EPS) * ga_ref[...]).astype(BF16)
    x1 = (x_ref[...] + _dot(ya, wo_ref[0:NSA_W, :])
          + _dot(yl_ref[...], wo_ref[NSA_W:NSA_W + LRU_W, :])
          + _dot(yc_ref[...], wo_ref[NSA_W + LRU_W:NSA_W + LRU_W + CV_W, :]))
    hm = (x1 * lax.rsqrt(jnp.mean(x1 * x1, axis=-1, keepdims=True) + EPS) * gm_ref[...]).astype(BF16)
    d_ff = w1_ref.shape[1]
    fc = 1024
    o_ref[...] = x1
    for c in range(d_ff // fc):
        hc = jnp.maximum(_dot(hm, w1_ref[:, c * fc:(c + 1) * fc]), 0.0)
        o_ref[...] += _dot((hc * hc).astype(BF16), w2_ref[c * fc:(c + 1) * fc, :])


def _out_mlp(x2, ya, yl, yc, ga, wo, gm, w1, w2):
    T, D = x2.shape
    tok = lambda w_: pl.BlockSpec((TM, w_), lambda t: (t, 0))
    const1 = lambda a: pl.BlockSpec(a.shape, lambda t: (0,) * a.ndim, pipeline_mode=pl.Buffered(1))
    return pl.pallas_call(
        _out_mlp_kernel,
        out_shape=jax.ShapeDtypeStruct((T, D), F32),
        grid=(T // TM,),
        in_specs=[tok(D), tok(NSA_W), tok(LRU_W), tok(CV_W),
                  const1(ga), const1(wo), const1(gm), const1(w1), const1(w2)],
        out_specs=tok(D),
        compiler_params=pltpu.CompilerParams(
            dimension_semantics=("arbitrary",), vmem_limit_bytes=VMEM_LIMIT),
        name="out_mlp",
    )(x2, ya, yl, yc, ga, wo, gm, w1, w2)


def _block_ones(n):
    idx = np.arange(n) // DH
    return jnp.asarray((idx[:, None] == idx[None, :]).astype(np.float32) / DH, dtype=BF16)


def _permute_w_in(w):
    off_kv = NSA_W
    kv = [w[:, off_kv + c * 128: off_kv + (c + 1) * 128] for c in range(6)]
    off_gate = off_kv + 6 * 128
    ngate = 3 * HKV * G
    off_lru = off_gate + ngate
    gate = jnp.pad(w[:, off_gate:off_lru], ((0, 0), (0, LANES - ngate)))
    cols = [w[:, 0:NSA_W], kv[0], kv[1], kv[2], kv[4], kv[3], kv[5],
            w[:, off_lru:off_lru + 2 * LRU_W + 2 * CV_W], gate]
    return jnp.concatenate(cols, axis=1).astype(BF16)


def _compress_weights(pos, w1, w2):
    slot_kv = jnp.array([0, 0, 1, 1])
    eye = jnp.eye(4, dtype=F32)
    w1r = w1.reshape(2, 2, CMP_STRIDE, DH, CMP_HID)
    slot = jnp.arange(4).reshape(1, 4, 1, 1)
    wbig = jnp.concatenate(
        [jnp.where(slot == t, w1r[t // 2, half][:, None, :, :], 0.0).reshape(CMP_STRIDE * 4 * DH, CMP_HID)
         for half in range(2) for t in range(4)], axis=1)
    posr = pos.reshape(2, 2, CMP_STRIDE, DH)[slot_kv]
    pos2 = jnp.transpose(posr, (1, 2, 0, 3)).reshape(2, CMP_STRIDE * 4 * DH)
    pos2 = jnp.pad(pos2, ((0, 6), (0, 0)))
    w2big = jnp.einsum('sod,st->sotd', w2[slot_kv], eye).reshape(4 * CMP_HID, 4 * DH)
    return pos2.astype(BF16), wbig.astype(BF16), w2big.astype(BF16)


def _block_diag(w):
    hh, bw, _ = w.shape
    eye = jnp.eye(hh, dtype=w.dtype)
    return jnp.einsum('hij,hg->higj', w, eye).reshape(hh * bw, hh * bw).astype(BF16)


def kernel(x, attn_norm, w_in, q_norm, k_norm, cmp_pos, cmp_w1, cmp_w2, lru_conv_w, lru_conv_b,
           lru_wa, lru_ba, lru_wx, lru_bx, lru_lambda, cv_dw_w, cv_dw_b, cv_ln_g, cv_ln_b,
           out_norm, w_out, mlp_norm, mlp_w1, mlp_w2):
    B, S, D = x.shape
    depth = w_in.shape[0]
    assert S % TM == 0 and S % TS == 0 and S >= WINDOW + QB and S // SLC_BLOCK >= N_SEL
    NC = S // CMP_STRIDE
    row = lambda v: v.reshape(1, -1).astype(F32)
    bd256 = _block_ones(256)
    bd128 = _block_ones(128)

    x2 = x.reshape(B * S, D)
    for l in range(depth):
        qgain = row(jnp.tile(q_norm[l], HKV * G)) * (DH ** -0.5 * LOG2E)
        kgain = row(jnp.concatenate([jnp.tile(k_norm[l, 1], HKV), jnp.tile(k_norm[l, 2], HKV)]))
        q, cmpr, kslc, kwin, vslcT, vwinT, lru, cv, gT = _in_proj(
            x2, row(attn_norm[l]), _permute_w_in(w_in[l]), bd256, qgain, kgain, B, S)

        pos2, wbig, w2big = _compress_weights(cmp_pos[l], cmp_w1[l], cmp_w2[l])
        kc, vcT = _compress(cmpr.reshape(B, NC, CMP_STRIDE * 256), pos2, wbig, w2big, bd128,
                            row(jnp.tile(k_norm[l, 0], HKV)), B, NC)

        sc = _score_bound(qgain, k_norm[l, 1], k_norm[l, 2], k_norm[l, 0])
        y_attn = _attention(sc, q.reshape(B, S, NSA_W), kc, vcT, kslc, vslcT, kwin, vwinT, gT, B, S)

        g_out = out_norm[l]
        seq_params = (lru_conv_w[l], row(lru_conv_b[l]), _block_diag(lru_wa[l]), row(lru_ba[l]),
                      _block_diag(lru_wx[l]), row(lru_bx[l]), row(lru_lambda[l]),
                      cv_dw_w[l], row(cv_dw_b[l]), row(cv_ln_g[l]), row(cv_ln_b[l]),
                      row(g_out[NSA_W:NSA_W + LRU_W]), row(g_out[NSA_W + LRU_W:]))
        yl, yc = _seq_mix(lru.reshape(B, S, 2 * LRU_W), cv.reshape(B, S, 2 * CV_W), seq_params, B, S)

        x2 = _out_mlp(x2, y_attn.reshape(B * S, NSA_W), yl.reshape(B * S, LRU_W),
                      yc.reshape(B * S, CV_W), row(g_out[:NSA_W]), w_out[l].astype(BF16),
                      row(mlp_norm[l]), mlp_w1[l].astype(BF16), mlp_w2[l].astype(BF16))
    return x2.reshape(B, S, D)
```

```python
import functools

import numpy as np
import jax
import jax.numpy as jnp
from jax import lax
from jax.experimental import pallas as pl
from jax.experimental.pallas import tpu as pltpu

F32 = jnp.float32
BF16 = jnp.bfloat16

EPS = 1e-6
NEG = -1e30
DH = 64
HKV = 2
G = 4
NSA_W = HKV * G * DH
CMP_BLOCK = 32
CMP_STRIDE = 16
CMP_HID = 128
SLC_BLOCK = 64
N_SEL = 16
WINDOW = 512
QB = 256
LRU_W = 256
LRU_HEADS = 8
LRU_CONV = 4
LRU_C = 8.0
CV_W = 256
CV_KERNEL = 31
LOG2E = 1.4426950408889634
SAFE_SHIFT = 50.0

LANES = 128
KCH = 256
NCHUNK = 8
CCH = 256
TM = 512
TS = 512
VMEM_LIMIT = 48 * 1024 * 1024

C_Q = 0
C_CMP = C_Q + NSA_W
C_KSLC = C_CMP + 256
C_KWIN = C_KSLC + 128
C_VSLC = C_KWIN + 128
C_VWIN = C_VSLC + 128
C_LRU = C_VWIN + 128
C_CV = C_LRU + 2 * LRU_W
C_GATE = C_CV + 2 * CV_W
N_INP = C_GATE + LANES
GATE_ROWS = 32
VROWS = DH + 16
BIAS_BLOCKS = LANES - DH


def _dot(a, b):
    return jnp.dot(a, b, preferred_element_type=F32)


def _head_rms_scale(z, bd):
    sq = z * z
    hi = sq.astype(BF16)
    lo = (sq - hi.astype(F32)).astype(BF16)
    ms = _dot(hi, bd) + _dot(lo, bd)
    return lax.rsqrt(ms + EPS)


def _gelu_tanh(x):
    return 0.5 * x * (1.0 + jnp.tanh(0.7978845608028654 * (x + 0.044715 * (x * x * x))))


def _sigmoid(x):
    return jax.nn.sigmoid(x)


def _in_proj_kernel(x_ref, g_ref, w_ref, bd_ref, qgain_ref, kgain_ref,
                    q_ref, cmp_ref, kslc_ref, kwin_ref, vslcT_ref, vwinT_ref,
                    lru_ref, cv_ref, gT_ref, cmpbuf, *, nt):
    x = x_ref[...]
    ms = jnp.mean(x * x, axis=-1, keepdims=True)
    hn = (x * lax.rsqrt(ms + EPS) * g_ref[...]).astype(BF16)

    zq = _dot(hn, w_ref[:, C_Q:C_CMP])
    rs = jnp.concatenate([_head_rms_scale(zq[:, c:c + 256], bd_ref[...]) for c in (0, 256)], axis=1)
    q_ref[...] = (zq * rs * qgain_ref[...]).astype(BF16)

    zc = _dot(hn, w_ref[:, C_CMP:C_KSLC])
    for w in range(2):
        cmpbuf[w] = zc[:, w * LANES:(w + 1) * LANES]
    for p in range(CMP_STRIDE):
        for w in range(2):
            col = p * 256 + w * LANES
            cmp_ref[:, col:col + LANES] = cmpbuf[w, pl.ds(p, TM // CMP_STRIDE, stride=CMP_STRIDE),
                                                 :].astype(BF16)

    zk = _dot(hn, w_ref[:, C_KSLC:C_VSLC])
    kn = zk * _head_rms_scale(zk, bd_ref[...]) * kgain_ref[...]
    lane = lax.broadcasted_iota(jnp.int32, (TM, LANES), 1)
    rowg = (pl.program_id(0) % nt) * TM + lax.broadcasted_iota(jnp.int32, (TM, LANES), 0)
    blk = lax.shift_right_logical(rowg, 6) & (BIAS_BLOCKS - 1)
    onehot = jnp.where(lane == DH + blk, 1.0, 0.0)
    lo = lane < DH
    ks, kw = kn[:, 0:128], kn[:, 128:256]
    kslc_ref[0] = jnp.where(lo, ks, onehot).astype(BF16)
    kslc_ref[1] = jnp.where(lo, pltpu.roll(ks, DH, 1), onehot).astype(BF16)
    kwin_ref[0] = jnp.where(lo, kw, 0.0).astype(BF16)
    kwin_ref[1] = jnp.where(lo, pltpu.roll(kw, DH, 1), 0.0).astype(BF16)

    zv = _dot(hn, w_ref[:, C_VSLC:C_LRU])
    zvT = zv.T
    ones_row = jnp.where(lax.broadcasted_iota(jnp.int32, (VROWS - DH, LANES), 0) == 0,
                         1.0, 0.0).astype(BF16)
    for j in range(TM // LANES):
        for h in range(HKV):
            cols = slice(j * LANES, (j + 1) * LANES)
            vslcT_ref[h, j, 0:DH, :] = zvT[h * DH:(h + 1) * DH, cols].astype(BF16)
            vslcT_ref[h, j, DH:VROWS, :] = ones_row
            vwinT_ref[h, j, 0:DH, :] = zvT[128 + h * DH:128 + (h + 1) * DH, cols].astype(BF16)
            vwinT_ref[h, j, DH:VROWS, :] = ones_row

    lru_ref[...] = _dot(hn, w_ref[:, C_LRU:C_CV])
    cv_ref[...] = _dot(hn, w_ref[:, C_CV:C_GATE])

    zg = _sigmoid(_dot(hn, w_ref[:, C_GATE:N_INP]))
    gT_ref[...] = zg.T[0:GATE_ROWS, :]


def _in_proj(x2, g, w, bd, qgain, kgain, B, S):
    T, D = x2.shape
    nt = S // TM
    tok = lambda w_: pl.BlockSpec((TM, w_), lambda t: (t, 0))
    const = lambda a: pl.BlockSpec(a.shape, lambda t: (0,) * a.ndim)
    vT_spec = pl.BlockSpec((None, HKV, TM // LANES, VROWS, LANES),
                           lambda t: (t // nt, 0, t % nt, 0, 0))
    k_spec = pl.BlockSpec((None, HKV, TM, LANES), lambda t: (t // nt, 0, t % nt, 0))
    out_shape = (
        jax.ShapeDtypeStruct((T, NSA_W), BF16),
        jax.ShapeDtypeStruct((T // CMP_STRIDE, CMP_STRIDE * 256), BF16),
        jax.ShapeDtypeStruct((B, HKV, S, LANES), BF16),
        jax.ShapeDtypeStruct((B, HKV, S, LANES), BF16),
        jax.ShapeDtypeStruct((B, HKV, S // LANES, VROWS, LANES), BF16),
        jax.ShapeDtypeStruct((B, HKV, S // LANES, VROWS, LANES), BF16),
        jax.ShapeDtypeStruct((T, 2 * LRU_W), F32),
        jax.ShapeDtypeStruct((T, 2 * CV_W), F32),
        jax.ShapeDtypeStruct((B, GATE_ROWS, S), F32),
    )
    out_specs = (
        tok(NSA_W), pl.BlockSpec((TM // CMP_STRIDE, CMP_STRIDE * 256), lambda t: (t, 0)),
        k_spec, k_spec, vT_spec, vT_spec,
        tok(2 * LRU_W), tok(2 * CV_W),
        pl.BlockSpec((None, GATE_ROWS, TM), lambda t: (t // nt, 0, t % nt)),
    )
    return pl.pallas_call(
        functools.partial(_in_proj_kernel, nt=nt),
        out_shape=out_shape,
        grid=(T // TM,),
        in_specs=[tok(D), const(g), const(w), const(bd), const(qgain), const(kgain)],
        out_specs=out_specs,
        scratch_shapes=[pltpu.VMEM((2, TM, LANES), F32)],
        compiler_params=pltpu.CompilerParams(
            dimension_semantics=("arbitrary",), vmem_limit_bytes=VMEM_LIMIT),
        name="in_proj",
    )(x2, g, w, bd, qgain, kgain)


def _compress_kernel(x_ref, pos_ref, wbig_ref, w2_ref, bd_ref, kgain_ref, kc_ref, vcT_ref):
    nc = x_ref.shape[0]
    half = 4 * CMP_HID
    p = _dot(x_ref[...], wbig_ref[...])
    pc = _dot(pos_ref[...], wbig_ref[...])
    const = pc[0:1, 0:half] + pc[1:2, half:2 * half]
    nxt = pltpu.roll(p[:, half:2 * half], nc - 1, 0)
    hid = _gelu_tanh(p[:, 0:half] + nxt + const).astype(BF16)
    kv = _dot(hid, w2_ref[...])
    kc = kv[:, 0:128]
    kc = kc * _head_rms_scale(kc, bd_ref[...]) * kgain_ref[...]
    lo = lax.broadcasted_iota(jnp.int32, (nc, LANES), 1) < DH
    kc_ref[0] = jnp.where(lo, kc, 0.0).astype(BF16)
    kc_ref[1] = jnp.where(lo, pltpu.roll(kc, DH, 1), 0.0).astype(BF16)
    vT = kv[:, 128:256].T
    for h in range(HKV):
        vcT_ref[h] = vT[h * DH:(h + 1) * DH, :].astype(BF16)


def _compress(cmpx, pos2, wbig, w2big, bd128, kgain, B, NC):
    const = lambda a: pl.BlockSpec(a.shape, lambda b: (0,) * a.ndim, pipeline_mode=pl.Buffered(1))
    return pl.pallas_call(
        _compress_kernel,
        out_shape=(jax.ShapeDtypeStruct((B, HKV, NC, LANES), BF16),
                   jax.ShapeDtypeStruct((B, HKV, DH, NC), BF16)),
        grid=(B,),
        in_specs=[pl.BlockSpec((None, NC, cmpx.shape[2]), lambda b: (b, 0, 0)),
                  const(pos2), const(wbig), const(w2big), const(bd128), const(kgain)],
        out_specs=(pl.BlockSpec((None, HKV, NC, LANES), lambda b: (b, 0, 0, 0)),
                   pl.BlockSpec((None, HKV, DH, NC), lambda b: (b, 0, 0, 0))),
        compiler_params=pltpu.CompilerParams(
            dimension_semantics=("arbitrary",), vmem_limit_bytes=VMEM_LIMIT),
        name="compress",
    )(cmpx, pos2, wbig, w2big, bd128, kgain)


def _attn_kernel(sc_ref, q_ref, kc_ref, vcT_ref, ks_ref, vsT_ref, kw_ref, vwT_ref, gT_ref,
                 y_ref, qT_ref, imp_ref, selb_ref, selo_ref, oc_ref, os_ref, ow_ref, acc_ref, *, cch):
    h = pl.program_id(1)
    i = pl.program_id(2)
    nc = kc_ref.shape[0]
    ns = selb_ref.shape[0]
    wspan = WINDOW + QB
    t0 = i * QB
    iota = lambda shape, ax: lax.broadcasted_iota(jnp.int32, shape, ax)
    use_bound = sc_ref[1] > 0.5
    shift = jnp.where(use_bound, -sc_ref[0], 0.0)

    qfT = q_ref[...].astype(F32).T
    for g in range(G):
        qT_ref[0:DH, g * QB:(g + 1) * QB] = qfT[g * DH:(g + 1) * DH, :].astype(BF16)
    qT_ref[DH:LANES, :] = jnp.zeros((LANES - DH, G * QB), BF16)

    def window(bounded):
        wc = jnp.maximum(i * (QB // LANES) - WINDOW // LANES, 0)
        ws = wc * LANES
        sw = _dot(kw_ref[pl.ds(pl.multiple_of(ws, LANES), wspan), :], qT_ref[...])
        kpos = ws + iota((wspan, QB), 0)
        t_w = t0 + iota((wspan, QB), 1)
        inside = -sc_ref[3] if bounded else 0.0
        wb = jnp.where(kpos <= t_w, jnp.where(kpos > t_w - WINDOW, inside, NEG), NEG)
        sw = sw + jnp.concatenate([wb] * G, axis=1)
        p_w = jnp.exp2(sw if bounded else sw - jnp.max(sw, axis=0, keepdims=True)).astype(BF16)
        vw = jnp.concatenate([vwT_ref[wc + u] for u in range(wspan // LANES)], axis=1)
        acc_w = _dot(vw, p_w)
        ow_ref[...] = acc_w[0:DH, :] * (1.0 / acc_w[DH:DH + 1, :])

    def front(nr, bounded):
        window(bounded)
        nsu = nr // (SLC_BLOCK // CMP_STRIDE)
        cmask = iota((nr, QB), 0) * CMP_STRIDE + (CMP_BLOCK - 1) <= t0 + iota((nr, QB), 1)
        cb = jnp.where(cmask, -sc_ref[2] if bounded else 0.0, NEG)
        anyv = jnp.where(t0 + iota((1, QB), 1) >= CMP_BLOCK - 1, 1.0, 0.0)
        hg = G // 2
        imp = None
        for hh in range(2):
            lanes = slice(hh * hg * QB, (hh + 1) * hg * QB)
            s = _dot(kc_ref[0:nr, :], qT_ref[:, lanes]) + jnp.concatenate([cb] * hg, axis=1)
            p = jnp.exp2(s if bounded else s - jnp.max(s, axis=0, keepdims=True))
            l = jnp.sum(p, axis=0, keepdims=True)
            pn = p * jnp.where(jnp.concatenate([anyv] * hg, axis=1) > 0.5, 1.0 / l, 0.0)
            oc_ref[:, lanes] = _dot(vcT_ref[:, 0:nr], pn.astype(BF16))
            for g in range(hg):
                part = pn[:, g * QB:(g + 1) * QB]
                imp = part if imp is None else imp + part

        parts = []
        for w in range(QB // LANES):
            imp_ref[w, 0:8, :] = jnp.zeros((8, LANES), F32)
            imp_ref[w, 8:8 + nr, :] = imp[:, w * LANES:(w + 1) * LANES]
            acc = imp_ref[w, pl.ds(7, nsu, stride=4), :]
            for r in range(4):
                acc = acc + imp_ref[w, pl.ds(8 + r, nsu, stride=4), :]
            parts.append(acc)
        islc = jnp.concatenate(parts, axis=1)

        j_i = iota((nsu, QB), 0)
        t_s = t0 + iota((nsu, QB), 1)
        cur = lax.shift_right_logical(t_s, 6)
        valid = j_i * SLC_BLOCK <= t_s
        ninf = -jnp.inf
        score = jnp.where(j_i == 0, ninf, jnp.where(j_i == cur, ninf,
                                                    jnp.where(j_i == cur - 1, ninf, islc)))
        score = jnp.where(valid, score, NEG)
        j_f = j_i.astype(F32)
        for _ in range(N_SEL - 3):
            mx = jnp.max(score, axis=0, keepdims=True)
            jm = jnp.min(jnp.where(score == mx, j_f, float(nsu)), axis=0, keepdims=True)
            score = jnp.where(j_f == jm, -jnp.inf, score)
        own = lax.shift_right_logical(j_i, (QB // SLC_BLOCK).bit_length() - 1) == i
        picked = jnp.where(valid, jnp.where(score == -jnp.inf, shift, NEG), NEG)
        selo_ref[0:nsu, :] = picked
        selb_ref[0:nsu, :] = jnp.where(own, NEG, picked)
        pad = -nsu % min(BIAS_BLOCKS, ns)
        if pad:
            selb_ref[nsu:nsu + pad, :] = jnp.full((pad, QB), NEG, F32)

    cls = (i * (QB // CMP_STRIDE) + (QB - CMP_BLOCK) // CMP_STRIDE) // cch
    for kk in range(nc // cch):
        for bounded in (True, False):
            pl.when(jnp.logical_and(cls == kk, use_bound == bounded))(
                functools.partial(front, (kk + 1) * cch, bounded))

    bpt = QB // SLC_BLOCK
    own_b = jnp.concatenate(
        [jnp.broadcast_to(selo_ref[pl.ds(i * bpt + b, 1), :], (SLC_BLOCK, QB)) for b in range(bpt)],
        axis=0)
    own_b = jnp.where(iota((QB, QB), 0) <= iota((QB, QB), 1), own_b, NEG)
    own_v = jnp.concatenate([vsT_ref[i * (QB // LANES) + w] for w in range(QB // LANES)], axis=1)
    sd = (_dot(ks_ref[pl.ds(pl.multiple_of(t0, QB), QB), :], qT_ref[...])
          + jnp.concatenate([own_b] * G, axis=1))

    nv = KCH // LANES
    cpg = BIAS_BLOCKS // (KCH // SLC_BLOCK)
    brows = min(BIAS_BLOCKS, ns)
    nstep = (i + NCHUNK - 1) // NCHUNK

    def chunk_scores(c0, n=NCHUNK):
        @pl.when(c0 % cpg == 0)
        def _():
            r0 = pl.multiple_of((c0 // cpg) * brows, brows)
            rows = selb_ref[pl.ds(r0, brows), :].astype(BF16)
            qT_ref[DH:DH + brows, :] = jnp.concatenate([rows] * G, axis=1)

        qT = qT_ref[...]
        return [_dot(ks_ref[pl.ds(pl.multiple_of((c0 + u) * KCH, KCH), KCH), :], qT)
                for u in range(n)]

    def values(c):
        return jnp.concatenate([vsT_ref[c * nv + w] for w in range(nv)], axis=1)

    def finish(acc):
        os_ref[...] = acc[0:DH, :] * (1.0 / acc[DH:DH + 1, :])

    @pl.when(use_bound)
    def _():
        def add_chunks(c0, n, acc):
            r0 = pl.multiple_of((c0 // cpg) * brows, brows)
            rows = selb_ref[pl.ds(r0, brows), :].astype(BF16)
            qT_ref[DH:DH + brows, :] = jnp.concatenate([rows] * G, axis=1)
            qT = qT_ref[...]
            for sub in range(0, n, NCHUNK):
                cs = [c0 + sub + u for u in range(min(NCHUNK, n - sub))]
                ss = [_dot(ks_ref[pl.ds(pl.multiple_of(c * KCH, KCH), KCH), :], qT) for c in cs]
                for c, s in zip(cs, ss):
                    acc = acc + _dot(values(c), jnp.exp2(s).astype(BF16))
            return acc

        nfull = i // cpg
        rem = i - nfull * cpg
        acc0 = _dot(own_v, jnp.exp2(sd).astype(BF16))
        acc_ref[...] = lax.fori_loop(
            0, nfull, lambda grp, acc: add_chunks(grp * cpg, cpg, acc), acc0)

        half = jnp.where(rem >= NCHUNK, NCHUNK, 0)
        c1 = nfull * cpg + half
        rest = rem - half

        @pl.when(half > 0)
        def _():
            acc_ref[...] = add_chunks(nfull * cpg, NCHUNK, acc_ref[...])

        @pl.when(jnp.logical_and(rest > 0, rest <= NCHUNK // 2))
        def _():
            acc_ref[...] = add_chunks(c1, NCHUNK // 2, acc_ref[...])

        @pl.when(rest > NCHUNK // 2)
        def _():
            acc_ref[...] = add_chunks(c1, NCHUNK, acc_ref[...])

        finish(acc_ref[...])

    @pl.when(jnp.logical_not(use_bound))
    def _():
        def step(grp, carry):
            m, acc = carry
            ss = chunk_scores(grp * NCHUNK)
            for u in range(NCHUNK):
                s = ss[u]
                m_new = jnp.maximum(m, jnp.max(s, axis=0, keepdims=True))
                p = jnp.exp2(s - m_new).astype(BF16)
                acc = jnp.exp2(m - m_new) * acc + _dot(values(grp * NCHUNK + u), p)
                m = m_new
            return m, acc

        m0 = jnp.max(sd, axis=0, keepdims=True)
        acc0 = _dot(own_v, jnp.exp2(sd - m0).astype(BF16))
        finish(lax.fori_loop(0, nstep, step, (m0, acc0))[1])

    os_ = os_ref[...]
    ow = ow_ref[...]

    outs = []
    for g in range(G):
        sl = slice(g * QB, (g + 1) * QB)
        row = h * (3 * G) + 3 * g
        gc = gT_ref[pl.ds(row, 1), :]
        gs = gT_ref[pl.ds(row + 1, 1), :]
        gw = gT_ref[pl.ds(row + 2, 1), :]
        outs.append(gc * oc_ref[:, sl] + gs * os_[:, sl] + gw * ow[:, sl])
    y_ref[...] = jnp.concatenate(outs, axis=0).T


def _score_bound(qgain, kgain_slc, kgain_win, kgain_cmp):
    qmax = jnp.max(jnp.abs(qgain))
    ms = [(1.05 * DH * qmax * jnp.max(jnp.abs(kg))).astype(BF16).astype(F32)
          for kg in (kgain_slc, kgain_cmp, kgain_win)]
    ok = (jnp.maximum(jnp.maximum(ms[0], ms[1]), ms[2]) <= SAFE_SHIFT).astype(F32)
    return jnp.stack([ms[0], ok, ms[1], ms[2]])


def _attention(sc, q, kc, vcT, kslc, vslcT, kwin, vwinT, gT, B, S):
    NC = S // CMP_STRIDE
    NS = S // SLC_BLOCK
    NQ = S // QB
    cch = min(CCH, NC)
    assert NC % cch == 0 and (S // KCH) % NCHUNK == 0 and QB == KCH
    assert BIAS_BLOCKS * SLC_BLOCK // KCH == 2 * NCHUNK
    head = lambda *blk: pl.BlockSpec((None, None) + blk, lambda b, h, i: (b, h) + (0,) * len(blk),
                                     pipeline_mode=pl.Buffered(1))
    return pl.pallas_call(
        functools.partial(_attn_kernel, cch=cch),
        out_shape=jax.ShapeDtypeStruct((B, S, NSA_W), F32),
        grid=(B, HKV, NQ),
        in_specs=[
            pl.BlockSpec(memory_space=pltpu.SMEM),
            pl.BlockSpec((None, QB, G * DH), lambda b, h, i: (b, i, h)),
            head(NC, LANES), head(DH, NC),
            head(S, LANES), head(S // LANES, VROWS, LANES),
            head(S, LANES), head(S // LANES, VROWS, LANES),
            pl.BlockSpec((None, GATE_ROWS, QB), lambda b, h, i: (b, 0, i)),
        ],
        out_specs=pl.BlockSpec((None, QB, G * DH), lambda b, h, i: (b, i, h)),
        scratch_shapes=[pltpu.VMEM((LANES, G * QB), BF16),
                        pltpu.VMEM((QB // LANES, NC + 8, LANES), F32),
                        pltpu.VMEM((NS, QB), F32), pltpu.VMEM((NS, QB), F32),
                        pltpu.VMEM((DH, G * QB), F32),
                        pltpu.VMEM((DH, G * QB), F32), pltpu.VMEM((DH, G * QB), F32),
                        pltpu.VMEM((VROWS, G * QB), F32)],
        compiler_params=pltpu.CompilerParams(
            dimension_semantics=("arbitrary", "arbitrary", "arbitrary"),
            vmem_limit_bytes=VMEM_LIMIT),
        name="attention",
    )(sc, q, kc, vcT, kslc, vslcT, kwin, vwinT, gT)


XH = 8
YH = 32


def _causal_taps(buf, w_ref, bias, hist, ntaps):
    rows = TS + 8
    base = hist - (ntaps - 1)
    out = jnp.broadcast_to(bias, (TS, buf.shape[1]))
    for r in range(8):
        z = None
        for k in range(ntaps):
            if (base + k) % 8 == r:
                term = w_ref[k:k + 1, :] * buf[base + k - r:base + k - r + rows, :]
                z = term if z is None else z + term
        if z is not None:
            out = out + (z[0:TS, :] if r == 0 else pltpu.roll(z, rows - r, 0)[0:TS, :])
    return out


def _seq_mix_kernel(lru_ref, cv_ref, cw_ref, cb_ref, wa_ref, ba_ref, wx_ref, bx_ref, lam_ref,
                    dw_ref, db_ref, lng_ref, lnb_ref, gl_ref, gc_ref,
                    yl_ref, yc_ref, xbuf, ybuf, hbuf):
    t = pl.program_id(1)

    @pl.when(t == 0)
    def _():
        xbuf[0:XH, :] = jnp.zeros((XH, LRU_W), F32)
        xbuf[XH + TS:XH + TS + 8, :] = jnp.zeros((8, LRU_W), F32)
        ybuf[0:YH, :] = jnp.zeros((YH, CV_W), F32)
        ybuf[YH + TS:YH + TS + 8, :] = jnp.zeros((8, CV_W), F32)
        hbuf[...] = jnp.zeros((8, LRU_W), F32)

    xb = lru_ref[:, 0:LRU_W]
    gb = lru_ref[:, LRU_W:2 * LRU_W]
    xbuf[XH:XH + TS, :] = xb
    xr = _causal_taps(xbuf, cw_ref, cb_ref[...], XH, LRU_CONV)
    xbuf[0:XH, :] = xbuf[TS:TS + XH, :]

    xr16 = xr.astype(BF16)
    r = _sigmoid(_dot(xr16, wa_ref[...]) + ba_ref[...])
    ig = _sigmoid(_dot(xr16, wx_ref[...]) + bx_ref[...])
    nl = -lam_ref[...]
    softplus = jnp.maximum(nl, 0.0) + jnp.log1p(jnp.exp(-jnp.abs(nl)))
    log_a = -LRU_C * r * softplus
    a = jnp.exp(log_a)
    u = xr * ig * jnp.sqrt(-jnp.tanh(log_a) * (a * a + 1.0))

    row = lax.broadcasted_iota(jnp.int32, (TS, LRU_W), 0)
    d = 1
    while d < TS:
        keep = row >= d
        a_sh = jnp.where(keep, pltpu.roll(a, d, 0), 1.0)
        u_sh = jnp.where(keep, pltpu.roll(u, d, 0), 0.0)
        u = a * u_sh + u
        a = a * a_sh
        d *= 2
    hseq = a * hbuf[0:1, :] + u
    hbuf[...] = jnp.broadcast_to(hseq[TS - 1:TS, :], (8, LRU_W))
    yl = hseq * _gelu_tanh(gb)
    yl = yl * lax.rsqrt(jnp.mean(yl * yl, axis=-1, keepdims=True) + EPS) * gl_ref[...]
    yl_ref[...] = yl.astype(BF16)

    y = cv_ref[:, 0:CV_W] * _sigmoid(cv_ref[:, CV_W:2 * CV_W])
    ybuf[YH:YH + TS, :] = y
    c = _causal_taps(ybuf, dw_ref, db_ref[...], YH, CV_KERNEL)
    ybuf[0:YH, :] = ybuf[TS:TS + YH, :]
    mu = jnp.mean(c, axis=-1, keepdims=True)
    cc = c - mu
    var = jnp.mean(cc * cc, axis=-1, keepdims=True)
    ln = cc * lax.rsqrt(var + EPS) * lng_ref[...] + lnb_ref[...]
    yc = ln * _sigmoid(ln)
    yc = yc * lax.rsqrt(jnp.mean(yc * yc, axis=-1, keepdims=True) + EPS) * gc_ref[...]
    yc_ref[...] = yc.astype(BF16)


def _seq_mix(lru, cv, params, B, S):
    const = lambda a: pl.BlockSpec(a.shape, lambda b, t: (0,) * a.ndim)
    tile = lambda w_: pl.BlockSpec((None, TS, w_), lambda b, t: (b, t, 0))
    return pl.pallas_call(
        _seq_mix_kernel,
        out_shape=(jax.ShapeDtypeStruct((B, S, LRU_W), BF16),
                   jax.ShapeDtypeStruct((B, S, CV_W), BF16)),
        grid=(B, S // TS),
        in_specs=[tile(2 * LRU_W), tile(2 * CV_W)] + [const(p) for p in params],
        out_specs=(tile(LRU_W), tile(CV_W)),
        scratch_shapes=[pltpu.VMEM((XH + TS + 8, LRU_W), F32), pltpu.VMEM((YH + TS + 8, CV_W), F32),
                        pltpu.VMEM((8, LRU_W), F32)],
        compiler_params=pltpu.CompilerParams(
            dimension_semantics=("arbitrary", "arbitrary"), vmem_limit_bytes=VMEM_LIMIT),
        name="seq_mix",
    )(lru, cv, *params)


def _out_mlp_kernel(x_ref, ya_ref, yl_ref, yc_ref, ga_ref, wo_ref, gm_ref, w1_ref, w2_ref, o_ref):
    ya = ya_ref[...]
    ya = (ya * lax.rsqrt(jnp.mean(ya * ya, axis=-1, keepdims=True) + EPS) * ga_ref[...]).astype(BF16)
    x1 = (x_ref[...] + _dot(ya, wo_ref[0:NSA_W, :])
          + _dot(yl_ref[...], wo_ref[NSA_W:NSA_W + LRU_W, :])
          + _dot(yc_ref[...], wo_ref[NSA_W + LRU_W:NSA_W + LRU_W + CV_W, :]))
    hm = (x1 * lax.rsqrt(jnp.mean(x1 * x1, axis=-1, keepdims=True) + EPS) * gm_ref[...]).astype(BF16)
    d_ff = w1_ref.shape[1]
    fc = 1024
    o_ref[...] = x1
    for c in range(d_ff // fc):
        hc = jnp.maximum(_dot(hm, w1_ref[:, c * fc:(c + 1) * fc]), 0.0)
        o_ref[...] += _dot((hc * hc).astype(BF16), w2_ref[c * fc:(c + 1) * fc, :])


def _out_mlp(x2, ya, yl, yc, ga, wo, gm, w1, w2):
    T, D = x2.shape
    tok = lambda w_: pl.BlockSpec((TM, w_), lambda t: (t, 0))
    const1 = lambda a: pl.BlockSpec(a.shape, lambda t: (0,) * a.ndim, pipeline_mode=pl.Buffered(1))
    return pl.pallas_call(
        _out_mlp_kernel,
        out_shape=jax.ShapeDtypeStruct((T, D), F32),
        grid=(T // TM,),
        in_specs=[tok(D), tok(NSA_W), tok(LRU_W), tok(CV_W),
                  const1(ga), const1(wo), const1(gm), const1(w1), const1(w2)],
        out_specs=tok(D),
        compiler_params=pltpu.CompilerParams(
            dimension_semantics=("arbitrary",), vmem_limit_bytes=VMEM_LIMIT),
        name="out_mlp",
    )(x2, ya, yl, yc, ga, wo, gm, w1, w2)


def _block_ones(n):
    idx = np.arange(n) // DH
    return jnp.asarray((idx[:, None] == idx[None, :]).astype(np.float32) / DH, dtype=BF16)


def _permute_w_in(w):
    off_kv = NSA_W
    kv = [w[:, off_kv + c * 128: off_kv + (c + 1) * 128] for c in range(6)]
    off_gate = off_kv + 6 * 128
    ngate = 3 * HKV * G
    off_lru = off_gate + ngate
    gate = jnp.pad(w[:, off_gate:off_lru], ((0, 0), (0, LANES - ngate)))
    cols = [w[:, 0:NSA_W], kv[0], kv[1], kv[2], kv[4], kv[3], kv[5],
            w[:, off_lru:off_lru + 2 * LRU_W + 2 * CV_W], gate]
    return jnp.concatenate(cols, axis=1).astype(BF16)


def _compress_weights(pos, w1, w2):
    slot_kv = jnp.array([0, 0, 1, 1])
    eye = jnp.eye(4, dtype=F32)
    w1r = w1.reshape(2, 2, CMP_STRIDE, DH, CMP_HID)
    slot = jnp.arange(4).reshape(1, 4, 1, 1)
    wbig = jnp.concatenate(
        [jnp.where(slot == t, w1r[t // 2, half][:, None, :, :], 0.0).reshape(CMP_STRIDE * 4 * DH, CMP_HID)
         for half in range(2) for t in range(4)], axis=1)
    posr = pos.reshape(2, 2, CMP_STRIDE, DH)[slot_kv]
    pos2 = jnp.transpose(posr, (1, 2, 0, 3)).reshape(2, CMP_STRIDE * 4 * DH)
    pos2 = jnp.pad(pos2, ((0, 6), (0, 0)))
    w2big = jnp.einsum('sod,st->sotd', w2[slot_kv], eye).reshape(4 * CMP_HID, 4 * DH)
    return pos2.astype(BF16), wbig.astype(BF16), w2big.astype(BF16)


def _block_diag(w):
    hh, bw, _ = w.shape
    eye = jnp.eye(hh, dtype=w.dtype)
    return jnp.einsum('hij,hg->higj', w, eye).reshape(hh * bw, hh * bw).astype(BF16)


def kernel(x, attn_norm, w_in, q_norm, k_norm, cmp_pos, cmp_w1, cmp_w2, lru_conv_w, lru_conv_b,
           lru_wa, lru_ba, lru_wx, lru_bx, lru_lambda, cv_dw_w, cv_dw_b, cv_ln_g, cv_ln_b,
           out_norm, w_out, mlp_norm, mlp_w1, mlp_w2):
    B, S, D = x.shape
    depth = w_in.shape[0]
    assert S % TM == 0 and S % TS == 0 and S >= WINDOW + QB and S // SLC_BLOCK >= N_SEL
    NC = S // CMP_STRIDE
    row = lambda v: v.reshape(1, -1).astype(F32)
    bd256 = _block_ones(256)
    bd128 = _block_ones(128)

    x2 = x.reshape(B * S, D)
    for l in range(depth):
        qgain = row(jnp.tile(q_norm[l], HKV * G)) * (DH ** -0.5 * LOG2E)
        kgain = row(jnp.concatenate([jnp.tile(k_norm[l, 1], HKV), jnp.tile(k_norm[l, 2], HKV)]))
        q, cmpr, kslc, kwin, vslcT, vwinT, lru, cv, gT = _in_proj(
            x2, row(attn_norm[l]), _permute_w_in(w_in[l]), bd256, qgain, kgain, B, S)

        pos2, wbig, w2big = _compress_weights(cmp_pos[l], cmp_w1[l], cmp_w2[l])
        kc, vcT = _compress(cmpr.reshape(B, NC, CMP_STRIDE * 256), pos2, wbig, w2big, bd128,
                            row(jnp.tile(k_norm[l, 0], HKV)), B, NC)

        sc = _score_bound(qgain, k_norm[l, 1], k_norm[l, 2], k_norm[l, 0])
        y_attn = _attention(sc, q.reshape(B, S, NSA_W), kc, vcT, kslc, vslcT, kwin, vwinT, gT, B, S)

        g_out = out_norm[l]
        seq_params = (lru_conv_w[l], row(lru_conv_b[l]), _block_diag(lru_wa[l]), row(lru_ba[l]),
                      _block_diag(lru_wx[l]), row(lru_bx[l]), row(lru_lambda[l]),
                      cv_dw_w[l], row(cv_dw_b[l]), row(cv_ln_g[l]), row(cv_ln_b[l]),
                      row(g_out[NSA_W:NSA_W + LRU_W]), row(g_out[NSA_W + LRU_W:]))
        yl, yc = _seq_mix(lru.reshape(B, S, 2 * LRU_W), cv.reshape(B, S, 2 * CV_W), seq_params, B, S)

        x2 = _out_mlp(x2, y_attn.reshape(B * S, NSA_W), yl.reshape(B * S, LRU_W),
                      yc.reshape(B * S, CV_W), row(g_out[:NSA_W]), w_out[l].astype(BF16),
                      row(mlp_norm[l]), mlp_w1[l].astype(BF16), mlp_w2[l].astype(BF16))
    return x2.reshape(B, S, D)
```
